```python
import jax
import jax.numpy as jnp
from jax import lax
import numpy as np

D_MODEL = 1024
BATCH = 8
SEQ = 2048
DEPTH = 4

GRID_W = 64
CTX_LEN = 256
N_BRANCH = 4
BRANCH_W = D_MODEL // 2
RET_HEADS = 4
RET_DK = BRANCH_W // RET_HEADS
HG_HEADS = 4
HG_DK = BRANCH_W // HG_HEADS
NA_HEADS = 8
NA_HD = BRANCH_W // NA_HEADS
NA_KH = 8
NA_KW = 16
GQA_HEADS = 8
GQA_KV_HEADS = 2
GQA_HD = BRANCH_W // GQA_HEADS
GQA_KV_W = GQA_KV_HEADS * GQA_HD
GQA_BLOCK = 128
SCAN_CHUNK = 64
ROPE_BASE = 10000.0
N_EXPERTS = 32
TOP_K = 4
D_FF_EXPERT = D_MODEL
SWIGLU_LIMIT = 7.0
SWIGLU_ALPHA = 1.702
MOE_BLOCK = 256
LN_EPS = 1e-5
NORM_EPS = 1e-6
NEG_INF = -1e30
DN_ALPHA = (2 * DEPTH) ** 0.25
DN_BETA = (8 * DEPTH) ** -0.25
PROJ_WIDTHS = (BRANCH_W,) * 4 + (BRANCH_W,) * 5 + (BRANCH_W,) * 3 + (BRANCH_W, GQA_KV_W, GQA_KV_W, N_BRANCH * D_MODEL)
PROJ_TOTAL = sum(PROJ_WIDTHS)

kernel_name = 'hybrid_gated_ret_hgrn2_na_gqa_moe_dit'


def _heads(a, n):
    b, t, w = a.shape
    return a.reshape(b, t, n, w // n).transpose(0, 2, 1, 3)


def _merge_heads(a):
    b, n, t, d = a.shape
    return a.transpose(0, 2, 1, 3).reshape(b, t, n * d)


def _layernorm(x, g, b):
    xf = x.astype(jnp.float32)
    mu = jnp.mean(xf, -1, keepdims=True)
    var = jnp.mean(jnp.square(xf - mu), -1, keepdims=True)
    return ((xf - mu) * lax.rsqrt(var + LN_EPS) * g + b).astype(x.dtype)


def _rmsnorm(x, g):
    xf = x.astype(jnp.float32)
    return (xf * lax.rsqrt(jnp.mean(jnp.square(xf), -1, keepdims=True) + NORM_EPS) * g).astype(x.dtype)


def _grid_pos(t):
    idx = jnp.arange(t, dtype=jnp.int32)
    return (idx // GRID_W).astype(jnp.float32), (idx % GRID_W).astype(jnp.float32)


def _rotate(x, pos):
    n = x.shape[-1]
    inv = ROPE_BASE ** (-jnp.arange(0, n, 2, dtype=jnp.float32) / n)
    ang = pos[:, None] * inv[None, :]
    cos, sin = jnp.cos(ang), jnp.sin(ang)
    x1 = x[..., : n // 2].astype(jnp.float32)
    x2 = x[..., n // 2:].astype(jnp.float32)
    return jnp.concatenate([x1 * cos - x2 * sin, x1 * sin + x2 * cos], -1).astype(x.dtype)


def _rope2d(x, row, col):
    d = x.shape[-1]
    return jnp.concatenate([_rotate(x[..., : d // 2], row), _rotate(x[..., d // 2:], col)], -1)


def _attend(q, k, v):
    s = jnp.einsum('bhqd,bhkd->bhqk', q, k).astype(jnp.float32) * (q.shape[-1] ** -0.5)
    p = jax.nn.softmax(s, axis=-1)
    return jnp.einsum('bhqk,bhkd->bhqd', p.astype(v.dtype), v)


def _chunk_scan(q, k, v, logf, s0, include_diag):
    out_dtype = v.dtype
    b_, h_, t_, _ = q.shape
    dv = v.shape[-1]
    n = t_ // SCAN_CHUNK
    scalar_decay = logf.shape[-1] == 1
    mask = jnp.tril(jnp.ones((SCAN_CHUNK, SCAN_CHUNK), bool), 0 if include_diag else -1)

    def to_chunks(a):
        a = a.astype(jnp.float32)
        return jnp.moveaxis(a.reshape(b_, h_, n, SCAN_CHUNK, a.shape[-1]), 2, 0)

    def step(s, inp):
        qc, kc, vc, lf = inp
        cum = jnp.cumsum(lf, axis=2)
        seg = cum[:, :, :, None, :] - cum[:, :, None, :, :]
        dec = jnp.exp(jnp.where(mask[:, :, None], seg, -jnp.inf))
        if scalar_decay:
            a = jnp.einsum('bhid,bhjd->bhij', qc, kc) * dec[..., 0]
        else:
            a = jnp.einsum('bhid,bhjd,bhijd->bhij', qc, kc, dec)
        o = jnp.einsum('bhij,bhje->bhie', a, vc) + jnp.einsum('bhid,bhde->bhie', qc * jnp.exp(cum), s)
        last = cum[:, :, -1:, :]
        s_new = jnp.exp(last[:, :, 0, :])[..., None] * s + jnp.einsum('bhjd,bhje->bhde', kc * jnp.exp(last - cum), vc)
        return s_new, o

    s_fin, o = lax.scan(step, s0, (to_chunks(q), to_chunks(k), to_chunks(v), to_chunks(logf)))
    o = jnp.moveaxis(o, 0, 2).reshape(b_, h_, t_, dv)
    return o.astype(out_dtype), s_fin


def _bidir_prefix(lat_dirs, ctx_dirs, diags):
    outs_l, outs_c = [], []
    for d in range(2):
        lat, cx = lat_dirs[d], ctx_dirs[d]
        if d == 1:
            lat = tuple(jnp.flip(a, axis=2) for a in lat)
            cx = tuple(jnp.flip(a, axis=2) for a in cx)
        b_, h_, _, dk = cx[0].shape
        dv = cx[2].shape[-1]
        s0 = jnp.zeros((b_, h_, dk, dv), jnp.float32)
        o_c, s_c = _chunk_scan(cx[0], cx[1], cx[2], cx[3], s0, diags[d])
        o_l, _ = _chunk_scan(lat[0], lat[1], lat[2], lat[3], s_c, diags[d])
        if d == 1:
            o_c, o_l = jnp.flip(o_c, axis=2), jnp.flip(o_l, axis=2)
        outs_l.append(o_l)
        outs_c.append(o_c)
    return outs_l[0] + outs_l[1], outs_c[0] + outs_c[1]


def _retention(parts, parts_c, decay_logit, gn_g, gn_b, row, col):
    q, k, v, g = parts
    qc, kc, vc, gc = parts_c
    b_, t_, _ = q.shape
    l_ = qc.shape[1]
    log_gamma = jax.nn.log_sigmoid(decay_logit.astype(jnp.float32))

    def heads(q_, k_, v_):
        return _heads(q_, RET_HEADS), _heads(k_, RET_HEADS) * (RET_DK ** -0.5), _heads(v_, RET_HEADS)

    ql, kl, vl = heads(q, k, v)
    ql, kl = _rope2d(ql, row, col), _rope2d(kl, row, col)
    qx, kx, vx = heads(qc, kc, vc)

    def decay(d, t):
        return jnp.broadcast_to(log_gamma[d][None, :, None, None], (b_, RET_HEADS, t, 1))

    lat_dirs = [(ql, kl, vl, decay(d, t_)) for d in range(2)]
    ctx_dirs = [(qx, kx, vx, decay(d, l_)) for d in range(2)]
    o_l, o_c = _bidir_prefix(lat_dirs, ctx_dirs, (True, False))

    def out(o, gate):
        of = o.astype(jnp.float32)
        mu = jnp.mean(of, -1, keepdims=True)
        var = jnp.mean(jnp.square(of - mu), -1, keepdims=True)
        y = _merge_heads((of - mu) * lax.rsqrt(var + LN_EPS)) * gn_g + gn_b
        return (y * jax.nn.silu(gate.astype(jnp.float32))).astype(gate.dtype)

    return out(o_l, g), out(o_c, gc)


def _hgrn2(parts, parts_c, lower, norm_g):
    def prep(q, f_fwd, f_bwd, i):
        qh = _heads(jax.nn.silu(q), HG_HEADS)
        vh = _heads(i, HG_HEADS)
        dirs = []
        for d, f_raw in enumerate((f_fwd, f_bwd)):
            z = _heads(f_raw, HG_HEADS).astype(jnp.float32)
            lb = lower[d].reshape(1, HG_HEADS, 1, HG_DK)
            f = lb + (1.0 - lb) * jax.nn.sigmoid(z)
            key = (1.0 - lb) * jax.nn.sigmoid(-z)
            dirs.append((qh, key, vh, jnp.log(f)))
        return dirs

    o_l, o_c = _bidir_prefix(prep(*parts[:4]), prep(*parts_c[:4]), (True, True))

    def out(o, gate):
        of = o.astype(jnp.float32)
        on = of * lax.rsqrt(jnp.mean(jnp.square(of), -1, keepdims=True) + NORM_EPS)
        y = _merge_heads(on) * norm_g
        return (y * jax.nn.silu(gate.astype(jnp.float32))).astype(gate.dtype)

    return out(o_l, parts[4]), out(o_c, parts_c[4])


def _neighbourhood_attn(ql, kl, vl, qc, kc, vc, rpb):
    b_, t_, _ = ql.shape
    rows = t_ // GRID_W
    kh = min(NA_KH, rows)
    n_cb = GRID_W // NA_KW
    band = 2 * NA_KW
    scale = NA_HD ** -0.5

    def grid(a):
        return _heads(a, NA_HEADS).reshape(b_, NA_HEADS, rows, GRID_W, NA_HD)

    qg, kg, vg = grid(ql), grid(kl), grid(vl)
    r = jnp.arange(rows, dtype=jnp.int32)
    row_idx = jnp.clip(r - kh // 2, 0, rows - kh)[:, None] + jnp.arange(kh, dtype=jnp.int32)[None, :]
    cidx = jnp.arange(GRID_W, dtype=jnp.int32)
    col0 = jnp.clip(cidx - NA_KW // 2, 0, GRID_W - NA_KW)
    band_idx = (jnp.clip(jnp.arange(n_cb, dtype=jnp.int32) * NA_KW - NA_KW // 2, 0, GRID_W - band)[:, None]
                + jnp.arange(band, dtype=jnp.int32)[None, :])

    k_blk = kg[:, :, row_idx][:, :, :, :, band_idx]
    v_blk = vg[:, :, row_idx][:, :, :, :, band_idx]
    q_blk = qg.reshape(b_, NA_HEADS, rows, n_cb, NA_KW, NA_HD)
    s_loc = jnp.einsum('bhrjqd,bhrajkd->bhrjqak', q_blk, k_blk).astype(jnp.float32) * scale

    key_col = band_idx[:, None, :]
    q_col = cidx.reshape(n_cb, NA_KW)[:, :, None]
    q_col0 = col0.reshape(n_cb, NA_KW)[:, :, None]
    valid = (key_col >= q_col0) & (key_col < q_col0 + NA_KW)
    dr = row_idx - r[:, None] + (NA_KH - 1)
    dc = jnp.clip(key_col - q_col + (NA_KW - 1), 0, 2 * NA_KW - 2)
    bias = rpb[:, dr[:, None, None, :, None], dc[None, :, :, None, :]]
    s_loc = jnp.where(valid[:, :, None, :], s_loc + bias.astype(jnp.float32), NEG_INF)

    qx, kx, vx = _heads(qc, NA_HEADS), _heads(kc, NA_HEADS), _heads(vc, NA_HEADS)
    s_ctx = jnp.einsum('bhrjqd,bhld->bhrjql', q_blk, kx).astype(jnp.float32) * scale
    n_loc = kh * band
    s_all = jnp.concatenate([s_loc.reshape(b_, NA_HEADS, rows, n_cb, NA_KW, n_loc), s_ctx], -1)
    p = jax.nn.softmax(s_all, axis=-1).astype(vl.dtype)
    p_loc = p[..., :n_loc].reshape(b_, NA_HEADS, rows, n_cb, NA_KW, kh, band)
    o = (jnp.einsum('bhrjqak,bhrajkd->bhrjqd', p_loc, v_blk)
         + jnp.einsum('bhrjql,bhld->bhrjqd', p[..., n_loc:], vx))
    o_lat = _merge_heads(o.reshape(b_, NA_HEADS, t_, NA_HD))
    o_ctx = _merge_heads(_attend(qx, kx, vx))
    return o_lat, o_ctx


def _gqa_attend(q, k, v):
    s = jnp.einsum('bhgqd,bhkd->bhgqk', q, k).astype(jnp.float32) * (q.shape[-1] ** -0.5)
    p = jax.nn.softmax(s, axis=-1)
    return jnp.einsum('bhgqk,bhkd->bhgqd', p.astype(v.dtype), v)


def _gqa(ql, kl, vl, qc, kc, vc, qn_g, kn_g, row, col):
    b_, t_, _ = ql.shape
    grp = GQA_HEADS // GQA_KV_HEADS

    def q_heads(a):
        return a.reshape(b_, a.shape[1], GQA_KV_HEADS, grp, GQA_HD).transpose(0, 2, 3, 1, 4)

    qlh = _rope2d(_rmsnorm(q_heads(ql), qn_g), row, col)
    klh = _rope2d(_rmsnorm(_heads(kl, GQA_KV_HEADS), kn_g), row, col)
    qch = _rmsnorm(q_heads(qc), qn_g)
    kch = _rmsnorm(_heads(kc, GQA_KV_HEADS), kn_g)
    vlh, vch = _heads(vl, GQA_KV_HEADS), _heads(vc, GQA_KV_HEADS)
    k_all = jnp.concatenate([klh, kch], axis=2)
    v_all = jnp.concatenate([vlh, vch], axis=2)
    nb = t_ // GQA_BLOCK
    q_blocks = jnp.moveaxis(qlh.reshape(b_, GQA_KV_HEADS, grp, nb, GQA_BLOCK, GQA_HD), 3, 0)
    o = lax.map(lambda qb: _gqa_attend(qb, k_all, v_all), q_blocks)
    o = jnp.moveaxis(o, 0, 3).reshape(b_, GQA_KV_HEADS, grp, t_, GQA_HD)
    o_lat = o.transpose(0, 3, 1, 2, 4).reshape(b_, t_, GQA_HEADS * GQA_HD)
    oc = _gqa_attend(qch, kch, vch)
    o_ctx = oc.transpose(0, 3, 1, 2, 4).reshape(b_, qc.shape[1], GQA_HEADS * GQA_HD)
    return o_lat, o_ctx


def _merge_branches(outs, gate_raw, w_branch, w_out):
    b_, t_, _ = gate_raw.shape
    yb = jnp.einsum('btnw,nwd->btnd', jnp.stack(outs, axis=2), w_branch)
    g = jax.nn.sigmoid(gate_raw.reshape(b_, t_, N_BRANCH, D_MODEL))
    return jnp.sum(g * yb, axis=2) @ w_out


def _moe(h, w_router, b_router, w_gu, b_gu, w_down, b_down):
    n_tok, d = h.shape
    logits = (h @ w_router).astype(jnp.float32) + b_router.astype(jnp.float32)
    top_val, top_idx = lax.top_k(logits, TOP_K)
    gate = jax.nn.softmax(top_val, axis=-1)
    nk = n_tok * TOP_K
    flat_e = top_idx.reshape(nk).astype(jnp.int32)
    flat_t = jnp.arange(nk, dtype=jnp.int32) // TOP_K
    order = jnp.argsort(flat_e)
    e_s, t_s, g_s = flat_e[order], flat_t[order], gate.reshape(nk)[order]
    counts = jnp.zeros((N_EXPERTS,), jnp.int32).at[flat_e].add(1)
    padded = (counts + MOE_BLOCK - 1) // MOE_BLOCK * MOE_BLOCK
    start_s = jnp.cumsum(counts) - counts
    ends_p = jnp.cumsum(padded)
    start_p = ends_p - padded
    dest = start_p[e_s] + (jnp.arange(nk, dtype=jnp.int32) - start_s[e_s])
    n_blocks = (nk + N_EXPERTS * (MOE_BLOCK - 1) + MOE_BLOCK - 1) // MOE_BLOCK
    n_slots = n_blocks * MOE_BLOCK
    slot_tok = jnp.full((n_slots,), n_tok, jnp.int32).at[dest].set(t_s)
    slot_gate = jnp.zeros((n_slots,), jnp.float32).at[dest].set(g_s)
    block_start = jnp.arange(n_blocks, dtype=jnp.int32) * MOE_BLOCK
    block_e = jnp.minimum(jnp.sum(ends_p[None, :] <= block_start[:, None], axis=1), N_EXPERTS - 1)
    h_pad = jnp.concatenate([h, jnp.zeros((1, d), h.dtype)], axis=0)
    xb = h_pad[slot_tok].reshape(n_blocks, MOE_BLOCK, d)

    def expert_block(args):
        xe, e = args
        gu = xe @ w_gu[e] + b_gu[e]
        g_, u_ = gu[:, :D_FF_EXPERT], gu[:, D_FF_EXPERT:]
        g_ = jnp.minimum(g_, SWIGLU_LIMIT)
        u_ = jnp.clip(u_, -SWIGLU_LIMIT, SWIGLU_LIMIT)
        act = g_ * jax.nn.sigmoid(SWIGLU_ALPHA * g_) * (u_ + 1.0)
        return act @ w_down[e] + b_down[e]

    yb = lax.map(expert_block, (xb, block_e))
    y = yb.reshape(n_slots, d) * slot_gate[:, None].astype(yb.dtype)
    return jnp.zeros((n_tok + 1, d), y.dtype).at[slot_tok].add(y)[:n_tok]


def setup_inputs(seed: int = 0) -> dict:
    key = jax.random.key(seed)
    ks = jax.random.split(key, 26)

    def nrm(k, shape, scale):
        return jax.random.normal(k, shape, jnp.float32) * scale

    d = D_MODEL
    p = 5.0 + jnp.arange(RET_HEADS, dtype=jnp.float32)
    ret_base = jnp.log(2.0 ** p - 1.0)
    return {
        'x': nrm(ks[0], (BATCH, SEQ, d), 1.0),
        'c': nrm(ks[1], (BATCH, d), 1.0),
        'ctx': nrm(ks[2], (BATCH, CTX_LEN, d), 1.0),
        'c_ctx': nrm(ks[3], (d,), 1.0),
        'w_mod': nrm(ks[4], (DEPTH, d, 6 * d), d ** -0.5),
        'b_mod': nrm(ks[5], (DEPTH, 6 * d), 0.02),
        'w_in': nrm(ks[6], (DEPTH, d, PROJ_TOTAL), d ** -0.5),
        'ret_decay': ret_base[None, None, :] + nrm(ks[7], (DEPTH, 2, RET_HEADS), 0.05),
        'ret_gn_g': 1.0 + nrm(ks[8], (DEPTH, BRANCH_W), 0.02),
        'ret_gn_b': nrm(ks[9], (DEPTH, BRANCH_W), 0.02),
        'hg_lb': 1.0 + nrm(ks[10], (2, DEPTH, BRANCH_W), 0.1),
        'hg_norm_g': 1.0 + nrm(ks[11], (DEPTH, BRANCH_W), 0.02),
        'na_rpb': nrm(ks[12], (DEPTH, NA_HEADS, 2 * NA_KH - 1, 2 * NA_KW - 1), 0.1),
        'gq_qn_g': 1.0 + nrm(ks[13], (DEPTH, GQA_HD), 0.02),
        'gq_kn_g': 1.0 + nrm(ks[14], (DEPTH, GQA_HD), 0.02),
        'w_branch': nrm(ks[15], (DEPTH, N_BRANCH, BRANCH_W, d), BRANCH_W ** -0.5),
        'w_out': nrm(ks[16], (DEPTH, d, d), d ** -0.5 * DN_BETA),
        'ln_g': 1.0 + nrm(ks[17], (DEPTH, 2, d), 0.02),
        'ln_b': nrm(ks[18], (DEPTH, 2, d), 0.02),
        'w_router': nrm(ks[19], (DEPTH, d, N_EXPERTS), d ** -0.5),
        'b_router': nrm(ks[20], (DEPTH, N_EXPERTS), 0.01),
        'w_gu': nrm(ks[21], (DEPTH, N_EXPERTS, d, 2 * D_FF_EXPERT), d ** -0.5),
        'b_gu': nrm(ks[22], (DEPTH, N_EXPERTS, 2 * D_FF_EXPERT), 0.02),
        'w_down': nrm(ks[23], (DEPTH, N_EXPERTS, D_FF_EXPERT, d), D_FF_EXPERT ** -0.5 * DN_BETA),
        'b_down': nrm(ks[24], (DEPTH, N_EXPERTS, d), 0.02),
    }


def reference(x, c, ctx, c_ctx, w_mod, b_mod, w_in, ret_decay, ret_gn_g, ret_gn_b, hg_lb, hg_norm_g,
              na_rpb, gq_qn_g, gq_kn_g, w_branch, w_out, ln_g, ln_b, w_router, b_router, w_gu, b_gu,
              w_down, b_down):
    b_, t_, d = x.shape
    row, col = _grid_pos(t_)
    sm = jax.nn.softmax(hg_lb.astype(jnp.float32), axis=1)
    lower = jnp.cumsum(sm, axis=1) - sm[:, :1]
    split_at = np.cumsum(PROJ_WIDTHS)[:-1].tolist()
    silu_c = jax.nn.silu(c)
    silu_cc = jax.nn.silu(c_ctx)
    xc = ctx
    for l in range(DEPTH):
        last = l == DEPTH - 1
        sh1, sc1, g1, sh2, sc2, g2 = jnp.split((silu_c @ w_mod[l] + b_mod[l])[:, None, :], 6, axis=-1)
        csh1, csc1, cg1, csh2, csc2, cg2 = jnp.split(silu_cc @ w_mod[l] + b_mod[l], 6, axis=-1)
        h = x * (1.0 + sc1) + sh1
        hc = xc * (1.0 + csc1) + csh1
        z = jnp.split(h @ w_in[l], split_at, axis=-1)
        zc = jnp.split(hc @ w_in[l], split_at, axis=-1)
        ret_l, ret_c = _retention(z[0:4], zc[0:4], ret_decay[l], ret_gn_g[l], ret_gn_b[l], row, col)
        hg_l, hg_c = _hgrn2(z[4:9], zc[4:9], lower[:, l], hg_norm_g[l])
        na_l, na_c = _neighbourhood_attn(z[9], z[10], z[11], zc[9], zc[10], zc[11], na_rpb[l])
        gq_l, gq_c = _gqa(z[12], z[13], z[14], zc[12], zc[13], zc[14], gq_qn_g[l], gq_kn_g[l], row, col)
        y = _merge_branches((ret_l, hg_l, na_l, gq_l), z[15], w_branch[l], w_out[l])
        x = _layernorm(DN_ALPHA * x + g1 * y, ln_g[l, 0], ln_b[l, 0])
        h2 = (x * (1.0 + sc2) + sh2).reshape(b_ * t_, d)
        if last:
            y2 = _moe(h2, w_router[l], b_router[l], w_gu[l], b_gu[l], w_down[l], b_down[l]).reshape(b_, t_, d)
        else:
            yc = _merge_branches((ret_c, hg_c, na_c, gq_c), zc[15], w_branch[l], w_out[l])
            xc = _layernorm(DN_ALPHA * xc + cg1 * yc, ln_g[l, 0], ln_b[l, 0])
            h2c = (xc * (1.0 + csc2) + csh2).reshape(-1, d)
            y_all = _moe(jnp.concatenate([h2, h2c], axis=0), w_router[l], b_router[l], w_gu[l], b_gu[l],
                         w_down[l], b_down[l])
            y2 = y_all[: b_ * t_].reshape(b_, t_, d)
            y2c = y_all[b_ * t_:].reshape(xc.shape)
            xc = _layernorm(DN_ALPHA * xc + cg2 * y2c, ln_g[l, 1], ln_b[l, 1])
        x = _layernorm(DN_ALPHA * x + g2 * y2, ln_g[l, 1], ln_b[l, 1])
    return x
```

```python
import functools

import jax
import jax.numpy as jnp
from jax import lax
import numpy as np
from jax.experimental import pallas as pl
from jax.experimental.pallas import tpu as pltpu

D_MODEL = 1024
BATCH = 8
SEQ = 2048
DEPTH = 4

GRID_W = 64
CTX_LEN = 256
N_BRANCH = 4
BRANCH_W = D_MODEL // 2
RET_HEADS = 4
RET_DK = BRANCH_W // RET_HEADS
HG_HEADS = 4
HG_DK = BRANCH_W // HG_HEADS
NA_HEADS = 8
NA_HD = BRANCH_W // NA_HEADS
NA_KH = 8
NA_KW = 16
GQA_HEADS = 8
GQA_KV_HEADS = 2
GQA_HD = BRANCH_W // GQA_HEADS
GQA_KV_W = GQA_KV_HEADS * GQA_HD
GQA_BLOCK = 128
SCAN_CHUNK = 64
ROPE_BASE = 10000.0
N_EXPERTS = 32
TOP_K = 4
D_FF_EXPERT = D_MODEL
SWIGLU_LIMIT = 7.0
SWIGLU_ALPHA = 1.702
MOE_BLOCK = 256
LN_EPS = 1e-5
NORM_EPS = 1e-6
NEG_INF = -1e30
DN_ALPHA = (2 * DEPTH) ** 0.25
DN_BETA = (8 * DEPTH) ** -0.25
PROJ_WIDTHS = (BRANCH_W,) * 4 + (BRANCH_W,) * 5 + (BRANCH_W,) * 3 + (BRANCH_W, GQA_KV_W, GQA_KV_W, N_BRANCH * D_MODEL)
PROJ_TOTAL = sum(PROJ_WIDTHS)


def _mm_kernel(x_ref, w_ref, o_ref):
    o_ref[...] = jnp.dot(x_ref[...].astype(jnp.bfloat16), w_ref[...],
                         preferred_element_type=jnp.float32).astype(o_ref.dtype)


def _matmul(x, w, tm, tn, out_dtype=jnp.float32):
    m, k = x.shape
    _, n = w.shape
    return pl.pallas_call(
        _mm_kernel,
        out_shape=jax.ShapeDtypeStruct((m, n), out_dtype),
        grid=(n // tn, m // tm),
        in_specs=[pl.BlockSpec((tm, k), lambda j, i: (i, 0)),
                  pl.BlockSpec((k, tn), lambda j, i: (0, j))],
        out_specs=pl.BlockSpec((tm, tn), lambda j, i: (i, j)),
        compiler_params=pltpu.CompilerParams(
            dimension_semantics=("arbitrary", "arbitrary"),
            vmem_limit_bytes=48 * 1024 * 1024),
        name="in_proj",
    )(x, w)


def _heads(a, n):
    b, t, w = a.shape
    return a.reshape(b, t, n, w // n).transpose(0, 2, 1, 3)


def _merge_heads(a):
    b, n, t, d = a.shape
    return a.transpose(0, 2, 1, 3).reshape(b, t, n * d)


def _layernorm(x, g, b):
    xf = x.astype(jnp.float32)
    mu = jnp.mean(xf, -1, keepdims=True)
    var = jnp.mean(jnp.square(xf - mu), -1, keepdims=True)
    return ((xf - mu) * lax.rsqrt(var + LN_EPS) * g + b).astype(x.dtype)


def _rmsnorm(x, g):
    xf = x.astype(jnp.float32)
    return (xf * lax.rsqrt(jnp.mean(jnp.square(xf), -1, keepdims=True) + NORM_EPS) * g).astype(x.dtype)


def _grid_pos(t):
    idx = jnp.arange(t, dtype=jnp.int32)
    return (idx // GRID_W).astype(jnp.float32), (idx % GRID_W).astype(jnp.float32)


def _rotate(x, pos):
    n = x.shape[-1]
    inv = ROPE_BASE ** (-jnp.arange(0, n, 2, dtype=jnp.float32) / n)
    ang = pos[:, None] * inv[None, :]
    cos, sin = jnp.cos(ang), jnp.sin(ang)
    x1 = x[..., : n // 2].astype(jnp.float32)
    x2 = x[..., n // 2:].astype(jnp.float32)
    return jnp.concatenate([x1 * cos - x2 * sin, x1 * sin + x2 * cos], -1).astype(x.dtype)


def _rope2d(x, row, col):
    d = x.shape[-1]
    return jnp.concatenate([_rotate(x[..., : d // 2], row), _rotate(x[..., d // 2:], col)], -1)


def _attend(q, k, v):
    s = jnp.einsum('bhqd,bhkd->bhqk', q, k).astype(jnp.float32) * (q.shape[-1] ** -0.5)
    p = jax.nn.softmax(s, axis=-1)
    return jnp.einsum('bhqk,bhkd->bhqd', p.astype(v.dtype), v)


def _chunk_scan(q, k, v, logf, s0, include_diag):
    out_dtype = v.dtype
    b_, h_, t_, _ = q.shape
    dv = v.shape[-1]
    n = t_ // SCAN_CHUNK
    scalar_decay = logf.shape[-1] == 1
    mask = jnp.tril(jnp.ones((SCAN_CHUNK, SCAN_CHUNK), bool), 0 if include_diag else -1)

    def to_chunks(a):
        a = a.astype(jnp.float32)
        return jnp.moveaxis(a.reshape(b_, h_, n, SCAN_CHUNK, a.shape[-1]), 2, 0)

    def step(s, inp):
        qc, kc, vc, lf = inp
        cum = jnp.cumsum(lf, axis=2)
        seg = cum[:, :, :, None, :] - cum[:, :, None, :, :]
        dec = jnp.exp(jnp.where(mask[:, :, None], seg, -jnp.inf))
        if scalar_decay:
            a = jnp.einsum('bhid,bhjd->bhij', qc, kc) * dec[..., 0]
        else:
            a = jnp.einsum('bhid,bhjd,bhijd->bhij', qc, kc, dec)
        o = jnp.einsum('bhij,bhje->bhie', a, vc) + jnp.einsum('bhid,bhde->bhie', qc * jnp.exp(cum), s)
        last = cum[:, :, -1:, :]
        s_new = jnp.exp(last[:, :, 0, :])[..., None] * s + jnp.einsum('bhjd,bhje->bhde', kc * jnp.exp(last - cum), vc)
        return s_new, o

    s_fin, o = lax.scan(step, s0, (to_chunks(q), to_chunks(k), to_chunks(v), to_chunks(logf)))
    o = jnp.moveaxis(o, 0, 2).reshape(b_, h_, t_, dv)
    return o.astype(out_dtype), s_fin


def _bidir_prefix(lat_dirs, ctx_dirs, diags):
    outs_l, outs_c = [], []
    for d in range(2):
        lat, cx = lat_dirs[d], ctx_dirs[d]
        if d == 1:
            lat = tuple(jnp.flip(a, axis=2) for a in lat)
            cx = tuple(jnp.flip(a, axis=2) for a in cx)
        b_, h_, _, dk = cx[0].shape
        dv = cx[2].shape[-1]
        s0 = jnp.zeros((b_, h_, dk, dv), jnp.float32)
        o_c, s_c = _chunk_scan(cx[0], cx[1], cx[2], cx[3], s0, diags[d])
        o_l, _ = _chunk_scan(lat[0], lat[1], lat[2], lat[3], s_c, diags[d])
        if d == 1:
            o_c, o_l = jnp.flip(o_c, axis=2), jnp.flip(o_l, axis=2)
        outs_l.append(o_l)
        outs_c.append(o_c)
    return outs_l[0] + outs_l[1], outs_c[0] + outs_c[1]


def _retention(parts, parts_c, decay_logit, gn_g, gn_b, row, col):
    q, k, v, g = parts
    qc, kc, vc, gc = parts_c
    b_, t_, _ = q.shape
    l_ = qc.shape[1]
    log_gamma = jax.nn.log_sigmoid(decay_logit.astype(jnp.float32))

    def heads(q_, k_, v_):
        return _heads(q_, RET_HEADS), _heads(k_, RET_HEADS) * (RET_DK ** -0.5), _heads(v_, RET_HEADS)

    ql, kl, vl = heads(q, k, v)
    ql, kl = _rope2d(ql, row, col), _rope2d(kl, row, col)
    qx, kx, vx = heads(qc, kc, vc)

    def decay(d, t):
        return jnp.broadcast_to(log_gamma[d][None, :, None, None], (b_, RET_HEADS, t, 1))

    lat_dirs = [(ql, kl, vl, decay(d, t_)) for d in range(2)]
    ctx_dirs = [(qx, kx, vx, decay(d, l_)) for d in range(2)]
    o_l, o_c = _bidir_prefix(lat_dirs, ctx_dirs, (True, False))

    def out(o, gate):
        of = o.astype(jnp.float32)
        mu = jnp.mean(of, -1, keepdims=True)
        var = jnp.mean(jnp.square(of - mu), -1, keepdims=True)
        y = _merge_heads((of - mu) * lax.rsqrt(var + LN_EPS)) * gn_g + gn_b
        return (y * jax.nn.silu(gate.astype(jnp.float32))).astype(gate.dtype)

    return out(o_l, g), out(o_c, gc)


def _hgrn2(parts, parts_c, lower, norm_g):
    def prep(q, f_fwd, f_bwd, i):
        qh = _heads(jax.nn.silu(q), HG_HEADS)
        vh = _heads(i, HG_HEADS)
        dirs = []
        for d, f_raw in enumerate((f_fwd, f_bwd)):
            z = _heads(f_raw, HG_HEADS).astype(jnp.float32)
            lb = lower[d].reshape(1, HG_HEADS, 1, HG_DK)
            f = lb + (1.0 - lb) * jax.nn.sigmoid(z)
            key = (1.0 - lb) * jax.nn.sigmoid(-z)
            dirs.append((qh, key, vh, jnp.log(f)))
        return dirs

    o_l, o_c = _bidir_prefix(prep(*parts[:4]), prep(*parts_c[:4]), (True, True))

    def out(o, gate):
        of = o.astype(jnp.float32)
        on = of * lax.rsqrt(jnp.mean(jnp.square(of), -1, keepdims=True) + NORM_EPS)
        y = _merge_heads(on) * norm_g
        return (y * jax.nn.silu(gate.astype(jnp.float32))).astype(gate.dtype)

    return out(o_l, parts[4]), out(o_c, parts_c[4])


def _neighbourhood_attn(ql, kl, vl, qc, kc, vc, rpb):
    b_, t_, _ = ql.shape
    rows = t_ // GRID_W
    kh = min(NA_KH, rows)
    n_cb = GRID_W // NA_KW
    band = 2 * NA_KW
    scale = NA_HD ** -0.5

    def grid(a):
        return _heads(a, NA_HEADS).reshape(b_, NA_HEADS, rows, GRID_W, NA_HD)

    qg, kg, vg = grid(ql), grid(kl), grid(vl)
    r = jnp.arange(rows, dtype=jnp.int32)
    row_idx = jnp.clip(r - kh // 2, 0, rows - kh)[:, None] + jnp.arange(kh, dtype=jnp.int32)[None, :]
    cidx = jnp.arange(GRID_W, dtype=jnp.int32)
    col0 = jnp.clip(cidx - NA_KW // 2, 0, GRID_W - NA_KW)
    band_idx = (jnp.clip(jnp.arange(n_cb, dtype=jnp.int32) * NA_KW - NA_KW // 2, 0, GRID_W - band)[:, None]
                + jnp.arange(band, dtype=jnp.int32)[None, :])

    k_blk = kg[:, :, row_idx][:, :, :, :, band_idx]
    v_blk = vg[:, :, row_idx][:, :, :, :, band_idx]
    q_blk = qg.reshape(b_, NA_HEADS, rows, n_cb, NA_KW, NA_HD)
    s_loc = jnp.einsum('bhrjqd,bhrajkd->bhrjqak', q_blk, k_blk).astype(jnp.float32) * scale

    key_col = band_idx[:, None, :]
    q_col = cidx.reshape(n_cb, NA_KW)[:, :, None]
    q_col0 = col0.reshape(n_cb, NA_KW)[:, :, None]
    valid = (key_col >= q_col0) & (key_col < q_col0 + NA_KW)
    dr = row_idx - r[:, None] + (NA_KH - 1)
    dc = jnp.clip(key_col - q_col + (NA_KW - 1), 0, 2 * NA_KW - 2)
    bias = rpb[:, dr[:, None, None, :, None], dc[None, :, :, None, :]]
    s_loc = jnp.where(valid[:, :, None, :], s_loc + bias.astype(jnp.float32), NEG_INF)

    qx, kx, vx = _heads(qc, NA_HEADS), _heads(kc, NA_HEADS), _heads(vc, NA_HEADS)
    s_ctx = jnp.einsum('bhrjqd,bhld->bhrjql', q_blk, kx).astype(jnp.float32) * scale
    n_loc = kh * band
    s_all = jnp.concatenate([s_loc.reshape(b_, NA_HEADS, rows, n_cb, NA_KW, n_loc), s_ctx], -1)
    p = jax.nn.softmax(s_all, axis=-1).astype(vl.dtype)
    p_loc = p[..., :n_loc].reshape(b_, NA_HEADS, rows, n_cb, NA_KW, kh, band)
    o = (jnp.einsum('bhrjqak,bhrajkd->bhrjqd', p_loc, v_blk)
         + jnp.einsum('bhrjql,bhld->bhrjqd', p[..., n_loc:], vx))
    o_lat = _merge_heads(o.reshape(b_, NA_HEADS, t_, NA_HD))
    o_ctx = _merge_heads(_attend(qx, kx, vx))
    return o_lat, o_ctx


def _gqa_attend(q, k, v):
    s = jnp.einsum('bhgqd,bhkd->bhgqk', q, k).astype(jnp.float32) * (q.shape[-1] ** -0.5)
    p = jax.nn.softmax(s, axis=-1)
    return jnp.einsum('bhgqk,bhkd->bhgqd', p.astype(v.dtype), v)


def _gqa(ql, kl, vl, qc, kc, vc, qn_g, kn_g, row, col):
    b_, t_, _ = ql.shape
    grp = GQA_HEADS // GQA_KV_HEADS

    def q_heads(a):
        return a.reshape(b_, a.shape[1], GQA_KV_HEADS, grp, GQA_HD).transpose(0, 2, 3, 1, 4)

    qlh = _rope2d(_rmsnorm(q_heads(ql), qn_g), row, col)
    klh = _rope2d(_rmsnorm(_heads(kl, GQA_KV_HEADS), kn_g), row, col)
    qch = _rmsnorm(q_heads(qc), qn_g)
    kch = _rmsnorm(_heads(kc, GQA_KV_HEADS), kn_g)
    vlh, vch = _heads(vl, GQA_KV_HEADS), _heads(vc, GQA_KV_HEADS)
    k_all = jnp.concatenate([klh, kch], axis=2)
    v_all = jnp.concatenate([vlh, vch], axis=2)
    nb = t_ // GQA_BLOCK
    q_blocks = jnp.moveaxis(qlh.reshape(b_, GQA_KV_HEADS, grp, nb, GQA_BLOCK, GQA_HD), 3, 0)
    o = lax.map(lambda qb: _gqa_attend(qb, k_all, v_all), q_blocks)
    o = jnp.moveaxis(o, 0, 3).reshape(b_, GQA_KV_HEADS, grp, t_, GQA_HD)
    o_lat = o.transpose(0, 3, 1, 2, 4).reshape(b_, t_, GQA_HEADS * GQA_HD)
    oc = _gqa_attend(qch, kch, vch)
    o_ctx = oc.transpose(0, 3, 1, 2, 4).reshape(b_, qc.shape[1], GQA_HEADS * GQA_HD)
    return o_lat, o_ctx


def _merge_branches(outs, gate_raw, w_branch, w_out):
    b_, t_, _ = gate_raw.shape
    yb = jnp.einsum('btnw,nwd->btnd', jnp.stack(outs, axis=2), w_branch)
    g = jax.nn.sigmoid(gate_raw.reshape(b_, t_, N_BRANCH, D_MODEL))
    return jnp.sum(g * yb, axis=2) @ w_out


def _moe(h, w_router, b_router, w_gu, b_gu, w_down, b_down):
    n_tok, d = h.shape
    logits = (h @ w_router).astype(jnp.float32) + b_router.astype(jnp.float32)
    top_val, top_idx = lax.top_k(logits, TOP_K)
    gate = jax.nn.softmax(top_val, axis=-1)
    nk = n_tok * TOP_K
    flat_e = top_idx.reshape(nk).astype(jnp.int32)
    flat_t = jnp.arange(nk, dtype=jnp.int32) // TOP_K
    order = jnp.argsort(flat_e)
    e_s, t_s, g_s = flat_e[order], flat_t[order], gate.reshape(nk)[order]
    counts = jnp.zeros((N_EXPERTS,), jnp.int32).at[flat_e].add(1)
    padded = (counts + MOE_BLOCK - 1) // MOE_BLOCK * MOE_BLOCK
    start_s = jnp.cumsum(counts) - counts
    ends_p = jnp.cumsum(padded)
    start_p = ends_p - padded
    dest = start_p[e_s] + (jnp.arange(nk, dtype=jnp.int32) - start_s[e_s])
    n_blocks = (nk + N_EXPERTS * (MOE_BLOCK - 1) + MOE_BLOCK - 1) // MOE_BLOCK
    n_slots = n_blocks * MOE_BLOCK
    slot_tok = jnp.full((n_slots,), n_tok, jnp.int32).at[dest].set(t_s)
    slot_gate = jnp.zeros((n_slots,), jnp.float32).at[dest].set(g_s)
    block_start = jnp.arange(n_blocks, dtype=jnp.int32) * MOE_BLOCK
    block_e = jnp.minimum(jnp.sum(ends_p[None, :] <= block_start[:, None], axis=1), N_EXPERTS - 1)
    h_pad = jnp.concatenate([h, jnp.zeros((1, d), h.dtype)], axis=0)
    xb = h_pad[slot_tok].reshape(n_blocks, MOE_BLOCK, d)

    def expert_block(args):
        xe, e = args
        gu = xe @ w_gu[e] + b_gu[e]
        g_, u_ = gu[:, :D_FF_EXPERT], gu[:, D_FF_EXPERT:]
        g_ = jnp.minimum(g_, SWIGLU_LIMIT)
        u_ = jnp.clip(u_, -SWIGLU_LIMIT, SWIGLU_LIMIT)
        act = g_ * jax.nn.sigmoid(SWIGLU_ALPHA * g_) * (u_ + 1.0)
        return act @ w_down[e] + b_down[e]

    yb = lax.map(expert_block, (xb, block_e))
    y = yb.reshape(n_slots, d) * slot_gate[:, None].astype(yb.dtype)
    return jnp.zeros((n_tok + 1, d), y.dtype).at[slot_tok].add(y)[:n_tok]


def kernel(x, c, ctx, c_ctx, w_mod, b_mod, w_in, ret_decay, ret_gn_g, ret_gn_b, hg_lb, hg_norm_g,
           na_rpb, gq_qn_g, gq_kn_g, w_branch, w_out, ln_g, ln_b, w_router, b_router, w_gu, b_gu,
           w_down, b_down):
    b_, t_, d = x.shape
    row, col = _grid_pos(t_)
    sm = jax.nn.softmax(hg_lb.astype(jnp.float32), axis=1)
    lower = jnp.cumsum(sm, axis=1) - sm[:, :1]
    split_at = np.cumsum(PROJ_WIDTHS)[:-1].tolist()
    silu_c = jax.nn.silu(c)
    silu_cc = jax.nn.silu(c_ctx)
    xc = ctx
    for l in range(DEPTH):
        last = l == DEPTH - 1
        sh1, sc1, g1, sh2, sc2, g2 = jnp.split((silu_c @ w_mod[l] + b_mod[l])[:, None, :], 6, axis=-1)
        csh1, csc1, cg1, csh2, csc2, cg2 = jnp.split(silu_cc @ w_mod[l] + b_mod[l], 6, axis=-1)
        h = x * (1.0 + sc1) + sh1
        hc = xc * (1.0 + csc1) + csh1
        w_in_b = w_in[l].astype(jnp.bfloat16)
        zl = _matmul(h.reshape(b_ * t_, d), w_in_b, 256, 5504).reshape(b_, t_, PROJ_TOTAL)
        zcx = _matmul(hc.reshape(-1, d), w_in_b, 256, 5504).reshape(b_, -1, PROJ_TOTAL)
        z = jnp.split(zl, split_at, axis=-1)
        zc = jnp.split(zcx, split_at, axis=-1)
        ret_l, ret_c = _retention(z[0:4], zc[0:4], ret_decay[l], ret_gn_g[l], ret_gn_b[l], row, col)
        hg_l, hg_c = _hgrn2(z[4:9], zc[4:9], lower[:, l], hg_norm_g[l])
        na_l, na_c = _neighbourhood_attn(z[9], z[10], z[11], zc[9], zc[10], zc[11], na_rpb[l])
        gq_l, gq_c = _gqa(z[12], z[13], z[14], zc[12], zc[13], zc[14], gq_qn_g[l], gq_kn_g[l], row, col)
        y = _merge_branches((ret_l, hg_l, na_l, gq_l), z[15], w_branch[l], w_out[l])
        x = _layernorm(DN_ALPHA * x + g1 * y, ln_g[l, 0], ln_b[l, 0])
        h2 = (x * (1.0 + sc2) + sh2).reshape(b_ * t_, d)
        if last:
            y2 = _moe(h2, w_router[l], b_router[l], w_gu[l], b_gu[l], w_down[l], b_down[l]).reshape(b_, t_, d)
        else:
            yc = _merge_branches((ret_c, hg_c, na_c, gq_c), zc[15], w_branch[l], w_out[l])
            xc = _layernorm(DN_ALPHA * xc + cg1 * yc, ln_g[l, 0], ln_b[l, 0])
            h2c = (xc * (1.0 + csc2) + csh2).reshape(-1, d)
            y_all = _moe(jnp.concatenate([h2, h2c], axis=0), w_router[l], b_router[l], w_gu[l], b_gu[l],
                         w_down[l], b_down[l])
            y2 = y_all[: b_ * t_].reshape(b_, t_, d)
            y2c = y_all[b_ * t_:].reshape(xc.shape)
            xc = _layernorm(DN_ALPHA * xc + cg2 * y2c, ln_g[l, 1], ln_b[l, 1])
        x = _layernorm(DN_ALPHA * x + g2 * y2, ln_g[l, 1], ln_b[l, 1])
    return x
```

```python
import functools

import jax
import jax.numpy as jnp
from jax import lax
import numpy as np
from jax.experimental import pallas as pl
from jax.experimental.pallas import tpu as pltpu

D_MODEL = 1024
DEPTH = 4
GRID_W = 64
N_BRANCH = 4
BRANCH_W = D_MODEL // 2
RET_HEADS = 4
RET_DK = BRANCH_W // RET_HEADS
HG_HEADS = 4
HG_DK = BRANCH_W // HG_HEADS
NA_HEADS = 8
NA_HD = BRANCH_W // NA_HEADS
NA_KH = 8
NA_KW = 16
GQA_HEADS = 8
GQA_KV_HEADS = 2
GQA_HD = BRANCH_W // GQA_HEADS
GQA_KV_W = GQA_KV_HEADS * GQA_HD
ROPE_BASE = 10000.0
N_EXPERTS = 32
TOP_K = 4
D_FF_EXPERT = D_MODEL
SWIGLU_LIMIT = 7.0
SWIGLU_ALPHA = 1.702
MOE_BLOCK = 256
LN_EPS = 1e-5
NORM_EPS = 1e-6
NEG_INF = -1e30
DN_ALPHA = (2 * DEPTH) ** 0.25

LANES = 128
BF16 = jnp.bfloat16
F32 = jnp.float32
VMEM_LIMIT = 56 * 1024 * 1024

PROJ_ORIG = 4 * BRANCH_W + 5 * BRANCH_W + 3 * BRANCH_W + BRANCH_W + 2 * GQA_KV_W
PROJ_TOTAL = PROJ_ORIG + N_BRANCH * D_MODEL
PROJ_PAD = 11264
GATE_B = 0
RET_B = (N_BRANCH * D_MODEL) // LANES
HG_B = RET_B + 16
NA_B = HG_B + 20
GQ_B = NA_B + 12

RET_CHUNK = 256
HG_CHUNK = 128
HG_LEVELS = (64, 32, 16, 8, 4, 2, 1, 0)


def _cparams(sem):
    return pltpu.CompilerParams(dimension_semantics=sem, vmem_limit_bytes=VMEM_LIMIT)


def _dot(a, b):
    return jnp.dot(a, b, preferred_element_type=F32)


def _dot_nt(a, b):
    return lax.dot_general(a, b, (((1,), (1,)), ((), ())), preferred_element_type=F32)


def _dot_tn(a, b):
    return lax.dot_general(a, b, (((0,), (0,)), ((), ())), preferred_element_type=F32)


def _split_bf16(x):
    hi = x.astype(BF16)
    lo = (x - hi.astype(F32)).astype(BF16)
    return hi, lo


def _sigmoid(x):
    return 1.0 / (1.0 + jnp.exp(-x))


def _silu(x):
    return x * _sigmoid(x)


def _mod_kernel(c_ref, w_ref, b_ref, o_ref):
    s = _silu(c_ref[...]).astype(BF16)
    o_ref[0] = _dot(s, w_ref[0].astype(BF16)) + b_ref[0]


def _mod_all(cc, w_mod, b_mod):
    depth, d, n = w_mod.shape
    r = cc.shape[0]
    tn = 1536
    return pl.pallas_call(
        _mod_kernel,
        out_shape=jax.ShapeDtypeStruct((depth, r, n), F32),
        grid=(depth, n // tn),
        in_specs=[pl.BlockSpec((r, d), lambda l, j: (0, 0)),
                  pl.BlockSpec((1, d, tn), lambda l, j: (l, 0, j)),
                  pl.BlockSpec((1, 1, tn), lambda l, j: (l, 0, j))],
        out_specs=pl.BlockSpec((1, r, tn), lambda l, j: (l, 0, j)),
        compiler_params=_cparams(("arbitrary", "arbitrary")),
        name="adaln_mod",
    )(cc, w_mod, b_mod.reshape(depth, 1, n))


def _in_proj_kernel(x_ref, mod_ref, w_ref, o_ref, *, n_ctx, tm):
    i = pl.program_id(2)
    x = x_ref[0]
    mc = mod_ref[0, 0]
    ml = mod_ref[0, 1]
    rows = i * tm + lax.broadcasted_iota(jnp.int32, (tm, 1), 0)
    is_ctx = rows < n_ctx
    scale = jnp.where(is_ctx, mc[0:1, :], ml[0:1, :])
    shift = jnp.where(is_ctx, mc[1:2, :], ml[1:2, :])
    h = (x * scale + shift).astype(BF16)
    o_ref[0] = _dot(h, w_ref[...]).astype(o_ref.dtype)


def _in_proj(xs, mod, w, n_ctx, tm, tn):
    b, n, d = xs.shape
    ncol = w.shape[1]
    return pl.pallas_call(
        functools.partial(_in_proj_kernel, n_ctx=n_ctx, tm=tm),
        out_shape=jax.ShapeDtypeStruct((b, n, ncol), BF16),
        grid=(ncol // tn, b, n // tm),
        in_specs=[pl.BlockSpec((1, tm, d), lambda j, bb, i: (bb, i, 0)),
                  pl.BlockSpec((1, 2, 8, d), lambda j, bb, i: (bb, 0, 0, 0)),
                  pl.BlockSpec((d, tn), lambda j, bb, i: (0, j))],
        out_specs=pl.BlockSpec((1, tm, tn), lambda j, bb, i: (bb, i, j)),
        compiler_params=_cparams(("arbitrary", "arbitrary", "arbitrary")),
        name="in_proj",
    )(xs, mod, w)


def _rope_tables(n_ctx, t, head_dim):
    idx = jnp.arange(t, dtype=jnp.int32)
    row = (idx // GRID_W).astype(F32)
    col = (idx % GRID_W).astype(F32)
    n = head_dim // 2
    inv = ROPE_BASE ** (-jnp.arange(0, n, 2, dtype=F32) / n)

    def half(pos):
        ang = pos[:, None] * inv[None, :]
        c, s = jnp.cos(ang), jnp.sin(ang)
        return jnp.concatenate([c, c], -1), jnp.concatenate([-s, s], -1)

    cr, sr = half(row)
    cc, sc = half(col)
    cos = jnp.concatenate([cr, cc], -1)
    sin = jnp.concatenate([sr, sc], -1)
    reps = LANES // head_dim
    cos = jnp.tile(cos, (1, reps))
    sin = jnp.tile(sin, (1, reps))
    cos = jnp.concatenate([jnp.ones((n_ctx, LANES), F32), cos], 0)
    sin = jnp.concatenate([jnp.zeros((n_ctx, LANES), F32), sin], 0)
    return cos, sin


def _rope(x, cos, sin, quarter):
    lane = lax.broadcasted_iota(jnp.int32, x.shape, 1)
    first = (lane & (2 * quarter - 1)) < quarter
    rot = jnp.where(first, pltpu.roll(x, LANES - quarter, 1), pltpu.roll(x, quarter, 1))
    return x * cos + rot * sin


def _ret_kernel(lg_ref, q_ref, k_ref, v_ref, g_ref, cos_ref, sin_ref, gng_ref, gnb_ref, o_ref,
                ks_ref, sb_ref, *, n_ctx, chunk):
    hh = pl.program_id(1)
    lgf = lg_ref[hh]
    lgb = lg_ref[RET_HEADS + hh]
    n = q_ref.shape[1]
    c_ = chunk
    nch = n // c_
    n_cc = n_ctx // c_
    dk = q_ref.shape[2]

    ri = lax.broadcasted_iota(jnp.int32, (c_, dk), 0).astype(F32)
    df = jnp.exp((ri + 1.0) * lgf)
    db = jnp.exp((c_ - ri) * lgb)
    ef = jnp.exp((c_ - 1.0 - ri) * lgf)
    eb = jnp.exp(ri * lgb)
    gfc = jnp.exp(c_ * lgf)
    gbc = jnp.exp(c_ * lgb)
    ii = lax.broadcasted_iota(jnp.int32, (c_, c_), 0)
    jj = lax.broadcasted_iota(jnp.int32, (c_, c_), 1)
    dij = (ii - jj).astype(F32)
    w = jnp.where(ii >= jj, jnp.exp(dij * lgf), jnp.exp(-dij * lgb))

    def rope_k(c, carry):
        r0 = pl.multiple_of(c * c_, c_)
        kc = k_ref[0, pl.ds(r0, c_), :].astype(F32) * (RET_DK ** -0.5)
        kc = _rope(kc, cos_ref[pl.ds(r0, c_), :], sin_ref[pl.ds(r0, c_), :], 32)
        ks_ref[pl.ds(r0, c_), :] = kc.astype(BF16)
        return carry

    lax.fori_loop(0, nch, rope_k, 0)

    def bwd_step(c, s):
        r0 = pl.multiple_of(c * c_, c_)
        sb_ref[c] = s
        kc = (ks_ref[pl.ds(r0, c_), :].astype(F32) * eb).astype(BF16)
        return s * gbc + _dot_tn(kc, v_ref[0, pl.ds(r0, c_), :])

    s = jnp.zeros((dk, dk), F32)
    s = lax.fori_loop(0, n_cc, lambda t, s_: bwd_step(n_cc - 1 - t, s_), s)
    lax.fori_loop(0, nch - n_cc, lambda t, s_: bwd_step(nch - 1 - t, s_), s)

    gng = gng_ref[...]
    gnb = gnb_ref[...]

    def fwd_step(c, sf):
        r0 = pl.multiple_of(c * c_, c_)
        qc = _rope(q_ref[0, pl.ds(r0, c_), :].astype(F32), cos_ref[pl.ds(r0, c_), :],
                   sin_ref[pl.ds(r0, c_), :], 32)
        kc = ks_ref[pl.ds(r0, c_), :]
        vc = v_ref[0, pl.ds(r0, c_), :]
        a = _dot_nt(qc.astype(BF16), kc) * w
        qi = jnp.concatenate([qc * df, qc * db], axis=1).astype(BF16)
        si = jnp.concatenate([sf, sb_ref[c]], axis=0).astype(BF16)
        o = _dot(a.astype(BF16), vc) + _dot(qi, si)
        sf_new = sf * gfc + _dot_tn((kc.astype(F32) * ef).astype(BF16), vc)
        mu = jnp.mean(o, axis=-1, keepdims=True)
        oc = o - mu
        var = jnp.mean(oc * oc, axis=-1, keepdims=True)
        y = oc * lax.rsqrt(var + LN_EPS) * gng + gnb
        gate = g_ref[0, pl.ds(r0, c_), :].astype(F32)
        o_ref[0, pl.ds(r0, c_), :] = (y * _silu(gate)).astype(o_ref.dtype)
        return sf_new

    lax.fori_loop(0, nch, fwd_step, jnp.zeros((dk, dk), F32))


def _retention(z, log_gamma, cos, sin, gn_g, gn_b, n_ctx):
    b, n, _ = z.shape
    blk = lambda off: pl.BlockSpec((1, n, LANES), lambda bb, h, lg: (bb, 0, off + h))
    grid_spec = pltpu.PrefetchScalarGridSpec(
        num_scalar_prefetch=1,
        grid=(b, RET_HEADS),
        in_specs=[blk(RET_B), blk(RET_B + 4), blk(RET_B + 8), blk(RET_B + 12),
                  pl.BlockSpec((n, LANES), lambda bb, h, lg: (0, 0)),
                  pl.BlockSpec((n, LANES), lambda bb, h, lg: (0, 0)),
                  pl.BlockSpec((1, LANES), lambda bb, h, lg: (0, h)),
                  pl.BlockSpec((1, LANES), lambda bb, h, lg: (0, h))],
        out_specs=pl.BlockSpec((1, n, LANES), lambda bb, h, lg: (bb, 0, h)),
        scratch_shapes=[pltpu.VMEM((n, LANES), BF16),
                        pltpu.VMEM((n // RET_CHUNK, RET_DK, RET_DK), F32)],
    )
    return pl.pallas_call(
        functools.partial(_ret_kernel, n_ctx=n_ctx, chunk=RET_CHUNK),
        out_shape=jax.ShapeDtypeStruct((b, n, BRANCH_W), BF16),
        grid_spec=grid_spec,
        compiler_params=_cparams(("arbitrary", "arbitrary")),
        name="retention",
    )(log_gamma, z, z, z, z, cos, sin, gn_g.reshape(1, -1), gn_b.reshape(1, -1))


def _hg_constants():
    c = HG_CHUNK
    r = np.arange(c)[:, None]
    t = np.arange(c)[None, :]
    mats = []
    level = np.full((2, c, c), len(HG_LEVELS), np.int32)
    for d in range(2):
        blocks = []
        for li, s in enumerate(HG_LEVELS):
            if s == 0:
                level[d][np.arange(c), np.arange(c)] = li
                continue
            base = (r // (2 * s)) * (2 * s)
            if d == 0:
                m = base + s - 1
                mat = (t > np.minimum(r, m)) & (t <= np.maximum(r, m))
                q_side = (r % (2 * s)) >= s
            else:
                m = base + s
                mat = (t >= np.minimum(r, m)) & (t < np.maximum(r, m))
                q_side = (r % (2 * s)) < s
            blocks.append(mat.astype(np.float32))
            same = (r // (2 * s)) == (t // (2 * s))
            k_side_t = ((t % (2 * s)) < s) if d == 0 else ((t % (2 * s)) >= s)
            level[d][same & q_side & k_side_t] = li
        if d == 0:
            blocks.append((t <= r).astype(np.float32))
            blocks.append((t > r).astype(np.float32))
        else:
            blocks.append((t >= r).astype(np.float32))
            blocks.append((t < r).astype(np.float32))
        mats.append(np.concatenate(blocks, 0))
    return np.stack(mats), level


def _hg_gates(zf, lb):
    en = jnp.exp(-jnp.abs(zf))
    inv = 1.0 / (1.0 + en)
    pos = zf >= 0
    sg = jnp.where(pos, inv, en * inv)
    sgn = jnp.where(pos, en * inv, inv)
    one_m = 1.0 - lb
    return jnp.log(lb + one_m * sg), one_m * sgn


def _hg_kernel(q_ref, ff_ref, fb_ref, v_ref, g_ref, lb_ref, ng_ref, m_ref, lv_ref, o_ref,
               lfb_ref, kb_ref, sb_ref, *, n_ctx):
    c_ = HG_CHUNK
    n = q_ref.shape[1]
    nch = n // c_
    n_cc = n_ctx // c_
    dk = q_ref.shape[2]
    nl = len(HG_LEVELS)
    lbf = lb_ref[0:1, :]
    lbb = lb_ref[1:2, :]

    def mdot(mat, lf):
        hi, lo = _split_bf16(lf)
        return _dot(mat, hi) + _dot(mat, lo)

    def prep_b(c, carry):
        r0 = pl.multiple_of(c * c_, c_)
        lf, key = _hg_gates(fb_ref[0, pl.ds(r0, c_), :].astype(F32), lbb)
        lfb_ref[pl.ds(r0, c_), :] = lf
        kb_ref[pl.ds(r0, c_), :] = key
        return carry

    lax.fori_loop(0, nch, prep_b, 0)

    def bwd_step(c, st):
        r0 = pl.multiple_of(c * c_, c_)
        sb_ref[c] = st
        lf = lfb_ref[pl.ds(r0, c_), :]
        e = mdot(m_ref[1, (nl - 1) * c_:(nl + 1) * c_, :], lf)
        tot = e[0:1, :]
        kt = (kb_ref[pl.ds(r0, c_), :] * jnp.exp(e[c_:2 * c_, :])).astype(BF16)
        return st * jnp.exp(tot) + _dot_tn(v_ref[0, pl.ds(r0, c_), :], kt)

    st = jnp.zeros((dk, dk), F32)
    st = lax.fori_loop(0, n_cc, lambda t, s_: bwd_step(n_cc - 1 - t, s_), st)
    lax.fori_loop(0, nch - n_cc, lambda t, s_: bwd_step(nch - 1 - t, s_), st)

    ng = ng_ref[...]

    def intra(qs, key, lf, d):
        e = mdot(m_ref[d], lf)
        a = jnp.zeros((c_, c_), F32)
        lv = lv_ref[d]
        for li, s in enumerate(HG_LEVELS):
            if s == 0:
                qt, kt = qs, key
            else:
                dec = jnp.exp(e[li * c_:(li + 1) * c_, :])
                qt, kt = qs * dec, key * dec
            a = a + jnp.where(lv == li, _dot_nt(qt.astype(BF16), kt.astype(BF16)), 0.0)
        cum = e[(nl - 1) * c_:nl * c_, :]
        rest = e[nl * c_:(nl + 1) * c_, :]
        return a, cum, rest

    def fwd_step(c, sf):
        r0 = pl.multiple_of(c * c_, c_)
        qs = _silu(q_ref[0, pl.ds(r0, c_), :].astype(F32))
        vc = v_ref[0, pl.ds(r0, c_), :]
        lff, keyf = _hg_gates(ff_ref[0, pl.ds(r0, c_), :].astype(F32), lbf)
        af, cumf, restf = intra(qs, keyf, lff, 0)
        ab, cumb, _ = intra(qs, kb_ref[pl.ds(r0, c_), :], lfb_ref[pl.ds(r0, c_), :], 1)
        qi = jnp.concatenate([qs * jnp.exp(cumf), qs * jnp.exp(cumb)], axis=1).astype(BF16)
        si = jnp.concatenate([sf, sb_ref[c]], axis=1).astype(BF16)
        o = _dot((af + ab).astype(BF16), vc) + _dot_nt(qi, si)
        totf = cumf[c_ - 1:c_, :]
        sf_new = sf * jnp.exp(totf) + _dot_tn(vc, (keyf * jnp.exp(restf)).astype(BF16))
        on = o * lax.rsqrt(jnp.mean(o * o, axis=-1, keepdims=True) + NORM_EPS) * ng
        gate = g_ref[0, pl.ds(r0, c_), :].astype(F32)
        o_ref[0, pl.ds(r0, c_), :] = (on * _silu(gate)).astype(o_ref.dtype)
        return sf_new

    lax.fori_loop(0, nch, fwd_step, jnp.zeros((dk, dk), F32))


def _hgrn2(z, lower, norm_g, n_ctx):
    b, n, _ = z.shape
    mats, level = _hg_constants()
    mats = jnp.asarray(mats, BF16)
    level = jnp.asarray(level)
    blk = lambda off: pl.BlockSpec((1, n, LANES), lambda bb, h: (bb, 0, off + h))
    return pl.pallas_call(
        functools.partial(_hg_kernel, n_ctx=n_ctx),
        out_shape=jax.ShapeDtypeStruct((b, n, BRANCH_W), BF16),
        grid=(b, HG_HEADS),
        in_specs=[blk(HG_B), blk(HG_B + 4), blk(HG_B + 8), blk(HG_B + 12), blk(HG_B + 16),
                  pl.BlockSpec((2, LANES), lambda bb, h: (0, h)),
                  pl.BlockSpec((1, LANES), lambda bb, h: (0, h)),
                  pl.BlockSpec(mats.shape, lambda bb, h: (0, 0, 0)),
                  pl.BlockSpec(level.shape, lambda bb, h: (0, 0, 0))],
        out_specs=pl.BlockSpec((1, n, LANES), lambda bb, h: (bb, 0, h)),
        scratch_shapes=[pltpu.VMEM((n, LANES), F32), pltpu.VMEM((n, LANES), F32),
                        pltpu.VMEM((n // HG_CHUNK, HG_DK, HG_DK), F32)],
        compiler_params=_cparams(("arbitrary", "arbitrary")),
        name="hgrn2",
    )(z, z, z, z, z, lower, norm_g.reshape(1, -1), mats, level)


def _na_bias_table(rpb):
    qc = np.arange(GRID_W)[:, None]
    kc = np.arange(GRID_W)[None, :]
    col0 = np.clip(qc - NA_KW // 2, 0, GRID_W - NA_KW)
    valid = (kc >= col0) & (kc < col0 + NA_KW)
    dc = np.clip(kc - qc + (NA_KW - 1), 0, 2 * NA_KW - 2)
    dr = np.arange(NA_KH)[:, None] + np.arange(NA_KH)[None, :]
    t = rpb[:, dr][:, :, :, dc]
    t = jnp.where(jnp.asarray(valid)[None, None, None], t.astype(F32), NEG_INF)
    return t.transpose(0, 1, 3, 2, 4).reshape(rpb.shape[0], NA_KH, GRID_W, NA_KH * GRID_W)


def _softmax_pv(parts):
    m = parts[0][0].max(axis=-1, keepdims=True)
    for s, _ in parts[1:]:
        m = jnp.maximum(m, s.max(axis=-1, keepdims=True))
    l = None
    o = None
    for s, v in parts:
        p = jnp.exp(s - m)
        ls = p.sum(axis=-1, keepdims=True)
        os_ = _dot(p.astype(BF16), v)
        l = ls if l is None else l + ls
        o = os_ if o is None else o + os_
    return o / l


def _na_kernel(q_ref, k_ref, v_ref, tab_ref, o_ref, *, n_ctx):
    n = q_ref.shape[1]
    rows = (n - n_ctx) // GRID_W
    wlen = NA_KH * GRID_W
    scale = NA_HD ** -0.5
    kx = k_ref[0, 0:n_ctx, :]
    vx = v_ref[0, 0:n_ctx, :]

    def head_mask(shape, hh):
        lane = lax.broadcasted_iota(jnp.int32, shape, 1)
        return (lane < NA_HD) if hh == 0 else (lane >= NA_HD)

    qx = q_ref[0, 0:n_ctx, :].astype(F32) * scale
    outs = []
    for hh in range(2):
        qm = jnp.where(head_mask(qx.shape, hh), qx, 0.0).astype(BF16)
        outs.append(_softmax_pv([(_dot_nt(qm, kx), vx)]))
    o_ref[0, 0:n_ctx, :] = jnp.where(head_mask(outs[0].shape, 0), outs[0], outs[1]).astype(o_ref.dtype)

    def row_step(r, carry):
        start = jnp.clip(r - NA_KH // 2, 0, rows - NA_KH)
        delta = start - r + (NA_KH - 1)
        q0 = pl.multiple_of(n_ctx + r * GRID_W, GRID_W)
        k0 = pl.multiple_of(n_ctx + start * GRID_W, GRID_W)
        qr = q_ref[0, pl.ds(q0, GRID_W), :].astype(F32) * scale
        kl = k_ref[0, pl.ds(k0, wlen), :]
        vl = v_ref[0, pl.ds(k0, wlen), :]
        outs = []
        for hh in range(2):
            qm = jnp.where(head_mask(qr.shape, hh), qr, 0.0).astype(BF16)
            s_loc = _dot_nt(qm, kl) + tab_ref[hh, delta]
            s_ctx = _dot_nt(qm, kx)
            outs.append(_softmax_pv([(s_loc, vl), (s_ctx, vx)]))
        o_ref[0, pl.ds(q0, GRID_W), :] = jnp.where(head_mask(outs[0].shape, 0), outs[0], outs[1]).astype(o_ref.dtype)
        return carry

    lax.fori_loop(0, rows, row_step, 0)


def _neighbourhood(z, table, n_ctx):
    b, n, _ = z.shape
    npair = NA_HEADS // 2
    blk = lambda off: pl.BlockSpec((1, n, LANES), lambda bb, p: (bb, 0, off + p))
    return pl.pallas_call(
        functools.partial(_na_kernel, n_ctx=n_ctx),
        out_shape=jax.ShapeDtypeStruct((b, n, BRANCH_W), BF16),
        grid=(b, npair),
        in_specs=[blk(NA_B), blk(NA_B + 4), blk(NA_B + 8),
                  pl.BlockSpec((2, NA_KH, GRID_W, NA_KH * GRID_W), lambda bb, p: (p, 0, 0, 0))],
        out_specs=pl.BlockSpec((1, n, LANES), lambda bb, p: (bb, 0, p)),
        compiler_params=_cparams(("arbitrary", "arbitrary")),
        name="neighbourhood_attn",
    )(z, z, z, table)


def _gqa_prep_kernel(q_ref, k_ref, v_ref, cos_ref, sin_ref, qg_ref, kg_ref, bd_ref,
                     qo_ref, ko_ref, vo_ref):
    cos = cos_ref[...]
    sin = sin_ref[...]
    bd = bd_ref[...]

    def norm_rope(x, g):
        hi, lo = _split_bf16(x * x)
        ms = _dot(hi, bd) + _dot(lo, bd)
        xn = x * lax.rsqrt(ms + NORM_EPS) * g
        return _rope(xn, cos, sin, GQA_HD // 4)

    qg = qg_ref[...]
    for j in range(BRANCH_W // LANES):
        x = q_ref[0, :, j * LANES:(j + 1) * LANES].astype(F32)
        qo_ref[0, :, j * LANES:(j + 1) * LANES] = (norm_rope(x, qg) * (GQA_HD ** -0.5)).astype(qo_ref.dtype)
    kn = norm_rope(k_ref[0].astype(F32), kg_ref[...])
    vv = v_ref[0].astype(F32)
    lane = lax.broadcasted_iota(jnp.int32, kn.shape, 1)
    lo_half = lane < GQA_HD
    for a, dst in ((kn, ko_ref), (vv, vo_ref)):
        sw = pltpu.roll(a, GQA_HD, 1)
        dst[0, 0] = jnp.where(lo_half, a, sw).astype(dst.dtype)
        dst[0, 1] = jnp.where(lo_half, sw, a).astype(dst.dtype)


def _gqa_prep(z, cos, sin, qn_g, kn_g, tr):
    b, n, _ = z.shape
    bd = np.kron(np.eye(LANES // GQA_HD), np.ones((GQA_HD, GQA_HD))) / GQA_HD
    qg = jnp.tile(qn_g.reshape(1, -1), (1, LANES // GQA_HD))
    kg = jnp.tile(kn_g.reshape(1, -1), (1, LANES // GQA_HD))
    qb = (GQ_B * LANES) // BRANCH_W
    return pl.pallas_call(
        _gqa_prep_kernel,
        out_shape=(jax.ShapeDtypeStruct((b, n, BRANCH_W), BF16),
                   jax.ShapeDtypeStruct((b, GQA_KV_HEADS, n, LANES), BF16),
                   jax.ShapeDtypeStruct((b, GQA_KV_HEADS, n, LANES), BF16)),
        grid=(b, n // tr),
        in_specs=[pl.BlockSpec((1, tr, BRANCH_W), lambda bb, i: (bb, i, qb)),
                  pl.BlockSpec((1, tr, LANES), lambda bb, i: (bb, i, GQ_B + 4)),
                  pl.BlockSpec((1, tr, LANES), lambda bb, i: (bb, i, GQ_B + 5)),
                  pl.BlockSpec((tr, LANES), lambda bb, i: (i, 0)),
                  pl.BlockSpec((tr, LANES), lambda bb, i: (i, 0)),
                  pl.BlockSpec((1, LANES), lambda bb, i: (0, 0)),
                  pl.BlockSpec((1, LANES), lambda bb, i: (0, 0)),
                  pl.BlockSpec((LANES, LANES), lambda bb, i: (0, 0))],
        out_specs=(pl.BlockSpec((1, tr, BRANCH_W), lambda bb, i: (bb, i, 0)),
                   pl.BlockSpec((1, GQA_KV_HEADS, tr, LANES), lambda bb, i: (bb, 0, i, 0)),
                   pl.BlockSpec((1, GQA_KV_HEADS, tr, LANES), lambda bb, i: (bb, 0, i, 0))),
        compiler_params=_cparams(("arbitrary", "arbitrary")),
        name="gqa_prep",
    )(z, z, z, cos, sin, qg, kg, jnp.asarray(bd, BF16))


def _gqa_kernel(q_ref, k_ref, v_ref, o_ref, *, n_ctx):
    i = pl.program_id(1)
    tq = q_ref.shape[1]
    grp = GQA_HEADS // GQA_KV_HEADS

    def attend(nk):
        for j in range(BRANCH_W // LANES):
            qb = q_ref[0, :, j * LANES:(j + 1) * LANES]
            lane = lax.broadcasted_iota(jnp.int32, qb.shape, 1)
            outs = []
            for hh in range(2):
                g = (2 * j + hh) // grp
                msk = (lane < GQA_HD) if hh == 0 else (lane >= GQA_HD)
                qm = jnp.where(msk, qb, jnp.zeros_like(qb))
                outs.append(_softmax_pv([(_dot_nt(qm, k_ref[0, g, 0:nk, :]), v_ref[0, g, 0:nk, :])]))
            o_ref[0, :, j * LANES:(j + 1) * LANES] = jnp.where(lane < GQA_HD, outs[0], outs[1]).astype(o_ref.dtype)

    @pl.when(i * tq < n_ctx)
    def _():
        attend(n_ctx)

    @pl.when(i * tq >= n_ctx)
    def _():
        attend(k_ref.shape[2])


def _gqa_attn(qh, kd, vd, n_ctx, tq):
    b, n, _ = qh.shape
    return pl.pallas_call(
        functools.partial(_gqa_kernel, n_ctx=n_ctx),
        out_shape=jax.ShapeDtypeStruct((b, n, BRANCH_W), BF16),
        grid=(b, n // tq),
        in_specs=[pl.BlockSpec((1, tq, BRANCH_W), lambda bb, i: (bb, i, 0)),
                  pl.BlockSpec((1, GQA_KV_HEADS, n, LANES), lambda bb, i: (bb, 0, 0, 0)),
                  pl.BlockSpec((1, GQA_KV_HEADS, n, LANES), lambda bb, i: (bb, 0, 0, 0))],
        out_specs=pl.BlockSpec((1, tq, BRANCH_W), lambda bb, i: (bb, i, 0)),
        compiler_params=_cparams(("arbitrary", "arbitrary")),
        name="gqa_attn",
    )(qh, kd, vd)


def _layernorm_rows(v, g, b):
    mu = jnp.mean(v, axis=-1, keepdims=True)
    vc = v - mu
    var = jnp.mean(vc * vc, axis=-1, keepdims=True)
    return vc * lax.rsqrt(var + LN_EPS) * g + b


def _merge_kernel(z_ref, r_ref, h_ref, a_ref, q_ref, x_ref, mod_ref, wb_ref, wo_ref, lng_ref, lnb_ref,
                  wrh_ref, wrl_ref, br_ref, xo_ref, h2_ref, te_ref, tg_ref, *, n_ctx, tm):
    i = pl.program_id(1)
    is_ctx = i * tm < n_ctx
    mod = jnp.where(is_ctx, mod_ref[0, 0], mod_ref[0, 1])
    acc = None
    for nb, br in enumerate((r_ref, h_ref, a_ref, q_ref)):
        yb = _dot(br[0], wb_ref[nb])
        gt = _sigmoid(z_ref[0, :, nb * D_MODEL:(nb + 1) * D_MODEL].astype(F32))
        acc = gt * yb if acc is None else acc + gt * yb
    y = _dot(acc.astype(BF16), wo_ref[...])
    xn = _layernorm_rows(DN_ALPHA * x_ref[0] + mod[2:3, :] * y, lng_ref[...], lnb_ref[...])
    xo_ref[0] = xn
    h2 = xn * mod[3:4, :] + mod[4:5, :]
    h2_hi, h2_lo = _split_bf16(h2)
    h2_ref[0] = h2_hi
    wrh = wrh_ref[...]
    logit = _dot_nt(wrh, h2_hi) + _dot_nt(wrh, h2_lo) + _dot_nt(wrl_ref[...], h2_hi) + br_ref[...]
    eidx = lax.broadcasted_iota(jnp.int32, logit.shape, 0)
    vals, idxs = [], []
    for _ in range(TOP_K):
        m = logit.max(axis=0, keepdims=True)
        sel = jnp.where(logit == m, eidx, N_EXPERTS).min(axis=0, keepdims=True)
        vals.append(m)
        idxs.append(sel)
        logit = jnp.where(eidx == sel, -jnp.inf, logit)
    ex = [jnp.exp(v - vals[0]) for v in vals]
    tot = ex[0] + ex[1] + ex[2] + ex[3]
    te_ref[0] = jnp.concatenate(idxs, axis=0)
    tg_ref[0] = jnp.concatenate([e / tot for e in ex], axis=0)


def _merge(z, outs, xs, mod, wb, wo, ln_g, ln_b, wr_hi, wr_lo, b_r, n_ctx, tm):
    b, n, d = xs.shape
    row = lambda w: pl.BlockSpec((1, tm, w), lambda bb, i: (bb, i, 0))
    full = lambda a: pl.BlockSpec(a.shape, lambda bb, i: (0,) * a.ndim)
    ln_g = ln_g.reshape(1, d)
    ln_b = ln_b.reshape(1, d)
    b_r = b_r.reshape(N_EXPERTS, 1)
    return pl.pallas_call(
        functools.partial(_merge_kernel, n_ctx=n_ctx, tm=tm),
        out_shape=(jax.ShapeDtypeStruct((b, n, d), F32),
                   jax.ShapeDtypeStruct((b, n, d), BF16),
                   jax.ShapeDtypeStruct((b, TOP_K, n), jnp.int32),
                   jax.ShapeDtypeStruct((b, TOP_K, n), F32)),
        grid=(b, n // tm),
        in_specs=[row(N_BRANCH * D_MODEL), row(BRANCH_W), row(BRANCH_W), row(BRANCH_W), row(BRANCH_W),
                  row(d),
                  pl.BlockSpec((1, 2, 8, d), lambda bb, i: (bb, 0, 0, 0)),
                  full(wb), full(wo), full(ln_g), full(ln_b), full(wr_hi), full(wr_lo), full(b_r)],
        out_specs=(row(d), row(d),
                   pl.BlockSpec((1, TOP_K, tm), lambda bb, i: (bb, 0, i)),
                   pl.BlockSpec((1, TOP_K, tm), lambda bb, i: (bb, 0, i))),
        compiler_params=_cparams(("arbitrary", "arbitrary")),
        name="merge_ln_router",
    )(z, *outs, xs, mod, wb, wo, ln_g, ln_b, wr_hi, wr_lo, b_r)


def _ffn_kernel(be_ref, nu_ref, x_ref, wgu_ref, bgu_ref, wd_ref, bd_ref, o_ref):
    i = pl.program_id(0)

    @pl.when(i < nu_ref[0])
    def _():
        gu = _dot(x_ref[...], wgu_ref[0]) + bgu_ref[0]
        g = jnp.minimum(gu[:, :D_FF_EXPERT], SWIGLU_LIMIT)
        u = jnp.clip(gu[:, D_FF_EXPERT:], -SWIGLU_LIMIT, SWIGLU_LIMIT)
        act = g * _sigmoid(SWIGLU_ALPHA * g) * (u + 1.0)
        o_ref[...] = (_dot(act.astype(BF16), wd_ref[0]) + bd_ref[0]).astype(o_ref.dtype)

    @pl.when(i >= nu_ref[0])
    def _():
        o_ref[...] = jnp.zeros_like(o_ref)


def _expert_ffn(xb, block_e, n_used, wgu, bgu, wd, bd):
    n_slots, d = xb.shape
    n_blocks = n_slots // MOE_BLOCK
    ne, _, f2 = wgu.shape
    grid_spec = pltpu.PrefetchScalarGridSpec(
        num_scalar_prefetch=2,
        grid=(n_blocks,),
        in_specs=[pl.BlockSpec((MOE_BLOCK, d), lambda i, be, nu: (i, 0)),
                  pl.BlockSpec((1, d, f2), lambda i, be, nu: (be[i], 0, 0)),
                  pl.BlockSpec((1, 1, f2), lambda i, be, nu: (be[i], 0, 0)),
                  pl.BlockSpec((1, f2 // 2, d), lambda i, be, nu: (be[i], 0, 0)),
                  pl.BlockSpec((1, 1, d), lambda i, be, nu: (be[i], 0, 0))],
        out_specs=pl.BlockSpec((MOE_BLOCK, d), lambda i, be, nu: (i, 0)),
    )
    return pl.pallas_call(
        _ffn_kernel,
        out_shape=jax.ShapeDtypeStruct((n_slots, d), BF16),
        grid_spec=grid_spec,
        compiler_params=_cparams(("arbitrary",)),
        name="expert_ffn",
    )(block_e, n_used, xb, wgu, bgu.reshape(ne, 1, f2), wd, bd.reshape(ne, 1, d))


def _combine_kernel(y_ref, g_ref, x_ref, mod_ref, lng_ref, lnb_ref, xo_ref, *, n_ctx, tm):
    i = pl.program_id(1)
    is_ctx = i * tm < n_ctx
    g2 = jnp.where(is_ctx, mod_ref[0, 0, 5:6, :], mod_ref[0, 1, 5:6, :])
    gate = g_ref[...]
    y = None
    for k in range(TOP_K):
        t = gate[:, k:k + 1] * y_ref[k].astype(F32)
        y = t if y is None else y + t
    xo_ref[0] = _layernorm_rows(DN_ALPHA * x_ref[0] + g2 * y, lng_ref[...], lnb_ref[...])


def _combine(yk, gate, xs, mod, ln_g, ln_b, n_ctx, tm):
    b, n, d = xs.shape
    nt = n // tm
    return pl.pallas_call(
        functools.partial(_combine_kernel, n_ctx=n_ctx, tm=tm),
        out_shape=jax.ShapeDtypeStruct((b, n, d), F32),
        grid=(b, nt),
        in_specs=[pl.BlockSpec((TOP_K, tm, d), lambda bb, i: (0, bb * nt + i, 0)),
                  pl.BlockSpec((tm, TOP_K), lambda bb, i: (bb * nt + i, 0)),
                  pl.BlockSpec((1, tm, d), lambda bb, i: (bb, i, 0)),
                  pl.BlockSpec((1, 2, 8, d), lambda bb, i: (bb, 0, 0, 0)),
                  pl.BlockSpec((1, d), lambda bb, i: (0, 0)),
                  pl.BlockSpec((1, d), lambda bb, i: (0, 0))],
        out_specs=pl.BlockSpec((1, tm, d), lambda bb, i: (bb, i, 0)),
        compiler_params=_cparams(("arbitrary", "arbitrary")),
        name="moe_combine_ln",
    )(yk, gate, xs, mod, ln_g.reshape(1, d), ln_b.reshape(1, d))


def _moe(h2, top_e, top_g, wgu, bgu, wd, bd):
    b, n, d = h2.shape
    n_tok = b * n
    nk = n_tok * TOP_K
    flat_e = top_e.transpose(0, 2, 1).reshape(nk)
    gate = top_g.transpose(0, 2, 1).reshape(n_tok, TOP_K)
    order = jnp.argsort(flat_e)
    e_s = flat_e[order]
    counts = jnp.sum(flat_e[:, None] == jnp.arange(N_EXPERTS, dtype=jnp.int32)[None, :], axis=0, dtype=jnp.int32)
    padded = (counts + MOE_BLOCK - 1) // MOE_BLOCK * MOE_BLOCK
    start_s = jnp.cumsum(counts) - counts
    ends_p = jnp.cumsum(padded)
    start_p = ends_p - padded
    dest_s = start_p[e_s] + (jnp.arange(nk, dtype=jnp.int32) - start_s[e_s])
    n_blocks = (nk + N_EXPERTS * (MOE_BLOCK - 1) + MOE_BLOCK - 1) // MOE_BLOCK
    n_slots = n_blocks * MOE_BLOCK
    slot_tok = jnp.full((n_slots,), n_tok, jnp.int32).at[dest_s].set(order // TOP_K)
    dest = jnp.zeros((nk,), jnp.int32).at[order].set(dest_s)
    block_start = jnp.arange(n_blocks, dtype=jnp.int32) * MOE_BLOCK
    block_e = jnp.minimum(jnp.sum(ends_p[None, :] <= block_start[:, None], axis=1), N_EXPERTS - 1).astype(jnp.int32)
    n_used = (ends_p[-1] // MOE_BLOCK).astype(jnp.int32).reshape(1)
    h_pad = jnp.concatenate([h2.reshape(n_tok, d), jnp.zeros((1, d), h2.dtype)], axis=0)
    xb = h_pad[slot_tok]
    yb = _expert_ffn(xb, block_e, n_used, wgu, bgu, wd, bd)
    yk = yb[dest.reshape(n_tok, TOP_K).T]
    return yk, gate


def kernel(x, c, ctx, c_ctx, w_mod, b_mod, w_in, ret_decay, ret_gn_g, ret_gn_b, hg_lb, hg_norm_g,
           na_rpb, gq_qn_g, gq_kn_g, w_branch, w_out, ln_g, ln_b, w_router, b_router, w_gu, b_gu,
           w_down, b_down):
    b_, t_, d = x.shape
    n_ctx = ctx.shape[1]
    depth = w_mod.shape[0]
    tm = 256

    sm = jax.nn.softmax(hg_lb.astype(F32), axis=1)
    lower = jnp.cumsum(sm, axis=1) - sm[:, :1]
    log_gamma = jax.nn.log_sigmoid(ret_decay.astype(F32)).reshape(depth, 2 * RET_HEADS)

    cc = jnp.concatenate([c, c_ctx[None, :], jnp.zeros((16 - b_ - 1, d), F32)], axis=0)
    modv = _mod_all(cc, w_mod, b_mod).reshape(depth, 16, 6, d)
    one = jnp.asarray([0.0, 1.0, 0.0, 0.0, 1.0, 0.0], F32)[None, None, :, None]
    modv = (modv + one)[:, :, jnp.asarray([1, 0, 2, 4, 3, 5])]
    modv = jnp.concatenate([modv, jnp.zeros((depth, 16, 2, d), F32)], axis=2)
    mod = jnp.stack([jnp.broadcast_to(modv[:, b_:b_ + 1], (depth, b_, 8, d)), modv[:, :b_]], axis=2)

    perm = np.concatenate([np.arange(PROJ_ORIG, PROJ_TOTAL), np.arange(PROJ_ORIG)])
    cos_r, sin_r = _rope_tables(n_ctx, t_, RET_DK)
    cos_g, sin_g = _rope_tables(n_ctx, t_, GQA_HD)

    xs = jnp.concatenate([ctx, x], axis=1)
    tm_in = 768 if xs.shape[1] % 768 == 0 else tm
    for l in range(depth):
        w_in_l = jnp.pad(w_in[l][:, perm], ((0, 0), (0, PROJ_PAD - PROJ_TOTAL))).astype(BF16)
        z = _in_proj(xs, mod[l], w_in_l, n_ctx, tm_in, PROJ_PAD // 4)
        ret_o = _retention(z, log_gamma[l], cos_r, sin_r, ret_gn_g[l], ret_gn_b[l], n_ctx)
        hg_o = _hgrn2(z, lower[:, l], hg_norm_g[l], n_ctx)
        na_o = _neighbourhood(z, _na_bias_table(na_rpb[l]), n_ctx)
        qh, kd, vd = _gqa_prep(z, cos_g, sin_g, gq_qn_g[l], gq_kn_g[l], tm)
        gq_o = _gqa_attn(qh, kd, vd, n_ctx, tm)
        wr_hi, wr_lo = _split_bf16(w_router[l].T)
        xs, h2, top_e, top_g = _merge(z, (ret_o, hg_o, na_o, gq_o), xs, mod[l],
                                      w_branch[l].astype(BF16), w_out[l].astype(BF16),
                                      ln_g[l, 0], ln_b[l, 0], wr_hi, wr_lo, b_router[l], n_ctx, tm)
        yk, gate = _moe(h2, top_e, top_g, w_gu[l].astype(BF16), b_gu[l], w_down[l].astype(BF16), b_down[l])
        xs = _combine(yk, gate, xs, mod[l], ln_g[l, 1], ln_b[l, 1], n_ctx, tm)
    return xs[:, n_ctx:]
```

```python
import functools

import jax
import jax.numpy as jnp
from jax import lax
import numpy as np
from jax.experimental import pallas as pl
from jax.experimental.pallas import tpu as pltpu
from jax.experimental.pallas import tpu_sc as plsc

D_MODEL = 1024
DEPTH = 4
GRID_W = 64
N_BRANCH = 4
BRANCH_W = D_MODEL // 2
RET_HEADS = 4
RET_DK = BRANCH_W // RET_HEADS
HG_HEADS = 4
HG_DK = BRANCH_W // HG_HEADS
NA_HEADS = 8
NA_HD = BRANCH_W // NA_HEADS
NA_KH = 8
NA_KW = 16
GQA_HEADS = 8
GQA_KV_HEADS = 2
GQA_HD = BRANCH_W // GQA_HEADS
GQA_KV_W = GQA_KV_HEADS * GQA_HD
ROPE_BASE = 10000.0
N_EXPERTS = 32
TOP_K = 4
D_FF_EXPERT = D_MODEL
SWIGLU_LIMIT = 7.0
SWIGLU_ALPHA = 1.702
MOE_BLOCK = 256
LN_EPS = 1e-5
NORM_EPS = 1e-6
NEG_INF = -1e30
DN_ALPHA = (2 * DEPTH) ** 0.25

LANES = 128
BF16 = jnp.bfloat16
F32 = jnp.float32
VMEM_LIMIT = 56 * 1024 * 1024
SC_CORES = 2
SC_SUBCORES = 16
SC_WINDOW = 128
SC_ROW_WORDS = 256

PROJ_ORIG = 4 * BRANCH_W + 5 * BRANCH_W + 3 * BRANCH_W + BRANCH_W + 2 * GQA_KV_W
PROJ_TOTAL = PROJ_ORIG + N_BRANCH * D_MODEL
PROJ_PAD = 11264
GATE_B = 0
RET_B = (N_BRANCH * D_MODEL) // LANES
HG_B = RET_B + 16
NA_B = HG_B + 20
GQ_B = NA_B + 12

RET_CHUNK = 256
HG_CHUNK = 128
HG_LEVELS = (64, 32, 16, 8, 4, 2, 1, 0)


def _cparams(sem):
    return pltpu.CompilerParams(dimension_semantics=sem, vmem_limit_bytes=VMEM_LIMIT)


def _dot(a, b):
    return jnp.dot(a, b, preferred_element_type=F32)


def _dot_nt(a, b):
    return lax.dot_general(a, b, (((1,), (1,)), ((), ())), preferred_element_type=F32)


def _dot_tn(a, b):
    return lax.dot_general(a, b, (((0,), (0,)), ((), ())), preferred_element_type=F32)


def _split_bf16(x):
    hi = x.astype(BF16)
    lo = (x - hi.astype(F32)).astype(BF16)
    return hi, lo


def _sigmoid(x):
    return 1.0 / (1.0 + jnp.exp(-x))


def _silu(x):
    return x * _sigmoid(x)


def _mod_kernel(c_ref, w_ref, b_ref, o_ref):
    s = _silu(c_ref[...]).astype(BF16)
    o_ref[0] = _dot(s, w_ref[0].astype(BF16)) + b_ref[0]


def _mod_all(cc, w_mod, b_mod):
    depth, d, n = w_mod.shape
    r = cc.shape[0]
    tn = 1536
    return pl.pallas_call(
        _mod_kernel,
        out_shape=jax.ShapeDtypeStruct((depth, r, n), F32),
        grid=(depth, n // tn),
        in_specs=[pl.BlockSpec((r, d), lambda l, j: (0, 0)),
                  pl.BlockSpec((1, d, tn), lambda l, j: (l, 0, j)),
                  pl.BlockSpec((1, 1, tn), lambda l, j: (l, 0, j))],
        out_specs=pl.BlockSpec((1, r, tn), lambda l, j: (l, 0, j)),
        compiler_params=_cparams(("arbitrary", "arbitrary")),
        name="adaln_mod",
    )(cc, w_mod, b_mod.reshape(depth, 1, n))


def _in_proj_kernel(x_ref, mod_ref, w_ref, o_ref, *, n_ctx, tm):
    i = pl.program_id(2)
    x = x_ref[0]
    mc = mod_ref[0, 0]
    ml = mod_ref[0, 1]
    rows = i * tm + lax.broadcasted_iota(jnp.int32, (tm, 1), 0)
    is_ctx = rows < n_ctx
    scale = jnp.where(is_ctx, mc[0:1, :], ml[0:1, :])
    shift = jnp.where(is_ctx, mc[1:2, :], ml[1:2, :])
    h = (x * scale + shift).astype(BF16)
    o_ref[0] = _dot(h, w_ref[...]).astype(o_ref.dtype)


def _in_proj(xs, mod, w, n_ctx, tm, tn):
    b, n, d = xs.shape
    ncol = w.shape[1]
    return pl.pallas_call(
        functools.partial(_in_proj_kernel, n_ctx=n_ctx, tm=tm),
        out_shape=jax.ShapeDtypeStruct((b, n, ncol), BF16),
        grid=(ncol // tn, b, n // tm),
        in_specs=[pl.BlockSpec((1, tm, d), lambda j, bb, i: (bb, i, 0)),
                  pl.BlockSpec((1, 2, 8, d), lambda j, bb, i: (bb, 0, 0, 0)),
                  pl.BlockSpec((d, tn), lambda j, bb, i: (0, j))],
        out_specs=pl.BlockSpec((1, tm, tn), lambda j, bb, i: (bb, i, j)),
        compiler_params=_cparams(("arbitrary", "arbitrary", "arbitrary")),
        name="in_proj",
    )(xs, mod, w)


def _rope_tables(n_ctx, t, head_dim):
    idx = jnp.arange(t, dtype=jnp.int32)
    row = (idx // GRID_W).astype(F32)
    col = (idx % GRID_W).astype(F32)
    n = head_dim // 2
    inv = ROPE_BASE ** (-jnp.arange(0, n, 2, dtype=F32) / n)

    def half(pos):
        ang = pos[:, None] * inv[None, :]
        c, s = jnp.cos(ang), jnp.sin(ang)
        return jnp.concatenate([c, c], -1), jnp.concatenate([-s, s], -1)

    cr, sr = half(row)
    cc, sc = half(col)
    cos = jnp.concatenate([cr, cc], -1)
    sin = jnp.concatenate([sr, sc], -1)
    reps = LANES // head_dim
    cos = jnp.tile(cos, (1, reps))
    sin = jnp.tile(sin, (1, reps))
    cos = jnp.concatenate([jnp.ones((n_ctx, LANES), F32), cos], 0)
    sin = jnp.concatenate([jnp.zeros((n_ctx, LANES), F32), sin], 0)
    return cos, sin


def _rope(x, cos, sin, quarter):
    lane = lax.broadcasted_iota(jnp.int32, x.shape, 1)
    first = (lane & (2 * quarter - 1)) < quarter
    rot = jnp.where(first, pltpu.roll(x, LANES - quarter, 1), pltpu.roll(x, quarter, 1))
    return x * cos + rot * sin


def _ret_kernel(lg_ref, q_ref, k_ref, v_ref, g_ref, cos_ref, sin_ref, gng_ref, gnb_ref, o_ref,
                ks_ref, sb_ref, *, n_ctx, chunk):
    hh = pl.program_id(1)
    lgf = lg_ref[hh]
    lgb = lg_ref[RET_HEADS + hh]
    n = q_ref.shape[1]
    c_ = chunk
    nch = n // c_
    n_cc = n_ctx // c_
    dk = q_ref.shape[2]

    ri = lax.broadcasted_iota(jnp.int32, (c_, dk), 0).astype(F32)
    df = jnp.exp((ri + 1.0) * lgf)
    db = jnp.exp((c_ - ri) * lgb)
    ef = jnp.exp((c_ - 1.0 - ri) * lgf)
    eb = jnp.exp(ri * lgb)
    gfc = jnp.exp(c_ * lgf)
    gbc = jnp.exp(c_ * lgb)
    ii = lax.broadcasted_iota(jnp.int32, (c_, c_), 0)
    jj = lax.broadcasted_iota(jnp.int32, (c_, c_), 1)
    dij = (ii - jj).astype(F32)
    w = jnp.where(ii >= jj, jnp.exp(dij * lgf), jnp.exp(-dij * lgb))

    def rope_k(c, carry):
        r0 = pl.multiple_of(c * c_, c_)
        kc = k_ref[0, pl.ds(r0, c_), :].astype(F32) * (RET_DK ** -0.5)
        kc = _rope(kc, cos_ref[pl.ds(r0, c_), :], sin_ref[pl.ds(r0, c_), :], 32)
        ks_ref[pl.ds(r0, c_), :] = kc.astype(BF16)
        return carry

    lax.fori_loop(0, nch, rope_k, 0)

    def bwd_step(c, s):
        r0 = pl.multiple_of(c * c_, c_)
        sb_ref[c] = s
        kc = (ks_ref[pl.ds(r0, c_), :].astype(F32) * eb).astype(BF16)
        return s * gbc + _dot_tn(kc, v_ref[0, pl.ds(r0, c_), :])

    s = jnp.zeros((dk, dk), F32)
    s = lax.fori_loop(0, n_cc, lambda t, s_: bwd_step(n_cc - 1 - t, s_), s)
    lax.fori_loop(0, nch - n_cc, lambda t, s_: bwd_step(nch - 1 - t, s_), s)

    gng = gng_ref[...]
    gnb = gnb_ref[...]

    def fwd_step(c, sf):
        r0 = pl.multiple_of(c * c_, c_)
        qc = _rope(q_ref[0, pl.ds(r0, c_), :].astype(F32), cos_ref[pl.ds(r0, c_), :],
                   sin_ref[pl.ds(r0, c_), :], 32)
        kc = ks_ref[pl.ds(r0, c_), :]
        vc = v_ref[0, pl.ds(r0, c_), :]
        a = _dot_nt(qc.astype(BF16), kc) * w
        qi = jnp.concatenate([qc * df, qc * db], axis=1).astype(BF16)
        si = jnp.concatenate([sf, sb_ref[c]], axis=0).astype(BF16)
        o = _dot(a.astype(BF16), vc) + _dot(qi, si)
        sf_new = sf * gfc + _dot_tn((kc.astype(F32) * ef).astype(BF16), vc)
        mu = jnp.mean(o, axis=-1, keepdims=True)
        oc = o - mu
        var = jnp.mean(oc * oc, axis=-1, keepdims=True)
        y = oc * lax.rsqrt(var + LN_EPS) * gng + gnb
        gate = g_ref[0, pl.ds(r0, c_), :].astype(F32)
        o_ref[0, pl.ds(r0, c_), :] = (y * _silu(gate)).astype(o_ref.dtype)
        return sf_new

    lax.fori_loop(0, nch, fwd_step, jnp.zeros((dk, dk), F32))


def _retention(z, log_gamma, cos, sin, gn_g, gn_b, n_ctx):
    b, n, _ = z.shape
    blk = lambda off: pl.BlockSpec((1, n, LANES), lambda bb, h, lg: (bb, 0, off + h))
    grid_spec = pltpu.PrefetchScalarGridSpec(
        num_scalar_prefetch=1,
        grid=(b, RET_HEADS),
        in_specs=[blk(RET_B), blk(RET_B + 4), blk(RET_B + 8), blk(RET_B + 12),
                  pl.BlockSpec((n, LANES), lambda bb, h, lg: (0, 0)),
                  pl.BlockSpec((n, LANES), lambda bb, h, lg: (0, 0)),
                  pl.BlockSpec((1, LANES), lambda bb, h, lg: (0, h)),
                  pl.BlockSpec((1, LANES), lambda bb, h, lg: (0, h))],
        out_specs=pl.BlockSpec((1, n, LANES), lambda bb, h, lg: (bb, 0, h)),
        scratch_shapes=[pltpu.VMEM((n, LANES), BF16),
                        pltpu.VMEM((n // RET_CHUNK, RET_DK, RET_DK), F32)],
    )
    return pl.pallas_call(
        functools.partial(_ret_kernel, n_ctx=n_ctx, chunk=RET_CHUNK),
        out_shape=jax.ShapeDtypeStruct((b, n, BRANCH_W), BF16),
        grid_spec=grid_spec,
        compiler_params=_cparams(("arbitrary", "arbitrary")),
        name="retention",
    )(log_gamma, z, z, z, z, cos, sin, gn_g.reshape(1, -1), gn_b.reshape(1, -1))


def _hg_constants():
    c = HG_CHUNK
    r = np.arange(c)[:, None]
    t = np.arange(c)[None, :]
    mats = []
    level = np.full((2, c, c), len(HG_LEVELS), np.int32)
    for d in range(2):
        blocks = []
        for li, s in enumerate(HG_LEVELS):
            if s == 0:
                level[d][np.arange(c), np.arange(c)] = li
                continue
            base = (r // (2 * s)) * (2 * s)
            if d == 0:
                m = base + s - 1
                mat = (t > np.minimum(r, m)) & (t <= np.maximum(r, m))
                q_side = (r % (2 * s)) >= s
            else:
                m = base + s
                mat = (t >= np.minimum(r, m)) & (t < np.maximum(r, m))
                q_side = (r % (2 * s)) < s
            blocks.append(mat.astype(np.float32))
            same = (r // (2 * s)) == (t // (2 * s))
            k_side_t = ((t % (2 * s)) < s) if d == 0 else ((t % (2 * s)) >= s)
            level[d][same & q_side & k_side_t] = li
        if d == 0:
            blocks.append((t <= r).astype(np.float32))
            blocks.append((t > r).astype(np.float32))
        else:
            blocks.append((t >= r).astype(np.float32))
            blocks.append((t < r).astype(np.float32))
        mats.append(np.concatenate(blocks, 0))
    return np.stack(mats), level


def _hg_gates(zf, lb):
    en = jnp.exp(-jnp.abs(zf))
    inv = 1.0 / (1.0 + en)
    pos = zf >= 0
    sg = jnp.where(pos, inv, en * inv)
    sgn = jnp.where(pos, en * inv, inv)
    one_m = 1.0 - lb
    return jnp.log(lb + one_m * sg), one_m * sgn


def _hg_kernel(q_ref, ff_ref, fb_ref, v_ref, g_ref, lb_ref, ng_ref, m_ref, lv_ref, o_ref,
               lfb_ref, kb_ref, sb_ref, *, n_ctx):
    c_ = HG_CHUNK
    n = q_ref.shape[1]
    nch = n // c_
    n_cc = n_ctx // c_
    dk = q_ref.shape[2]
    nl = len(HG_LEVELS)
    lbf = lb_ref[0:1, :]
    lbb = lb_ref[1:2, :]

    def mdot(mat, lf):
        hi, lo = _split_bf16(lf)
        return _dot(mat, hi) + _dot(mat, lo)

    def prep_b(c, carry):
        r0 = pl.multiple_of(c * c_, c_)
        lf, key = _hg_gates(fb_ref[0, pl.ds(r0, c_), :].astype(F32), lbb)
        lfb_ref[pl.ds(r0, c_), :] = lf
        kb_ref[pl.ds(r0, c_), :] = key
        return carry

    lax.fori_loop(0, nch, prep_b, 0)

    def bwd_step(c, st):
        r0 = pl.multiple_of(c * c_, c_)
        sb_ref[c] = st
        lf = lfb_ref[pl.ds(r0, c_), :]
        e = mdot(m_ref[1, (nl - 1) * c_:(nl + 1) * c_, :], lf)
        tot = e[0:1, :]
        kt = (kb_ref[pl.ds(r0, c_), :] * jnp.exp(e[c_:2 * c_, :])).astype(BF16)
        return st * jnp.exp(tot) + _dot_tn(v_ref[0, pl.ds(r0, c_), :], kt)

    st = jnp.zeros((dk, dk), F32)
    st = lax.fori_loop(0, n_cc, lambda t, s_: bwd_step(n_cc - 1 - t, s_), st)
    lax.fori_loop(0, nch - n_cc, lambda t, s_: bwd_step(nch - 1 - t, s_), st)

    ng = ng_ref[...]

    def intra(qs, key, lf, d):
        e = mdot(m_ref[d], lf)
        a = jnp.zeros((c_, c_), F32)
        lv = lv_ref[d]
        for li, s in enumerate(HG_LEVELS):
            if s == 0:
                qt, kt = qs, key
            else:
                dec = jnp.exp(e[li * c_:(li + 1) * c_, :])
                qt, kt = qs * dec, key * dec
            a = a + jnp.where(lv == li, _dot_nt(qt.astype(BF16), kt.astype(BF16)), 0.0)
        cum = e[(nl - 1) * c_:nl * c_, :]
        rest = e[nl * c_:(nl + 1) * c_, :]
        return a, cum, rest

    def fwd_step(c, sf):
        r0 = pl.multiple_of(c * c_, c_)
        qs = _silu(q_ref[0, pl.ds(r0, c_), :].astype(F32))
        vc = v_ref[0, pl.ds(r0, c_), :]
        lff, keyf = _hg_gates(ff_ref[0, pl.ds(r0, c_), :].astype(F32), lbf)
        af, cumf, restf = intra(qs, keyf, lff, 0)
        ab, cumb, _ = intra(qs, kb_ref[pl.ds(r0, c_), :], lfb_ref[pl.ds(r0, c_), :], 1)
        qi = jnp.concatenate([qs * jnp.exp(cumf), qs * jnp.exp(cumb)], axis=1).astype(BF16)
        si = jnp.concatenate([sf, sb_ref[c]], axis=1).astype(BF16)
        o = _dot((af + ab).astype(BF16), vc) + _dot_nt(qi, si)
        totf = cumf[c_ - 1:c_, :]
        sf_new = sf * jnp.exp(totf) + _dot_tn(vc, (keyf * jnp.exp(restf)).astype(BF16))
        on = o * lax.rsqrt(jnp.mean(o * o, axis=-1, keepdims=True) + NORM_EPS) * ng
        gate = g_ref[0, pl.ds(r0, c_), :].astype(F32)
        o_ref[0, pl.ds(r0, c_), :] = (on * _silu(gate)).astype(o_ref.dtype)
        return sf_new

    lax.fori_loop(0, nch, fwd_step, jnp.zeros((dk, dk), F32), unroll=2)


def _hgrn2(z, lower, norm_g, n_ctx):
    b, n, _ = z.shape
    mats, level = _hg_constants()
    mats = jnp.asarray(mats, BF16)
    level = jnp.asarray(level)
    blk = lambda off: pl.BlockSpec((1, n, LANES), lambda bb, h: (bb, 0, off + h))
    return pl.pallas_call(
        functools.partial(_hg_kernel, n_ctx=n_ctx),
        out_shape=jax.ShapeDtypeStruct((b, n, BRANCH_W), BF16),
        grid=(b, HG_HEADS),
        in_specs=[blk(HG_B), blk(HG_B + 4), blk(HG_B + 8), blk(HG_B + 12), blk(HG_B + 16),
                  pl.BlockSpec((2, LANES), lambda bb, h: (0, h)),
                  pl.BlockSpec((1, LANES), lambda bb, h: (0, h)),
                  pl.BlockSpec(mats.shape, lambda bb, h: (0, 0, 0)),
                  pl.BlockSpec(level.shape, lambda bb, h: (0, 0, 0))],
        out_specs=pl.BlockSpec((1, n, LANES), lambda bb, h: (bb, 0, h)),
        scratch_shapes=[pltpu.VMEM((n, LANES), F32), pltpu.VMEM((n, LANES), F32),
                        pltpu.VMEM((n // HG_CHUNK, HG_DK, HG_DK), F32)],
        compiler_params=_cparams(("arbitrary", "arbitrary")),
        name="hgrn2",
    )(z, z, z, z, z, lower, norm_g.reshape(1, -1), mats, level)


def _na_bias_table(rpb):
    qc = np.arange(GRID_W)[:, None]
    kc = np.arange(GRID_W)[None, :]
    col0 = np.clip(qc - NA_KW // 2, 0, GRID_W - NA_KW)
    valid = (kc >= col0) & (kc < col0 + NA_KW)
    dc = np.clip(kc - qc + (NA_KW - 1), 0, 2 * NA_KW - 2)
    dr = np.arange(NA_KH)[:, None] + np.arange(NA_KH)[None, :]
    t = rpb[:, dr][:, :, :, dc]
    t = jnp.where(jnp.asarray(valid)[None, None, None], t.astype(F32), NEG_INF)
    return t.transpose(0, 1, 3, 2, 4).reshape(rpb.shape[0], NA_KH, GRID_W, NA_KH * GRID_W)


def _softmax_pv(parts):
    m = parts[0][0].max(axis=-1, keepdims=True)
    for s, _ in parts[1:]:
        m = jnp.maximum(m, s.max(axis=-1, keepdims=True))
    l = None
    o = None
    for s, v in parts:
        p = jnp.exp(s - m)
        ls = p.sum(axis=-1, keepdims=True)
        os_ = _dot(p.astype(BF16), v)
        l = ls if l is None else l + ls
        o = os_ if o is None else o + os_
    return o / l


def _na_kernel(q_ref, k_ref, v_ref, tab_ref, o_ref, *, n_ctx):
    n = q_ref.shape[1]
    rows = (n - n_ctx) // GRID_W
    wlen = NA_KH * GRID_W
    scale = NA_HD ** -0.5
    kx = k_ref[0, 0:n_ctx, :]
    vx = v_ref[0, 0:n_ctx, :]

    def head_mask(shape, hh):
        lane = lax.broadcasted_iota(jnp.int32, shape, 1)
        return (lane < NA_HD) if hh == 0 else (lane >= NA_HD)

    qx = q_ref[0, 0:n_ctx, :].astype(F32) * scale
    outs = []
    for hh in range(2):
        qm = jnp.where(head_mask(qx.shape, hh), qx, 0.0).astype(BF16)
        outs.append(_softmax_pv([(_dot_nt(qm, kx), vx)]))
    o_ref[0, 0:n_ctx, :] = jnp.where(head_mask(outs[0].shape, 0), outs[0], outs[1]).astype(o_ref.dtype)

    def row_step(r, carry):
        start = jnp.clip(r - NA_KH // 2, 0, rows - NA_KH)
        delta = start - r + (NA_KH - 1)
        q0 = pl.multiple_of(n_ctx + r * GRID_W, GRID_W)
        k0 = pl.multiple_of(n_ctx + start * GRID_W, GRID_W)
        qr = q_ref[0, pl.ds(q0, GRID_W), :].astype(F32) * scale
        kl = k_ref[0, pl.ds(k0, wlen), :]
        vl = v_ref[0, pl.ds(k0, wlen), :]
        outs = []
        for hh in range(2):
            qm = jnp.where(head_mask(qr.shape, hh), qr, 0.0).astype(BF16)
            s_loc = _dot_nt(qm, kl) + tab_ref[hh, delta]
            s_ctx = _dot_nt(qm, kx)
            outs.append(_softmax_pv([(s_loc, vl), (s_ctx, vx)]))
        o_ref[0, pl.ds(q0, GRID_W), :] = jnp.where(head_mask(outs[0].shape, 0), outs[0], outs[1]).astype(o_ref.dtype)
        return carry

    lax.fori_loop(0, rows, row_step, 0, unroll=2)


def _neighbourhood(z, table, n_ctx):
    b, n, _ = z.shape
    npair = NA_HEADS // 2
    blk = lambda off: pl.BlockSpec((1, n, LANES), lambda bb, p: (bb, 0, off + p))
    return pl.pallas_call(
        functools.partial(_na_kernel, n_ctx=n_ctx),
        out_shape=jax.ShapeDtypeStruct((b, n, BRANCH_W), BF16),
        grid=(b, npair),
        in_specs=[blk(NA_B), blk(NA_B + 4), blk(NA_B + 8),
                  pl.BlockSpec((2, NA_KH, GRID_W, NA_KH * GRID_W), lambda bb, p: (p, 0, 0, 0))],
        out_specs=pl.BlockSpec((1, n, LANES), lambda bb, p: (bb, 0, p)),
        compiler_params=_cparams(("arbitrary", "arbitrary")),
        name="neighbourhood_attn",
    )(z, z, z, table)


def _gqa_prep_kernel(q_ref, k_ref, v_ref, cos_ref, sin_ref, qg_ref, kg_ref, bd_ref,
                     qo_ref, ko_ref, vo_ref):
    cos = cos_ref[...]
    sin = sin_ref[...]
    bd = bd_ref[...]

    def norm_rope(x, g):
        hi, lo = _split_bf16(x * x)
        ms = _dot(hi, bd) + _dot(lo, bd)
        xn = x * lax.rsqrt(ms + NORM_EPS) * g
        return _rope(xn, cos, sin, GQA_HD // 4)

    qg = qg_ref[...]
    for j in range(BRANCH_W // LANES):
        x = q_ref[0, :, j * LANES:(j + 1) * LANES].astype(F32)
        qo_ref[0, :, j * LANES:(j + 1) * LANES] = (norm_rope(x, qg) * (GQA_HD ** -0.5)).astype(qo_ref.dtype)
    kn = norm_rope(k_ref[0].astype(F32), kg_ref[...])
    vv = v_ref[0].astype(F32)
    lane = lax.broadcasted_iota(jnp.int32, kn.shape, 1)
    lo_half = lane < GQA_HD
    for a, dst in ((kn, ko_ref), (vv, vo_ref)):
        sw = pltpu.roll(a, GQA_HD, 1)
        dst[0, 0] = jnp.where(lo_half, a, sw).astype(dst.dtype)
        dst[0, 1] = jnp.where(lo_half, sw, a).astype(dst.dtype)


def _gqa_prep(z, cos, sin, qn_g, kn_g, tr):
    b, n, _ = z.shape
    bd = np.kron(np.eye(LANES // GQA_HD), np.ones((GQA_HD, GQA_HD))) / GQA_HD
    qg = jnp.tile(qn_g.reshape(1, -1), (1, LANES // GQA_HD))
    kg = jnp.tile(kn_g.reshape(1, -1), (1, LANES // GQA_HD))
    qb = (GQ_B * LANES) // BRANCH_W
    return pl.pallas_call(
        _gqa_prep_kernel,
        out_shape=(jax.ShapeDtypeStruct((b, n, BRANCH_W), BF16),
                   jax.ShapeDtypeStruct((b, GQA_KV_HEADS, n, LANES), BF16),
                   jax.ShapeDtypeStruct((b, GQA_KV_HEADS, n, LANES), BF16)),
        grid=(b, n // tr),
        in_specs=[pl.BlockSpec((1, tr, BRANCH_W), lambda bb, i: (bb, i, qb)),
                  pl.BlockSpec((1, tr, LANES), lambda bb, i: (bb, i, GQ_B + 4)),
                  pl.BlockSpec((1, tr, LANES), lambda bb, i: (bb, i, GQ_B + 5)),
                  pl.BlockSpec((tr, LANES), lambda bb, i: (i, 0)),
                  pl.BlockSpec((tr, LANES), lambda bb, i: (i, 0)),
                  pl.BlockSpec((1, LANES), lambda bb, i: (0, 0)),
                  pl.BlockSpec((1, LANES), lambda bb, i: (0, 0)),
                  pl.BlockSpec((LANES, LANES), lambda bb, i: (0, 0))],
        out_specs=(pl.BlockSpec((1, tr, BRANCH_W), lambda bb, i: (bb, i, 0)),
                   pl.BlockSpec((1, GQA_KV_HEADS, tr, LANES), lambda bb, i: (bb, 0, i, 0)),
                   pl.BlockSpec((1, GQA_KV_HEADS, tr, LANES), lambda bb, i: (bb, 0, i, 0))),
        compiler_params=_cparams(("arbitrary", "arbitrary")),
        name="gqa_prep",
    )(z, z, z, cos, sin, qg, kg, jnp.asarray(bd, BF16))


def _gqa_kernel(q_ref, k_ref, v_ref, o_ref, *, n_ctx):
    i = pl.program_id(1)
    tq = q_ref.shape[1]
    grp = GQA_HEADS // GQA_KV_HEADS

    def attend(nk):
        for j in range(BRANCH_W // LANES):
            qb = q_ref[0, :, j * LANES:(j + 1) * LANES]
            lane = lax.broadcasted_iota(jnp.int32, qb.shape, 1)
            outs = []
            for hh in range(2):
                g = (2 * j + hh) // grp
                msk = (lane < GQA_HD) if hh == 0 else (lane >= GQA_HD)
                qm = jnp.where(msk, qb, jnp.zeros_like(qb))
                outs.append(_softmax_pv([(_dot_nt(qm, k_ref[0, g, 0:nk, :]), v_ref[0, g, 0:nk, :])]))
            o_ref[0, :, j * LANES:(j + 1) * LANES] = jnp.where(lane < GQA_HD, outs[0], outs[1]).astype(o_ref.dtype)

    @pl.when(i * tq < n_ctx)
    def _():
        attend(n_ctx)

    @pl.when(i * tq >= n_ctx)
    def _():
        attend(k_ref.shape[2])


def _gqa_attn(qh, kd, vd, n_ctx, tq):
    b, n, _ = qh.shape
    return pl.pallas_call(
        functools.partial(_gqa_kernel, n_ctx=n_ctx),
        out_shape=jax.ShapeDtypeStruct((b, n, BRANCH_W), BF16),
        grid=(b, n // tq),
        in_specs=[pl.BlockSpec((1, tq, BRANCH_W), lambda bb, i: (bb, i, 0)),
                  pl.BlockSpec((1, GQA_KV_HEADS, n, LANES), lambda bb, i: (bb, 0, 0, 0)),
                  pl.BlockSpec((1, GQA_KV_HEADS, n, LANES), lambda bb, i: (bb, 0, 0, 0))],
        out_specs=pl.BlockSpec((1, tq, BRANCH_W), lambda bb, i: (bb, i, 0)),
        compiler_params=_cparams(("arbitrary", "arbitrary")),
        name="gqa_attn",
    )(qh, kd, vd)


def _layernorm_rows(v, g, b):
    mu = jnp.mean(v, axis=-1, keepdims=True)
    vc = v - mu
    var = jnp.mean(vc * vc, axis=-1, keepdims=True)
    return vc * lax.rsqrt(var + LN_EPS) * g + b


def _pack_bf16_pairs(v):
    w = v.shape[1] // 2
    bits = lax.bitcast_convert_type(v.astype(BF16).astype(F32), jnp.int32)
    return bits[:, w:] | lax.shift_right_logical(bits[:, :w], 16)


def _unpack_bf16_pairs(p):
    lo = lax.bitcast_convert_type(lax.shift_left(p, 16), F32)
    hi = lax.bitcast_convert_type(p & jnp.int32(-65536), F32)
    return jnp.concatenate([lo, hi], axis=1)


def _merge_kernel(z_ref, r_ref, h_ref, a_ref, q_ref, x_ref, mod_ref, wb_ref, wo_ref, lng_ref, lnb_ref,
                  wrh_ref, wrl_ref, br_ref, tri_ref, xo_ref, h2_ref, te_ref, tg_ref, rk_ref, cnt_ref,
                  run_ref, *, n_ctx, tm):
    i = pl.program_id(1)

    @pl.when((pl.program_id(0) == 0) & (i == 0))
    def _():
        run_ref[...] = jnp.zeros_like(run_ref)

    is_ctx = i * tm < n_ctx
    mod = jnp.where(is_ctx, mod_ref[0, 0], mod_ref[0, 1])
    acc = None
    for nb, br in enumerate((r_ref, h_ref, a_ref, q_ref)):
        yb = _dot(br[0], wb_ref[nb])
        gt = _sigmoid(z_ref[0, :, nb * D_MODEL:(nb + 1) * D_MODEL].astype(F32))
        acc = gt * yb if acc is None else acc + gt * yb
    y = _dot(acc.astype(BF16), wo_ref[...])
    xn = _layernorm_rows(DN_ALPHA * x_ref[0] + mod[2:3, :] * y, lng_ref[...], lnb_ref[...])
    xo_ref[0] = xn
    h2 = xn * mod[3:4, :] + mod[4:5, :]
    h2_hi, h2_lo = _split_bf16(h2)
    h2_ref[0] = _pack_bf16_pairs(h2)
    wrh = wrh_ref[...]
    logit = _dot_nt(wrh, h2_hi) + _dot_nt(wrh, h2_lo) + _dot_nt(wrl_ref[...], h2_hi) + br_ref[...]
    eidx = lax.broadcasted_iota(jnp.int32, logit.shape, 0)
    vals, idxs = [], []
    for _ in range(TOP_K):
        m = logit.max(axis=0, keepdims=True)
        sel = jnp.where(logit == m, eidx, N_EXPERTS).min(axis=0, keepdims=True)
        vals.append(m)
        idxs.append(sel)
        logit = jnp.where(eidx == sel, -jnp.inf, logit)
    ex = [jnp.exp(v - vals[0]) for v in vals]
    tot = ex[0] + ex[1] + ex[2] + ex[3]
    te_ref[0] = jnp.concatenate(idxs, axis=0)
    tg_ref[0] = jnp.concatenate([e / tot for e in ex], axis=0)
    base = run_ref[:, 0:1]
    tri = tri_ref[...]
    ranks = []
    for sel in idxs:
        hit = eidx == sel
        ones = jnp.where(hit, 1.0, 0.0)
        pre = _dot(ones.astype(BF16), tri)
        ranks.append(jnp.sum(jnp.where(hit, pre + base, 0.0), axis=0, keepdims=True))
        base = base + jnp.sum(ones, axis=1, keepdims=True)
    rk_ref[0] = jnp.concatenate(ranks, axis=0).astype(jnp.int32)
    run_ref[...] = jnp.broadcast_to(base, run_ref.shape)
    cnt_ref[...] = run_ref[...]


def _merge(z, outs, xs, mod, wb, wo, ln_g, ln_b, wr_hi, wr_lo, b_r, n_ctx, tm):
    b, n, d = xs.shape
    row = lambda w: pl.BlockSpec((1, tm, w), lambda bb, i: (bb, i, 0))
    full = lambda a: pl.BlockSpec(a.shape, lambda bb, i: (0,) * a.ndim)
    topk = pl.BlockSpec((1, TOP_K, tm), lambda bb, i: (bb, 0, i))
    ln_g = ln_g.reshape(1, d)
    ln_b = ln_b.reshape(1, d)
    b_r = b_r.reshape(N_EXPERTS, 1)
    tri = jnp.asarray(np.triu(np.ones((tm, tm), np.float32), 1), BF16)
    return pl.pallas_call(
        functools.partial(_merge_kernel, n_ctx=n_ctx, tm=tm),
        out_shape=(jax.ShapeDtypeStruct((b, n, d), F32),
                   jax.ShapeDtypeStruct((b, n, d // 2), jnp.int32),
                   jax.ShapeDtypeStruct((b, TOP_K, n), jnp.int32),
                   jax.ShapeDtypeStruct((b, TOP_K, n), F32),
                   jax.ShapeDtypeStruct((b, TOP_K, n), jnp.int32),
                   jax.ShapeDtypeStruct((N_EXPERTS, LANES), F32)),
        grid=(b, n // tm),
        in_specs=[row(N_BRANCH * D_MODEL), row(BRANCH_W), row(BRANCH_W), row(BRANCH_W), row(BRANCH_W),
                  row(d),
                  pl.BlockSpec((1, 2, 8, d), lambda bb, i: (bb, 0, 0, 0)),
                  full(wb), full(wo), full(ln_g), full(ln_b), full(wr_hi), full(wr_lo), full(b_r), full(tri)],
        out_specs=(row(d), row(d // 2), topk, topk, topk,
                   pl.BlockSpec((N_EXPERTS, LANES), lambda bb, i: (0, 0))),
        scratch_shapes=[pltpu.VMEM((N_EXPERTS, LANES), F32)],
        compiler_params=_cparams(("arbitrary", "arbitrary")),
        name="merge_ln_router",
    )(z, *outs, xs, mod, wb, wo, ln_g, ln_b, wr_hi, wr_lo, b_r, tri)


def _ffn_kernel(be_ref, nv_ref, x_ref, wgu_ref, bgu_ref, wd_ref, bd_ref, o_ref):
    i = pl.program_id(0)
    nv = nv_ref[i]

    @pl.when(nv > 0)
    def _():
        rows = lax.broadcasted_iota(jnp.int32, (MOE_BLOCK, 1), 0)
        x = jnp.where(rows < nv, _unpack_bf16_pairs(x_ref[...]), 0.0).astype(BF16)
        gu = _dot(x, wgu_ref[0]) + bgu_ref[0]
        g = jnp.minimum(gu[:, :D_FF_EXPERT], SWIGLU_LIMIT)
        u = jnp.clip(gu[:, D_FF_EXPERT:], -SWIGLU_LIMIT, SWIGLU_LIMIT)
        act = g * _sigmoid(SWIGLU_ALPHA * g) * (u + 1.0)
        o_ref[...] = _pack_bf16_pairs(_dot(act.astype(BF16), wd_ref[0]) + bd_ref[0])

    @pl.when(nv <= 0)
    def _():
        o_ref[...] = jnp.zeros_like(o_ref)


def _expert_ffn(xb, block_e, n_valid, wgu, bgu, wd, bd):
    n_slots, dh = xb.shape
    n_blocks = n_slots // MOE_BLOCK
    ne, d, f2 = wgu.shape
    grid_spec = pltpu.PrefetchScalarGridSpec(
        num_scalar_prefetch=2,
        grid=(n_blocks,),
        in_specs=[pl.BlockSpec((MOE_BLOCK, dh), lambda i, be, nv: (i, 0)),
                  pl.BlockSpec((1, d, f2), lambda i, be, nv: (be[i], 0, 0)),
                  pl.BlockSpec((1, 1, f2), lambda i, be, nv: (be[i], 0, 0)),
                  pl.BlockSpec((1, f2 // 2, d), lambda i, be, nv: (be[i], 0, 0)),
                  pl.BlockSpec((1, 1, d), lambda i, be, nv: (be[i], 0, 0))],
        out_specs=pl.BlockSpec((MOE_BLOCK, dh), lambda i, be, nv: (i, 0)),
    )
    return pl.pallas_call(
        _ffn_kernel,
        out_shape=jax.ShapeDtypeStruct((n_slots, dh), jnp.int32),
        grid_spec=grid_spec,
        compiler_params=_cparams(("arbitrary",)),
        name="expert_ffn",
    )(block_e, n_valid, xb, wgu, bgu.reshape(ne, 1, f2), wd, bd.reshape(ne, 1, d))


def _combine_kernel(y_ref, g_ref, x_ref, mod_ref, lng_ref, lnb_ref, xo_ref, *, n_ctx, tm):
    i = pl.program_id(1)
    is_ctx = i * tm < n_ctx
    g2 = jnp.where(is_ctx, mod_ref[0, 0, 5:6, :], mod_ref[0, 1, 5:6, :])
    gate = g_ref[...]
    y = None
    for k in range(TOP_K):
        t = gate[:, k:k + 1] * _unpack_bf16_pairs(y_ref[k])
        y = t if y is None else y + t
    xo_ref[0] = _layernorm_rows(DN_ALPHA * x_ref[0] + g2 * y, lng_ref[...], lnb_ref[...])


def _combine(yk, gate, xs, mod, ln_g, ln_b, n_ctx, tm):
    b, n, d = xs.shape
    nt = n // tm
    return pl.pallas_call(
        functools.partial(_combine_kernel, n_ctx=n_ctx, tm=tm),
        out_shape=jax.ShapeDtypeStruct((b, n, d), F32),
        grid=(b, nt),
        in_specs=[pl.BlockSpec((TOP_K, tm, d // 2), lambda bb, i: (0, bb * nt + i, 0)),
                  pl.BlockSpec((tm, TOP_K), lambda bb, i: (bb * nt + i, 0)),
                  pl.BlockSpec((1, tm, d), lambda bb, i: (bb, i, 0)),
                  pl.BlockSpec((1, 2, 8, d), lambda bb, i: (bb, 0, 0, 0)),
                  pl.BlockSpec((1, d), lambda bb, i: (0, 0)),
                  pl.BlockSpec((1, d), lambda bb, i: (0, 0))],
        out_specs=pl.BlockSpec((1, tm, d), lambda bb, i: (bb, i, 0)),
        compiler_params=_cparams(("arbitrary", "arbitrary")),
        name="moe_combine_ln",
    )(yk, gate, xs, mod, ln_g.reshape(1, d), ln_b.reshape(1, d))


def _sc_mesh():
    return plsc.VectorSubcoreMesh(core_axis_name="core", subcore_axis_name="subcore",
                                  num_cores=SC_CORES, num_subcores=SC_SUBCORES)


def _sc_split_rows(a, idx):
    s = a.shape[1] // SC_ROW_WORDS
    a2 = a.reshape(a.shape[0] * s, SC_ROW_WORDS)
    idx2 = (idx[:, :, None] * s + jnp.arange(s, dtype=jnp.int32)).reshape(idx.shape[0], idx.shape[1] * s)
    return a2, idx2, s


def _sc_scatter_rows(x, dest, n_out):
    w = x.shape[1]
    x2, dest2, s = _sc_split_rows(x, dest)
    kk, n2 = dest2.shape

    @functools.partial(pl.kernel, out_type=jax.ShapeDtypeStruct((n_out * s, SC_ROW_WORDS), x.dtype),
                       mesh=_sc_mesh(), scratch_types=[], name="moe_dispatch_scatter")
    def scatter(x_hbm, i_hbm, o_hbm):
        def body(x_vmem, i_vmem):
            for j in range(kk):
                pltpu.sync_copy(x_vmem, o_hbm.at[i_vmem.at[j]])

        pltpu.emit_pipeline(
            body, grid=(n2 // SC_WINDOW,),
            in_specs=[pl.BlockSpec((SC_WINDOW, SC_ROW_WORDS), lambda i: (i, 0)),
                      pl.BlockSpec((kk, SC_WINDOW), lambda i: (0, i))],
            out_specs=[], core_axis_name=("core", "subcore"),
            dimension_semantics=(pltpu.PARALLEL,))(x_hbm, i_hbm)

    return scatter(x2, dest2).reshape(n_out, w)


def _sc_gather_rows(table, idx):
    w = table.shape[1]
    kk, n = idx.shape
    t2, idx2, s = _sc_split_rows(table, idx)
    m = kk * n * s

    @functools.partial(pl.kernel, out_type=jax.ShapeDtypeStruct((m, SC_ROW_WORDS), table.dtype),
                       mesh=_sc_mesh(), scratch_types=[], name="moe_combine_gather")
    def gather(t_hbm, i_hbm, o_hbm):
        def body(i_vmem, o_vmem):
            pltpu.sync_copy(t_hbm.at[i_vmem.at[0]], o_vmem)

        pltpu.emit_pipeline(
            body, grid=(m // SC_WINDOW,),
            in_specs=[pl.BlockSpec((1, SC_WINDOW), lambda i: (0, i))],
            out_specs=[pl.BlockSpec((SC_WINDOW, SC_ROW_WORDS), lambda i: (i, 0))],
            core_axis_name=("core", "subcore"),
            dimension_semantics=(pltpu.PARALLEL,))(i_hbm, o_hbm)

    return gather(t2, idx2.reshape(1, m)).reshape(kk, n, w)


def _moe(h2p, top_e, top_g, rank, cnt, wgu, bgu, wd, bd):
    b, n, dh = h2p.shape
    n_tok = b * n
    nk = n_tok * TOP_K
    experts = jnp.arange(N_EXPERTS, dtype=jnp.int32)
    counts = cnt[:, 0].astype(jnp.int32)
    padded = (counts + MOE_BLOCK - 1) // MOE_BLOCK * MOE_BLOCK
    ends_p = jnp.cumsum(padded)
    start_p = ends_p - padded
    n_blocks = (nk + N_EXPERTS * (MOE_BLOCK - 1) + MOE_BLOCK - 1) // MOE_BLOCK
    block_start = jnp.arange(n_blocks, dtype=jnp.int32) * MOE_BLOCK
    block_e = jnp.minimum(jnp.sum(ends_p[None, :] <= block_start[:, None], axis=1), N_EXPERTS - 1).astype(jnp.int32)
    is_e = block_e[:, None] == experts[None, :]
    filled = jnp.sum(jnp.where(is_e, (start_p + counts)[None, :], 0), axis=1)
    n_valid = jnp.clip(filled - block_start, 0, MOE_BLOCK).astype(jnp.int32)
    dest = rank + jnp.sum(jnp.where(top_e[..., None] == experts, start_p, 0), axis=-1)
    dest = dest.transpose(1, 0, 2).reshape(TOP_K, n_tok)
    gate = top_g.transpose(0, 2, 1).reshape(n_tok, TOP_K)
    xb = _sc_scatter_rows(h2p.reshape(n_tok, dh), dest, n_blocks * MOE_BLOCK)
    yb = _expert_ffn(xb, block_e, n_valid, wgu, bgu, wd, bd)
    yk = _sc_gather_rows(yb, dest)
    return yk, gate


def kernel(x, c, ctx, c_ctx, w_mod, b_mod, w_in, ret_decay, ret_gn_g, ret_gn_b, hg_lb, hg_norm_g,
           na_rpb, gq_qn_g, gq_kn_g, w_branch, w_out, ln_g, ln_b, w_router, b_router, w_gu, b_gu,
           w_down, b_down):
    b_, t_, d = x.shape
    n_ctx = ctx.shape[1]
    depth = w_mod.shape[0]
    tm = 256

    sm = jax.nn.softmax(hg_lb.astype(F32), axis=1)
    lower = jnp.cumsum(sm, axis=1) - sm[:, :1]
    log_gamma = jax.nn.log_sigmoid(ret_decay.astype(F32)).reshape(depth, 2 * RET_HEADS)

    cc = jnp.concatenate([c, c_ctx[None, :], jnp.zeros((16 - b_ - 1, d), F32)], axis=0)
    modv = _mod_all(cc, w_mod, b_mod).reshape(depth, 16, 6, d)
    one = jnp.asarray([0.0, 1.0, 0.0, 0.0, 1.0, 0.0], F32)[None, None, :, None]
    modv = (modv + one)[:, :, jnp.asarray([1, 0, 2, 4, 3, 5])]
    modv = jnp.concatenate([modv, jnp.zeros((depth, 16, 2, d), F32)], axis=2)
    mod = jnp.stack([jnp.broadcast_to(modv[:, b_:b_ + 1], (depth, b_, 8, d)), modv[:, :b_]], axis=2)

    perm = np.concatenate([np.arange(PROJ_ORIG, PROJ_TOTAL), np.arange(PROJ_ORIG)])
    cos_r, sin_r = _rope_tables(n_ctx, t_, RET_DK)
    cos_g, sin_g = _rope_tables(n_ctx, t_, GQA_HD)

    xs = jnp.concatenate([ctx, x], axis=1)
    tm_in = 768 if xs.shape[1] % 768 == 0 else tm
    for l in range(depth):
        w_in_l = jnp.pad(w_in[l][:, perm], ((0, 0), (0, PROJ_PAD - PROJ_TOTAL))).astype(BF16)
        z = _in_proj(xs, mod[l], w_in_l, n_ctx, tm_in, PROJ_PAD // 4)
        ret_o = _retention(z, log_gamma[l], cos_r, sin_r, ret_gn_g[l], ret_gn_b[l], n_ctx)
        hg_o = _hgrn2(z, lower[:, l], hg_norm_g[l], n_ctx)
        na_o = _neighbourhood(z, _na_bias_table(na_rpb[l]), n_ctx)
        qh, kd, vd = _gqa_prep(z, cos_g, sin_g, gq_qn_g[l], gq_kn_g[l], tm)
        gq_o = _gqa_attn(qh, kd, vd, n_ctx, tm)
        wr_hi, wr_lo = _split_bf16(w_router[l].T)
        xs, h2p, top_e, top_g, rank, cnt = _merge(z, (ret_o, hg_o, na_o, gq_o), xs, mod[l],
                                                  w_branch[l].astype(BF16), w_out[l].astype(BF16),
                                                  ln_g[l, 0], ln_b[l, 0], wr_hi, wr_lo, b_router[l], n_ctx, tm)
        yk, gate = _moe(h2p, top_e, top_g, rank, cnt, w_gu[l].astype(BF16), b_gu[l],
                        w_down[l].astype(BF16), b_down[l])
        xs = _combine(yk, gate, xs, mod[l], ln_g[l, 1], ln_b[l, 1], n_ctx, tm)
    return xs[:, n_ctx:]
```

```python
import functools

import jax
import jax.numpy as jnp
from jax import lax
import numpy as np
from jax.experimental import pallas as pl
from jax.experimental.pallas import tpu as pltpu
from jax.experimental.pallas import tpu_sc as plsc

D_MODEL = 1024
DEPTH = 4
GRID_W = 64
N_BRANCH = 4
BRANCH_W = D_MODEL // 2
RET_HEADS = 4
RET_DK = BRANCH_W // RET_HEADS
HG_HEADS = 4
HG_DK = BRANCH_W // HG_HEADS
NA_HEADS = 8
NA_HD = BRANCH_W // NA_HEADS
NA_KH = 8
NA_KW = 16
GQA_HEADS = 8
GQA_KV_HEADS = 2
GQA_HD = BRANCH_W // GQA_HEADS
GQA_KV_W = GQA_KV_HEADS * GQA_HD
ROPE_BASE = 10000.0
N_EXPERTS = 32
TOP_K = 4
D_FF_EXPERT = D_MODEL
SWIGLU_LIMIT = 7.0
SWIGLU_ALPHA = 1.702
MOE_BLOCK = 256
LN_EPS = 1e-5
NORM_EPS = 1e-6
NEG_INF = -1e30
DN_ALPHA = (2 * DEPTH) ** 0.25

LANES = 128
BF16 = jnp.bfloat16
F32 = jnp.float32
VMEM_LIMIT = 56 * 1024 * 1024
SC_CORES = 2
SC_SUBCORES = 16
SC_WINDOW = 128
SC_ROW_WORDS = 256
SC_PIECES = D_MODEL // 2 // SC_ROW_WORDS

PROJ_ORIG = 4 * BRANCH_W + 5 * BRANCH_W + 3 * BRANCH_W + BRANCH_W + 2 * GQA_KV_W
PROJ_TOTAL = PROJ_ORIG + N_BRANCH * D_MODEL
PROJ_PAD = 11264
GATE_B = 0
RET_B = (N_BRANCH * D_MODEL) // LANES
HG_B = RET_B + 16
NA_B = HG_B + 20
GQ_B = NA_B + 12

RET_CHUNK = 256
HG_CHUNK = 128
HG_LEVELS = (64, 32, 16, 8, 4, 2, 1, 0)


def _cparams(sem):
    return pltpu.CompilerParams(dimension_semantics=sem, vmem_limit_bytes=VMEM_LIMIT)


def _dot(a, b):
    return jnp.dot(a, b, preferred_element_type=F32)


def _dot_nt(a, b):
    return lax.dot_general(a, b, (((1,), (1,)), ((), ())), preferred_element_type=F32)


def _dot_tn(a, b):
    return lax.dot_general(a, b, (((0,), (0,)), ((), ())), preferred_element_type=F32)


def _split_bf16(x):
    hi = x.astype(BF16)
    lo = (x - hi.astype(F32)).astype(BF16)
    return hi, lo


def _sigmoid(x):
    return 1.0 / (1.0 + jnp.exp(-x))


def _silu(x):
    return x * _sigmoid(x)


def _mod_kernel(c_ref, w_ref, b_ref, o_ref):
    s = _silu(c_ref[...]).astype(BF16)
    o_ref[0] = _dot(s, w_ref[0].astype(BF16)) + b_ref[0]


def _mod_all(cc, w_mod, b_mod):
    depth, d, n = w_mod.shape
    r = cc.shape[0]
    tn = 1536
    return pl.pallas_call(
        _mod_kernel,
        out_shape=jax.ShapeDtypeStruct((depth, r, n), F32),
        grid=(depth, n // tn),
        in_specs=[pl.BlockSpec((r, d), lambda l, j: (0, 0)),
                  pl.BlockSpec((1, d, tn), lambda l, j: (l, 0, j)),
                  pl.BlockSpec((1, 1, tn), lambda l, j: (l, 0, j))],
        out_specs=pl.BlockSpec((1, r, tn), lambda l, j: (l, 0, j)),
        compiler_params=_cparams(("arbitrary", "arbitrary")),
        name="adaln_mod",
    )(cc, w_mod, b_mod.reshape(depth, 1, n))


def _in_proj_kernel(x_ref, mod_ref, w_ref, o_ref, *, n_ctx, tm):
    i = pl.program_id(2)
    x = x_ref[0]
    mc = mod_ref[0, 0]
    ml = mod_ref[0, 1]
    rows = i * tm + lax.broadcasted_iota(jnp.int32, (tm, 1), 0)
    is_ctx = rows < n_ctx
    scale = jnp.where(is_ctx, mc[0:1, :], ml[0:1, :])
    shift = jnp.where(is_ctx, mc[1:2, :], ml[1:2, :])
    h = (x * scale + shift).astype(BF16)
    o_ref[0] = _dot(h, w_ref[...]).astype(o_ref.dtype)


def _in_proj(xs, mod, w, n_ctx, tm, tn):
    b, n, d = xs.shape
    ncol = w.shape[1]
    return pl.pallas_call(
        functools.partial(_in_proj_kernel, n_ctx=n_ctx, tm=tm),
        out_shape=jax.ShapeDtypeStruct((b, n, ncol), BF16),
        grid=(ncol // tn, b, n // tm),
        in_specs=[pl.BlockSpec((1, tm, d), lambda j, bb, i: (bb, i, 0)),
                  pl.BlockSpec((1, 2, 8, d), lambda j, bb, i: (bb, 0, 0, 0)),
                  pl.BlockSpec((d, tn), lambda j, bb, i: (0, j))],
        out_specs=pl.BlockSpec((1, tm, tn), lambda j, bb, i: (bb, i, j)),
        compiler_params=_cparams(("arbitrary", "arbitrary", "arbitrary")),
        name="in_proj",
    )(xs, mod, w)


def _rope_tables(n_ctx, t, head_dim):
    idx = jnp.arange(t, dtype=jnp.int32)
    row = (idx // GRID_W).astype(F32)
    col = (idx % GRID_W).astype(F32)
    n = head_dim // 2
    inv = ROPE_BASE ** (-jnp.arange(0, n, 2, dtype=F32) / n)

    def half(pos):
        ang = pos[:, None] * inv[None, :]
        c, s = jnp.cos(ang), jnp.sin(ang)
        return jnp.concatenate([c, c], -1), jnp.concatenate([-s, s], -1)

    cr, sr = half(row)
    cc, sc = half(col)
    cos = jnp.concatenate([cr, cc], -1)
    sin = jnp.concatenate([sr, sc], -1)
    reps = LANES // head_dim
    cos = jnp.tile(cos, (1, reps))
    sin = jnp.tile(sin, (1, reps))
    cos = jnp.concatenate([jnp.ones((n_ctx, LANES), F32), cos], 0)
    sin = jnp.concatenate([jnp.zeros((n_ctx, LANES), F32), sin], 0)
    return cos, sin


def _rope(x, cos, sin, quarter):
    lane = lax.broadcasted_iota(jnp.int32, x.shape, 1)
    first = (lane & (2 * quarter - 1)) < quarter
    rot = jnp.where(first, pltpu.roll(x, LANES - quarter, 1), pltpu.roll(x, quarter, 1))
    return x * cos + rot * sin


def _ret_kernel(lg_ref, q_ref, k_ref, v_ref, g_ref, cos_ref, sin_ref, gng_ref, gnb_ref, o_ref,
                ks_ref, sb_ref, *, n_ctx, chunk):
    hh = pl.program_id(1)
    lgf = lg_ref[hh]
    lgb = lg_ref[RET_HEADS + hh]
    n = q_ref.shape[1]
    c_ = chunk
    nch = n // c_
    n_cc = n_ctx // c_
    dk = q_ref.shape[2]

    ri = lax.broadcasted_iota(jnp.int32, (c_, dk), 0).astype(F32)
    df = jnp.exp((ri + 1.0) * lgf)
    db = jnp.exp((c_ - ri) * lgb)
    ef = jnp.exp((c_ - 1.0 - ri) * lgf)
    eb = jnp.exp(ri * lgb)
    gfc = jnp.exp(c_ * lgf)
    gbc = jnp.exp(c_ * lgb)
    ii = lax.broadcasted_iota(jnp.int32, (c_, c_), 0)
    jj = lax.broadcasted_iota(jnp.int32, (c_, c_), 1)
    dij = (ii - jj).astype(F32)
    w = jnp.where(ii >= jj, jnp.exp(dij * lgf), jnp.exp(-dij * lgb))

    def rope_k(c, carry):
        r0 = pl.multiple_of(c * c_, c_)
        kc = k_ref[0, pl.ds(r0, c_), :].astype(F32) * (RET_DK ** -0.5)
        kc = _rope(kc, cos_ref[pl.ds(r0, c_), :], sin_ref[pl.ds(r0, c_), :], 32)
        ks_ref[pl.ds(r0, c_), :] = kc.astype(BF16)
        return carry

    lax.fori_loop(0, nch, rope_k, 0)

    def bwd_step(c, s):
        r0 = pl.multiple_of(c * c_, c_)
        sb_ref[c] = s
        kc = (ks_ref[pl.ds(r0, c_), :].astype(F32) * eb).astype(BF16)
        return s * gbc + _dot_tn(kc, v_ref[0, pl.ds(r0, c_), :])

    s = jnp.zeros((dk, dk), F32)
    s = lax.fori_loop(0, n_cc, lambda t, s_: bwd_step(n_cc - 1 - t, s_), s)
    lax.fori_loop(0, nch - n_cc, lambda t, s_: bwd_step(nch - 1 - t, s_), s)

    gng = gng_ref[...]
    gnb = gnb_ref[...]

    def fwd_step(c, sf):
        r0 = pl.multiple_of(c * c_, c_)
        qc = _rope(q_ref[0, pl.ds(r0, c_), :].astype(F32), cos_ref[pl.ds(r0, c_), :],
                   sin_ref[pl.ds(r0, c_), :], 32)
        kc = ks_ref[pl.ds(r0, c_), :]
        vc = v_ref[0, pl.ds(r0, c_), :]
        a = _dot_nt(qc.astype(BF16), kc) * w
        qi = jnp.concatenate([qc * df, qc * db], axis=1).astype(BF16)
        si = jnp.concatenate([sf, sb_ref[c]], axis=0).astype(BF16)
        o = _dot(a.astype(BF16), vc) + _dot(qi, si)
        sf_new = sf * gfc + _dot_tn((kc.astype(F32) * ef).astype(BF16), vc)
        mu = jnp.mean(o, axis=-1, keepdims=True)
        oc = o - mu
        var = jnp.mean(oc * oc, axis=-1, keepdims=True)
        y = oc * lax.rsqrt(var + LN_EPS) * gng + gnb
        gate = g_ref[0, pl.ds(r0, c_), :].astype(F32)
        o_ref[0, pl.ds(r0, c_), :] = (y * _silu(gate)).astype(o_ref.dtype)
        return sf_new

    lax.fori_loop(0, nch, fwd_step, jnp.zeros((dk, dk), F32))


def _retention(z, log_gamma, cos, sin, gn_g, gn_b, n_ctx):
    b, n, _ = z.shape
    blk = lambda off: pl.BlockSpec((1, n, LANES), lambda bb, h, lg: (bb, 0, off + h))
    grid_spec = pltpu.PrefetchScalarGridSpec(
        num_scalar_prefetch=1,
        grid=(b, RET_HEADS),
        in_specs=[blk(RET_B), blk(RET_B + 4), blk(RET_B + 8), blk(RET_B + 12),
                  pl.BlockSpec((n, LANES), lambda bb, h, lg: (0, 0)),
                  pl.BlockSpec((n, LANES), lambda bb, h, lg: (0, 0)),
                  pl.BlockSpec((1, LANES), lambda bb, h, lg: (0, h)),
                  pl.BlockSpec((1, LANES), lambda bb, h, lg: (0, h))],
        out_specs=pl.BlockSpec((1, n, LANES), lambda bb, h, lg: (bb, 0, h)),
        scratch_shapes=[pltpu.VMEM((n, LANES), BF16),
                        pltpu.VMEM((n // RET_CHUNK, RET_DK, RET_DK), F32)],
    )
    return pl.pallas_call(
        functools.partial(_ret_kernel, n_ctx=n_ctx, chunk=RET_CHUNK),
        out_shape=jax.ShapeDtypeStruct((b, n, BRANCH_W), BF16),
        grid_spec=grid_spec,
        compiler_params=_cparams(("arbitrary", "arbitrary")),
        name="retention",
    )(log_gamma, z, z, z, z, cos, sin, gn_g.reshape(1, -1), gn_b.reshape(1, -1))


def _hg_constants():
    c = HG_CHUNK
    r = np.arange(c)[:, None]
    t = np.arange(c)[None, :]
    mats = []
    level = np.full((2, c, c), len(HG_LEVELS), np.int32)
    for d in range(2):
        blocks = []
        for li, s in enumerate(HG_LEVELS):
            if s == 0:
                level[d][np.arange(c), np.arange(c)] = li
                continue
            base = (r // (2 * s)) * (2 * s)
            if d == 0:
                m = base + s - 1
                mat = (t > np.minimum(r, m)) & (t <= np.maximum(r, m))
                q_side = (r % (2 * s)) >= s
            else:
                m = base + s
                mat = (t >= np.minimum(r, m)) & (t < np.maximum(r, m))
                q_side = (r % (2 * s)) < s
            blocks.append(mat.astype(np.float32))
            same = (r // (2 * s)) == (t // (2 * s))
            k_side_t = ((t % (2 * s)) < s) if d == 0 else ((t % (2 * s)) >= s)
            level[d][same & q_side & k_side_t] = li
        if d == 0:
            blocks.append((t <= r).astype(np.float32))
            blocks.append((t > r).astype(np.float32))
        else:
            blocks.append((t >= r).astype(np.float32))
            blocks.append((t < r).astype(np.float32))
        mats.append(np.concatenate(blocks, 0))
    return np.stack(mats), level


def _hg_gates(zf, lb):
    en = jnp.exp(-jnp.abs(zf))
    inv = 1.0 / (1.0 + en)
    pos = zf >= 0
    sg = jnp.where(pos, inv, en * inv)
    sgn = jnp.where(pos, en * inv, inv)
    one_m = 1.0 - lb
    return jnp.log(lb + one_m * sg), one_m * sgn


def _hg_kernel(q_ref, ff_ref, fb_ref, v_ref, g_ref, lb_ref, ng_ref, m_ref, lv_ref, o_ref,
               lfb_ref, kb_ref, sb_ref, *, n_ctx):
    c_ = HG_CHUNK
    n = q_ref.shape[1]
    nch = n // c_
    n_cc = n_ctx // c_
    dk = q_ref.shape[2]
    nl = len(HG_LEVELS)
    lbf = lb_ref[0:1, :]
    lbb = lb_ref[1:2, :]

    def mdot(mat, lf):
        hi, lo = _split_bf16(lf)
        return _dot(mat, hi) + _dot(mat, lo)

    def prep_b(c, carry):
        r0 = pl.multiple_of(c * c_, c_)
        lf, key = _hg_gates(fb_ref[0, pl.ds(r0, c_), :].astype(F32), lbb)
        lfb_ref[pl.ds(r0, c_), :] = lf
        kb_ref[pl.ds(r0, c_), :] = key
        return carry

    lax.fori_loop(0, nch, prep_b, 0)

    def bwd_step(c, st):
        r0 = pl.multiple_of(c * c_, c_)
        sb_ref[c] = st
        lf = lfb_ref[pl.ds(r0, c_), :]
        e = mdot(m_ref[1, (nl - 1) * c_:(nl + 1) * c_, :], lf)
        tot = e[0:1, :]
        kt = (kb_ref[pl.ds(r0, c_), :] * jnp.exp(e[c_:2 * c_, :])).astype(BF16)
        return st * jnp.exp(tot) + _dot_tn(v_ref[0, pl.ds(r0, c_), :], kt)

    st = jnp.zeros((dk, dk), F32)
    st = lax.fori_loop(0, n_cc, lambda t, s_: bwd_step(n_cc - 1 - t, s_), st)
    lax.fori_loop(0, nch - n_cc, lambda t, s_: bwd_step(nch - 1 - t, s_), st)

    ng = ng_ref[...]

    def intra(qs, key, lf, d):
        e = mdot(m_ref[d], lf)
        a = jnp.zeros((c_, c_), F32)
        lv = lv_ref[d]
        for li, s in enumerate(HG_LEVELS):
            if s == 0:
                qt, kt = qs, key
            else:
                dec = jnp.exp(e[li * c_:(li + 1) * c_, :])
                qt, kt = qs * dec, key * dec
            a = a + jnp.where(lv == li, _dot_nt(qt.astype(BF16), kt.astype(BF16)), 0.0)
        cum = e[(nl - 1) * c_:nl * c_, :]
        rest = e[nl * c_:(nl + 1) * c_, :]
        return a, cum, rest

    def fwd_step(c, sf):
        r0 = pl.multiple_of(c * c_, c_)
        qs = _silu(q_ref[0, pl.ds(r0, c_), :].astype(F32))
        vc = v_ref[0, pl.ds(r0, c_), :]
        lff, keyf = _hg_gates(ff_ref[0, pl.ds(r0, c_), :].astype(F32), lbf)
        af, cumf, restf = intra(qs, keyf, lff, 0)
        ab, cumb, _ = intra(qs, kb_ref[pl.ds(r0, c_), :], lfb_ref[pl.ds(r0, c_), :], 1)
        qi = jnp.concatenate([qs * jnp.exp(cumf), qs * jnp.exp(cumb)], axis=1).astype(BF16)
        si = jnp.concatenate([sf, sb_ref[c]], axis=1).astype(BF16)
        o = _dot((af + ab).astype(BF16), vc) + _dot_nt(qi, si)
        totf = cumf[c_ - 1:c_, :]
        sf_new = sf * jnp.exp(totf) + _dot_tn(vc, (keyf * jnp.exp(restf)).astype(BF16))
        on = o * lax.rsqrt(jnp.mean(o * o, axis=-1, keepdims=True) + NORM_EPS) * ng
        gate = g_ref[0, pl.ds(r0, c_), :].astype(F32)
        o_ref[0, pl.ds(r0, c_), :] = (on * _silu(gate)).astype(o_ref.dtype)
        return sf_new

    lax.fori_loop(0, nch, fwd_step, jnp.zeros((dk, dk), F32), unroll=2)


def _hgrn2(z, lower, norm_g, n_ctx):
    b, n, _ = z.shape
    mats, level = _hg_constants()
    mats = jnp.asarray(mats, BF16)
    level = jnp.asarray(level)
    blk = lambda off: pl.BlockSpec((1, n, LANES), lambda bb, h: (bb, 0, off + h))
    return pl.pallas_call(
        functools.partial(_hg_kernel, n_ctx=n_ctx),
        out_shape=jax.ShapeDtypeStruct((b, n, BRANCH_W), BF16),
        grid=(b, HG_HEADS),
        in_specs=[blk(HG_B), blk(HG_B + 4), blk(HG_B + 8), blk(HG_B + 12), blk(HG_B + 16),
                  pl.BlockSpec((2, LANES), lambda bb, h: (0, h)),
                  pl.BlockSpec((1, LANES), lambda bb, h: (0, h)),
                  pl.BlockSpec(mats.shape, lambda bb, h: (0, 0, 0)),
                  pl.BlockSpec(level.shape, lambda bb, h: (0, 0, 0))],
        out_specs=pl.BlockSpec((1, n, LANES), lambda bb, h: (bb, 0, h)),
        scratch_shapes=[pltpu.VMEM((n, LANES), F32), pltpu.VMEM((n, LANES), F32),
                        pltpu.VMEM((n // HG_CHUNK, HG_DK, HG_DK), F32)],
        compiler_params=_cparams(("arbitrary", "arbitrary")),
        name="hgrn2",
    )(z, z, z, z, z, lower, norm_g.reshape(1, -1), mats, level)


def _na_bias_table(rpb):
    qc = np.arange(GRID_W)[:, None]
    kc = np.arange(GRID_W)[None, :]
    col0 = np.clip(qc - NA_KW // 2, 0, GRID_W - NA_KW)
    valid = (kc >= col0) & (kc < col0 + NA_KW)
    dc = np.clip(kc - qc + (NA_KW - 1), 0, 2 * NA_KW - 2)
    pick = (dc[:, :, None] == np.arange(2 * NA_KW - 1)).astype(np.float32)
    t15 = jnp.einsum('hrc,qkc->hrqk', rpb.astype(F32), pick, precision=lax.Precision.HIGHEST)
    t = jnp.stack([t15[:, dl:dl + NA_KH] for dl in range(NA_KH)], axis=1)
    t = jnp.where(jnp.asarray(valid)[None, None, None], t, NEG_INF)
    return t.transpose(0, 1, 3, 2, 4).reshape(rpb.shape[0], NA_KH, GRID_W, NA_KH * GRID_W)


def _softmax_pv(parts):
    m = parts[0][0].max(axis=-1, keepdims=True)
    for s, _ in parts[1:]:
        m = jnp.maximum(m, s.max(axis=-1, keepdims=True))
    l = None
    o = None
    for s, v in parts:
        p = jnp.exp(s - m)
        ls = p.sum(axis=-1, keepdims=True)
        os_ = _dot(p.astype(BF16), v)
        l = ls if l is None else l + ls
        o = os_ if o is None else o + os_
    return o / l


def _na_kernel(q_ref, k_ref, v_ref, tab_ref, o_ref, *, n_ctx):
    n = q_ref.shape[1]
    rows = (n - n_ctx) // GRID_W
    wlen = NA_KH * GRID_W
    scale = NA_HD ** -0.5
    kx = k_ref[0, 0:n_ctx, :]
    vx = v_ref[0, 0:n_ctx, :]

    def head_mask(shape, hh):
        lane = lax.broadcasted_iota(jnp.int32, shape, 1)
        return (lane < NA_HD) if hh == 0 else (lane >= NA_HD)

    qx = q_ref[0, 0:n_ctx, :].astype(F32) * scale
    outs = []
    for hh in range(2):
        qm = jnp.where(head_mask(qx.shape, hh), qx, 0.0).astype(BF16)
        outs.append(_softmax_pv([(_dot_nt(qm, kx), vx)]))
    o_ref[0, 0:n_ctx, :] = jnp.where(head_mask(outs[0].shape, 0), outs[0], outs[1]).astype(o_ref.dtype)

    def row_step(r, carry):
        start = jnp.clip(r - NA_KH // 2, 0, rows - NA_KH)
        delta = start - r + (NA_KH - 1)
        q0 = pl.multiple_of(n_ctx + r * GRID_W, GRID_W)
        k0 = pl.multiple_of(n_ctx + start * GRID_W, GRID_W)
        qr = q_ref[0, pl.ds(q0, GRID_W), :].astype(F32) * scale
        kl = k_ref[0, pl.ds(k0, wlen), :]
        vl = v_ref[0, pl.ds(k0, wlen), :]
        outs = []
        for hh in range(2):
            qm = jnp.where(head_mask(qr.shape, hh), qr, 0.0).astype(BF16)
            s_loc = _dot_nt(qm, kl) + tab_ref[hh, delta]
            s_ctx = _dot_nt(qm, kx)
            outs.append(_softmax_pv([(s_loc, vl), (s_ctx, vx)]))
        o_ref[0, pl.ds(q0, GRID_W), :] = jnp.where(head_mask(outs[0].shape, 0), outs[0], outs[1]).astype(o_ref.dtype)
        return carry

    lax.fori_loop(0, rows, row_step, 0, unroll=2)


def _neighbourhood(z, table, n_ctx):
    b, n, _ = z.shape
    npair = NA_HEADS // 2
    blk = lambda off: pl.BlockSpec((1, n, LANES), lambda bb, p: (bb, 0, off + p))
    return pl.pallas_call(
        functools.partial(_na_kernel, n_ctx=n_ctx),
        out_shape=jax.ShapeDtypeStruct((b, n, BRANCH_W), BF16),
        grid=(b, npair),
        in_specs=[blk(NA_B), blk(NA_B + 4), blk(NA_B + 8),
                  pl.BlockSpec((2, NA_KH, GRID_W, NA_KH * GRID_W), lambda bb, p: (p, 0, 0, 0))],
        out_specs=pl.BlockSpec((1, n, LANES), lambda bb, p: (bb, 0, p)),
        compiler_params=_cparams(("arbitrary", "arbitrary")),
        name="neighbourhood_attn",
    )(z, z, z, table)


def _gqa_prep_kernel(q_ref, k_ref, v_ref, cos_ref, sin_ref, qg_ref, kg_ref, bd_ref,
                     qo_ref, ko_ref, vo_ref):
    cos = cos_ref[...]
    sin = sin_ref[...]
    bd = bd_ref[...]

    def norm_rope(x, g):
        hi, lo = _split_bf16(x * x)
        ms = _dot(hi, bd) + _dot(lo, bd)
        xn = x * lax.rsqrt(ms + NORM_EPS) * g
        return _rope(xn, cos, sin, GQA_HD // 4)

    qg = qg_ref[...]
    for j in range(BRANCH_W // LANES):
        x = q_ref[0, :, j * LANES:(j + 1) * LANES].astype(F32)
        qo_ref[0, :, j * LANES:(j + 1) * LANES] = (norm_rope(x, qg) * (GQA_HD ** -0.5)).astype(qo_ref.dtype)
    kn = norm_rope(k_ref[0].astype(F32), kg_ref[...])
    vv = v_ref[0].astype(F32)
    lane = lax.broadcasted_iota(jnp.int32, kn.shape, 1)
    lo_half = lane < GQA_HD
    for a, dst in ((kn, ko_ref), (vv, vo_ref)):
        sw = pltpu.roll(a, GQA_HD, 1)
        dst[0, 0] = jnp.where(lo_half, a, sw).astype(dst.dtype)
        dst[0, 1] = jnp.where(lo_half, sw, a).astype(dst.dtype)


def _gqa_prep(z, cos, sin, qn_g, kn_g, tr):
    b, n, _ = z.shape
    bd = np.kron(np.eye(LANES // GQA_HD), np.ones((GQA_HD, GQA_HD))) / GQA_HD
    qg = jnp.tile(qn_g.reshape(1, -1), (1, LANES // GQA_HD))
    kg = jnp.tile(kn_g.reshape(1, -1), (1, LANES // GQA_HD))
    qb = (GQ_B * LANES) // BRANCH_W
    return pl.pallas_call(
        _gqa_prep_kernel,
        out_shape=(jax.ShapeDtypeStruct((b, n, BRANCH_W), BF16),
                   jax.ShapeDtypeStruct((b, GQA_KV_HEADS, n, LANES), BF16),
                   jax.ShapeDtypeStruct((b, GQA_KV_HEADS, n, LANES), BF16)),
        grid=(b, n // tr),
        in_specs=[pl.BlockSpec((1, tr, BRANCH_W), lambda bb, i: (bb, i, qb)),
                  pl.BlockSpec((1, tr, LANES), lambda bb, i: (bb, i, GQ_B + 4)),
                  pl.BlockSpec((1, tr, LANES), lambda bb, i: (bb, i, GQ_B + 5)),
                  pl.BlockSpec((tr, LANES), lambda bb, i: (i, 0)),
                  pl.BlockSpec((tr, LANES), lambda bb, i: (i, 0)),
                  pl.BlockSpec((1, LANES), lambda bb, i: (0, 0)),
                  pl.BlockSpec((1, LANES), lambda bb, i: (0, 0)),
                  pl.BlockSpec((LANES, LANES), lambda bb, i: (0, 0))],
        out_specs=(pl.BlockSpec((1, tr, BRANCH_W), lambda bb, i: (bb, i, 0)),
                   pl.BlockSpec((1, GQA_KV_HEADS, tr, LANES), lambda bb, i: (bb, 0, i, 0)),
                   pl.BlockSpec((1, GQA_KV_HEADS, tr, LANES), lambda bb, i: (bb, 0, i, 0))),
        compiler_params=_cparams(("arbitrary", "arbitrary")),
        name="gqa_prep",
    )(z, z, z, cos, sin, qg, kg, jnp.asarray(bd, BF16))


def _gqa_kernel(q_ref, k_ref, v_ref, o_ref, *, n_ctx):
    i = pl.program_id(1)
    tq = q_ref.shape[1]
    grp = GQA_HEADS // GQA_KV_HEADS

    def attend(nk):
        for j in range(BRANCH_W // LANES):
            qb = q_ref[0, :, j * LANES:(j + 1) * LANES]
            lane = lax.broadcasted_iota(jnp.int32, qb.shape, 1)
            outs = []
            for hh in range(2):
                g = (2 * j + hh) // grp
                msk = (lane < GQA_HD) if hh == 0 else (lane >= GQA_HD)
                qm = jnp.where(msk, qb, jnp.zeros_like(qb))
                outs.append(_softmax_pv([(_dot_nt(qm, k_ref[0, g, 0:nk, :]), v_ref[0, g, 0:nk, :])]))
            o_ref[0, :, j * LANES:(j + 1) * LANES] = jnp.where(lane < GQA_HD, outs[0], outs[1]).astype(o_ref.dtype)

    @pl.when(i * tq < n_ctx)
    def _():
        attend(n_ctx)

    @pl.when(i * tq >= n_ctx)
    def _():
        attend(k_ref.shape[2])


def _gqa_attn(qh, kd, vd, n_ctx, tq):
    b, n, _ = qh.shape
    return pl.pallas_call(
        functools.partial(_gqa_kernel, n_ctx=n_ctx),
        out_shape=jax.ShapeDtypeStruct((b, n, BRANCH_W), BF16),
        grid=(b, n // tq),
        in_specs=[pl.BlockSpec((1, tq, BRANCH_W), lambda bb, i: (bb, i, 0)),
                  pl.BlockSpec((1, GQA_KV_HEADS, n, LANES), lambda bb, i: (bb, 0, 0, 0)),
                  pl.BlockSpec((1, GQA_KV_HEADS, n, LANES), lambda bb, i: (bb, 0, 0, 0))],
        out_specs=pl.BlockSpec((1, tq, BRANCH_W), lambda bb, i: (bb, i, 0)),
        compiler_params=_cparams(("arbitrary", "arbitrary")),
        name="gqa_attn",
    )(qh, kd, vd)


def _layernorm_rows(v, g, b):
    mu = jnp.mean(v, axis=-1, keepdims=True)
    vc = v - mu
    var = jnp.mean(vc * vc, axis=-1, keepdims=True)
    return vc * lax.rsqrt(var + LN_EPS) * g + b


def _pack_bf16_pairs(v):
    w = v.shape[1] // 2
    bits = lax.bitcast_convert_type(v.astype(BF16).astype(F32), jnp.int32)
    return bits[:, w:] | lax.shift_right_logical(bits[:, :w], 16)


def _unpack_bf16_pairs(pieces):
    lo = [lax.bitcast_convert_type(lax.shift_left(p, 16), F32) for p in pieces]
    hi = [lax.bitcast_convert_type(p & jnp.int32(-65536), F32) for p in pieces]
    return jnp.concatenate(lo + hi, axis=1)


def _store_pieces(ref, lead, packed):
    for h in range(SC_PIECES):
        ref[(h,) + lead] = packed[:, h * SC_ROW_WORDS:(h + 1) * SC_ROW_WORDS]


def _merge_kernel(z_ref, r_ref, h_ref, a_ref, q_ref, x_ref, mod_ref, wb_ref, wo_ref, lng_ref, lnb_ref,
                  wrh_ref, wrl_ref, br_ref, tri_ref, xo_ref, h2_ref, te_ref, tg_ref, rk_ref, cnt_ref,
                  run_ref, *, n_ctx, tm):
    i = pl.program_id(1)

    @pl.when((pl.program_id(0) == 0) & (i == 0))
    def _():
        run_ref[...] = jnp.zeros_like(run_ref)

    is_ctx = i * tm < n_ctx
    mod = jnp.where(is_ctx, mod_ref[0, 0], mod_ref[0, 1])
    acc = None
    for nb, br in enumerate((r_ref, h_ref, a_ref, q_ref)):
        yb = _dot(br[0], wb_ref[nb])
        gt = _sigmoid(z_ref[0, :, nb * D_MODEL:(nb + 1) * D_MODEL].astype(F32))
        acc = gt * yb if acc is None else acc + gt * yb
    y = _dot(acc.astype(BF16), wo_ref[...])
    xn = _layernorm_rows(DN_ALPHA * x_ref[0] + mod[2:3, :] * y, lng_ref[...], lnb_ref[...])
    xo_ref[0] = xn
    h2 = xn * mod[3:4, :] + mod[4:5, :]
    h2_hi, h2_lo = _split_bf16(h2)
    _store_pieces(h2_ref, (0,), _pack_bf16_pairs(h2))
    wrh = wrh_ref[...]
    logit = _dot_nt(wrh, h2_hi) + _dot_nt(wrh, h2_lo) + _dot_nt(wrl_ref[...], h2_hi) + br_ref[...]
    eidx = lax.broadcasted_iota(jnp.int32, logit.shape, 0)
    vals, idxs = [], []
    for _ in range(TOP_K):
        m = logit.max(axis=0, keepdims=True)
        sel = jnp.where(logit == m, eidx, N_EXPERTS).min(axis=0, keepdims=True)
        vals.append(m)
        idxs.append(sel)
        logit = jnp.where(eidx == sel, -jnp.inf, logit)
    ex = [jnp.exp(v - vals[0]) for v in vals]
    tot = ex[0] + ex[1] + ex[2] + ex[3]
    te_ref[0] = jnp.concatenate(idxs, axis=0)
    tg_ref[0] = jnp.concatenate([e / tot for e in ex], axis=0)
    base = run_ref[:, 0:1]
    tri = tri_ref[...]
    ranks = []
    for sel in idxs:
        hit = eidx == sel
        ones = jnp.where(hit, 1.0, 0.0)
        pre = _dot(ones.astype(BF16), tri)
        ranks.append(jnp.sum(jnp.where(hit, pre + base, 0.0), axis=0, keepdims=True))
        base = base + jnp.sum(ones, axis=1, keepdims=True)
    rk_ref[0] = jnp.concatenate(ranks, axis=0).astype(jnp.int32)
    run_ref[...] = jnp.broadcast_to(base, run_ref.shape)
    cnt_ref[...] = run_ref[...]


def _merge(z, outs, xs, mod, wb, wo, ln_g, ln_b, wr_hi, wr_lo, b_r, n_ctx, tm):
    b, n, d = xs.shape
    row = lambda w: pl.BlockSpec((1, tm, w), lambda bb, i: (bb, i, 0))
    full = lambda a: pl.BlockSpec(a.shape, lambda bb, i: (0,) * a.ndim)
    topk = pl.BlockSpec((1, TOP_K, tm), lambda bb, i: (bb, 0, i))
    ln_g = ln_g.reshape(1, d)
    ln_b = ln_b.reshape(1, d)
    b_r = b_r.reshape(N_EXPERTS, 1)
    tri = jnp.asarray(np.triu(np.ones((tm, tm), np.float32), 1), BF16)
    return pl.pallas_call(
        functools.partial(_merge_kernel, n_ctx=n_ctx, tm=tm),
        out_shape=(jax.ShapeDtypeStruct((b, n, d), F32),
                   jax.ShapeDtypeStruct((SC_PIECES, b, n, SC_ROW_WORDS), jnp.int32),
                   jax.ShapeDtypeStruct((b, TOP_K, n), jnp.int32),
                   jax.ShapeDtypeStruct((b, TOP_K, n), F32),
                   jax.ShapeDtypeStruct((b, TOP_K, n), jnp.int32),
                   jax.ShapeDtypeStruct((N_EXPERTS, LANES), F32)),
        grid=(b, n // tm),
        in_specs=[row(N_BRANCH * D_MODEL), row(BRANCH_W), row(BRANCH_W), row(BRANCH_W), row(BRANCH_W),
                  row(d),
                  pl.BlockSpec((1, 2, 8, d), lambda bb, i: (bb, 0, 0, 0)),
                  full(wb), full(wo), full(ln_g), full(ln_b), full(wr_hi), full(wr_lo), full(b_r), full(tri)],
        out_specs=(row(d), pl.BlockSpec((SC_PIECES, 1, tm, SC_ROW_WORDS), lambda bb, i: (0, bb, i, 0)),
                   topk, topk, topk,
                   pl.BlockSpec((N_EXPERTS, LANES), lambda bb, i: (0, 0))),
        scratch_shapes=[pltpu.VMEM((N_EXPERTS, LANES), F32)],
        compiler_params=_cparams(("arbitrary", "arbitrary")),
        name="merge_ln_router",
    )(z, *outs, xs, mod, wb, wo, ln_g, ln_b, wr_hi, wr_lo, b_r, tri)


def _ffn_kernel(be_ref, nv_ref, x_ref, wgu_ref, bgu_ref, wd_ref, bd_ref, o_ref):
    i = pl.program_id(0)
    nv = nv_ref[i]

    @pl.when(nv > 0)
    def _():
        rows = lax.broadcasted_iota(jnp.int32, (MOE_BLOCK, 1), 0)
        x = _unpack_bf16_pairs([x_ref[h] for h in range(SC_PIECES)])
        x = jnp.where(rows < nv, x, 0.0).astype(BF16)
        gu = _dot(x, wgu_ref[0]) + bgu_ref[0]
        g = jnp.minimum(gu[:, :D_FF_EXPERT], SWIGLU_LIMIT)
        u = jnp.clip(gu[:, D_FF_EXPERT:], -SWIGLU_LIMIT, SWIGLU_LIMIT)
        act = g * _sigmoid(SWIGLU_ALPHA * g) * (u + 1.0)
        _store_pieces(o_ref, (), _pack_bf16_pairs(_dot(act.astype(BF16), wd_ref[0]) + bd_ref[0]))

    @pl.when(nv <= 0)
    def _():
        o_ref[...] = jnp.zeros_like(o_ref)


def _expert_ffn(xb, block_e, n_valid, wgu, bgu, wd, bd):
    _, n_slots, _ = xb.shape
    n_blocks = n_slots // MOE_BLOCK
    ne, d, f2 = wgu.shape
    slot_blk = pl.BlockSpec((SC_PIECES, MOE_BLOCK, SC_ROW_WORDS), lambda i, be, nv: (0, i, 0))
    grid_spec = pltpu.PrefetchScalarGridSpec(
        num_scalar_prefetch=2,
        grid=(n_blocks,),
        in_specs=[slot_blk,
                  pl.BlockSpec((1, d, f2), lambda i, be, nv: (be[i], 0, 0)),
                  pl.BlockSpec((1, 1, f2), lambda i, be, nv: (be[i], 0, 0)),
                  pl.BlockSpec((1, f2 // 2, d), lambda i, be, nv: (be[i], 0, 0)),
                  pl.BlockSpec((1, 1, d), lambda i, be, nv: (be[i], 0, 0))],
        out_specs=slot_blk,
    )
    return pl.pallas_call(
        _ffn_kernel,
        out_shape=jax.ShapeDtypeStruct(xb.shape, jnp.int32),
        grid_spec=grid_spec,
        compiler_params=_cparams(("arbitrary",)),
        name="expert_ffn",
    )(block_e, n_valid, xb, wgu, bgu.reshape(ne, 1, f2), wd, bd.reshape(ne, 1, d))


def _combine_kernel(y_ref, g_ref, x_ref, mod_ref, lng_ref, lnb_ref, xo_ref, *, n_ctx, tm):
    i = pl.program_id(1)
    is_ctx = i * tm < n_ctx
    g2 = jnp.where(is_ctx, mod_ref[0, 0, 5:6, :], mod_ref[0, 1, 5:6, :])
    gate = g_ref[...]
    y = None
    for k in range(TOP_K):
        t = gate[:, k:k + 1] * _unpack_bf16_pairs([y_ref[k, h] for h in range(SC_PIECES)])
        y = t if y is None else y + t
    xo_ref[0] = _layernorm_rows(DN_ALPHA * x_ref[0] + g2 * y, lng_ref[...], lnb_ref[...])


def _combine(yk, gate, xs, mod, ln_g, ln_b, n_ctx, tm):
    b, n, d = xs.shape
    nt = n // tm
    return pl.pallas_call(
        functools.partial(_combine_kernel, n_ctx=n_ctx, tm=tm),
        out_shape=jax.ShapeDtypeStruct((b, n, d), F32),
        grid=(b, nt),
        in_specs=[pl.BlockSpec((TOP_K, SC_PIECES, tm, SC_ROW_WORDS), lambda bb, i: (0, 0, bb * nt + i, 0)),
                  pl.BlockSpec((tm, TOP_K), lambda bb, i: (bb * nt + i, 0)),
                  pl.BlockSpec((1, tm, d), lambda bb, i: (bb, i, 0)),
                  pl.BlockSpec((1, 2, 8, d), lambda bb, i: (bb, 0, 0, 0)),
                  pl.BlockSpec((1, d), lambda bb, i: (0, 0)),
                  pl.BlockSpec((1, d), lambda bb, i: (0, 0))],
        out_specs=pl.BlockSpec((1, tm, d), lambda bb, i: (bb, i, 0)),
        compiler_params=_cparams(("arbitrary", "arbitrary")),
        name="moe_combine_ln",
    )(yk, gate, xs, mod, ln_g.reshape(1, d), ln_b.reshape(1, d))


def _sc_mesh():
    return plsc.VectorSubcoreMesh(core_axis_name="core", subcore_axis_name="subcore",
                                  num_cores=SC_CORES, num_subcores=SC_SUBCORES)


def _sc_piece_index(idx, rows):
    return jnp.concatenate([idx + h * rows for h in range(SC_PIECES)], axis=1)


def _sc_scatter_rows(x, dest, n_out):
    p, n, w = x.shape
    x2 = x.reshape(p * n, w)
    dest2 = _sc_piece_index(dest, n_out)
    kk, n2 = dest2.shape

    @functools.partial(pl.kernel, out_type=jax.ShapeDtypeStruct((p * n_out, w), x.dtype),
                       mesh=_sc_mesh(), scratch_types=[], name="moe_dispatch_scatter")
    def scatter(x_hbm, i_hbm, o_hbm):
        def body(x_vmem, i_vmem):
            for j in range(kk):
                pltpu.sync_copy(x_vmem, o_hbm.at[i_vmem.at[j]])

        pltpu.emit_pipeline(
            body, grid=(n2 // SC_WINDOW,),
            in_specs=[pl.BlockSpec((SC_WINDOW, SC_ROW_WORDS), lambda i: (i, 0)),
                      pl.BlockSpec((kk, SC_WINDOW), lambda i: (0, i))],
            out_specs=[], core_axis_name=("core", "subcore"),
            dimension_semantics=(pltpu.PARALLEL,))(x_hbm, i_hbm)

    return scatter(x2, dest2).reshape(p, n_out, w)


def _sc_gather_rows(table, idx):
    p, v, w = table.shape
    kk, n = idx.shape
    t2 = table.reshape(p * v, w)
    idx2 = _sc_piece_index(idx, v)
    m = kk * n * p

    @functools.partial(pl.kernel, out_type=jax.ShapeDtypeStruct((m, SC_ROW_WORDS), table.dtype),
                       mesh=_sc_mesh(), scratch_types=[], name="moe_combine_gather")
    def gather(t_hbm, i_hbm, o_hbm):
        def body(i_vmem, o_vmem):
            pltpu.sync_copy(t_hbm.at[i_vmem.at[0]], o_vmem)

        pltpu.emit_pipeline(
            body, grid=(m // SC_WINDOW,),
            in_specs=[pl.BlockSpec((1, SC_WINDOW), lambda i: (0, i))],
            out_specs=[pl.BlockSpec((SC_WINDOW, SC_ROW_WORDS), lambda i: (i, 0))],
            core_axis_name=("core", "subcore"),
            dimension_semantics=(pltpu.PARALLEL,))(i_hbm, o_hbm)

    return gather(t2, idx2.reshape(1, m)).reshape(kk, p, n, w)


def _moe(h2p, top_e, top_g, rank, cnt, wgu, bgu, wd, bd):
    p, b, n, w = h2p.shape
    n_tok = b * n
    nk = n_tok * TOP_K
    experts = jnp.arange(N_EXPERTS, dtype=jnp.int32)
    counts = cnt[:, 0].astype(jnp.int32)
    padded = (counts + MOE_BLOCK - 1) // MOE_BLOCK * MOE_BLOCK
    ends_p = jnp.cumsum(padded)
    start_p = ends_p - padded
    n_blocks = (nk + N_EXPERTS * (MOE_BLOCK - 1) + MOE_BLOCK - 1) // MOE_BLOCK
    block_start = jnp.arange(n_blocks, dtype=jnp.int32) * MOE_BLOCK
    block_e = jnp.minimum(jnp.sum(ends_p[None, :] <= block_start[:, None], axis=1), N_EXPERTS - 1).astype(jnp.int32)
    is_e = block_e[:, None] == experts[None, :]
    filled = jnp.sum(jnp.where(is_e, (start_p + counts)[None, :], 0), axis=1)
    n_valid = jnp.clip(filled - block_start, 0, MOE_BLOCK).astype(jnp.int32)
    dest = rank + jnp.sum(jnp.where(top_e[..., None] == experts, start_p, 0), axis=-1)
    dest = dest.transpose(1, 0, 2).reshape(TOP_K, n_tok)
    gate = top_g.transpose(0, 2, 1).reshape(n_tok, TOP_K)
    xb = _sc_scatter_rows(h2p.reshape(p, n_tok, w), dest, n_blocks * MOE_BLOCK)
    yb = _expert_ffn(xb, block_e, n_valid, wgu, bgu, wd, bd)
    yk = _sc_gather_rows(yb, dest)
    return yk, gate


def kernel(x, c, ctx, c_ctx, w_mod, b_mod, w_in, ret_decay, ret_gn_g, ret_gn_b, hg_lb, hg_norm_g,
           na_rpb, gq_qn_g, gq_kn_g, w_branch, w_out, ln_g, ln_b, w_router, b_router, w_gu, b_gu,
           w_down, b_down):
    b_, t_, d = x.shape
    n_ctx = ctx.shape[1]
    depth = w_mod.shape[0]
    tm = 256

    sm = jax.nn.softmax(hg_lb.astype(F32), axis=1)
    lower = jnp.cumsum(sm, axis=1) - sm[:, :1]
    log_gamma = jax.nn.log_sigmoid(ret_decay.astype(F32)).reshape(depth, 2 * RET_HEADS)

    cc = jnp.concatenate([c, c_ctx[None, :], jnp.zeros((16 - b_ - 1, d), F32)], axis=0)
    modv = _mod_all(cc, w_mod, b_mod).reshape(depth, 16, 6, d)
    one = jnp.asarray([0.0, 1.0, 0.0, 0.0, 1.0, 0.0], F32)[None, None, :, None]
    modv = (modv + one)[:, :, jnp.asarray([1, 0, 2, 4, 3, 5])]
    modv = jnp.concatenate([modv, jnp.zeros((depth, 16, 2, d), F32)], axis=2)
    mod = jnp.stack([jnp.broadcast_to(modv[:, b_:b_ + 1], (depth, b_, 8, d)), modv[:, :b_]], axis=2)

    cos_r, sin_r = _rope_tables(n_ctx, t_, RET_DK)
    cos_g, sin_g = _rope_tables(n_ctx, t_, GQA_HD)

    xs = jnp.concatenate([ctx, x], axis=1)
    tm_in = 768 if xs.shape[1] % 768 == 0 else tm
    for l in range(depth):
        w_in_l = jnp.concatenate([w_in[l][:, PROJ_ORIG:], w_in[l][:, :PROJ_ORIG],
                                  jnp.zeros((d, PROJ_PAD - PROJ_TOTAL), F32)], axis=1).astype(BF16)
        z = _in_proj(xs, mod[l], w_in_l, n_ctx, tm_in, PROJ_PAD // 4)
        ret_o = _retention(z, log_gamma[l], cos_r, sin_r, ret_gn_g[l], ret_gn_b[l], n_ctx)
        hg_o = _hgrn2(z, lower[:, l], hg_norm_g[l], n_ctx)
        na_o = _neighbourhood(z, _na_bias_table(na_rpb[l]), n_ctx)
        qh, kd, vd = _gqa_prep(z, cos_g, sin_g, gq_qn_g[l], gq_kn_g[l], tm)
        gq_o = _gqa_attn(qh, kd, vd, n_ctx, tm)
        wr_hi, wr_lo = _split_bf16(w_router[l].T)
        xs, h2p, top_e, top_g, rank, cnt = _merge(z, (ret_o, hg_o, na_o, gq_o), xs, mod[l],
                                                  w_branch[l].astype(BF16), w_out[l].astype(BF16),
                                                  ln_g[l, 0], ln_b[l, 0], wr_hi, wr_lo, b_router[l], n_ctx, tm)
        yk, gate = _moe(h2p, top_e, top_g, rank, cnt, w_gu[l].astype(BF16), b_gu[l],
                        w_down[l].astype(BF16), b_down[l])
        xs = _combine(yk, gate, xs, mod[l], ln_g[l, 1], ln_b[l, 1], n_ctx, tm)
    return xs[:, n_ctx:]
```

```python
import functools

import jax
import jax.numpy as jnp
from jax import lax
import numpy as np
from jax.experimental import pallas as pl
from jax.experimental.pallas import tpu as pltpu
from jax.experimental.pallas import tpu_sc as plsc

D_MODEL = 1024
DEPTH = 4
GRID_W = 64
N_BRANCH = 4
BRANCH_W = D_MODEL // 2
RET_HEADS = 4
RET_DK = BRANCH_W // RET_HEADS
HG_HEADS = 4
HG_DK = BRANCH_W // HG_HEADS
NA_HEADS = 8
NA_HD = BRANCH_W // NA_HEADS
NA_KH = 8
NA_KW = 16
GQA_HEADS = 8
GQA_KV_HEADS = 2
GQA_HD = BRANCH_W // GQA_HEADS
GQA_KV_W = GQA_KV_HEADS * GQA_HD
ROPE_BASE = 10000.0
N_EXPERTS = 32
TOP_K = 4
D_FF_EXPERT = D_MODEL
SWIGLU_LIMIT = 7.0
SWIGLU_ALPHA = 1.702
MOE_BLOCK = 256
LN_EPS = 1e-5
NORM_EPS = 1e-6
NEG_INF = -1e30
DN_ALPHA = (2 * DEPTH) ** 0.25

LANES = 128
BF16 = jnp.bfloat16
F32 = jnp.float32
VMEM_LIMIT = 56 * 1024 * 1024
SC_CORES = 2
SC_SUBCORES = 16
SC_WINDOW = 128
SC_ROW_WORDS = 256
SC_PIECES = D_MODEL // 2 // SC_ROW_WORDS

PROJ_ORIG = 4 * BRANCH_W + 5 * BRANCH_W + 3 * BRANCH_W + BRANCH_W + 2 * GQA_KV_W
PROJ_TOTAL = PROJ_ORIG + N_BRANCH * D_MODEL
PROJ_PAD = 11264
GATE_B = 0
RET_B = (N_BRANCH * D_MODEL) // LANES
HG_B = RET_B + 16
NA_B = HG_B + 20
GQ_B = NA_B + 12

RET_CHUNK = 256
HG_CHUNK = 128
HG_LEVELS = (64, 32, 16, 8, 4, 2, 1, 0)


def _cparams(sem):
    return pltpu.CompilerParams(dimension_semantics=sem, vmem_limit_bytes=VMEM_LIMIT)


def _dot(a, b):
    return jnp.dot(a, b, preferred_element_type=F32)


def _dot_nt(a, b):
    return lax.dot_general(a, b, (((1,), (1,)), ((), ())), preferred_element_type=F32)


def _dot_tn(a, b):
    return lax.dot_general(a, b, (((0,), (0,)), ((), ())), preferred_element_type=F32)


def _split_bf16(x):
    hi = x.astype(BF16)
    lo = (x - hi.astype(F32)).astype(BF16)
    return hi, lo


def _sigmoid(x):
    return 1.0 / (1.0 + jnp.exp(-x))


def _silu(x):
    return x * _sigmoid(x)


def _mod_kernel(c_ref, w_ref, b_ref, o_ref):
    s = _silu(c_ref[...]).astype(BF16)
    o_ref[0] = _dot(s, w_ref[0].astype(BF16)) + b_ref[0]


def _mod_all(cc, w_mod, b_mod):
    depth, d, n = w_mod.shape
    r = cc.shape[0]
    tn = 1536
    return pl.pallas_call(
        _mod_kernel,
        out_shape=jax.ShapeDtypeStruct((depth, r, n), F32),
        grid=(depth, n // tn),
        in_specs=[pl.BlockSpec((r, d), lambda l, j: (0, 0)),
                  pl.BlockSpec((1, d, tn), lambda l, j: (l, 0, j)),
                  pl.BlockSpec((1, 1, tn), lambda l, j: (l, 0, j))],
        out_specs=pl.BlockSpec((1, r, tn), lambda l, j: (l, 0, j)),
        compiler_params=_cparams(("arbitrary", "arbitrary")),
        name="adaln_mod",
    )(cc, w_mod, b_mod.reshape(depth, 1, n))


def _in_proj_kernel(x_ref, mod_ref, w_ref, o_ref, *, n_ctx, tm):
    i = pl.program_id(2)
    x = x_ref[0]
    mc = mod_ref[0, 0]
    ml = mod_ref[0, 1]
    rows = i * tm + lax.broadcasted_iota(jnp.int32, (tm, 1), 0)
    is_ctx = rows < n_ctx
    scale = jnp.where(is_ctx, mc[0:1, :], ml[0:1, :])
    shift = jnp.where(is_ctx, mc[1:2, :], ml[1:2, :])
    h = (x * scale + shift).astype(BF16)
    o_ref[0] = _dot(h, w_ref[...]).astype(o_ref.dtype)


def _in_proj(xs, mod, w, n_ctx, tm, tn):
    b, n, d = xs.shape
    ncol = w.shape[1]
    return pl.pallas_call(
        functools.partial(_in_proj_kernel, n_ctx=n_ctx, tm=tm),
        out_shape=jax.ShapeDtypeStruct((b, n, ncol), BF16),
        grid=(ncol // tn, b, n // tm),
        in_specs=[pl.BlockSpec((1, tm, d), lambda j, bb, i: (bb, i, 0)),
                  pl.BlockSpec((1, 2, 8, d), lambda j, bb, i: (bb, 0, 0, 0)),
                  pl.BlockSpec((d, tn), lambda j, bb, i: (0, j))],
        out_specs=pl.BlockSpec((1, tm, tn), lambda j, bb, i: (bb, i, j)),
        compiler_params=_cparams(("arbitrary", "arbitrary", "arbitrary")),
        name="in_proj",
    )(xs, mod, w)


def _rope_tables(n_ctx, t, head_dim):
    idx = jnp.arange(t, dtype=jnp.int32)
    row = (idx // GRID_W).astype(F32)
    col = (idx % GRID_W).astype(F32)
    n = head_dim // 2
    inv = ROPE_BASE ** (-jnp.arange(0, n, 2, dtype=F32) / n)

    def half(pos):
        ang = pos[:, None] * inv[None, :]
        c, s = jnp.cos(ang), jnp.sin(ang)
        return jnp.concatenate([c, c], -1), jnp.concatenate([-s, s], -1)

    cr, sr = half(row)
    cc, sc = half(col)
    cos = jnp.concatenate([cr, cc], -1)
    sin = jnp.concatenate([sr, sc], -1)
    reps = LANES // head_dim
    cos = jnp.tile(cos, (1, reps))
    sin = jnp.tile(sin, (1, reps))
    cos = jnp.concatenate([jnp.ones((n_ctx, LANES), F32), cos], 0)
    sin = jnp.concatenate([jnp.zeros((n_ctx, LANES), F32), sin], 0)
    return cos, sin


def _rope(x, cos, sin, quarter):
    lane = lax.broadcasted_iota(jnp.int32, x.shape, 1)
    first = (lane & (2 * quarter - 1)) < quarter
    rot = jnp.where(first, pltpu.roll(x, LANES - quarter, 1), pltpu.roll(x, quarter, 1))
    return x * cos + rot * sin


def _ret_kernel(lg_ref, q_ref, k_ref, v_ref, g_ref, cos_ref, sin_ref, gng_ref, gnb_ref, o_ref,
                ks_ref, sb_ref, *, n_ctx, chunk):
    hh = pl.program_id(1)
    lgf = lg_ref[hh]
    lgb = lg_ref[RET_HEADS + hh]
    n = q_ref.shape[1]
    c_ = chunk
    nch = n // c_
    n_cc = n_ctx // c_
    dk = q_ref.shape[2]

    ri = lax.broadcasted_iota(jnp.int32, (c_, dk), 0).astype(F32)
    df = jnp.exp((ri + 1.0) * lgf)
    db = jnp.exp((c_ - ri) * lgb)
    ef = jnp.exp((c_ - 1.0 - ri) * lgf)
    eb = jnp.exp(ri * lgb)
    gfc = jnp.exp(c_ * lgf)
    gbc = jnp.exp(c_ * lgb)
    ii = lax.broadcasted_iota(jnp.int32, (c_, c_), 0)
    jj = lax.broadcasted_iota(jnp.int32, (c_, c_), 1)
    dij = (ii - jj).astype(F32)
    w = jnp.where(ii >= jj, jnp.exp(dij * lgf), jnp.exp(-dij * lgb))

    def rope_k(c, carry):
        r0 = pl.multiple_of(c * c_, c_)
        kc = k_ref[0, pl.ds(r0, c_), :].astype(F32) * (RET_DK ** -0.5)
        kc = _rope(kc, cos_ref[pl.ds(r0, c_), :], sin_ref[pl.ds(r0, c_), :], 32)
        ks_ref[pl.ds(r0, c_), :] = kc.astype(BF16)
        return carry

    lax.fori_loop(0, nch, rope_k, 0)

    def bwd_step(c, s):
        r0 = pl.multiple_of(c * c_, c_)
        sb_ref[c] = s
        kc = (ks_ref[pl.ds(r0, c_), :].astype(F32) * eb).astype(BF16)
        return s * gbc + _dot_tn(kc, v_ref[0, pl.ds(r0, c_), :])

    s = jnp.zeros((dk, dk), F32)
    s = lax.fori_loop(0, n_cc, lambda t, s_: bwd_step(n_cc - 1 - t, s_), s)
    lax.fori_loop(0, nch - n_cc, lambda t, s_: bwd_step(nch - 1 - t, s_), s)

    gng = gng_ref[...]
    gnb = gnb_ref[...]

    def fwd_step(c, sf):
        r0 = pl.multiple_of(c * c_, c_)
        qc = _rope(q_ref[0, pl.ds(r0, c_), :].astype(F32), cos_ref[pl.ds(r0, c_), :],
                   sin_ref[pl.ds(r0, c_), :], 32)
        kc = ks_ref[pl.ds(r0, c_), :]
        vc = v_ref[0, pl.ds(r0, c_), :]
        a = _dot_nt(qc.astype(BF16), kc) * w
        qi = jnp.concatenate([qc * df, qc * db], axis=1).astype(BF16)
        si = jnp.concatenate([sf, sb_ref[c]], axis=0).astype(BF16)
        o = _dot(a.astype(BF16), vc) + _dot(qi, si)
        sf_new = sf * gfc + _dot_tn((kc.astype(F32) * ef).astype(BF16), vc)
        mu = jnp.mean(o, axis=-1, keepdims=True)
        oc = o - mu
        var = jnp.mean(oc * oc, axis=-1, keepdims=True)
        y = oc * lax.rsqrt(var + LN_EPS) * gng + gnb
        gate = g_ref[0, pl.ds(r0, c_), :].astype(F32)
        o_ref[0, pl.ds(r0, c_), :] = (y * _silu(gate)).astype(o_ref.dtype)
        return sf_new

    lax.fori_loop(0, nch, fwd_step, jnp.zeros((dk, dk), F32))


def _retention(z, log_gamma, cos, sin, gn_g, gn_b, n_ctx):
    b, n, _ = z.shape
    blk = lambda off: pl.BlockSpec((1, n, LANES), lambda bb, h, lg: (bb, 0, off + h))
    grid_spec = pltpu.PrefetchScalarGridSpec(
        num_scalar_prefetch=1,
        grid=(b, RET_HEADS),
        in_specs=[blk(RET_B), blk(RET_B + 4), blk(RET_B + 8), blk(RET_B + 12),
                  pl.BlockSpec((n, LANES), lambda bb, h, lg: (0, 0)),
                  pl.BlockSpec((n, LANES), lambda bb, h, lg: (0, 0)),
                  pl.BlockSpec((1, LANES), lambda bb, h, lg: (0, h)),
                  pl.BlockSpec((1, LANES), lambda bb, h, lg: (0, h))],
        out_specs=pl.BlockSpec((1, n, LANES), lambda bb, h, lg: (bb, 0, h)),
        scratch_shapes=[pltpu.VMEM((n, LANES), BF16),
                        pltpu.VMEM((n // RET_CHUNK, RET_DK, RET_DK), F32)],
    )
    return pl.pallas_call(
        functools.partial(_ret_kernel, n_ctx=n_ctx, chunk=RET_CHUNK),
        out_shape=jax.ShapeDtypeStruct((b, n, BRANCH_W), BF16),
        grid_spec=grid_spec,
        compiler_params=_cparams(("arbitrary", "arbitrary")),
        name="retention",
    )(log_gamma, z, z, z, z, cos, sin, gn_g.reshape(1, -1), gn_b.reshape(1, -1))


def _hg_constants():
    c = HG_CHUNK
    r = np.arange(c)[:, None]
    t = np.arange(c)[None, :]
    mats = []
    level = np.full((2, c, c), len(HG_LEVELS), np.int32)
    for d in range(2):
        blocks = []
        for li, s in enumerate(HG_LEVELS):
            if s == 0:
                level[d][np.arange(c), np.arange(c)] = li
                continue
            base = (r // (2 * s)) * (2 * s)
            if d == 0:
                m = base + s - 1
                mat = (t > np.minimum(r, m)) & (t <= np.maximum(r, m))
                q_side = (r % (2 * s)) >= s
            else:
                m = base + s
                mat = (t >= np.minimum(r, m)) & (t < np.maximum(r, m))
                q_side = (r % (2 * s)) < s
            blocks.append(mat.astype(np.float32))
            same = (r // (2 * s)) == (t // (2 * s))
            k_side_t = ((t % (2 * s)) < s) if d == 0 else ((t % (2 * s)) >= s)
            level[d][same & q_side & k_side_t] = li
        if d == 0:
            blocks.append((t <= r).astype(np.float32))
            blocks.append((t > r).astype(np.float32))
        else:
            blocks.append((t >= r).astype(np.float32))
            blocks.append((t < r).astype(np.float32))
        mats.append(np.concatenate(blocks, 0))
    mats = np.stack(mats)
    masks = (level[:, None] == np.arange(len(HG_LEVELS))[None, :, None, None]).astype(np.float32)
    return np.concatenate([mats, mats], axis=2), masks


def _hg_gates(zf, lb):
    en = jnp.exp(-jnp.abs(zf))
    inv = 1.0 / (1.0 + en)
    pos = zf >= 0
    sg = jnp.where(pos, inv, en * inv)
    sgn = jnp.where(pos, en * inv, inv)
    one_m = 1.0 - lb
    return jnp.log(lb + one_m * sg), one_m * sgn


def _hg_kernel(q_ref, ff_ref, fb_ref, v_ref, g_ref, lb_ref, ng_ref, m_ref, lv_ref, o_ref,
               lfb_ref, kb_ref, sb_ref, *, n_ctx):
    c_ = HG_CHUNK
    n = q_ref.shape[1]
    nch = n // c_
    n_cc = n_ctx // c_
    dk = q_ref.shape[2]
    nl = len(HG_LEVELS)
    lbf = lb_ref[0:1, :]
    lbb = lb_ref[1:2, :]

    def mdot(mat, lf):
        hi, lo = _split_bf16(lf)
        return _dot(mat, jnp.concatenate([hi, lo], axis=0))

    def prep_b(c, carry):
        r0 = pl.multiple_of(c * c_, c_)
        lf, key = _hg_gates(fb_ref[0, pl.ds(r0, c_), :].astype(F32), lbb)
        lfb_ref[pl.ds(r0, c_), :] = lf
        kb_ref[pl.ds(r0, c_), :] = key
        return carry

    lax.fori_loop(0, nch, prep_b, 0)

    def bwd_step(c, st):
        r0 = pl.multiple_of(c * c_, c_)
        sb_ref[c] = st
        lf = lfb_ref[pl.ds(r0, c_), :]
        e = mdot(m_ref[1, (nl - 1) * c_:(nl + 1) * c_, :], lf)
        tot = e[0:1, :]
        kt = (kb_ref[pl.ds(r0, c_), :] * jnp.exp(e[c_:2 * c_, :])).astype(BF16)
        return st * jnp.exp(tot) + _dot_tn(v_ref[0, pl.ds(r0, c_), :], kt)

    st = jnp.zeros((dk, dk), F32)
    st = lax.fori_loop(0, n_cc, lambda t, s_: bwd_step(n_cc - 1 - t, s_), st)
    lax.fori_loop(0, nch - n_cc, lambda t, s_: bwd_step(nch - 1 - t, s_), st)

    ng = ng_ref[...]

    def intra(qs, key, lf, d):
        e = mdot(m_ref[d], lf)
        a = jnp.zeros((c_, c_), F32)
        for li, s in enumerate(HG_LEVELS):
            if s == 0:
                qt, kt = qs, key
            else:
                dec = jnp.exp(e[li * c_:(li + 1) * c_, :])
                qt, kt = qs * dec, key * dec
            a = a + _dot_nt(qt.astype(BF16), kt.astype(BF16)) * lv_ref[d, li]
        cum = e[(nl - 1) * c_:nl * c_, :]
        rest = e[nl * c_:(nl + 1) * c_, :]
        return a, cum, rest

    def fwd_step(c, sf):
        r0 = pl.multiple_of(c * c_, c_)
        qs = _silu(q_ref[0, pl.ds(r0, c_), :].astype(F32))
        vc = v_ref[0, pl.ds(r0, c_), :]
        lff, keyf = _hg_gates(ff_ref[0, pl.ds(r0, c_), :].astype(F32), lbf)
        af, cumf, restf = intra(qs, keyf, lff, 0)
        ab, cumb, _ = intra(qs, kb_ref[pl.ds(r0, c_), :], lfb_ref[pl.ds(r0, c_), :], 1)
        qi = jnp.concatenate([qs * jnp.exp(cumf), qs * jnp.exp(cumb)], axis=1).astype(BF16)
        si = jnp.concatenate([sf, sb_ref[c]], axis=1).astype(BF16)
        o = _dot((af + ab).astype(BF16), vc) + _dot_nt(qi, si)
        totf = cumf[c_ - 1:c_, :]
        sf_new = sf * jnp.exp(totf) + _dot_tn(vc, (keyf * jnp.exp(restf)).astype(BF16))
        on = o * lax.rsqrt(jnp.mean(o * o, axis=-1, keepdims=True) + NORM_EPS) * ng
        gate = g_ref[0, pl.ds(r0, c_), :].astype(F32)
        o_ref[0, pl.ds(r0, c_), :] = (on * _silu(gate)).astype(o_ref.dtype)
        return sf_new

    lax.fori_loop(0, nch, fwd_step, jnp.zeros((dk, dk), F32), unroll=2)


def _hgrn2(z, lower, norm_g, n_ctx):
    b, n, _ = z.shape
    mats, level = _hg_constants()
    mats = jnp.asarray(mats, BF16)
    level = jnp.asarray(level)
    blk = lambda off: pl.BlockSpec((1, n, LANES), lambda bb, h: (bb, 0, off + h))
    return pl.pallas_call(
        functools.partial(_hg_kernel, n_ctx=n_ctx),
        out_shape=jax.ShapeDtypeStruct((b, n, BRANCH_W), BF16),
        grid=(b, HG_HEADS),
        in_specs=[blk(HG_B), blk(HG_B + 4), blk(HG_B + 8), blk(HG_B + 12), blk(HG_B + 16),
                  pl.BlockSpec((2, LANES), lambda bb, h: (0, h)),
                  pl.BlockSpec((1, LANES), lambda bb, h: (0, h)),
                  pl.BlockSpec(mats.shape, lambda bb, h: (0, 0, 0)),
                  pl.BlockSpec(level.shape, lambda bb, h: (0, 0, 0, 0))],
        out_specs=pl.BlockSpec((1, n, LANES), lambda bb, h: (bb, 0, h)),
        scratch_shapes=[pltpu.VMEM((n, LANES), F32), pltpu.VMEM((n, LANES), F32),
                        pltpu.VMEM((n // HG_CHUNK, HG_DK, HG_DK), F32)],
        compiler_params=_cparams(("arbitrary", "arbitrary")),
        name="hgrn2",
    )(z, z, z, z, z, lower, norm_g.reshape(1, -1), mats, level)


def _na_bias_table(rpb):
    qc = np.arange(GRID_W)[:, None]
    kc = np.arange(GRID_W)[None, :]
    col0 = np.clip(qc - NA_KW // 2, 0, GRID_W - NA_KW)
    valid = (kc >= col0) & (kc < col0 + NA_KW)
    dc = np.clip(kc - qc + (NA_KW - 1), 0, 2 * NA_KW - 2)
    pick = (dc[:, :, None] == np.arange(2 * NA_KW - 1)).astype(np.float32)
    t15 = jnp.einsum('hrc,qkc->hrqk', rpb.astype(F32), pick, precision=lax.Precision.HIGHEST)
    t = jnp.stack([t15[:, dl:dl + NA_KH] for dl in range(NA_KH)], axis=1)
    t = jnp.where(jnp.asarray(valid)[None, None, None], t, NEG_INF)
    t = t.transpose(0, 1, 3, 2, 4).reshape(rpb.shape[0] // 2, 2, NA_KH, GRID_W, NA_KH * GRID_W)
    return t.transpose(0, 2, 1, 3, 4).reshape(rpb.shape[0] // 2, NA_KH, 2 * GRID_W, NA_KH * GRID_W)


def _softmax_pv(parts):
    m = parts[0][0].max(axis=-1, keepdims=True)
    for s, _ in parts[1:]:
        m = jnp.maximum(m, s.max(axis=-1, keepdims=True))
    l = None
    o = None
    for s, v in parts:
        p = jnp.exp(s - m)
        ls = p.sum(axis=-1, keepdims=True)
        os_ = _dot(p.astype(BF16), v)
        l = ls if l is None else l + ls
        o = os_ if o is None else o + os_
    return o / l


def _na_kernel(q_ref, k_ref, v_ref, tab_ref, o_ref, *, n_ctx):
    n = q_ref.shape[1]
    rows = (n - n_ctx) // GRID_W
    wlen = NA_KH * GRID_W
    scale = NA_HD ** -0.5
    kx = k_ref[0, 0:n_ctx, :]
    vx = v_ref[0, 0:n_ctx, :]

    def head_mask(shape, hh):
        lane = lax.broadcasted_iota(jnp.int32, shape, 1)
        return (lane < NA_HD) if hh == 0 else (lane >= NA_HD)

    def stack_heads(qf):
        return jnp.concatenate([jnp.where(head_mask(qf.shape, 0), qf, 0.0),
                                jnp.where(head_mask(qf.shape, 1), qf, 0.0)], axis=0).astype(BF16)

    def unstack_heads(o):
        m = o.shape[0] // 2
        return jnp.where(head_mask((m, LANES), 0), o[:m], o[m:])

    qx = stack_heads(q_ref[0, 0:n_ctx, :].astype(F32) * scale)
    o_ref[0, 0:n_ctx, :] = unstack_heads(_softmax_pv([(_dot_nt(qx, kx), vx)])).astype(o_ref.dtype)

    def row_step(r, carry):
        start = jnp.clip(r - NA_KH // 2, 0, rows - NA_KH)
        delta = start - r + (NA_KH - 1)
        q0 = pl.multiple_of(n_ctx + r * GRID_W, GRID_W)
        k0 = pl.multiple_of(n_ctx + start * GRID_W, GRID_W)
        qr = stack_heads(q_ref[0, pl.ds(q0, GRID_W), :].astype(F32) * scale)
        kl = k_ref[0, pl.ds(k0, wlen), :]
        vl = v_ref[0, pl.ds(k0, wlen), :]
        s_loc = _dot_nt(qr, kl) + tab_ref[0, delta]
        s_ctx = _dot_nt(qr, kx)
        o = unstack_heads(_softmax_pv([(s_loc, vl), (s_ctx, vx)]))
        o_ref[0, pl.ds(q0, GRID_W), :] = o.astype(o_ref.dtype)
        return carry

    lax.fori_loop(0, rows, row_step, 0, unroll=2)


def _neighbourhood(z, table, n_ctx):
    b, n, _ = z.shape
    npair = NA_HEADS // 2
    blk = lambda off: pl.BlockSpec((1, n, LANES), lambda bb, p: (bb, 0, off + p))
    return pl.pallas_call(
        functools.partial(_na_kernel, n_ctx=n_ctx),
        out_shape=jax.ShapeDtypeStruct((b, n, BRANCH_W), BF16),
        grid=(b, npair),
        in_specs=[blk(NA_B), blk(NA_B + 4), blk(NA_B + 8),
                  pl.BlockSpec((1, NA_KH, 2 * GRID_W, NA_KH * GRID_W), lambda bb, p: (p, 0, 0, 0))],
        out_specs=pl.BlockSpec((1, n, LANES), lambda bb, p: (bb, 0, p)),
        compiler_params=_cparams(("arbitrary", "arbitrary")),
        name="neighbourhood_attn",
    )(z, z, z, table)


def _gqa_prep_kernel(q_ref, k_ref, v_ref, cos_ref, sin_ref, qg_ref, kg_ref, bd_ref,
                     qo_ref, ko_ref, vo_ref):
    cos = cos_ref[...]
    sin = sin_ref[...]
    bd = bd_ref[...]

    def norm_rope(x, g):
        hi, lo = _split_bf16(x * x)
        ms = _dot(hi, bd) + _dot(lo, bd)
        xn = x * lax.rsqrt(ms + NORM_EPS) * g
        return _rope(xn, cos, sin, GQA_HD // 4)

    qg = qg_ref[...]
    for j in range(BRANCH_W // LANES):
        x = q_ref[0, :, j * LANES:(j + 1) * LANES].astype(F32)
        qo_ref[0, :, j * LANES:(j + 1) * LANES] = (norm_rope(x, qg) * (GQA_HD ** -0.5)).astype(qo_ref.dtype)
    kn = norm_rope(k_ref[0].astype(F32), kg_ref[...])
    vv = v_ref[0].astype(F32)
    lane = lax.broadcasted_iota(jnp.int32, kn.shape, 1)
    lo_half = lane < GQA_HD
    for a, dst in ((kn, ko_ref), (vv, vo_ref)):
        sw = pltpu.roll(a, GQA_HD, 1)
        dst[0, 0] = jnp.where(lo_half, a, sw).astype(dst.dtype)
        dst[0, 1] = jnp.where(lo_half, sw, a).astype(dst.dtype)


def _gqa_prep(z, cos, sin, qn_g, kn_g, tr):
    b, n, _ = z.shape
    bd = np.kron(np.eye(LANES // GQA_HD), np.ones((GQA_HD, GQA_HD))) / GQA_HD
    qg = jnp.tile(qn_g.reshape(1, -1), (1, LANES // GQA_HD))
    kg = jnp.tile(kn_g.reshape(1, -1), (1, LANES // GQA_HD))
    qb = (GQ_B * LANES) // BRANCH_W
    return pl.pallas_call(
        _gqa_prep_kernel,
        out_shape=(jax.ShapeDtypeStruct((b, n, BRANCH_W), BF16),
                   jax.ShapeDtypeStruct((b, GQA_KV_HEADS, n, LANES), BF16),
                   jax.ShapeDtypeStruct((b, GQA_KV_HEADS, n, LANES), BF16)),
        grid=(b, n // tr),
        in_specs=[pl.BlockSpec((1, tr, BRANCH_W), lambda bb, i: (bb, i, qb)),
                  pl.BlockSpec((1, tr, LANES), lambda bb, i: (bb, i, GQ_B + 4)),
                  pl.BlockSpec((1, tr, LANES), lambda bb, i: (bb, i, GQ_B + 5)),
                  pl.BlockSpec((tr, LANES), lambda bb, i: (i, 0)),
                  pl.BlockSpec((tr, LANES), lambda bb, i: (i, 0)),
                  pl.BlockSpec((1, LANES), lambda bb, i: (0, 0)),
                  pl.BlockSpec((1, LANES), lambda bb, i: (0, 0)),
                  pl.BlockSpec((LANES, LANES), lambda bb, i: (0, 0))],
        out_specs=(pl.BlockSpec((1, tr, BRANCH_W), lambda bb, i: (bb, i, 0)),
                   pl.BlockSpec((1, GQA_KV_HEADS, tr, LANES), lambda bb, i: (bb, 0, i, 0)),
                   pl.BlockSpec((1, GQA_KV_HEADS, tr, LANES), lambda bb, i: (bb, 0, i, 0))),
        compiler_params=_cparams(("arbitrary", "arbitrary")),
        name="gqa_prep",
    )(z, z, z, cos, sin, qg, kg, jnp.asarray(bd, BF16))


def _gqa_kernel(q_ref, k_ref, v_ref, o_ref, *, n_ctx):
    i = pl.program_id(1)
    tq = q_ref.shape[1]
    grp = GQA_HEADS // GQA_KV_HEADS

    def attend(nk):
        for j in range(BRANCH_W // LANES):
            qb = q_ref[0, :, j * LANES:(j + 1) * LANES]
            lane = lax.broadcasted_iota(jnp.int32, qb.shape, 1)
            g = (2 * j) // grp
            zero = jnp.zeros_like(qb)
            qs = jnp.concatenate([jnp.where(lane < GQA_HD, qb, zero), jnp.where(lane >= GQA_HD, qb, zero)], axis=0)
            o = _softmax_pv([(_dot_nt(qs, k_ref[0, g, 0:nk, :]), v_ref[0, g, 0:nk, :])])
            o_ref[0, :, j * LANES:(j + 1) * LANES] = jnp.where(lane < GQA_HD, o[:tq], o[tq:]).astype(o_ref.dtype)

    @pl.when(i * tq < n_ctx)
    def _():
        attend(n_ctx)

    @pl.when(i * tq >= n_ctx)
    def _():
        attend(k_ref.shape[2])


def _gqa_attn(qh, kd, vd, n_ctx, tq):
    b, n, _ = qh.shape
    return pl.pallas_call(
        functools.partial(_gqa_kernel, n_ctx=n_ctx),
        out_shape=jax.ShapeDtypeStruct((b, n, BRANCH_W), BF16),
        grid=(b, n // tq),
        in_specs=[pl.BlockSpec((1, tq, BRANCH_W), lambda bb, i: (bb, i, 0)),
                  pl.BlockSpec((1, GQA_KV_HEADS, n, LANES), lambda bb, i: (bb, 0, 0, 0)),
                  pl.BlockSpec((1, GQA_KV_HEADS, n, LANES), lambda bb, i: (bb, 0, 0, 0))],
        out_specs=pl.BlockSpec((1, tq, BRANCH_W), lambda bb, i: (bb, i, 0)),
        compiler_params=_cparams(("arbitrary", "arbitrary")),
        name="gqa_attn",
    )(qh, kd, vd)


def _layernorm_rows(v, g, b):
    mu = jnp.mean(v, axis=-1, keepdims=True)
    vc = v - mu
    var = jnp.mean(vc * vc, axis=-1, keepdims=True)
    return vc * lax.rsqrt(var + LN_EPS) * g + b


def _pack_bf16_pairs(v):
    w = v.shape[1] // 2
    bits = lax.bitcast_convert_type(v.astype(BF16).astype(F32), jnp.int32)
    return bits[:, w:] | lax.shift_right_logical(bits[:, :w], 16)


def _unpack_bf16_pairs(pieces):
    lo = [lax.bitcast_convert_type(lax.shift_left(p, 16), F32) for p in pieces]
    hi = [lax.bitcast_convert_type(p & jnp.int32(-65536), F32) for p in pieces]
    return jnp.concatenate(lo + hi, axis=1)


def _store_pieces(ref, lead, packed):
    for h in range(SC_PIECES):
        ref[(h,) + lead] = packed[:, h * SC_ROW_WORDS:(h + 1) * SC_ROW_WORDS]


def _merge_kernel(z_ref, r_ref, h_ref, a_ref, q_ref, x_ref, mod_ref, wb_ref, wo_ref, lng_ref, lnb_ref,
                  wrh_ref, wrl_ref, br_ref, tri_ref, xo_ref, h2_ref, te_ref, tg_ref, rk_ref, cnt_ref,
                  run_ref, *, n_ctx, tm):
    i = pl.program_id(1)

    @pl.when((pl.program_id(0) == 0) & (i == 0))
    def _():
        run_ref[...] = jnp.zeros_like(run_ref)

    is_ctx = i * tm < n_ctx
    mod = jnp.where(is_ctx, mod_ref[0, 0], mod_ref[0, 1])
    acc = None
    for nb, br in enumerate((r_ref, h_ref, a_ref, q_ref)):
        yb = _dot(br[0], wb_ref[nb])
        gt = _sigmoid(z_ref[0, :, nb * D_MODEL:(nb + 1) * D_MODEL].astype(F32))
        acc = gt * yb if acc is None else acc + gt * yb
    y = _dot(acc.astype(BF16), wo_ref[...])
    xn = _layernorm_rows(DN_ALPHA * x_ref[0] + mod[2:3, :] * y, lng_ref[...], lnb_ref[...])
    xo_ref[0] = xn
    h2 = xn * mod[3:4, :] + mod[4:5, :]
    h2_hi, h2_lo = _split_bf16(h2)
    _store_pieces(h2_ref, (0,), _pack_bf16_pairs(h2))
    wrh = wrh_ref[...]
    logit = _dot_nt(wrh, h2_hi) + _dot_nt(wrh, h2_lo) + _dot_nt(wrl_ref[...], h2_hi) + br_ref[...]
    eidx = lax.broadcasted_iota(jnp.int32, logit.shape, 0)
    vals, idxs = [], []
    for _ in range(TOP_K):
        m = logit.max(axis=0, keepdims=True)
        sel = jnp.where(logit == m, eidx, N_EXPERTS).min(axis=0, keepdims=True)
        vals.append(m)
        idxs.append(sel)
        logit = jnp.where(eidx == sel, -jnp.inf, logit)
    ex = [jnp.exp(v - vals[0]) for v in vals]
    tot = ex[0] + ex[1] + ex[2] + ex[3]
    te_ref[0] = jnp.concatenate(idxs, axis=0)
    tg_ref[0] = jnp.concatenate([e / tot for e in ex], axis=0)
    base = run_ref[:, 0:1]
    tri = tri_ref[...]
    ranks = []
    for sel in idxs:
        hit = eidx == sel
        ones = jnp.where(hit, 1.0, 0.0)
        pre = _dot(ones.astype(BF16), tri)
        ranks.append(jnp.sum(jnp.where(hit, pre + base, 0.0), axis=0, keepdims=True))
        base = base + jnp.sum(ones, axis=1, keepdims=True)
    rk_ref[0] = jnp.concatenate(ranks, axis=0).astype(jnp.int32)
    run_ref[...] = jnp.broadcast_to(base, run_ref.shape)
    cnt_ref[...] = run_ref[...]


def _merge(z, outs, xs, mod, wb, wo, ln_g, ln_b, wr_hi, wr_lo, b_r, n_ctx, tm):
    b, n, d = xs.shape
    row = lambda w: pl.BlockSpec((1, tm, w), lambda bb, i: (bb, i, 0))
    full = lambda a: pl.BlockSpec(a.shape, lambda bb, i: (0,) * a.ndim)
    topk = pl.BlockSpec((1, TOP_K, tm), lambda bb, i: (bb, 0, i))
    ln_g = ln_g.reshape(1, d)
    ln_b = ln_b.reshape(1, d)
    b_r = b_r.reshape(N_EXPERTS, 1)
    tri = jnp.asarray(np.triu(np.ones((tm, tm), np.float32), 1), BF16)
    return pl.pallas_call(
        functools.partial(_merge_kernel, n_ctx=n_ctx, tm=tm),
        out_shape=(jax.ShapeDtypeStruct((b, n, d), F32),
                   jax.ShapeDtypeStruct((SC_PIECES, b, n, SC_ROW_WORDS), jnp.int32),
                   jax.ShapeDtypeStruct((b, TOP_K, n), jnp.int32),
                   jax.ShapeDtypeStruct((b, TOP_K, n), F32),
                   jax.ShapeDtypeStruct((b, TOP_K, n), jnp.int32),
                   jax.ShapeDtypeStruct((N_EXPERTS, LANES), F32)),
        grid=(b, n // tm),
        in_specs=[row(N_BRANCH * D_MODEL), row(BRANCH_W), row(BRANCH_W), row(BRANCH_W), row(BRANCH_W),
                  row(d),
                  pl.BlockSpec((1, 2, 8, d), lambda bb, i: (bb, 0, 0, 0)),
                  full(wb), full(wo), full(ln_g), full(ln_b), full(wr_hi), full(wr_lo), full(b_r), full(tri)],
        out_specs=(row(d), pl.BlockSpec((SC_PIECES, 1, tm, SC_ROW_WORDS), lambda bb, i: (0, bb, i, 0)),
                   topk, topk, topk,
                   pl.BlockSpec((N_EXPERTS, LANES), lambda bb, i: (0, 0))),
        scratch_shapes=[pltpu.VMEM((N_EXPERTS, LANES), F32)],
        compiler_params=_cparams(("arbitrary", "arbitrary")),
        name="merge_ln_router",
    )(z, *outs, xs, mod, wb, wo, ln_g, ln_b, wr_hi, wr_lo, b_r, tri)


def _ffn_kernel(be_ref, nv_ref, x_ref, wgu_ref, bgu_ref, wd_ref, bd_ref, o_ref):
    i = pl.program_id(0)
    nv = nv_ref[i]

    @pl.when(nv > 0)
    def _():
        rows = lax.broadcasted_iota(jnp.int32, (MOE_BLOCK, 1), 0)
        x = _unpack_bf16_pairs([x_ref[h] for h in range(SC_PIECES)])
        x = jnp.where(rows < nv, x, 0.0).astype(BF16)
        gu = _dot(x, wgu_ref[0]) + bgu_ref[0]
        g = jnp.minimum(gu[:, :D_FF_EXPERT], SWIGLU_LIMIT)
        u = jnp.clip(gu[:, D_FF_EXPERT:], -SWIGLU_LIMIT, SWIGLU_LIMIT)
        act = g * _sigmoid(SWIGLU_ALPHA * g) * (u + 1.0)
        _store_pieces(o_ref, (), _pack_bf16_pairs(_dot(act.astype(BF16), wd_ref[0]) + bd_ref[0]))

    @pl.when(nv <= 0)
    def _():
        o_ref[...] = jnp.zeros_like(o_ref)


def _expert_ffn(xb, block_e, n_valid, wgu, bgu, wd, bd):
    _, n_slots, _ = xb.shape
    n_blocks = n_slots // MOE_BLOCK
    ne, d, f2 = wgu.shape
    slot_blk = pl.BlockSpec((SC_PIECES, MOE_BLOCK, SC_ROW_WORDS), lambda i, be, nv: (0, i, 0))
    grid_spec = pltpu.PrefetchScalarGridSpec(
        num_scalar_prefetch=2,
        grid=(n_blocks,),
        in_specs=[slot_blk,
                  pl.BlockSpec((1, d, f2), lambda i, be, nv: (be[i], 0, 0)),
                  pl.BlockSpec((1, 1, f2), lambda i, be, nv: (be[i], 0, 0)),
                  pl.BlockSpec((1, f2 // 2, d), lambda i, be, nv: (be[i], 0, 0)),
                  pl.BlockSpec((1, 1, d), lambda i, be, nv: (be[i], 0, 0))],
        out_specs=slot_blk,
    )
    return pl.pallas_call(
        _ffn_kernel,
        out_shape=jax.ShapeDtypeStruct(xb.shape, jnp.int32),
        grid_spec=grid_spec,
        compiler_params=_cparams(("arbitrary",)),
        name="expert_ffn",
    )(block_e, n_valid, xb, wgu, bgu.reshape(ne, 1, f2), wd, bd.reshape(ne, 1, d))


def _combine_kernel(y_ref, g_ref, x_ref, mod_ref, lng_ref, lnb_ref, xo_ref, *, n_ctx, tm):
    i = pl.program_id(1)
    is_ctx = i * tm < n_ctx
    g2 = jnp.where(is_ctx, mod_ref[0, 0, 5:6, :], mod_ref[0, 1, 5:6, :])
    gate = g_ref[...]
    y = None
    for k in range(TOP_K):
        t = gate[:, k:k + 1] * _unpack_bf16_pairs([y_ref[k, h] for h in range(SC_PIECES)])
        y = t if y is None else y + t
    xo_ref[0] = _layernorm_rows(DN_ALPHA * x_ref[0] + g2 * y, lng_ref[...], lnb_ref[...])


def _combine(yk, gate, xs, mod, ln_g, ln_b, n_ctx, tm):
    b, n, d = xs.shape
    nt = n // tm
    return pl.pallas_call(
        functools.partial(_combine_kernel, n_ctx=n_ctx, tm=tm),
        out_shape=jax.ShapeDtypeStruct((b, n, d), F32),
        grid=(b, nt),
        in_specs=[pl.BlockSpec((TOP_K, SC_PIECES, tm, SC_ROW_WORDS), lambda bb, i: (0, 0, bb * nt + i, 0)),
                  pl.BlockSpec((tm, TOP_K), lambda bb, i: (bb * nt + i, 0)),
                  pl.BlockSpec((1, tm, d), lambda bb, i: (bb, i, 0)),
                  pl.BlockSpec((1, 2, 8, d), lambda bb, i: (bb, 0, 0, 0)),
                  pl.BlockSpec((1, d), lambda bb, i: (0, 0)),
                  pl.BlockSpec((1, d), lambda bb, i: (0, 0))],
        out_specs=pl.BlockSpec((1, tm, d), lambda bb, i: (bb, i, 0)),
        compiler_params=_cparams(("arbitrary", "arbitrary")),
        name="moe_combine_ln",
    )(yk, gate, xs, mod, ln_g.reshape(1, d), ln_b.reshape(1, d))


def _sc_mesh():
    return plsc.VectorSubcoreMesh(core_axis_name="core", subcore_axis_name="subcore",
                                  num_cores=SC_CORES, num_subcores=SC_SUBCORES)


def _sc_piece_index(idx, rows):
    return jnp.concatenate([idx + h * rows for h in range(SC_PIECES)], axis=1)


def _sc_scatter_rows(x, dest, n_out):
    p, n, w = x.shape
    x2 = x.reshape(p * n, w)
    dest2 = _sc_piece_index(dest, n_out)
    kk, n2 = dest2.shape

    @functools.partial(pl.kernel, out_type=jax.ShapeDtypeStruct((p * n_out, w), x.dtype),
                       mesh=_sc_mesh(), scratch_types=[], name="moe_dispatch_scatter")
    def scatter(x_hbm, i_hbm, o_hbm):
        def body(x_vmem, i_vmem):
            for j in range(kk):
                pltpu.sync_copy(x_vmem, o_hbm.at[i_vmem.at[j]])

        pltpu.emit_pipeline(
            body, grid=(n2 // SC_WINDOW,),
            in_specs=[pl.BlockSpec((SC_WINDOW, SC_ROW_WORDS), lambda i: (i, 0)),
                      pl.BlockSpec((kk, SC_WINDOW), lambda i: (0, i))],
            out_specs=[], core_axis_name=("core", "subcore"),
            dimension_semantics=(pltpu.PARALLEL,))(x_hbm, i_hbm)

    return scatter(x2, dest2).reshape(p, n_out, w)


def _sc_gather_rows(table, idx):
    p, v, w = table.shape
    kk, n = idx.shape
    t2 = table.reshape(p * v, w)
    idx2 = _sc_piece_index(idx, v)
    m = kk * n * p

    @functools.partial(pl.kernel, out_type=jax.ShapeDtypeStruct((m, SC_ROW_WORDS), table.dtype),
                       mesh=_sc_mesh(), scratch_types=[], name="moe_combine_gather")
    def gather(t_hbm, i_hbm, o_hbm):
        def body(i_vmem, o_vmem):
            pltpu.sync_copy(t_hbm.at[i_vmem.at[0]], o_vmem)

        pltpu.emit_pipeline(
            body, grid=(m // SC_WINDOW,),
            in_specs=[pl.BlockSpec((1, SC_WINDOW), lambda i: (0, i))],
            out_specs=[pl.BlockSpec((SC_WINDOW, SC_ROW_WORDS), lambda i: (i, 0))],
            core_axis_name=("core", "subcore"),
            dimension_semantics=(pltpu.PARALLEL,))(i_hbm, o_hbm)

    return gather(t2, idx2.reshape(1, m)).reshape(kk, p, n, w)


def _moe(h2p, top_e, top_g, rank, cnt, wgu, bgu, wd, bd):
    p, b, n, w = h2p.shape
    n_tok = b * n
    nk = n_tok * TOP_K
    experts = jnp.arange(N_EXPERTS, dtype=jnp.int32)
    counts = cnt[:, 0].astype(jnp.int32)
    padded = (counts + MOE_BLOCK - 1) // MOE_BLOCK * MOE_BLOCK
    ends_p = jnp.cumsum(padded)
    start_p = ends_p - padded
    n_blocks = (nk + N_EXPERTS * (MOE_BLOCK - 1) + MOE_BLOCK - 1) // MOE_BLOCK
    block_start = jnp.arange(n_blocks, dtype=jnp.int32) * MOE_BLOCK
    block_e = jnp.minimum(jnp.sum(ends_p[None, :] <= block_start[:, None], axis=1), N_EXPERTS - 1).astype(jnp.int32)
    is_e = block_e[:, None] == experts[None, :]
    filled = jnp.sum(jnp.where(is_e, (start_p + counts)[None, :], 0), axis=1)
    n_valid = jnp.clip(filled - block_start, 0, MOE_BLOCK).astype(jnp.int32)
    dest = rank + jnp.sum(jnp.where(top_e[..., None] == experts, start_p, 0), axis=-1)
    dest = dest.transpose(1, 0, 2).reshape(TOP_K, n_tok)
    gate = top_g.transpose(0, 2, 1).reshape(n_tok, TOP_K)
    xb = _sc_scatter_rows(h2p.reshape(p, n_tok, w), dest, n_blocks * MOE_BLOCK)
    yb = _expert_ffn(xb, block_e, n_valid, wgu, bgu, wd, bd)
    yk = _sc_gather_rows(yb, dest)
    return yk, gate


def kernel(x, c, ctx, c_ctx, w_mod, b_mod, w_in, ret_decay, ret_gn_g, ret_gn_b, hg_lb, hg_norm_g,
           na_rpb, gq_qn_g, gq_kn_g, w_branch, w_out, ln_g, ln_b, w_router, b_router, w_gu, b_gu,
           w_down, b_down):
    b_, t_, d = x.shape
    n_ctx = ctx.shape[1]
    depth = w_mod.shape[0]
    tm = 256

    sm = jax.nn.softmax(hg_lb.astype(F32), axis=1)
    lower = jnp.cumsum(sm, axis=1) - sm[:, :1]
    log_gamma = jax.nn.log_sigmoid(ret_decay.astype(F32)).reshape(depth, 2 * RET_HEADS)

    cc = jnp.concatenate([c, c_ctx[None, :], jnp.zeros((16 - b_ - 1, d), F32)], axis=0)
    modv = _mod_all(cc, w_mod, b_mod).reshape(depth, 16, 6, d)
    one = jnp.asarray([0.0, 1.0, 0.0, 0.0, 1.0, 0.0], F32)[None, None, :, None]
    modv = (modv + one)[:, :, jnp.asarray([1, 0, 2, 4, 3, 5])]
    modv = jnp.concatenate([modv, jnp.zeros((depth, 16, 2, d), F32)], axis=2)
    mod = jnp.stack([jnp.broadcast_to(modv[:, b_:b_ + 1], (depth, b_, 8, d)), modv[:, :b_]], axis=2)

    cos_r, sin_r = _rope_tables(n_ctx, t_, RET_DK)
    cos_g, sin_g = _rope_tables(n_ctx, t_, GQA_HD)

    xs = jnp.concatenate([ctx, x], axis=1)
    tm_in = 768 if xs.shape[1] % 768 == 0 else tm
    for l in range(depth):
        w_in_l = jnp.concatenate([w_in[l][:, PROJ_ORIG:], w_in[l][:, :PROJ_ORIG],
                                  jnp.zeros((d, PROJ_PAD - PROJ_TOTAL), F32)], axis=1).astype(BF16)
        z = _in_proj(xs, mod[l], w_in_l, n_ctx, tm_in, PROJ_PAD // 4)
        ret_o = _retention(z, log_gamma[l], cos_r, sin_r, ret_gn_g[l], ret_gn_b[l], n_ctx)
        hg_o = _hgrn2(z, lower[:, l], hg_norm_g[l], n_ctx)
        na_o = _neighbourhood(z, _na_bias_table(na_rpb[l]), n_ctx)
        qh, kd, vd = _gqa_prep(z, cos_g, sin_g, gq_qn_g[l], gq_kn_g[l], tm)
        gq_o = _gqa_attn(qh, kd, vd, n_ctx, tm)
        wr_hi, wr_lo = _split_bf16(w_router[l].T)
        xs, h2p, top_e, top_g, rank, cnt = _merge(z, (ret_o, hg_o, na_o, gq_o), xs, mod[l],
                                                  w_branch[l].astype(BF16), w_out[l].astype(BF16),
                                                  ln_g[l, 0], ln_b[l, 0], wr_hi, wr_lo, b_router[l], n_ctx, tm)
        yk, gate = _moe(h2p, top_e, top_g, rank, cnt, w_gu[l].astype(BF16), b_gu[l],
                        w_down[l].astype(BF16), b_down[l])
        xs = _combine(yk, gate, xs, mod[l], ln_g[l, 1], ln_b[l, 1], n_ctx, tm)
    return xs[:, n_ctx:]
```

```python
import functools

import jax
import jax.numpy as jnp
from jax import lax
import numpy as np
from jax.experimental import pallas as pl
from jax.experimental.pallas import tpu as pltpu
from jax.experimental.pallas import tpu_sc as plsc

D_MODEL = 1024
DEPTH = 4
GRID_W = 64
N_BRANCH = 4
BRANCH_W = D_MODEL // 2
RET_HEADS = 4
RET_DK = BRANCH_W // RET_HEADS
HG_HEADS = 4
HG_DK = BRANCH_W // HG_HEADS
NA_HEADS = 8
NA_HD = BRANCH_W // NA_HEADS
NA_KH = 8
NA_KW = 16
GQA_HEADS = 8
GQA_KV_HEADS = 2
GQA_HD = BRANCH_W // GQA_HEADS
GQA_KV_W = GQA_KV_HEADS * GQA_HD
ROPE_BASE = 10000.0
N_EXPERTS = 32
TOP_K = 4
D_FF_EXPERT = D_MODEL
SWIGLU_LIMIT = 7.0
SWIGLU_ALPHA = 1.702
MOE_BLOCK = 256
LN_EPS = 1e-5
NORM_EPS = 1e-6
NEG_INF = -1e30
DN_ALPHA = (2 * DEPTH) ** 0.25

LANES = 128
BF16 = jnp.bfloat16
F32 = jnp.float32
VMEM_LIMIT = 56 * 1024 * 1024
SC_CORES = 2
SC_SUBCORES = 16
SC_WINDOW = 128
SC_ROW_WORDS = 256
SC_PIECES = D_MODEL // 2 // SC_ROW_WORDS

PROJ_ORIG = 4 * BRANCH_W + 5 * BRANCH_W + 3 * BRANCH_W + BRANCH_W + 2 * GQA_KV_W
PROJ_TOTAL = PROJ_ORIG + N_BRANCH * D_MODEL
PROJ_PAD = 11264
GATE_B = 0
RET_B = (N_BRANCH * D_MODEL) // LANES
HG_B = RET_B + 16
NA_B = HG_B + 20
GQ_B = NA_B + 12

RET_CHUNK = 256
HG_CHUNK = 128
HG_LEVELS = (64, 32, 16, 8, 4, 2, 1, 0)


def _cparams(sem):
    return pltpu.CompilerParams(dimension_semantics=sem, vmem_limit_bytes=VMEM_LIMIT)


def _dot(a, b):
    return jnp.dot(a, b, preferred_element_type=F32)


def _dot_nt(a, b):
    return lax.dot_general(a, b, (((1,), (1,)), ((), ())), preferred_element_type=F32)


def _dot_tn(a, b):
    return lax.dot_general(a, b, (((0,), (0,)), ((), ())), preferred_element_type=F32)


def _split_bf16(x):
    hi = x.astype(BF16)
    lo = (x - hi.astype(F32)).astype(BF16)
    return hi, lo


def _sigmoid(x):
    return 1.0 / (1.0 + jnp.exp(-x))


def _silu(x):
    return x * _sigmoid(x)


def _mod_kernel(c_ref, w_ref, b_ref, o_ref):
    s = _silu(c_ref[...]).astype(BF16)
    o_ref[0] = _dot(s, w_ref[0].astype(BF16)) + b_ref[0]


def _mod_all(cc, w_mod, b_mod):
    depth, d, n = w_mod.shape
    r = cc.shape[0]
    tn = 1536
    return pl.pallas_call(
        _mod_kernel,
        out_shape=jax.ShapeDtypeStruct((depth, r, n), F32),
        grid=(depth, n // tn),
        in_specs=[pl.BlockSpec((r, d), lambda l, j: (0, 0)),
                  pl.BlockSpec((1, d, tn), lambda l, j: (l, 0, j)),
                  pl.BlockSpec((1, 1, tn), lambda l, j: (l, 0, j))],
        out_specs=pl.BlockSpec((1, r, tn), lambda l, j: (l, 0, j)),
        compiler_params=_cparams(("arbitrary", "arbitrary")),
        name="adaln_mod",
    )(cc, w_mod, b_mod.reshape(depth, 1, n))


W_PREP_TILE = 256


def _w_in_prep_kernel(w_ref, o_ref, *, n_src):
    j = pl.program_id(1)
    o_ref[0] = jnp.where(j < n_src, w_ref[0], 0.0).astype(BF16)


def _w_in_prep(w_in):
    depth, d, total = w_in.shape
    n_src = total // W_PREP_TILE
    shift = PROJ_ORIG // W_PREP_TILE
    src = lambda l, j: (l, 0, jnp.where(j < n_src, (j + shift) % n_src, 0))
    return pl.pallas_call(
        functools.partial(_w_in_prep_kernel, n_src=n_src),
        out_shape=jax.ShapeDtypeStruct((depth, d, PROJ_PAD), BF16),
        grid=(depth, PROJ_PAD // W_PREP_TILE),
        in_specs=[pl.BlockSpec((1, d, W_PREP_TILE), src)],
        out_specs=pl.BlockSpec((1, d, W_PREP_TILE), lambda l, j: (l, 0, j)),
        compiler_params=_cparams(("arbitrary", "arbitrary")),
        name="w_in_prep",
    )(w_in)


def _in_proj_kernel(x_ref, mod_ref, w_ref, o_ref, *, n_ctx, tm):
    i = pl.program_id(2)
    x = x_ref[0]
    mc = mod_ref[0, 0]
    ml = mod_ref[0, 1]
    rows = i * tm + lax.broadcasted_iota(jnp.int32, (tm, 1), 0)
    is_ctx = rows < n_ctx
    scale = jnp.where(is_ctx, mc[0:1, :], ml[0:1, :])
    shift = jnp.where(is_ctx, mc[1:2, :], ml[1:2, :])
    h = (x * scale + shift).astype(BF16)
    o_ref[0] = _dot(h, w_ref[...]).astype(o_ref.dtype)


def _in_proj(xs, mod, w, n_ctx, tm, tn):
    b, n, d = xs.shape
    ncol = w.shape[1]
    return pl.pallas_call(
        functools.partial(_in_proj_kernel, n_ctx=n_ctx, tm=tm),
        out_shape=jax.ShapeDtypeStruct((b, n, ncol), BF16),
        grid=(ncol // tn, b, n // tm),
        in_specs=[pl.BlockSpec((1, tm, d), lambda j, bb, i: (bb, i, 0)),
                  pl.BlockSpec((1, 2, 8, d), lambda j, bb, i: (bb, 0, 0, 0)),
                  pl.BlockSpec((d, tn), lambda j, bb, i: (0, j))],
        out_specs=pl.BlockSpec((1, tm, tn), lambda j, bb, i: (bb, i, j)),
        compiler_params=_cparams(("arbitrary", "arbitrary", "arbitrary")),
        name="in_proj",
    )(xs, mod, w)


def _rope_tables(n_ctx, t, head_dim):
    idx = jnp.arange(t, dtype=jnp.int32)
    row = (idx // GRID_W).astype(F32)
    col = (idx % GRID_W).astype(F32)
    n = head_dim // 2
    inv = ROPE_BASE ** (-jnp.arange(0, n, 2, dtype=F32) / n)

    def half(pos):
        ang = pos[:, None] * inv[None, :]
        c, s = jnp.cos(ang), jnp.sin(ang)
        return jnp.concatenate([c, c], -1), jnp.concatenate([-s, s], -1)

    cr, sr = half(row)
    cc, sc = half(col)
    cos = jnp.concatenate([cr, cc], -1)
    sin = jnp.concatenate([sr, sc], -1)
    reps = LANES // head_dim
    cos = jnp.tile(cos, (1, reps))
    sin = jnp.tile(sin, (1, reps))
    cos = jnp.concatenate([jnp.ones((n_ctx, LANES), F32), cos], 0)
    sin = jnp.concatenate([jnp.zeros((n_ctx, LANES), F32), sin], 0)
    return cos, sin


def _rope(x, cos, sin, quarter):
    lane = lax.broadcasted_iota(jnp.int32, x.shape, 1)
    first = (lane & (2 * quarter - 1)) < quarter
    rot = jnp.where(first, pltpu.roll(x, LANES - quarter, 1), pltpu.roll(x, quarter, 1))
    return x * cos + rot * sin


def _ret_kernel(lg_ref, q_ref, k_ref, v_ref, g_ref, cos_ref, sin_ref, gng_ref, gnb_ref, o_ref,
                ks_ref, sb_ref, *, n_ctx, chunk):
    hh = pl.program_id(1)
    lgf = lg_ref[hh]
    lgb = lg_ref[RET_HEADS + hh]
    n = q_ref.shape[1]
    c_ = chunk
    nch = n // c_
    n_cc = n_ctx // c_
    dk = q_ref.shape[2]

    ri = lax.broadcasted_iota(jnp.int32, (c_, dk), 0).astype(F32)
    df = jnp.exp((ri + 1.0) * lgf)
    db = jnp.exp((c_ - ri) * lgb)
    ef = jnp.exp((c_ - 1.0 - ri) * lgf)
    eb = jnp.exp(ri * lgb)
    gfc = jnp.exp(c_ * lgf)
    gbc = jnp.exp(c_ * lgb)
    ii = lax.broadcasted_iota(jnp.int32, (c_, c_), 0)
    jj = lax.broadcasted_iota(jnp.int32, (c_, c_), 1)
    dij = (ii - jj).astype(F32)
    w = jnp.where(ii >= jj, jnp.exp(dij * lgf), jnp.exp(-dij * lgb))

    def rope_k(c, carry):
        r0 = pl.multiple_of(c * c_, c_)
        kc = k_ref[0, pl.ds(r0, c_), :].astype(F32) * (RET_DK ** -0.5)
        kc = _rope(kc, cos_ref[pl.ds(r0, c_), :], sin_ref[pl.ds(r0, c_), :], 32)
        ks_ref[pl.ds(r0, c_), :] = kc.astype(BF16)
        return carry

    lax.fori_loop(0, nch, rope_k, 0, unroll=3)

    def bwd_step(c, s):
        r0 = pl.multiple_of(c * c_, c_)
        sb_ref[c] = s
        kc = (ks_ref[pl.ds(r0, c_), :].astype(F32) * eb).astype(BF16)
        return s * gbc + _dot_tn(kc, v_ref[0, pl.ds(r0, c_), :])

    s = jnp.zeros((dk, dk), F32)
    s = lax.fori_loop(0, n_cc, lambda t, s_: bwd_step(n_cc - 1 - t, s_), s)
    lax.fori_loop(0, nch - n_cc, lambda t, s_: bwd_step(nch - 1 - t, s_), s, unroll=4)

    gng = gng_ref[...]
    gnb = gnb_ref[...]

    def fwd_step(c, sf):
        r0 = pl.multiple_of(c * c_, c_)
        qc = _rope(q_ref[0, pl.ds(r0, c_), :].astype(F32), cos_ref[pl.ds(r0, c_), :],
                   sin_ref[pl.ds(r0, c_), :], 32)
        kc = ks_ref[pl.ds(r0, c_), :]
        vc = v_ref[0, pl.ds(r0, c_), :]
        a = _dot_nt(qc.astype(BF16), kc) * w
        qi = jnp.concatenate([qc * df, qc * db], axis=1).astype(BF16)
        si = jnp.concatenate([sf, sb_ref[c]], axis=0).astype(BF16)
        o = _dot(a.astype(BF16), vc) + _dot(qi, si)
        sf_new = sf * gfc + _dot_tn((kc.astype(F32) * ef).astype(BF16), vc)
        mu = jnp.mean(o, axis=-1, keepdims=True)
        oc = o - mu
        var = jnp.mean(oc * oc, axis=-1, keepdims=True)
        y = oc * lax.rsqrt(var + LN_EPS) * gng + gnb
        gate = g_ref[0, pl.ds(r0, c_), :].astype(F32)
        o_ref[0, pl.ds(r0, c_), :] = (y * _silu(gate)).astype(o_ref.dtype)
        return sf_new

    lax.fori_loop(0, nch, fwd_step, jnp.zeros((dk, dk), F32), unroll=3)


def _retention(z, log_gamma, cos, sin, gn_g, gn_b, n_ctx):
    b, n, _ = z.shape
    blk = lambda off: pl.BlockSpec((1, n, LANES), lambda bb, h, lg: (bb, 0, off + h))
    grid_spec = pltpu.PrefetchScalarGridSpec(
        num_scalar_prefetch=1,
        grid=(b, RET_HEADS),
        in_specs=[blk(RET_B), blk(RET_B + 4), blk(RET_B + 8), blk(RET_B + 12),
                  pl.BlockSpec((n, LANES), lambda bb, h, lg: (0, 0)),
                  pl.BlockSpec((n, LANES), lambda bb, h, lg: (0, 0)),
                  pl.BlockSpec((1, LANES), lambda bb, h, lg: (0, h)),
                  pl.BlockSpec((1, LANES), lambda bb, h, lg: (0, h))],
        out_specs=pl.BlockSpec((1, n, LANES), lambda bb, h, lg: (bb, 0, h)),
        scratch_shapes=[pltpu.VMEM((n, LANES), BF16),
                        pltpu.VMEM((n // RET_CHUNK, RET_DK, RET_DK), F32)],
    )
    return pl.pallas_call(
        functools.partial(_ret_kernel, n_ctx=n_ctx, chunk=RET_CHUNK),
        out_shape=jax.ShapeDtypeStruct((b, n, BRANCH_W), BF16),
        grid_spec=grid_spec,
        compiler_params=_cparams(("arbitrary", "arbitrary")),
        name="retention",
    )(log_gamma, z, z, z, z, cos, sin, gn_g.reshape(1, -1), gn_b.reshape(1, -1))


def _hg_constants():
    c = HG_CHUNK
    r = np.arange(c)[:, None]
    t = np.arange(c)[None, :]
    mats = []
    level = np.full((2, c, c), len(HG_LEVELS), np.int32)
    for d in range(2):
        blocks = []
        for li, s in enumerate(HG_LEVELS):
            if s == 0:
                level[d][np.arange(c), np.arange(c)] = li
                continue
            base = (r // (2 * s)) * (2 * s)
            if d == 0:
                m = base + s - 1
                mat = (t > np.minimum(r, m)) & (t <= np.maximum(r, m))
                q_side = (r % (2 * s)) >= s
            else:
                m = base + s
                mat = (t >= np.minimum(r, m)) & (t < np.maximum(r, m))
                q_side = (r % (2 * s)) < s
            blocks.append(mat.astype(np.float32))
            same = (r // (2 * s)) == (t // (2 * s))
            k_side_t = ((t % (2 * s)) < s) if d == 0 else ((t % (2 * s)) >= s)
            level[d][same & q_side & k_side_t] = li
        if d == 0:
            blocks.append((t <= r).astype(np.float32))
            blocks.append((t > r).astype(np.float32))
        else:
            blocks.append((t >= r).astype(np.float32))
            blocks.append((t < r).astype(np.float32))
        mats.append(np.concatenate(blocks, 0))
    mats = np.stack(mats)
    masks = (level[:, None] == np.arange(len(HG_LEVELS))[None, :, None, None]).astype(np.float32)
    upper = np.stack([np.broadcast_to(((np.arange(c) % (2 * s)) >= s)[:, None], (c, c)) for s in HG_LEVELS if s > 0])
    return np.concatenate([mats, mats], axis=2), np.maximum(masks[0], masks[1]), upper.astype(np.float32)


def _hg_gates(zf, lb):
    en = jnp.exp(-jnp.abs(zf))
    inv = 1.0 / (1.0 + en)
    pos = zf >= 0
    sg = jnp.where(pos, inv, en * inv)
    sgn = jnp.where(pos, en * inv, inv)
    one_m = 1.0 - lb
    return jnp.log(lb + one_m * sg), one_m * sgn


def _hg_kernel(q_ref, ff_ref, fb_ref, v_ref, g_ref, lb_ref, ng_ref, m_ref, lv_ref, up_ref, o_ref,
               lfb_ref, kb_ref, sb_ref, ktb_ref, dtot_ref, *, n_ctx):
    c_ = HG_CHUNK
    n = q_ref.shape[1]
    nch = n // c_
    n_cc = n_ctx // c_
    dk = q_ref.shape[2]
    nl = len(HG_LEVELS)
    lbf = lb_ref[0:1, :]
    lbb = lb_ref[1:2, :]

    def mdot(mat, lf):
        hi, lo = _split_bf16(lf)
        return _dot(mat, jnp.concatenate([hi, lo], axis=0))

    def prep_b(c, carry):
        r0 = pl.multiple_of(c * c_, c_)
        lf, key = _hg_gates(fb_ref[0, pl.ds(r0, c_), :].astype(F32), lbb)
        lfb_ref[pl.ds(r0, c_), :] = lf
        kb_ref[pl.ds(r0, c_), :] = key
        e = mdot(m_ref[1, (nl - 1) * c_:(nl + 1) * c_, :], lf)
        ktb_ref[pl.ds(r0, c_), :] = (key * jnp.exp(e[c_:2 * c_, :])).astype(BF16)
        dtot_ref[c] = jnp.broadcast_to(jnp.exp(e[0:1, :]), (8, dk))
        return carry

    lax.fori_loop(0, nch, prep_b, 0, unroll=3)

    def bwd_step(c, st):
        r0 = pl.multiple_of(c * c_, c_)
        sb_ref[c] = st
        return st * dtot_ref[c][0:1, :] + _dot_tn(v_ref[0, pl.ds(r0, c_), :], ktb_ref[pl.ds(r0, c_), :])

    st = jnp.zeros((dk, dk), F32)
    st = lax.fori_loop(0, n_cc, lambda t, s_: bwd_step(n_cc - 1 - t, s_), st, unroll=2)
    lax.fori_loop(0, nch - n_cc, lambda t, s_: bwd_step(nch - 1 - t, s_), st, unroll=8)

    ng = ng_ref[...]

    def mdot_pair(mat, lf0, lf1):
        hi0, lo0 = _split_bf16(lf0)
        hi1, lo1 = _split_bf16(lf1)
        rhs = jnp.concatenate([jnp.concatenate([hi0, hi1], axis=1), jnp.concatenate([lo0, lo1], axis=1)], axis=0)
        return _dot(mat, rhs)

    def intra(qs, keyf, keyb, ef, eb):
        a = jnp.zeros((c_, c_), F32)
        qsb, kfb, kbb = qs.astype(BF16), keyf.astype(BF16), keyb.astype(BF16)
        for li, s in enumerate(HG_LEVELS):
            if s == 0:
                qt = jnp.concatenate([qsb, qsb], axis=1)
                kt = jnp.concatenate([kfb, kbb], axis=1)
            else:
                decf = jnp.exp(ef[li * c_:(li + 1) * c_, :])
                decb = jnp.exp(eb[li * c_:(li + 1) * c_, :])
                up = up_ref[li]
                dqf = decf * up
                dkb = decb * up
                qt = jnp.concatenate([qsb * dqf.astype(BF16), qsb * (decb - dkb).astype(BF16)], axis=1)
                kt = jnp.concatenate([kfb * (decf - dqf).astype(BF16), kbb * dkb.astype(BF16)], axis=1)
            a = a + _dot_nt(qt, kt) * lv_ref[li]
        cumf = ef[(nl - 1) * c_:nl * c_, :]
        restf = ef[nl * c_:(nl + 1) * c_, :]
        cumb = eb[(nl - 1) * c_:nl * c_, :]
        return a, cumf, restf, cumb

    def fwd_chunk(c, sf, qs, keyf, ef, eb):
        r0 = pl.multiple_of(c * c_, c_)
        vc = v_ref[0, pl.ds(r0, c_), :]
        a, cumf, restf, cumb = intra(qs, keyf, kb_ref[pl.ds(r0, c_), :], ef, eb)
        qi = jnp.concatenate([qs * jnp.exp(cumf), qs * jnp.exp(cumb)], axis=1).astype(BF16)
        si = jnp.concatenate([sf, sb_ref[c]], axis=1).astype(BF16)
        o = _dot(a.astype(BF16), vc) + _dot_nt(qi, si)
        totf = cumf[c_ - 1:c_, :]
        sf_new = sf * jnp.exp(totf) + _dot_tn(vc, (keyf * jnp.exp(restf)).astype(BF16))
        on = o * lax.rsqrt(jnp.mean(o * o, axis=-1, keepdims=True) + NORM_EPS) * ng
        gate = g_ref[0, pl.ds(r0, c_), :].astype(F32)
        o_ref[0, pl.ds(r0, c_), :] = (on * _silu(gate)).astype(o_ref.dtype)
        return sf_new

    def fwd_pair(p, sf):
        r0 = pl.multiple_of(p * (2 * c_), 2 * c_)
        qs, keyf, lff = [], [], []
        for t in range(2):
            rows = pl.ds(r0 + t * c_, c_)
            qs.append(_silu(q_ref[0, rows, :].astype(F32)))
            lf, key = _hg_gates(ff_ref[0, rows, :].astype(F32), lbf)
            lff.append(lf)
            keyf.append(key)
        ef = mdot_pair(m_ref[0], lff[0], lff[1])
        eb = mdot_pair(m_ref[1], lfb_ref[pl.ds(r0, c_), :], lfb_ref[pl.ds(r0 + c_, c_), :])
        for t in range(2):
            sf = fwd_chunk(2 * p + t, sf, qs[t], keyf[t], ef[:, t * dk:(t + 1) * dk], eb[:, t * dk:(t + 1) * dk])
        return sf

    lax.fori_loop(0, nch // 2, fwd_pair, jnp.zeros((dk, dk), F32))


def _hgrn2(z, lower, norm_g, n_ctx):
    b, n, _ = z.shape
    mats, level, upper = _hg_constants()
    mats = jnp.asarray(mats, BF16)
    level = jnp.asarray(level)
    upper = jnp.asarray(upper)
    blk = lambda off: pl.BlockSpec((1, n, LANES), lambda bb, h: (bb, 0, off + h))
    return pl.pallas_call(
        functools.partial(_hg_kernel, n_ctx=n_ctx),
        out_shape=jax.ShapeDtypeStruct((b, n, BRANCH_W), BF16),
        grid=(b, HG_HEADS),
        in_specs=[blk(HG_B), blk(HG_B + 4), blk(HG_B + 8), blk(HG_B + 12), blk(HG_B + 16),
                  pl.BlockSpec((2, LANES), lambda bb, h: (0, h)),
                  pl.BlockSpec((1, LANES), lambda bb, h: (0, h)),
                  pl.BlockSpec(mats.shape, lambda bb, h: (0, 0, 0)),
                  pl.BlockSpec(level.shape, lambda bb, h: (0, 0, 0)),
                  pl.BlockSpec(upper.shape, lambda bb, h: (0, 0, 0))],
        out_specs=pl.BlockSpec((1, n, LANES), lambda bb, h: (bb, 0, h)),
        scratch_shapes=[pltpu.VMEM((n, LANES), F32), pltpu.VMEM((n, LANES), F32),
                        pltpu.VMEM((n // HG_CHUNK, HG_DK, HG_DK), F32),
                        pltpu.VMEM((n, LANES), BF16), pltpu.VMEM((n // HG_CHUNK, 8, LANES), F32)],
        compiler_params=_cparams(("arbitrary", "arbitrary")),
        name="hgrn2",
    )(z, z, z, z, z, lower, norm_g.reshape(1, -1), mats, level, upper)


def _na_bias_table(rpb):
    qc = np.arange(GRID_W)[:, None]
    kc = np.arange(GRID_W)[None, :]
    col0 = np.clip(qc - NA_KW // 2, 0, GRID_W - NA_KW)
    valid = (kc >= col0) & (kc < col0 + NA_KW)
    dc = np.clip(kc - qc + (NA_KW - 1), 0, 2 * NA_KW - 2)
    pick = (dc[:, :, None] == np.arange(2 * NA_KW - 1)).astype(np.float32)
    t15 = jnp.einsum('hrc,qkc->hrqk', rpb.astype(F32), pick, precision=lax.Precision.HIGHEST)
    t = jnp.stack([t15[:, dl:dl + NA_KH] for dl in range(NA_KH)], axis=1)
    t = jnp.where(jnp.asarray(valid)[None, None, None], t, NEG_INF)
    t = t.transpose(0, 1, 3, 2, 4).reshape(rpb.shape[0] // 2, 2, NA_KH, GRID_W, NA_KH * GRID_W)
    return t.transpose(0, 2, 1, 3, 4).reshape(rpb.shape[0] // 2, NA_KH, 2 * GRID_W, NA_KH * GRID_W)


def _softmax_pv(parts):
    m = parts[0][0].max(axis=-1, keepdims=True)
    for s, _ in parts[1:]:
        m = jnp.maximum(m, s.max(axis=-1, keepdims=True))
    l = None
    o = None
    for s, v in parts:
        p = jnp.exp(s - m)
        ls = p.sum(axis=-1, keepdims=True)
        os_ = _dot(p.astype(BF16), v)
        l = ls if l is None else l + ls
        o = os_ if o is None else o + os_
    return o / l


def _na_kernel(q_ref, k_ref, v_ref, tab_ref, o_ref, *, n_ctx):
    n = q_ref.shape[1]
    rows = (n - n_ctx) // GRID_W
    wlen = NA_KH * GRID_W
    scale = NA_HD ** -0.5
    kx = k_ref[0, 0:n_ctx, :]
    vx = v_ref[0, 0:n_ctx, :]

    def head_mask(shape, hh):
        lane = lax.broadcasted_iota(jnp.int32, shape, 1)
        return (lane < NA_HD) if hh == 0 else (lane >= NA_HD)

    def stack_heads(qf):
        return jnp.concatenate([jnp.where(head_mask(qf.shape, 0), qf, 0.0),
                                jnp.where(head_mask(qf.shape, 1), qf, 0.0)], axis=0).astype(BF16)

    def unstack_heads(o):
        m = o.shape[0] // 2
        return jnp.where(head_mask((m, LANES), 0), o[:m], o[m:])

    qx = stack_heads(q_ref[0, 0:n_ctx, :].astype(F32) * scale)
    o_ref[0, 0:n_ctx, :] = unstack_heads(_softmax_pv([(_dot_nt(qx, kx), vx)])).astype(o_ref.dtype)

    def row_step(r, carry):
        start = jnp.clip(r - NA_KH // 2, 0, rows - NA_KH)
        delta = start - r + (NA_KH - 1)
        q0 = pl.multiple_of(n_ctx + r * GRID_W, GRID_W)
        k0 = pl.multiple_of(n_ctx + start * GRID_W, GRID_W)
        qr = stack_heads(q_ref[0, pl.ds(q0, GRID_W), :].astype(F32) * scale)
        kl = k_ref[0, pl.ds(k0, wlen), :]
        vl = v_ref[0, pl.ds(k0, wlen), :]
        s_loc = _dot_nt(qr, kl) + tab_ref[0, delta]
        s_ctx = _dot_nt(qr, kx)
        o = unstack_heads(_softmax_pv([(s_loc, vl), (s_ctx, vx)]))
        o_ref[0, pl.ds(q0, GRID_W), :] = o.astype(o_ref.dtype)
        return carry

    lax.fori_loop(0, rows, row_step, 0, unroll=4)


def _neighbourhood(z, table, n_ctx):
    b, n, _ = z.shape
    npair = NA_HEADS // 2
    blk = lambda off: pl.BlockSpec((1, n, LANES), lambda bb, p: (bb, 0, off + p))
    return pl.pallas_call(
        functools.partial(_na_kernel, n_ctx=n_ctx),
        out_shape=jax.ShapeDtypeStruct((b, n, BRANCH_W), BF16),
        grid=(b, npair),
        in_specs=[blk(NA_B), blk(NA_B + 4), blk(NA_B + 8),
                  pl.BlockSpec((1, NA_KH, 2 * GRID_W, NA_KH * GRID_W), lambda bb, p: (p, 0, 0, 0))],
        out_specs=pl.BlockSpec((1, n, LANES), lambda bb, p: (bb, 0, p)),
        compiler_params=_cparams(("arbitrary", "arbitrary")),
        name="neighbourhood_attn",
    )(z, z, z, table)


def _gqa_prep_kernel(q_ref, k_ref, v_ref, cos_ref, sin_ref, qg_ref, kg_ref, bd_ref,
                     qo_ref, ko_ref, vo_ref):
    cos = cos_ref[...]
    sin = sin_ref[...]
    bd = bd_ref[...]

    def norm_rope(x, g):
        hi, lo = _split_bf16(x * x)
        ms = _dot(hi, bd) + _dot(lo, bd)
        xn = x * lax.rsqrt(ms + NORM_EPS) * g
        return _rope(xn, cos, sin, GQA_HD // 4)

    qg = qg_ref[...]
    for j in range(BRANCH_W // LANES):
        x = q_ref[0, :, j * LANES:(j + 1) * LANES].astype(F32)
        qo_ref[0, :, j * LANES:(j + 1) * LANES] = (norm_rope(x, qg) * (GQA_HD ** -0.5)).astype(qo_ref.dtype)
    kn = norm_rope(k_ref[0].astype(F32), kg_ref[...])
    vv = v_ref[0].astype(F32)
    lane = lax.broadcasted_iota(jnp.int32, kn.shape, 1)
    lo_half = lane < GQA_HD
    for a, dst in ((kn, ko_ref), (vv, vo_ref)):
        sw = pltpu.roll(a, GQA_HD, 1)
        dst[0, 0] = jnp.where(lo_half, a, sw).astype(dst.dtype)
        dst[0, 1] = jnp.where(lo_half, sw, a).astype(dst.dtype)


def _gqa_prep(z, cos, sin, qn_g, kn_g, tr):
    b, n, _ = z.shape
    bd = np.kron(np.eye(LANES // GQA_HD), np.ones((GQA_HD, GQA_HD))) / GQA_HD
    qg = jnp.tile(qn_g.reshape(1, -1), (1, LANES // GQA_HD))
    kg = jnp.tile(kn_g.reshape(1, -1), (1, LANES // GQA_HD))
    qb = (GQ_B * LANES) // BRANCH_W
    return pl.pallas_call(
        _gqa_prep_kernel,
        out_shape=(jax.ShapeDtypeStruct((b, n, BRANCH_W), BF16),
                   jax.ShapeDtypeStruct((b, GQA_KV_HEADS, n, LANES), BF16),
                   jax.ShapeDtypeStruct((b, GQA_KV_HEADS, n, LANES), BF16)),
        grid=(b, n // tr),
        in_specs=[pl.BlockSpec((1, tr, BRANCH_W), lambda bb, i: (bb, i, qb)),
                  pl.BlockSpec((1, tr, LANES), lambda bb, i: (bb, i, GQ_B + 4)),
                  pl.BlockSpec((1, tr, LANES), lambda bb, i: (bb, i, GQ_B + 5)),
                  pl.BlockSpec((tr, LANES), lambda bb, i: (i, 0)),
                  pl.BlockSpec((tr, LANES), lambda bb, i: (i, 0)),
                  pl.BlockSpec((1, LANES), lambda bb, i: (0, 0)),
                  pl.BlockSpec((1, LANES), lambda bb, i: (0, 0)),
                  pl.BlockSpec((LANES, LANES), lambda bb, i: (0, 0))],
        out_specs=(pl.BlockSpec((1, tr, BRANCH_W), lambda bb, i: (bb, i, 0)),
                   pl.BlockSpec((1, GQA_KV_HEADS, tr, LANES), lambda bb, i: (bb, 0, i, 0)),
                   pl.BlockSpec((1, GQA_KV_HEADS, tr, LANES), lambda bb, i: (bb, 0, i, 0))),
        compiler_params=_cparams(("arbitrary", "arbitrary")),
        name="gqa_prep",
    )(z, z, z, cos, sin, qg, kg, jnp.asarray(bd, BF16))


def _gqa_kernel(q_ref, k_ref, v_ref, o_ref, *, n_ctx):
    i = pl.program_id(1)
    tq = q_ref.shape[1]
    grp = GQA_HEADS // GQA_KV_HEADS

    def attend(nk):
        for j in range(BRANCH_W // LANES):
            qb = q_ref[0, :, j * LANES:(j + 1) * LANES]
            lane = lax.broadcasted_iota(jnp.int32, qb.shape, 1)
            g = (2 * j) // grp
            zero = jnp.zeros_like(qb)
            qs = jnp.concatenate([jnp.where(lane < GQA_HD, qb, zero), jnp.where(lane >= GQA_HD, qb, zero)], axis=0)
            o = _softmax_pv([(_dot_nt(qs, k_ref[0, g, 0:nk, :]), v_ref[0, g, 0:nk, :])])
            o_ref[0, :, j * LANES:(j + 1) * LANES] = jnp.where(lane < GQA_HD, o[:tq], o[tq:]).astype(o_ref.dtype)

    @pl.when(i * tq < n_ctx)
    def _():
        attend(n_ctx)

    @pl.when(i * tq >= n_ctx)
    def _():
        attend(k_ref.shape[2])


def _gqa_attn(qh, kd, vd, n_ctx, tq):
    b, n, _ = qh.shape
    return pl.pallas_call(
        functools.partial(_gqa_kernel, n_ctx=n_ctx),
        out_shape=jax.ShapeDtypeStruct((b, n, BRANCH_W), BF16),
        grid=(b, n // tq),
        in_specs=[pl.BlockSpec((1, tq, BRANCH_W), lambda bb, i: (bb, i, 0)),
                  pl.BlockSpec((1, GQA_KV_HEADS, n, LANES), lambda bb, i: (bb, 0, 0, 0)),
                  pl.BlockSpec((1, GQA_KV_HEADS, n, LANES), lambda bb, i: (bb, 0, 0, 0))],
        out_specs=pl.BlockSpec((1, tq, BRANCH_W), lambda bb, i: (bb, i, 0)),
        compiler_params=_cparams(("arbitrary", "arbitrary")),
        name="gqa_attn",
    )(qh, kd, vd)


def _layernorm_rows(v, g, b):
    mu = jnp.mean(v, axis=-1, keepdims=True)
    vc = v - mu
    var = jnp.mean(vc * vc, axis=-1, keepdims=True)
    return vc * lax.rsqrt(var + LN_EPS) * g + b


def _pack_bf16_pairs(v):
    w = v.shape[1] // 2
    bits = lax.bitcast_convert_type(v.astype(BF16).astype(F32), jnp.int32)
    return bits[:, w:] | lax.shift_right_logical(bits[:, :w], 16)


def _unpack_bf16_pairs(pieces):
    lo = [lax.bitcast_convert_type(lax.shift_left(p, 16), F32) for p in pieces]
    hi = [lax.bitcast_convert_type(p & jnp.int32(-65536), F32) for p in pieces]
    return jnp.concatenate(lo + hi, axis=1)


def _store_pieces(ref, lead, packed):
    for h in range(SC_PIECES):
        ref[(h,) + lead] = packed[:, h * SC_ROW_WORDS:(h + 1) * SC_ROW_WORDS]


def _merge_kernel(z_ref, r_ref, h_ref, a_ref, q_ref, x_ref, mod_ref, wb_ref, wo_ref, lng_ref, lnb_ref,
                  wrh_ref, wrl_ref, br_ref, tri_ref, xo_ref, h2_ref, te_ref, tg_ref, rk_ref, cnt_ref,
                  run_ref, *, n_ctx, tm):
    i = pl.program_id(1)

    @pl.when((pl.program_id(0) == 0) & (i == 0))
    def _():
        run_ref[...] = jnp.zeros_like(run_ref)

    is_ctx = i * tm < n_ctx
    mod = jnp.where(is_ctx, mod_ref[0, 0], mod_ref[0, 1])
    acc = None
    for nb, br in enumerate((r_ref, h_ref, a_ref, q_ref)):
        yb = _dot(br[0], wb_ref[nb])
        gt = _sigmoid(z_ref[0, :, nb * D_MODEL:(nb + 1) * D_MODEL].astype(F32))
        acc = gt * yb if acc is None else acc + gt * yb
    y = _dot(acc.astype(BF16), wo_ref[...])
    xn = _layernorm_rows(DN_ALPHA * x_ref[0] + mod[2:3, :] * y, lng_ref[...], lnb_ref[...])
    xo_ref[0] = xn
    h2 = xn * mod[3:4, :] + mod[4:5, :]
    h2_hi, h2_lo = _split_bf16(h2)
    _store_pieces(h2_ref, (0,), _pack_bf16_pairs(h2))
    wrh = wrh_ref[...]
    logit = _dot_nt(wrh, h2_hi) + _dot_nt(wrh, h2_lo) + _dot_nt(wrl_ref[...], h2_hi) + br_ref[...]
    eidx = lax.broadcasted_iota(jnp.int32, logit.shape, 0)
    vals, idxs = [], []
    for _ in range(TOP_K):
        m = logit.max(axis=0, keepdims=True)
        sel = jnp.where(logit == m, eidx, N_EXPERTS).min(axis=0, keepdims=True)
        vals.append(m)
        idxs.append(sel)
        logit = jnp.where(eidx == sel, -jnp.inf, logit)
    ex = [jnp.exp(v - vals[0]) for v in vals]
    tot = ex[0] + ex[1] + ex[2] + ex[3]
    te_ref[0] = jnp.concatenate(idxs, axis=0)
    tg_ref[0] = jnp.concatenate([e / tot for e in ex], axis=0)
    base = run_ref[:, 0:1]
    tri = tri_ref[...]
    ranks = []
    for sel in idxs:
        hit = eidx == sel
        ones = jnp.where(hit, 1.0, 0.0)
        pre = _dot(ones.astype(BF16), tri)
        ranks.append(jnp.sum(jnp.where(hit, pre + base, 0.0), axis=0, keepdims=True))
        base = base + jnp.sum(ones, axis=1, keepdims=True)
    rk_ref[0] = jnp.concatenate(ranks, axis=0).astype(jnp.int32)
    run_ref[...] = jnp.broadcast_to(base, run_ref.shape)
    cnt_ref[...] = run_ref[...]


def _merge(z, outs, xs, mod, wb, wo, ln_g, ln_b, wr_hi, wr_lo, b_r, n_ctx, tm):
    b, n, d = xs.shape
    row = lambda w: pl.BlockSpec((1, tm, w), lambda bb, i: (bb, i, 0))
    full = lambda a: pl.BlockSpec(a.shape, lambda bb, i: (0,) * a.ndim)
    topk = pl.BlockSpec((1, TOP_K, tm), lambda bb, i: (bb, 0, i))
    ln_g = ln_g.reshape(1, d)
    ln_b = ln_b.reshape(1, d)
    b_r = b_r.reshape(N_EXPERTS, 1)
    tri = jnp.asarray(np.triu(np.ones((tm, tm), np.float32), 1), BF16)
    return pl.pallas_call(
        functools.partial(_merge_kernel, n_ctx=n_ctx, tm=tm),
        out_shape=(jax.ShapeDtypeStruct((b, n, d), F32),
                   jax.ShapeDtypeStruct((SC_PIECES, b, n, SC_ROW_WORDS), jnp.int32),
                   jax.ShapeDtypeStruct((b, TOP_K, n), jnp.int32),
                   jax.ShapeDtypeStruct((b, TOP_K, n), F32),
                   jax.ShapeDtypeStruct((b, TOP_K, n), jnp.int32),
                   jax.ShapeDtypeStruct((N_EXPERTS, LANES), F32)),
        grid=(b, n // tm),
        in_specs=[row(N_BRANCH * D_MODEL), row(BRANCH_W), row(BRANCH_W), row(BRANCH_W), row(BRANCH_W),
                  row(d),
                  pl.BlockSpec((1, 2, 8, d), lambda bb, i: (bb, 0, 0, 0)),
                  full(wb), full(wo), full(ln_g), full(ln_b), full(wr_hi), full(wr_lo), full(b_r), full(tri)],
        out_specs=(row(d), pl.BlockSpec((SC_PIECES, 1, tm, SC_ROW_WORDS), lambda bb, i: (0, bb, i, 0)),
                   topk, topk, topk,
                   pl.BlockSpec((N_EXPERTS, LANES), lambda bb, i: (0, 0))),
        scratch_shapes=[pltpu.VMEM((N_EXPERTS, LANES), F32)],
        compiler_params=_cparams(("arbitrary", "arbitrary")),
        name="merge_ln_router",
    )(z, *outs, xs, mod, wb, wo, ln_g, ln_b, wr_hi, wr_lo, b_r, tri)


def _ffn_kernel(be_ref, nv_ref, x_ref, wgu_ref, bgu_ref, wd_ref, bd_ref, o_ref):
    i = pl.program_id(0)
    nv = nv_ref[i]

    @pl.when(nv > 0)
    def _():
        rows = lax.broadcasted_iota(jnp.int32, (MOE_BLOCK, 1), 0)
        x = _unpack_bf16_pairs([x_ref[h] for h in range(SC_PIECES)])
        x = jnp.where(rows < nv, x, 0.0).astype(BF16)
        gu = _dot(x, wgu_ref[0]) + bgu_ref[0]
        g = jnp.minimum(gu[:, :D_FF_EXPERT], SWIGLU_LIMIT)
        u = jnp.clip(gu[:, D_FF_EXPERT:], -SWIGLU_LIMIT, SWIGLU_LIMIT)
        act = g * _sigmoid(SWIGLU_ALPHA * g) * (u + 1.0)
        _store_pieces(o_ref, (), _pack_bf16_pairs(_dot(act.astype(BF16), wd_ref[0]) + bd_ref[0]))

    @pl.when(nv <= 0)
    def _():
        o_ref[...] = jnp.zeros_like(o_ref)


def _expert_ffn(xb, block_e, n_valid, wgu, bgu, wd, bd):
    _, n_slots, _ = xb.shape
    n_blocks = n_slots // MOE_BLOCK
    ne, d, f2 = wgu.shape
    slot_blk = pl.BlockSpec((SC_PIECES, MOE_BLOCK, SC_ROW_WORDS), lambda i, be, nv: (0, i, 0))
    grid_spec = pltpu.PrefetchScalarGridSpec(
        num_scalar_prefetch=2,
        grid=(n_blocks,),
        in_specs=[slot_blk,
                  pl.BlockSpec((1, d, f2), lambda i, be, nv: (be[i], 0, 0)),
                  pl.BlockSpec((1, 1, f2), lambda i, be, nv: (be[i], 0, 0)),
                  pl.BlockSpec((1, f2 // 2, d), lambda i, be, nv: (be[i], 0, 0)),
                  pl.BlockSpec((1, 1, d), lambda i, be, nv: (be[i], 0, 0))],
        out_specs=slot_blk,
    )
    return pl.pallas_call(
        _ffn_kernel,
        out_shape=jax.ShapeDtypeStruct(xb.shape, jnp.int32),
        grid_spec=grid_spec,
        compiler_params=_cparams(("arbitrary",)),
        name="expert_ffn",
    )(block_e, n_valid, xb, wgu, bgu.reshape(ne, 1, f2), wd, bd.reshape(ne, 1, d))


def _combine_kernel(y_ref, g_ref, x_ref, mod_ref, lng_ref, lnb_ref, xo_ref, *, n_ctx, tm):
    i = pl.program_id(1)
    is_ctx = i * tm < n_ctx
    g2 = jnp.where(is_ctx, mod_ref[0, 0, 5:6, :], mod_ref[0, 1, 5:6, :])
    gate = g_ref[...]
    y = None
    for k in range(TOP_K):
        t = gate[:, k:k + 1] * _unpack_bf16_pairs([y_ref[k, h] for h in range(SC_PIECES)])
        y = t if y is None else y + t
    xo_ref[0] = _layernorm_rows(DN_ALPHA * x_ref[0] + g2 * y, lng_ref[...], lnb_ref[...])


def _combine(yk, gate, xs, mod, ln_g, ln_b, n_ctx, tm):
    b, n, d = xs.shape
    nt = n // tm
    return pl.pallas_call(
        functools.partial(_combine_kernel, n_ctx=n_ctx, tm=tm),
        out_shape=jax.ShapeDtypeStruct((b, n, d), F32),
        grid=(b, nt),
        in_specs=[pl.BlockSpec((TOP_K, SC_PIECES, tm, SC_ROW_WORDS), lambda bb, i: (0, 0, bb * nt + i, 0)),
                  pl.BlockSpec((tm, TOP_K), lambda bb, i: (bb * nt + i, 0)),
                  pl.BlockSpec((1, tm, d), lambda bb, i: (bb, i, 0)),
                  pl.BlockSpec((1, 2, 8, d), lambda bb, i: (bb, 0, 0, 0)),
                  pl.BlockSpec((1, d), lambda bb, i: (0, 0)),
                  pl.BlockSpec((1, d), lambda bb, i: (0, 0))],
        out_specs=pl.BlockSpec((1, tm, d), lambda bb, i: (bb, i, 0)),
        compiler_params=_cparams(("arbitrary", "arbitrary")),
        name="moe_combine_ln",
    )(yk, gate, xs, mod, ln_g.reshape(1, d), ln_b.reshape(1, d))


def _sc_mesh():
    return plsc.VectorSubcoreMesh(core_axis_name="core", subcore_axis_name="subcore",
                                  num_cores=SC_CORES, num_subcores=SC_SUBCORES)


def _sc_piece_index(idx, rows):
    return jnp.concatenate([idx + h * rows for h in range(SC_PIECES)], axis=1)


def _sc_scatter_rows(x, dest, n_out):
    p, n, w = x.shape
    x2 = x.reshape(p * n, w)
    dest2 = _sc_piece_index(dest, n_out)
    kk, n2 = dest2.shape

    @functools.partial(pl.kernel, out_type=jax.ShapeDtypeStruct((p * n_out, w), x.dtype),
                       mesh=_sc_mesh(), scratch_types=[], name="moe_dispatch_scatter")
    def scatter(x_hbm, i_hbm, o_hbm):
        def body(x_vmem, i_vmem):
            for j in range(kk):
                pltpu.sync_copy(x_vmem, o_hbm.at[i_vmem.at[j]])

        pltpu.emit_pipeline(
            body, grid=(n2 // SC_WINDOW,),
            in_specs=[pl.BlockSpec((SC_WINDOW, SC_ROW_WORDS), lambda i: (i, 0)),
                      pl.BlockSpec((kk, SC_WINDOW), lambda i: (0, i))],
            out_specs=[], core_axis_name=("core", "subcore"),
            dimension_semantics=(pltpu.PARALLEL,))(x_hbm, i_hbm)

    return scatter(x2, dest2).reshape(p, n_out, w)


def _sc_gather_rows(table, idx):
    p, v, w = table.shape
    kk, n = idx.shape
    t2 = table.reshape(p * v, w)
    idx2 = _sc_piece_index(idx, v)
    m = kk * n * p

    @functools.partial(pl.kernel, out_type=jax.ShapeDtypeStruct((m, SC_ROW_WORDS), table.dtype),
                       mesh=_sc_mesh(), scratch_types=[], name="moe_combine_gather")
    def gather(t_hbm, i_hbm, o_hbm):
        def body(i_vmem, o_vmem):
            pltpu.sync_copy(t_hbm.at[i_vmem.at[0]], o_vmem)

        pltpu.emit_pipeline(
            body, grid=(m // SC_WINDOW,),
            in_specs=[pl.BlockSpec((1, SC_WINDOW), lambda i: (0, i))],
            out_specs=[pl.BlockSpec((SC_WINDOW, SC_ROW_WORDS), lambda i: (i, 0))],
            core_axis_name=("core", "subcore"),
            dimension_semantics=(pltpu.PARALLEL,))(i_hbm, o_hbm)

    return gather(t2, idx2.reshape(1, m)).reshape(kk, p, n, w)


def _moe(h2p, top_e, top_g, rank, cnt, wgu, bgu, wd, bd):
    p, b, n, w = h2p.shape
    n_tok = b * n
    nk = n_tok * TOP_K
    experts = jnp.arange(N_EXPERTS, dtype=jnp.int32)
    counts = cnt[:, 0].astype(jnp.int32)
    padded = (counts + MOE_BLOCK - 1) // MOE_BLOCK * MOE_BLOCK
    ends_p = jnp.cumsum(padded)
    start_p = ends_p - padded
    n_blocks = (nk + N_EXPERTS * (MOE_BLOCK - 1) + MOE_BLOCK - 1) // MOE_BLOCK
    block_start = jnp.arange(n_blocks, dtype=jnp.int32) * MOE_BLOCK
    block_e = jnp.minimum(jnp.sum(ends_p[None, :] <= block_start[:, None], axis=1), N_EXPERTS - 1).astype(jnp.int32)
    is_e = block_e[:, None] == experts[None, :]
    filled = jnp.sum(jnp.where(is_e, (start_p + counts)[None, :], 0), axis=1)
    n_valid = jnp.clip(filled - block_start, 0, MOE_BLOCK).astype(jnp.int32)
    dest = rank + jnp.sum(jnp.where(top_e[..., None] == experts, start_p, 0), axis=-1)
    dest = dest.transpose(1, 0, 2).reshape(TOP_K, n_tok)
    gate = top_g.transpose(0, 2, 1).reshape(n_tok, TOP_K)
    xb = _sc_scatter_rows(h2p.reshape(p, n_tok, w), dest, n_blocks * MOE_BLOCK)
    yb = _expert_ffn(xb, block_e, n_valid, wgu, bgu, wd, bd)
    yk = _sc_gather_rows(yb, dest)
    return yk, gate


def kernel(x, c, ctx, c_ctx, w_mod, b_mod, w_in, ret_decay, ret_gn_g, ret_gn_b, hg_lb, hg_norm_g,
           na_rpb, gq_qn_g, gq_kn_g, w_branch, w_out, ln_g, ln_b, w_router, b_router, w_gu, b_gu,
           w_down, b_down):
    b_, t_, d = x.shape
    n_ctx = ctx.shape[1]
    depth = w_mod.shape[0]
    tm = 256

    sm = jax.nn.softmax(hg_lb.astype(F32), axis=1)
    lower = jnp.cumsum(sm, axis=1) - sm[:, :1]
    log_gamma = jax.nn.log_sigmoid(ret_decay.astype(F32)).reshape(depth, 2 * RET_HEADS)

    cc = jnp.concatenate([c, c_ctx[None, :], jnp.zeros((16 - b_ - 1, d), F32)], axis=0)
    modv = _mod_all(cc, w_mod, b_mod).reshape(depth, 16, 6, d)
    one = jnp.asarray([0.0, 1.0, 0.0, 0.0, 1.0, 0.0], F32)[None, None, :, None]
    modv = (modv + one)[:, :, jnp.asarray([1, 0, 2, 4, 3, 5])]
    modv = jnp.concatenate([modv, jnp.zeros((depth, 16, 2, d), F32)], axis=2)
    mod = jnp.stack([jnp.broadcast_to(modv[:, b_:b_ + 1], (depth, b_, 8, d)), modv[:, :b_]], axis=2)

    cos_r, sin_r = _rope_tables(n_ctx, t_, RET_DK)
    cos_g, sin_g = _rope_tables(n_ctx, t_, GQA_HD)

    xs = jnp.concatenate([ctx, x], axis=1)
    tm_in = 768 if xs.shape[1] % 768 == 0 else tm
    w_in_b = _w_in_prep(w_in)
    for l in range(depth):
        z = _in_proj(xs, mod[l], w_in_b[l], n_ctx, tm_in, PROJ_PAD // 4)
        ret_o = _retention(z, log_gamma[l], cos_r, sin_r, ret_gn_g[l], ret_gn_b[l], n_ctx)
        hg_o = _hgrn2(z, lower[:, l], hg_norm_g[l], n_ctx)
        na_o = _neighbourhood(z, _na_bias_table(na_rpb[l]), n_ctx)
        qh, kd, vd = _gqa_prep(z, cos_g, sin_g, gq_qn_g[l], gq_kn_g[l], tm)
        gq_o = _gqa_attn(qh, kd, vd, n_ctx, tm)
        wr_hi, wr_lo = _split_bf16(w_router[l].T)
        xs, h2p, top_e, top_g, rank, cnt = _merge(z, (ret_o, hg_o, na_o, gq_o), xs, mod[l],
                                                  w_branch[l].astype(BF16), w_out[l].astype(BF16),
                                                  ln_g[l, 0], ln_b[l, 0], wr_hi, wr_lo, b_router[l], n_ctx, tm)
        yk, gate = _moe(h2p, top_e, top_g, rank, cnt, w_gu[l].astype(BF16), b_gu[l],
                        w_down[l].astype(BF16), b_down[l])
        xs = _combine(yk, gate, xs, mod[l], ln_g[l, 1], ln_b[l, 1], n_ctx, tm)
    return xs[:, n_ctx:]
```

```python
import functools

import jax
import jax.numpy as jnp
from jax import lax
import numpy as np
from jax.experimental import pallas as pl
from jax.experimental.pallas import tpu as pltpu
from jax.experimental.pallas import tpu_sc as plsc

D_MODEL = 1024
DEPTH = 4
GRID_W = 64
N_BRANCH = 4
BRANCH_W = D_MODEL // 2
RET_HEADS = 4
RET_DK = BRANCH_W // RET_HEADS
HG_HEADS = 4
HG_DK = BRANCH_W // HG_HEADS
NA_HEADS = 8
NA_HD = BRANCH_W // NA_HEADS
NA_KH = 8
NA_KW = 16
GQA_HEADS = 8
GQA_KV_HEADS = 2
GQA_HD = BRANCH_W // GQA_HEADS
GQA_KV_W = GQA_KV_HEADS * GQA_HD
ROPE_BASE = 10000.0
N_EXPERTS = 32
TOP_K = 4
D_FF_EXPERT = D_MODEL
SWIGLU_LIMIT = 7.0
SWIGLU_ALPHA = 1.702
MOE_BLOCK = 256
LN_EPS = 1e-5
NORM_EPS = 1e-6
NEG_INF = -1e30
DN_ALPHA = (2 * DEPTH) ** 0.25

LANES = 128
BF16 = jnp.bfloat16
F32 = jnp.float32
VMEM_LIMIT = 56 * 1024 * 1024
SC_CORES = 2
SC_SUBCORES = 16
SC_WINDOW = 128
SC_ROW_WORDS = 256
SC_PIECES = D_MODEL // 2 // SC_ROW_WORDS

PROJ_ORIG = 4 * BRANCH_W + 5 * BRANCH_W + 3 * BRANCH_W + BRANCH_W + 2 * GQA_KV_W
PROJ_TOTAL = PROJ_ORIG + N_BRANCH * D_MODEL
PROJ_PAD = 11264
GATE_B = 0
RET_B = (N_BRANCH * D_MODEL) // LANES
HG_B = RET_B + 16
NA_B = HG_B + 20
GQ_B = NA_B + 12

RET_CHUNK = 256
HG_CHUNK = 128
HG_LEVELS = (64, 32, 16, 8, 4, 2, 1, 0)


def _cparams(sem):
    return pltpu.CompilerParams(dimension_semantics=sem, vmem_limit_bytes=VMEM_LIMIT)


def _dot(a, b):
    return jnp.dot(a, b, preferred_element_type=F32)


def _dot_nt(a, b):
    return lax.dot_general(a, b, (((1,), (1,)), ((), ())), preferred_element_type=F32)


def _dot_tn(a, b):
    return lax.dot_general(a, b, (((0,), (0,)), ((), ())), preferred_element_type=F32)


def _split_bf16(x):
    hi = x.astype(BF16)
    lo = (x - hi.astype(F32)).astype(BF16)
    return hi, lo


def _sigmoid(x):
    return 1.0 / (1.0 + jnp.exp(-x))


def _silu(x):
    return x * _sigmoid(x)


def _mod_kernel(c_ref, w_ref, b_ref, o_ref):
    s = _silu(c_ref[...]).astype(BF16)
    o_ref[0] = _dot(s, w_ref[0].astype(BF16)) + b_ref[0]


def _mod_all(cc, w_mod, b_mod):
    depth, d, n = w_mod.shape
    r = cc.shape[0]
    tn = 1536
    return pl.pallas_call(
        _mod_kernel,
        out_shape=jax.ShapeDtypeStruct((depth, r, n), F32),
        grid=(depth, n // tn),
        in_specs=[pl.BlockSpec((r, d), lambda l, j: (0, 0)),
                  pl.BlockSpec((1, d, tn), lambda l, j: (l, 0, j)),
                  pl.BlockSpec((1, 1, tn), lambda l, j: (l, 0, j))],
        out_specs=pl.BlockSpec((1, r, tn), lambda l, j: (l, 0, j)),
        compiler_params=_cparams(("arbitrary", "arbitrary")),
        name="adaln_mod",
    )(cc, w_mod, b_mod.reshape(depth, 1, n))


W_PREP_TILE = 256


def _w_in_prep_kernel(w_ref, o_ref, *, n_src):
    j = pl.program_id(1)
    o_ref[0] = jnp.where(j < n_src, w_ref[0], 0.0).astype(BF16)


def _w_in_prep(w_in):
    depth, d, total = w_in.shape
    n_src = total // W_PREP_TILE
    shift = PROJ_ORIG // W_PREP_TILE
    src = lambda l, j: (l, 0, jnp.where(j < n_src, (j + shift) % n_src, 0))
    return pl.pallas_call(
        functools.partial(_w_in_prep_kernel, n_src=n_src),
        out_shape=jax.ShapeDtypeStruct((depth, d, PROJ_PAD), BF16),
        grid=(depth, PROJ_PAD // W_PREP_TILE),
        in_specs=[pl.BlockSpec((1, d, W_PREP_TILE), src)],
        out_specs=pl.BlockSpec((1, d, W_PREP_TILE), lambda l, j: (l, 0, j)),
        compiler_params=_cparams(("arbitrary", "arbitrary")),
        name="w_in_prep",
    )(w_in)


def _in_proj_kernel(x_ref, mod_ref, w_ref, o_ref, *, n_ctx, tm):
    i = pl.program_id(2)
    x = x_ref[0]
    mc = mod_ref[0, 0]
    ml = mod_ref[0, 1]
    rows = i * tm + lax.broadcasted_iota(jnp.int32, (tm, 1), 0)
    is_ctx = rows < n_ctx
    scale = jnp.where(is_ctx, mc[0:1, :], ml[0:1, :])
    shift = jnp.where(is_ctx, mc[1:2, :], ml[1:2, :])
    h = (x * scale + shift).astype(BF16)
    o_ref[0] = _dot(h, w_ref[...]).astype(o_ref.dtype)


def _in_proj(xs, mod, w, n_ctx, tm, tn):
    b, n, d = xs.shape
    ncol = w.shape[1]
    return pl.pallas_call(
        functools.partial(_in_proj_kernel, n_ctx=n_ctx, tm=tm),
        out_shape=jax.ShapeDtypeStruct((b, n, ncol), BF16),
        grid=(ncol // tn, b, n // tm),
        in_specs=[pl.BlockSpec((1, tm, d), lambda j, bb, i: (bb, i, 0)),
                  pl.BlockSpec((1, 2, 8, d), lambda j, bb, i: (bb, 0, 0, 0)),
                  pl.BlockSpec((d, tn), lambda j, bb, i: (0, j))],
        out_specs=pl.BlockSpec((1, tm, tn), lambda j, bb, i: (bb, i, j)),
        compiler_params=_cparams(("arbitrary", "arbitrary", "arbitrary")),
        name="in_proj",
    )(xs, mod, w)


def _rope_tables(n_ctx, t, head_dim):
    idx = jnp.arange(t, dtype=jnp.int32)
    row = (idx // GRID_W).astype(F32)
    col = (idx % GRID_W).astype(F32)
    n = head_dim // 2
    inv = ROPE_BASE ** (-jnp.arange(0, n, 2, dtype=F32) / n)

    def half(pos):
        ang = pos[:, None] * inv[None, :]
        c, s = jnp.cos(ang), jnp.sin(ang)
        return jnp.concatenate([c, c], -1), jnp.concatenate([-s, s], -1)

    cr, sr = half(row)
    cc, sc = half(col)
    cos = jnp.concatenate([cr, cc], -1)
    sin = jnp.concatenate([sr, sc], -1)
    reps = LANES // head_dim
    cos = jnp.tile(cos, (1, reps))
    sin = jnp.tile(sin, (1, reps))
    cos = jnp.concatenate([jnp.ones((n_ctx, LANES), F32), cos], 0)
    sin = jnp.concatenate([jnp.zeros((n_ctx, LANES), F32), sin], 0)
    return cos, sin


def _rope(x, cos, sin, quarter):
    lane = lax.broadcasted_iota(jnp.int32, x.shape, 1)
    first = (lane & (2 * quarter - 1)) < quarter
    rot = jnp.where(first, pltpu.roll(x, LANES - quarter, 1), pltpu.roll(x, quarter, 1))
    return x * cos + rot * sin


def _ret_kernel(lg_ref, q_ref, k_ref, v_ref, g_ref, cos_ref, sin_ref, gng_ref, gnb_ref, o_ref,
                ks_ref, sb_ref, *, n_ctx, chunk):
    hh = pl.program_id(1)
    lgf = lg_ref[hh]
    lgb = lg_ref[RET_HEADS + hh]
    n = q_ref.shape[1]
    c_ = chunk
    nch = n // c_
    n_cc = n_ctx // c_
    dk = q_ref.shape[2]

    ri = lax.broadcasted_iota(jnp.int32, (c_, dk), 0).astype(F32)
    df = jnp.exp((ri + 1.0) * lgf)
    db = jnp.exp((c_ - ri) * lgb)
    ef = jnp.exp((c_ - 1.0 - ri) * lgf)
    eb = jnp.exp(ri * lgb)
    gfc = jnp.exp(c_ * lgf)
    gbc = jnp.exp(c_ * lgb)
    ii = lax.broadcasted_iota(jnp.int32, (c_, c_), 0)
    jj = lax.broadcasted_iota(jnp.int32, (c_, c_), 1)
    dij = (ii - jj).astype(F32)
    w = jnp.where(ii >= jj, jnp.exp(dij * lgf), jnp.exp(-dij * lgb))

    def rope_k(c, carry):
        r0 = pl.multiple_of(c * c_, c_)
        kc = k_ref[0, pl.ds(r0, c_), :].astype(F32) * (RET_DK ** -0.5)
        kc = _rope(kc, cos_ref[pl.ds(r0, c_), :], sin_ref[pl.ds(r0, c_), :], 32)
        ks_ref[pl.ds(r0, c_), :] = kc.astype(BF16)
        return carry

    lax.fori_loop(0, nch, rope_k, 0, unroll=3)

    def bwd_step(c, s):
        r0 = pl.multiple_of(c * c_, c_)
        sb_ref[c] = s
        kc = (ks_ref[pl.ds(r0, c_), :].astype(F32) * eb).astype(BF16)
        return s * gbc + _dot_tn(kc, v_ref[0, pl.ds(r0, c_), :])

    s = jnp.zeros((dk, dk), F32)
    s = lax.fori_loop(0, n_cc, lambda t, s_: bwd_step(n_cc - 1 - t, s_), s)
    lax.fori_loop(0, nch - n_cc, lambda t, s_: bwd_step(nch - 1 - t, s_), s, unroll=4)

    gng = gng_ref[...]
    gnb = gnb_ref[...]

    def fwd_step(c, sf):
        r0 = pl.multiple_of(c * c_, c_)
        qc = _rope(q_ref[0, pl.ds(r0, c_), :].astype(F32), cos_ref[pl.ds(r0, c_), :],
                   sin_ref[pl.ds(r0, c_), :], 32)
        kc = ks_ref[pl.ds(r0, c_), :]
        vc = v_ref[0, pl.ds(r0, c_), :]
        a = _dot_nt(qc.astype(BF16), kc) * w
        qi = jnp.concatenate([qc * df, qc * db], axis=1).astype(BF16)
        si = jnp.concatenate([sf, sb_ref[c]], axis=0).astype(BF16)
        o = _dot(a.astype(BF16), vc) + _dot(qi, si)
        sf_new = sf * gfc + _dot_tn((kc.astype(F32) * ef).astype(BF16), vc)
        mu = jnp.mean(o, axis=-1, keepdims=True)
        oc = o - mu
        var = jnp.mean(oc * oc, axis=-1, keepdims=True)
        y = oc * lax.rsqrt(var + LN_EPS) * gng + gnb
        gate = g_ref[0, pl.ds(r0, c_), :].astype(F32)
        o_ref[0, pl.ds(r0, c_), :] = (y * _silu(gate)).astype(o_ref.dtype)
        return sf_new

    lax.fori_loop(0, nch, fwd_step, jnp.zeros((dk, dk), F32), unroll=3)


def _retention(z, log_gamma, cos, sin, gn_g, gn_b, n_ctx):
    b, n, _ = z.shape
    blk = lambda off: pl.BlockSpec((1, n, LANES), lambda bb, h, lg: (bb, 0, off + h))
    grid_spec = pltpu.PrefetchScalarGridSpec(
        num_scalar_prefetch=1,
        grid=(b, RET_HEADS),
        in_specs=[blk(RET_B), blk(RET_B + 4), blk(RET_B + 8), blk(RET_B + 12),
                  pl.BlockSpec((n, LANES), lambda bb, h, lg: (0, 0)),
                  pl.BlockSpec((n, LANES), lambda bb, h, lg: (0, 0)),
                  pl.BlockSpec((1, LANES), lambda bb, h, lg: (0, h)),
                  pl.BlockSpec((1, LANES), lambda bb, h, lg: (0, h))],
        out_specs=pl.BlockSpec((1, n, LANES), lambda bb, h, lg: (bb, 0, h)),
        scratch_shapes=[pltpu.VMEM((n, LANES), BF16),
                        pltpu.VMEM((n // RET_CHUNK, RET_DK, RET_DK), F32)],
    )
    return pl.pallas_call(
        functools.partial(_ret_kernel, n_ctx=n_ctx, chunk=RET_CHUNK),
        out_shape=jax.ShapeDtypeStruct((b, n, BRANCH_W), BF16),
        grid_spec=grid_spec,
        compiler_params=_cparams(("arbitrary", "arbitrary")),
        name="retention",
    )(log_gamma, z, z, z, z, cos, sin, gn_g.reshape(1, -1), gn_b.reshape(1, -1))


def _hg_constants():
    c = HG_CHUNK
    r = np.arange(c)[:, None]
    t = np.arange(c)[None, :]
    mats = []
    level = np.full((2, c, c), len(HG_LEVELS), np.int32)
    for d in range(2):
        blocks = []
        for li, s in enumerate(HG_LEVELS):
            if s == 0:
                level[d][np.arange(c), np.arange(c)] = li
                continue
            base = (r // (2 * s)) * (2 * s)
            if d == 0:
                m = base + s - 1
                mat = (t > np.minimum(r, m)) & (t <= np.maximum(r, m))
                q_side = (r % (2 * s)) >= s
            else:
                m = base + s
                mat = (t >= np.minimum(r, m)) & (t < np.maximum(r, m))
                q_side = (r % (2 * s)) < s
            blocks.append(mat.astype(np.float32))
            same = (r // (2 * s)) == (t // (2 * s))
            k_side_t = ((t % (2 * s)) < s) if d == 0 else ((t % (2 * s)) >= s)
            level[d][same & q_side & k_side_t] = li
        if d == 0:
            blocks.append((t <= r).astype(np.float32))
            blocks.append((t > r).astype(np.float32))
        else:
            blocks.append((t >= r).astype(np.float32))
            blocks.append((t < r).astype(np.float32))
        mats.append(np.concatenate(blocks, 0))
    mats = np.stack(mats)
    masks = (level[:, None] == np.arange(len(HG_LEVELS))[None, :, None, None]).astype(np.float32)
    upper = np.stack([np.broadcast_to(((np.arange(c) % (2 * s)) >= s)[:, None], (c, c)) for s in HG_LEVELS if s > 0])
    return np.concatenate([mats, mats], axis=2), np.maximum(masks[0], masks[1]), upper.astype(np.float32)


def _hg_gates(zf, lb):
    en = jnp.exp(-jnp.abs(zf))
    inv = 1.0 / (1.0 + en)
    pos = zf >= 0
    sg = jnp.where(pos, inv, en * inv)
    sgn = jnp.where(pos, en * inv, inv)
    one_m = 1.0 - lb
    return jnp.log(lb + one_m * sg), one_m * sgn


def _hg_kernel(q_ref, ff_ref, fb_ref, v_ref, g_ref, lb_ref, ng_ref, m_ref, lv_ref, up_ref, o_ref,
               lfb_ref, kb_ref, sb_ref, ktb_ref, dtot_ref, *, n_ctx):
    c_ = HG_CHUNK
    n = q_ref.shape[1]
    nch = n // c_
    n_cc = n_ctx // c_
    dk = q_ref.shape[2]
    nl = len(HG_LEVELS)
    lbf = lb_ref[0:1, :]
    lbb = lb_ref[1:2, :]

    def mdot(mat, lf):
        hi, lo = _split_bf16(lf)
        return _dot(mat, jnp.concatenate([hi, lo], axis=0))

    def prep_b(c, carry):
        r0 = pl.multiple_of(c * c_, c_)
        lf, key = _hg_gates(fb_ref[0, pl.ds(r0, c_), :].astype(F32), lbb)
        lfb_ref[pl.ds(r0, c_), :] = lf
        kb_ref[pl.ds(r0, c_), :] = key
        e = mdot(m_ref[1, (nl - 1) * c_:(nl + 1) * c_, :], lf)
        ktb_ref[pl.ds(r0, c_), :] = (key * jnp.exp(e[c_:2 * c_, :])).astype(BF16)
        dtot_ref[c] = jnp.broadcast_to(jnp.exp(e[0:1, :]), (8, dk))
        return carry

    lax.fori_loop(0, nch, prep_b, 0, unroll=3)

    def bwd_step(c, st):
        r0 = pl.multiple_of(c * c_, c_)
        sb_ref[c] = st
        return st * dtot_ref[c][0:1, :] + _dot_tn(v_ref[0, pl.ds(r0, c_), :], ktb_ref[pl.ds(r0, c_), :])

    st = jnp.zeros((dk, dk), F32)
    st = lax.fori_loop(0, n_cc, lambda t, s_: bwd_step(n_cc - 1 - t, s_), st, unroll=2)
    lax.fori_loop(0, nch - n_cc, lambda t, s_: bwd_step(nch - 1 - t, s_), st, unroll=8)

    ng = ng_ref[...]

    def mdot_pair(mat, lf0, lf1):
        hi0, lo0 = _split_bf16(lf0)
        hi1, lo1 = _split_bf16(lf1)
        rhs = jnp.concatenate([jnp.concatenate([hi0, hi1], axis=1), jnp.concatenate([lo0, lo1], axis=1)], axis=0)
        return _dot(mat, rhs)

    def intra(qs, keyf, keyb, ef, eb):
        a = jnp.zeros((c_, c_), F32)
        qsb, kfb, kbb = qs.astype(BF16), keyf.astype(BF16), keyb.astype(BF16)
        for li, s in enumerate(HG_LEVELS):
            if s == 0:
                qt = jnp.concatenate([qsb, qsb], axis=1)
                kt = jnp.concatenate([kfb, kbb], axis=1)
            else:
                decf = jnp.exp(ef[li * c_:(li + 1) * c_, :])
                decb = jnp.exp(eb[li * c_:(li + 1) * c_, :])
                up = up_ref[li]
                dqf = decf * up
                dkb = decb * up
                qt = jnp.concatenate([qsb * dqf.astype(BF16), qsb * (decb - dkb).astype(BF16)], axis=1)
                kt = jnp.concatenate([kfb * (decf - dqf).astype(BF16), kbb * dkb.astype(BF16)], axis=1)
            a = a + _dot_nt(qt, kt) * lv_ref[li]
        cumf = ef[(nl - 1) * c_:nl * c_, :]
        restf = ef[nl * c_:(nl + 1) * c_, :]
        cumb = eb[(nl - 1) * c_:nl * c_, :]
        return a, cumf, restf, cumb

    def fwd_chunk(c, sf, qs, keyf, ef, eb):
        r0 = pl.multiple_of(c * c_, c_)
        vc = v_ref[0, pl.ds(r0, c_), :]
        a, cumf, restf, cumb = intra(qs, keyf, kb_ref[pl.ds(r0, c_), :], ef, eb)
        qi = jnp.concatenate([qs * jnp.exp(cumf), qs * jnp.exp(cumb)], axis=1).astype(BF16)
        si = jnp.concatenate([sf, sb_ref[c]], axis=1).astype(BF16)
        o = _dot(a.astype(BF16), vc) + _dot_nt(qi, si)
        totf = cumf[c_ - 1:c_, :]
        sf_new = sf * jnp.exp(totf) + _dot_tn(vc, (keyf * jnp.exp(restf)).astype(BF16))
        on = o * lax.rsqrt(jnp.mean(o * o, axis=-1, keepdims=True) + NORM_EPS) * ng
        gate = g_ref[0, pl.ds(r0, c_), :].astype(F32)
        o_ref[0, pl.ds(r0, c_), :] = (on * _silu(gate)).astype(o_ref.dtype)
        return sf_new

    def fwd_pair(p, sf):
        r0 = pl.multiple_of(p * (2 * c_), 2 * c_)
        qs, keyf, lff = [], [], []
        for t in range(2):
            rows = pl.ds(r0 + t * c_, c_)
            qs.append(_silu(q_ref[0, rows, :].astype(F32)))
            lf, key = _hg_gates(ff_ref[0, rows, :].astype(F32), lbf)
            lff.append(lf)
            keyf.append(key)
        ef = mdot_pair(m_ref[0], lff[0], lff[1])
        eb = mdot_pair(m_ref[1], lfb_ref[pl.ds(r0, c_), :], lfb_ref[pl.ds(r0 + c_, c_), :])
        for t in range(2):
            sf = fwd_chunk(2 * p + t, sf, qs[t], keyf[t], ef[:, t * dk:(t + 1) * dk], eb[:, t * dk:(t + 1) * dk])
        return sf

    lax.fori_loop(0, nch // 2, fwd_pair, jnp.zeros((dk, dk), F32))


def _hgrn2(z, lower, norm_g, n_ctx):
    b, n, _ = z.shape
    mats, level, upper = _hg_constants()
    mats = jnp.asarray(mats, BF16)
    level = jnp.asarray(level)
    upper = jnp.asarray(upper)
    blk = lambda off: pl.BlockSpec((1, n, LANES), lambda bb, h: (bb, 0, off + h))
    return pl.pallas_call(
        functools.partial(_hg_kernel, n_ctx=n_ctx),
        out_shape=jax.ShapeDtypeStruct((b, n, BRANCH_W), BF16),
        grid=(b, HG_HEADS),
        in_specs=[blk(HG_B), blk(HG_B + 4), blk(HG_B + 8), blk(HG_B + 12), blk(HG_B + 16),
                  pl.BlockSpec((2, LANES), lambda bb, h: (0, h)),
                  pl.BlockSpec((1, LANES), lambda bb, h: (0, h)),
                  pl.BlockSpec(mats.shape, lambda bb, h: (0, 0, 0)),
                  pl.BlockSpec(level.shape, lambda bb, h: (0, 0, 0)),
                  pl.BlockSpec(upper.shape, lambda bb, h: (0, 0, 0))],
        out_specs=pl.BlockSpec((1, n, LANES), lambda bb, h: (bb, 0, h)),
        scratch_shapes=[pltpu.VMEM((n, LANES), F32), pltpu.VMEM((n, LANES), F32),
                        pltpu.VMEM((n // HG_CHUNK, HG_DK, HG_DK), F32),
                        pltpu.VMEM((n, LANES), BF16), pltpu.VMEM((n // HG_CHUNK, 8, LANES), F32)],
        compiler_params=_cparams(("arbitrary", "arbitrary")),
        name="hgrn2",
    )(z, z, z, z, z, lower, norm_g.reshape(1, -1), mats, level, upper)


def _na_bias_table(rpb):
    qc = np.arange(GRID_W)[:, None]
    kc = np.arange(GRID_W)[None, :]
    col0 = np.clip(qc - NA_KW // 2, 0, GRID_W - NA_KW)
    valid = (kc >= col0) & (kc < col0 + NA_KW)
    dc = np.clip(kc - qc + (NA_KW - 1), 0, 2 * NA_KW - 2)
    pick = (dc[:, :, None] == np.arange(2 * NA_KW - 1)).astype(np.float32)
    t15 = jnp.einsum('hrc,qkc->hrqk', rpb.astype(F32), pick, precision=lax.Precision.HIGHEST)
    t = jnp.stack([t15[:, dl:dl + NA_KH] for dl in range(NA_KH)], axis=1)
    t = jnp.where(jnp.asarray(valid)[None, None, None], t, NEG_INF)
    t = t.transpose(0, 1, 3, 2, 4).reshape(rpb.shape[0] // 2, 2, NA_KH, GRID_W, NA_KH * GRID_W)
    return t.transpose(0, 2, 1, 3, 4).reshape(rpb.shape[0] // 2, NA_KH, 2 * GRID_W, NA_KH * GRID_W)


def _softmax_pv(parts):
    m = parts[0][0].max(axis=-1, keepdims=True)
    for s, _ in parts[1:]:
        m = jnp.maximum(m, s.max(axis=-1, keepdims=True))
    l = None
    o = None
    for s, v in parts:
        p = jnp.exp(s - m)
        ls = p.sum(axis=-1, keepdims=True)
        os_ = _dot(p.astype(BF16), v)
        l = ls if l is None else l + ls
        o = os_ if o is None else o + os_
    return o / l


def _na_kernel(q_ref, k_ref, v_ref, tab_ref, o_ref, *, n_ctx):
    n = q_ref.shape[1]
    rows = (n - n_ctx) // GRID_W
    wlen = NA_KH * GRID_W
    scale = NA_HD ** -0.5
    kx = k_ref[0, 0:n_ctx, :]
    vx = v_ref[0, 0:n_ctx, :]

    def head_mask(shape, hh):
        lane = lax.broadcasted_iota(jnp.int32, shape, 1)
        return (lane < NA_HD) if hh == 0 else (lane >= NA_HD)

    def stack_heads(qf):
        return jnp.concatenate([jnp.where(head_mask(qf.shape, 0), qf, 0.0),
                                jnp.where(head_mask(qf.shape, 1), qf, 0.0)], axis=0).astype(BF16)

    def unstack_heads(o):
        m = o.shape[0] // 2
        return jnp.where(head_mask((m, LANES), 0), o[:m], o[m:])

    qx = stack_heads(q_ref[0, 0:n_ctx, :].astype(F32) * scale)
    o_ref[0, 0:n_ctx, :] = unstack_heads(_softmax_pv([(_dot_nt(qx, kx), vx)])).astype(o_ref.dtype)

    def row_step(r, carry):
        start = jnp.clip(r - NA_KH // 2, 0, rows - NA_KH)
        delta = start - r + (NA_KH - 1)
        q0 = pl.multiple_of(n_ctx + r * GRID_W, GRID_W)
        k0 = pl.multiple_of(n_ctx + start * GRID_W, GRID_W)
        qr = stack_heads(q_ref[0, pl.ds(q0, GRID_W), :].astype(F32) * scale)
        kl = k_ref[0, pl.ds(k0, wlen), :]
        vl = v_ref[0, pl.ds(k0, wlen), :]
        s_loc = _dot_nt(qr, kl) + tab_ref[0, delta]
        s_ctx = _dot_nt(qr, kx)
        o = unstack_heads(_softmax_pv([(s_loc, vl), (s_ctx, vx)]))
        o_ref[0, pl.ds(q0, GRID_W), :] = o.astype(o_ref.dtype)
        return carry

    lax.fori_loop(0, rows, row_step, 0, unroll=4)


def _neighbourhood(z, table, n_ctx):
    b, n, _ = z.shape
    npair = NA_HEADS // 2
    blk = lambda off: pl.BlockSpec((1, n, LANES), lambda bb, p: (bb, 0, off + p))
    return pl.pallas_call(
        functools.partial(_na_kernel, n_ctx=n_ctx),
        out_shape=jax.ShapeDtypeStruct((b, n, BRANCH_W), BF16),
        grid=(b, npair),
        in_specs=[blk(NA_B), blk(NA_B + 4), blk(NA_B + 8),
                  pl.BlockSpec((1, NA_KH, 2 * GRID_W, NA_KH * GRID_W), lambda bb, p: (p, 0, 0, 0))],
        out_specs=pl.BlockSpec((1, n, LANES), lambda bb, p: (bb, 0, p)),
        compiler_params=_cparams(("arbitrary", "arbitrary")),
        name="neighbourhood_attn",
    )(z, z, z, table)


def _gqa_prep_kernel(q_ref, k_ref, v_ref, cos_ref, sin_ref, qg_ref, kg_ref, bd_ref,
                     qo_ref, ko_ref, vo_ref):
    cos = cos_ref[...]
    sin = sin_ref[...]
    bd = bd_ref[...]

    def norm_rope(x, g):
        hi, lo = _split_bf16(x * x)
        ms = _dot(hi, bd) + _dot(lo, bd)
        xn = x * lax.rsqrt(ms + NORM_EPS) * g
        return _rope(xn, cos, sin, GQA_HD // 4)

    qg = qg_ref[...]
    for j in range(BRANCH_W // LANES):
        x = q_ref[0, :, j * LANES:(j + 1) * LANES].astype(F32)
        qo_ref[0, :, j * LANES:(j + 1) * LANES] = (norm_rope(x, qg) * (GQA_HD ** -0.5)).astype(qo_ref.dtype)
    kn = norm_rope(k_ref[0].astype(F32), kg_ref[...])
    vv = v_ref[0].astype(F32)
    lane = lax.broadcasted_iota(jnp.int32, kn.shape, 1)
    lo_half = lane < GQA_HD
    for a, dst in ((kn, ko_ref), (vv, vo_ref)):
        sw = pltpu.roll(a, GQA_HD, 1)
        dst[0, 0] = jnp.where(lo_half, a, sw).astype(dst.dtype)
        dst[0, 1] = jnp.where(lo_half, sw, a).astype(dst.dtype)


def _gqa_prep(z, cos, sin, qn_g, kn_g, tr):
    b, n, _ = z.shape
    bd = np.kron(np.eye(LANES // GQA_HD), np.ones((GQA_HD, GQA_HD))) / GQA_HD
    qg = jnp.tile(qn_g.reshape(1, -1), (1, LANES // GQA_HD))
    kg = jnp.tile(kn_g.reshape(1, -1), (1, LANES // GQA_HD))
    qb = (GQ_B * LANES) // BRANCH_W
    return pl.pallas_call(
        _gqa_prep_kernel,
        out_shape=(jax.ShapeDtypeStruct((b, n, BRANCH_W), BF16),
                   jax.ShapeDtypeStruct((b, GQA_KV_HEADS, n, LANES), BF16),
                   jax.ShapeDtypeStruct((b, GQA_KV_HEADS, n, LANES), BF16)),
        grid=(b, n // tr),
        in_specs=[pl.BlockSpec((1, tr, BRANCH_W), lambda bb, i: (bb, i, qb)),
                  pl.BlockSpec((1, tr, LANES), lambda bb, i: (bb, i, GQ_B + 4)),
                  pl.BlockSpec((1, tr, LANES), lambda bb, i: (bb, i, GQ_B + 5)),
                  pl.BlockSpec((tr, LANES), lambda bb, i: (i, 0)),
                  pl.BlockSpec((tr, LANES), lambda bb, i: (i, 0)),
                  pl.BlockSpec((1, LANES), lambda bb, i: (0, 0)),
                  pl.BlockSpec((1, LANES), lambda bb, i: (0, 0)),
                  pl.BlockSpec((LANES, LANES), lambda bb, i: (0, 0))],
        out_specs=(pl.BlockSpec((1, tr, BRANCH_W), lambda bb, i: (bb, i, 0)),
                   pl.BlockSpec((1, GQA_KV_HEADS, tr, LANES), lambda bb, i: (bb, 0, i, 0)),
                   pl.BlockSpec((1, GQA_KV_HEADS, tr, LANES), lambda bb, i: (bb, 0, i, 0))),
        compiler_params=_cparams(("arbitrary", "arbitrary")),
        name="gqa_prep",
    )(z, z, z, cos, sin, qg, kg, jnp.asarray(bd, BF16))


def _gqa_kernel(q_ref, k_ref, v_ref, o_ref, *, n_ctx):
    i = pl.program_id(1)
    tq = q_ref.shape[1]
    grp = GQA_HEADS // GQA_KV_HEADS

    def attend(nk):
        for j in range(BRANCH_W // LANES):
            qb = q_ref[0, :, j * LANES:(j + 1) * LANES]
            lane = lax.broadcasted_iota(jnp.int32, qb.shape, 1)
            g = (2 * j) // grp
            zero = jnp.zeros_like(qb)
            qs = jnp.concatenate([jnp.where(lane < GQA_HD, qb, zero), jnp.where(lane >= GQA_HD, qb, zero)], axis=0)
            o = _softmax_pv([(_dot_nt(qs, k_ref[0, g, 0:nk, :]), v_ref[0, g, 0:nk, :])])
            o_ref[0, :, j * LANES:(j + 1) * LANES] = jnp.where(lane < GQA_HD, o[:tq], o[tq:]).astype(o_ref.dtype)

    @pl.when(i * tq < n_ctx)
    def _():
        attend(n_ctx)

    @pl.when(i * tq >= n_ctx)
    def _():
        attend(k_ref.shape[2])


def _gqa_attn(qh, kd, vd, n_ctx, tq):
    b, n, _ = qh.shape
    return pl.pallas_call(
        functools.partial(_gqa_kernel, n_ctx=n_ctx),
        out_shape=jax.ShapeDtypeStruct((b, n, BRANCH_W), BF16),
        grid=(b, n // tq),
        in_specs=[pl.BlockSpec((1, tq, BRANCH_W), lambda bb, i: (bb, i, 0)),
                  pl.BlockSpec((1, GQA_KV_HEADS, n, LANES), lambda bb, i: (bb, 0, 0, 0)),
                  pl.BlockSpec((1, GQA_KV_HEADS, n, LANES), lambda bb, i: (bb, 0, 0, 0))],
        out_specs=pl.BlockSpec((1, tq, BRANCH_W), lambda bb, i: (bb, i, 0)),
        compiler_params=_cparams(("arbitrary", "arbitrary")),
        name="gqa_attn",
    )(qh, kd, vd)


def _layernorm_rows(v, g, b):
    mu = jnp.mean(v, axis=-1, keepdims=True)
    vc = v - mu
    var = jnp.mean(vc * vc, axis=-1, keepdims=True)
    return vc * lax.rsqrt(var + LN_EPS) * g + b


def _pack_bf16_pairs(v):
    w = v.shape[1] // 2
    bits = lax.bitcast_convert_type(v.astype(BF16).astype(F32), jnp.int32)
    return bits[:, w:] | lax.shift_right_logical(bits[:, :w], 16)


def _unpack_bf16_pairs(pieces):
    lo = [lax.bitcast_convert_type(lax.shift_left(p, 16), F32) for p in pieces]
    hi = [lax.bitcast_convert_type(p & jnp.int32(-65536), F32) for p in pieces]
    return jnp.concatenate(lo + hi, axis=1)


def _store_pieces(ref, lead, packed):
    for h in range(SC_PIECES):
        ref[(h,) + lead] = packed[:, h * SC_ROW_WORDS:(h + 1) * SC_ROW_WORDS]


def _merge_kernel(z_ref, r_ref, h_ref, a_ref, q_ref, x_ref, mod_ref, wb_ref, wo_ref, lng_ref, lnb_ref,
                  wrh_ref, wrl_ref, br_ref, tri_ref, xo_ref, h2_ref, te_ref, tg_ref, rk_ref, cnt_ref,
                  run_ref, *, n_ctx, tm):
    i = pl.program_id(1)

    @pl.when((pl.program_id(0) == 0) & (i == 0))
    def _():
        run_ref[...] = jnp.zeros_like(run_ref)

    is_ctx = i * tm < n_ctx
    mod = jnp.where(is_ctx, mod_ref[0, 0], mod_ref[0, 1])
    acc = None
    for nb, br in enumerate((r_ref, h_ref, a_ref, q_ref)):
        yb = _dot(br[0], wb_ref[nb])
        gt = _sigmoid(z_ref[0, :, nb * D_MODEL:(nb + 1) * D_MODEL].astype(F32))
        acc = gt * yb if acc is None else acc + gt * yb
    y = _dot(acc.astype(BF16), wo_ref[...])
    xn = _layernorm_rows(DN_ALPHA * x_ref[0] + mod[2:3, :] * y, lng_ref[...], lnb_ref[...])
    xo_ref[0] = xn
    h2 = xn * mod[3:4, :] + mod[4:5, :]
    h2_hi, h2_lo = _split_bf16(h2)
    _store_pieces(h2_ref, (0,), _pack_bf16_pairs(h2))
    wrh = wrh_ref[...]
    logit = _dot_nt(wrh, h2_hi) + _dot_nt(wrh, h2_lo) + _dot_nt(wrl_ref[...], h2_hi) + br_ref[...]
    eidx = lax.broadcasted_iota(jnp.int32, logit.shape, 0)
    vals, idxs = [], []
    for _ in range(TOP_K):
        m = logit.max(axis=0, keepdims=True)
        sel = jnp.where(logit == m, eidx, N_EXPERTS).min(axis=0, keepdims=True)
        vals.append(m)
        idxs.append(sel)
        logit = jnp.where(eidx == sel, -jnp.inf, logit)
    ex = [jnp.exp(v - vals[0]) for v in vals]
    tot = ex[0] + ex[1] + ex[2] + ex[3]
    te_ref[0] = jnp.concatenate(idxs, axis=0)
    tg_ref[0] = jnp.concatenate([e / tot for e in ex], axis=0)
    base = run_ref[:, 0:1]
    tri = tri_ref[...]
    ranks = []
    for sel in idxs:
        hit = eidx == sel
        ones = jnp.where(hit, 1.0, 0.0)
        pre = _dot(ones.astype(BF16), tri)
        ranks.append(jnp.sum(jnp.where(hit, pre + base, 0.0), axis=0, keepdims=True))
        base = base + jnp.sum(ones, axis=1, keepdims=True)
    rk_ref[0] = jnp.concatenate(ranks, axis=0).astype(jnp.int32)
    run_ref[...] = jnp.broadcast_to(base, run_ref.shape)
    cnt_ref[...] = run_ref[...]


def _merge(z, outs, xs, mod, wb, wo, ln_g, ln_b, wr_hi, wr_lo, b_r, n_ctx, tm):
    b, n, d = xs.shape
    row = lambda w: pl.BlockSpec((1, tm, w), lambda bb, i: (bb, i, 0))
    full = lambda a: pl.BlockSpec(a.shape, lambda bb, i: (0,) * a.ndim)
    topk = pl.BlockSpec((1, TOP_K, tm), lambda bb, i: (bb, 0, i))
    ln_g = ln_g.reshape(1, d)
    ln_b = ln_b.reshape(1, d)
    b_r = b_r.reshape(N_EXPERTS, 1)
    tri = jnp.asarray(np.triu(np.ones((tm, tm), np.float32), 1), BF16)
    return pl.pallas_call(
        functools.partial(_merge_kernel, n_ctx=n_ctx, tm=tm),
        out_shape=(jax.ShapeDtypeStruct((b, n, d), F32),
                   jax.ShapeDtypeStruct((SC_PIECES, b, n, SC_ROW_WORDS), jnp.int32),
                   jax.ShapeDtypeStruct((b, TOP_K, n), jnp.int32),
                   jax.ShapeDtypeStruct((b, TOP_K, n), F32),
                   jax.ShapeDtypeStruct((b, TOP_K, n), jnp.int32),
                   jax.ShapeDtypeStruct((N_EXPERTS, LANES), F32)),
        grid=(b, n // tm),
        in_specs=[row(N_BRANCH * D_MODEL), row(BRANCH_W), row(BRANCH_W), row(BRANCH_W), row(BRANCH_W),
                  row(d),
                  pl.BlockSpec((1, 2, 8, d), lambda bb, i: (bb, 0, 0, 0)),
                  full(wb), full(wo), full(ln_g), full(ln_b), full(wr_hi), full(wr_lo), full(b_r), full(tri)],
        out_specs=(row(d), pl.BlockSpec((SC_PIECES, 1, tm, SC_ROW_WORDS), lambda bb, i: (0, bb, i, 0)),
                   topk, topk, topk,
                   pl.BlockSpec((N_EXPERTS, LANES), lambda bb, i: (0, 0))),
        scratch_shapes=[pltpu.VMEM((N_EXPERTS, LANES), F32)],
        compiler_params=_cparams(("arbitrary", "arbitrary")),
        name="merge_ln_router",
    )(z, *outs, xs, mod, wb, wo, ln_g, ln_b, wr_hi, wr_lo, b_r, tri)


def _ffn_kernel(be_ref, nv_ref, nw_ref, x_ref, wgu_ref, bgu_ref, wd_ref, bd_ref, o_ref, wgu_s, wd_s):
    i = pl.program_id(0)
    nv = nv_ref[i]

    @pl.when(nw_ref[i] > 0)
    def _():
        wgu_s[...] = wgu_ref[0, 0].astype(BF16)
        wd_s[...] = wd_ref[0, 0].astype(BF16)

    @pl.when(nv > 0)
    def _():
        rows = lax.broadcasted_iota(jnp.int32, (MOE_BLOCK, 1), 0)
        x = _unpack_bf16_pairs([x_ref[h] for h in range(SC_PIECES)])
        x = jnp.where(rows < nv, x, 0.0).astype(BF16)
        gu = _dot(x, wgu_s[...]) + bgu_ref[0, 0]
        g = jnp.minimum(gu[:, :D_FF_EXPERT], SWIGLU_LIMIT)
        u = jnp.clip(gu[:, D_FF_EXPERT:], -SWIGLU_LIMIT, SWIGLU_LIMIT)
        act = g * _sigmoid(SWIGLU_ALPHA * g) * (u + 1.0)
        _store_pieces(o_ref, (), _pack_bf16_pairs(_dot(act.astype(BF16), wd_s[...]) + bd_ref[0, 0]))

    @pl.when(nv <= 0)
    def _():
        o_ref[...] = jnp.zeros_like(o_ref)


def _expert_ffn(xb, block_e, n_valid, layer, wgu, bgu, wd, bd):
    _, n_slots, _ = xb.shape
    n_blocks = n_slots // MOE_BLOCK
    depth, ne, d, f2 = wgu.shape
    new_w = jnp.concatenate([jnp.ones((1,), jnp.int32), (block_e[1:] != block_e[:-1]).astype(jnp.int32)])
    slot_blk = pl.BlockSpec((SC_PIECES, MOE_BLOCK, SC_ROW_WORDS), lambda i, be, nv, nw: (0, i, 0))
    wblk = lambda r, c: pl.BlockSpec((1, 1, r, c), lambda i, be, nv, nw: (layer, be[i], 0, 0))
    grid_spec = pltpu.PrefetchScalarGridSpec(
        num_scalar_prefetch=3,
        grid=(n_blocks,),
        in_specs=[slot_blk, wblk(d, f2), wblk(1, f2), wblk(f2 // 2, d), wblk(1, d)],
        out_specs=slot_blk,
        scratch_shapes=[pltpu.VMEM((d, f2), BF16), pltpu.VMEM((f2 // 2, d), BF16)],
    )
    return pl.pallas_call(
        _ffn_kernel,
        out_shape=jax.ShapeDtypeStruct(xb.shape, jnp.int32),
        grid_spec=grid_spec,
        compiler_params=_cparams(("arbitrary",)),
        name="expert_ffn",
    )(block_e, n_valid, new_w, xb, wgu, bgu.reshape(depth, ne, 1, f2), wd, bd.reshape(depth, ne, 1, d))


def _combine_kernel(y_ref, g_ref, x_ref, mod_ref, lng_ref, lnb_ref, xo_ref, *, n_ctx, tm):
    i = pl.program_id(1)
    is_ctx = i * tm < n_ctx
    g2 = jnp.where(is_ctx, mod_ref[0, 0, 5:6, :], mod_ref[0, 1, 5:6, :])
    gate = g_ref[...]
    y = None
    for k in range(TOP_K):
        t = gate[:, k:k + 1] * _unpack_bf16_pairs([y_ref[k, h] for h in range(SC_PIECES)])
        y = t if y is None else y + t
    xo_ref[0] = _layernorm_rows(DN_ALPHA * x_ref[0] + g2 * y, lng_ref[...], lnb_ref[...])


def _combine(yk, gate, xs, mod, ln_g, ln_b, n_ctx, tm):
    b, n, d = xs.shape
    nt = n // tm
    return pl.pallas_call(
        functools.partial(_combine_kernel, n_ctx=n_ctx, tm=tm),
        out_shape=jax.ShapeDtypeStruct((b, n, d), F32),
        grid=(b, nt),
        in_specs=[pl.BlockSpec((TOP_K, SC_PIECES, tm, SC_ROW_WORDS), lambda bb, i: (0, 0, bb * nt + i, 0)),
                  pl.BlockSpec((tm, TOP_K), lambda bb, i: (bb * nt + i, 0)),
                  pl.BlockSpec((1, tm, d), lambda bb, i: (bb, i, 0)),
                  pl.BlockSpec((1, 2, 8, d), lambda bb, i: (bb, 0, 0, 0)),
                  pl.BlockSpec((1, d), lambda bb, i: (0, 0)),
                  pl.BlockSpec((1, d), lambda bb, i: (0, 0))],
        out_specs=pl.BlockSpec((1, tm, d), lambda bb, i: (bb, i, 0)),
        compiler_params=_cparams(("arbitrary", "arbitrary")),
        name="moe_combine_ln",
    )(yk, gate, xs, mod, ln_g.reshape(1, d), ln_b.reshape(1, d))


def _sc_mesh():
    return plsc.VectorSubcoreMesh(core_axis_name="core", subcore_axis_name="subcore",
                                  num_cores=SC_CORES, num_subcores=SC_SUBCORES)


def _sc_piece_index(idx, rows):
    return jnp.concatenate([idx + h * rows for h in range(SC_PIECES)], axis=1)


def _sc_scatter_rows(x, dest, n_out):
    p, n, w = x.shape
    x2 = x.reshape(p * n, w)
    dest2 = _sc_piece_index(dest, n_out)
    kk, n2 = dest2.shape

    @functools.partial(pl.kernel, out_type=jax.ShapeDtypeStruct((p * n_out, w), x.dtype),
                       mesh=_sc_mesh(), scratch_types=[], name="moe_dispatch_scatter")
    def scatter(x_hbm, i_hbm, o_hbm):
        def body(x_vmem, i_vmem):
            for j in range(kk):
                pltpu.sync_copy(x_vmem, o_hbm.at[i_vmem.at[j]])

        pltpu.emit_pipeline(
            body, grid=(n2 // SC_WINDOW,),
            in_specs=[pl.BlockSpec((SC_WINDOW, SC_ROW_WORDS), lambda i: (i, 0)),
                      pl.BlockSpec((kk, SC_WINDOW), lambda i: (0, i))],
            out_specs=[], core_axis_name=("core", "subcore"),
            dimension_semantics=(pltpu.PARALLEL,))(x_hbm, i_hbm)

    return scatter(x2, dest2).reshape(p, n_out, w)


def _sc_gather_rows(table, idx):
    p, v, w = table.shape
    kk, n = idx.shape
    t2 = table.reshape(p * v, w)
    idx2 = _sc_piece_index(idx, v)
    m = kk * n * p

    @functools.partial(pl.kernel, out_type=jax.ShapeDtypeStruct((m, SC_ROW_WORDS), table.dtype),
                       mesh=_sc_mesh(), scratch_types=[], name="moe_combine_gather")
    def gather(t_hbm, i_hbm, o_hbm):
        def body(i_vmem, o_vmem):
            pltpu.sync_copy(t_hbm.at[i_vmem.at[0]], o_vmem)

        pltpu.emit_pipeline(
            body, grid=(m // SC_WINDOW,),
            in_specs=[pl.BlockSpec((1, SC_WINDOW), lambda i: (0, i))],
            out_specs=[pl.BlockSpec((SC_WINDOW, SC_ROW_WORDS), lambda i: (i, 0))],
            core_axis_name=("core", "subcore"),
            dimension_semantics=(pltpu.PARALLEL,))(i_hbm, o_hbm)

    return gather(t2, idx2.reshape(1, m)).reshape(kk, p, n, w)


def _moe(h2p, top_e, top_g, rank, cnt, layer, wgu, bgu, wd, bd):
    p, b, n, w = h2p.shape
    n_tok = b * n
    nk = n_tok * TOP_K
    experts = jnp.arange(N_EXPERTS, dtype=jnp.int32)
    counts = cnt[:, 0].astype(jnp.int32)
    padded = (counts + MOE_BLOCK - 1) // MOE_BLOCK * MOE_BLOCK
    ends_p = jnp.cumsum(padded)
    start_p = ends_p - padded
    n_blocks = (nk + N_EXPERTS * (MOE_BLOCK - 1) + MOE_BLOCK - 1) // MOE_BLOCK
    block_start = jnp.arange(n_blocks, dtype=jnp.int32) * MOE_BLOCK
    block_e = jnp.minimum(jnp.sum(ends_p[None, :] <= block_start[:, None], axis=1), N_EXPERTS - 1).astype(jnp.int32)
    is_e = block_e[:, None] == experts[None, :]
    filled = jnp.sum(jnp.where(is_e, (start_p + counts)[None, :], 0), axis=1)
    n_valid = jnp.clip(filled - block_start, 0, MOE_BLOCK).astype(jnp.int32)
    dest = rank + jnp.sum(jnp.where(top_e[..., None] == experts, start_p, 0), axis=-1)
    dest = dest.transpose(1, 0, 2).reshape(TOP_K, n_tok)
    gate = top_g.transpose(0, 2, 1).reshape(n_tok, TOP_K)
    xb = _sc_scatter_rows(h2p.reshape(p, n_tok, w), dest, n_blocks * MOE_BLOCK)
    yb = _expert_ffn(xb, block_e, n_valid, layer, wgu, bgu, wd, bd)
    yk = _sc_gather_rows(yb, dest)
    return yk, gate


def kernel(x, c, ctx, c_ctx, w_mod, b_mod, w_in, ret_decay, ret_gn_g, ret_gn_b, hg_lb, hg_norm_g,
           na_rpb, gq_qn_g, gq_kn_g, w_branch, w_out, ln_g, ln_b, w_router, b_router, w_gu, b_gu,
           w_down, b_down):
    b_, t_, d = x.shape
    n_ctx = ctx.shape[1]
    depth = w_mod.shape[0]
    tm = 256

    sm = jax.nn.softmax(hg_lb.astype(F32), axis=1)
    lower = jnp.cumsum(sm, axis=1) - sm[:, :1]
    log_gamma = jax.nn.log_sigmoid(ret_decay.astype(F32)).reshape(depth, 2 * RET_HEADS)

    cc = jnp.concatenate([c, c_ctx[None, :], jnp.zeros((16 - b_ - 1, d), F32)], axis=0)
    modv = _mod_all(cc, w_mod, b_mod).reshape(depth, 16, 6, d)
    one = jnp.asarray([0.0, 1.0, 0.0, 0.0, 1.0, 0.0], F32)[None, None, :, None]
    modv = (modv + one)[:, :, jnp.asarray([1, 0, 2, 4, 3, 5])]
    modv = jnp.concatenate([modv, jnp.zeros((depth, 16, 2, d), F32)], axis=2)
    mod = jnp.stack([jnp.broadcast_to(modv[:, b_:b_ + 1], (depth, b_, 8, d)), modv[:, :b_]], axis=2)

    cos_r, sin_r = _rope_tables(n_ctx, t_, RET_DK)
    cos_g, sin_g = _rope_tables(n_ctx, t_, GQA_HD)

    xs = jnp.concatenate([ctx, x], axis=1)
    tm_in = 768 if xs.shape[1] % 768 == 0 else tm
    w_in_b = _w_in_prep(w_in)
    for l in range(depth):
        z = _in_proj(xs, mod[l], w_in_b[l], n_ctx, tm_in, PROJ_PAD // 4)
        ret_o = _retention(z, log_gamma[l], cos_r, sin_r, ret_gn_g[l], ret_gn_b[l], n_ctx)
        hg_o = _hgrn2(z, lower[:, l], hg_norm_g[l], n_ctx)
        na_o = _neighbourhood(z, _na_bias_table(na_rpb[l]), n_ctx)
        qh, kd, vd = _gqa_prep(z, cos_g, sin_g, gq_qn_g[l], gq_kn_g[l], tm)
        gq_o = _gqa_attn(qh, kd, vd, n_ctx, tm)
        wr_hi, wr_lo = _split_bf16(w_router[l].T)
        xs, h2p, top_e, top_g, rank, cnt = _merge(z, (ret_o, hg_o, na_o, gq_o), xs, mod[l],
                                                  w_branch[l].astype(BF16), w_out[l].astype(BF16),
                                                  ln_g[l, 0], ln_b[l, 0], wr_hi, wr_lo, b_router[l], n_ctx, tm)
        yk, gate = _moe(h2p, top_e, top_g, rank, cnt, l, w_gu, b_gu, w_down, b_down)
        xs = _combine(yk, gate, xs, mod[l], ln_g[l, 1], ln_b[l, 1], n_ctx, tm)
    return xs[:, n_ctx:]
```

```python
import functools

import jax
import jax.numpy as jnp
from jax import lax
import numpy as np
from jax.experimental import pallas as pl
from jax.experimental.pallas import tpu as pltpu
from jax.experimental.pallas import tpu_sc as plsc

D_MODEL = 1024
DEPTH = 4
GRID_W = 64
N_BRANCH = 4
BRANCH_W = D_MODEL // 2
RET_HEADS = 4
RET_DK = BRANCH_W // RET_HEADS
HG_HEADS = 4
HG_DK = BRANCH_W // HG_HEADS
NA_HEADS = 8
NA_HD = BRANCH_W // NA_HEADS
NA_KH = 8
NA_KW = 16
GQA_HEADS = 8
GQA_KV_HEADS = 2
GQA_HD = BRANCH_W // GQA_HEADS
GQA_KV_W = GQA_KV_HEADS * GQA_HD
ROPE_BASE = 10000.0
N_EXPERTS = 32
TOP_K = 4
D_FF_EXPERT = D_MODEL
SWIGLU_LIMIT = 7.0
SWIGLU_ALPHA = 1.702
MOE_BLOCK = 256
LN_EPS = 1e-5
NORM_EPS = 1e-6
NEG_INF = -1e30
DN_ALPHA = (2 * DEPTH) ** 0.25

LANES = 128
BF16 = jnp.bfloat16
F32 = jnp.float32
VMEM_LIMIT = 56 * 1024 * 1024
SC_CORES = 2
SC_SUBCORES = 16
SC_WINDOW = 128
SC_ROW_WORDS = 256
SC_PIECES = D_MODEL // 2 // SC_ROW_WORDS

PROJ_ORIG = 4 * BRANCH_W + 5 * BRANCH_W + 3 * BRANCH_W + BRANCH_W + 2 * GQA_KV_W
PROJ_TOTAL = PROJ_ORIG + N_BRANCH * D_MODEL
PROJ_PAD = 11264
GATE_B = 0
RET_B = (N_BRANCH * D_MODEL) // LANES
HG_B = RET_B + 16
NA_B = HG_B + 20
GQ_B = NA_B + 12

RET_CHUNK = 256
HG_CHUNK = 128
HG_LEVELS = (64, 32, 16, 8, 4, 2, 1, 0)


def _cparams(sem):
    return pltpu.CompilerParams(dimension_semantics=sem, vmem_limit_bytes=VMEM_LIMIT)


def _dot(a, b):
    return jnp.dot(a, b, preferred_element_type=F32)


def _dot_nt(a, b):
    return lax.dot_general(a, b, (((1,), (1,)), ((), ())), preferred_element_type=F32)


def _dot_tn(a, b):
    return lax.dot_general(a, b, (((0,), (0,)), ((), ())), preferred_element_type=F32)


def _split_bf16(x):
    hi = x.astype(BF16)
    lo = (x - hi.astype(F32)).astype(BF16)
    return hi, lo


def _sigmoid(x):
    return 1.0 / (1.0 + jnp.exp(-x))


def _silu(x):
    return x * _sigmoid(x)


def _mod_kernel(c_ref, w_ref, b_ref, o_ref):
    s = _silu(c_ref[...]).astype(BF16)
    o_ref[0] = _dot(s, w_ref[0].astype(BF16)) + b_ref[0]


def _mod_all(cc, w_mod, b_mod):
    depth, d, n = w_mod.shape
    r = cc.shape[0]
    tn = 1536
    return pl.pallas_call(
        _mod_kernel,
        out_shape=jax.ShapeDtypeStruct((depth, r, n), F32),
        grid=(depth, n // tn),
        in_specs=[pl.BlockSpec((r, d), lambda l, j: (0, 0)),
                  pl.BlockSpec((1, d, tn), lambda l, j: (l, 0, j)),
                  pl.BlockSpec((1, 1, tn), lambda l, j: (l, 0, j))],
        out_specs=pl.BlockSpec((1, r, tn), lambda l, j: (l, 0, j)),
        compiler_params=_cparams(("arbitrary", "arbitrary")),
        name="adaln_mod",
    )(cc, w_mod, b_mod.reshape(depth, 1, n))


W_PREP_TILE = 256


def _w_in_prep_kernel(w_ref, o_ref, *, n_src):
    j = pl.program_id(1)
    o_ref[0] = jnp.where(j < n_src, w_ref[0], 0.0).astype(BF16)


def _w_in_prep(w_in):
    depth, d, total = w_in.shape
    n_src = total // W_PREP_TILE
    shift = PROJ_ORIG // W_PREP_TILE
    src = lambda l, j: (l, 0, jnp.where(j < n_src, (j + shift) % n_src, 0))
    return pl.pallas_call(
        functools.partial(_w_in_prep_kernel, n_src=n_src),
        out_shape=jax.ShapeDtypeStruct((depth, d, PROJ_PAD), BF16),
        grid=(depth, PROJ_PAD // W_PREP_TILE),
        in_specs=[pl.BlockSpec((1, d, W_PREP_TILE), src)],
        out_specs=pl.BlockSpec((1, d, W_PREP_TILE), lambda l, j: (l, 0, j)),
        compiler_params=_cparams(("arbitrary", "arbitrary")),
        name="w_in_prep",
    )(w_in)


def _in_proj_kernel(x_ref, mod_ref, w_ref, o_ref, *, n_ctx, tm):
    i = pl.program_id(2)
    x = x_ref[0]
    mc = mod_ref[0, 0]
    ml = mod_ref[0, 1]
    rows = i * tm + lax.broadcasted_iota(jnp.int32, (tm, 1), 0)
    is_ctx = rows < n_ctx
    scale = jnp.where(is_ctx, mc[0:1, :], ml[0:1, :])
    shift = jnp.where(is_ctx, mc[1:2, :], ml[1:2, :])
    h = (x * scale + shift).astype(BF16)
    o_ref[0] = _dot(h, w_ref[...]).astype(o_ref.dtype)


def _in_proj(xs, mod, w, n_ctx, tm, tn):
    b, n, d = xs.shape
    ncol = w.shape[1]
    return pl.pallas_call(
        functools.partial(_in_proj_kernel, n_ctx=n_ctx, tm=tm),
        out_shape=jax.ShapeDtypeStruct((b, n, ncol), BF16),
        grid=(ncol // tn, b, n // tm),
        in_specs=[pl.BlockSpec((1, tm, d), lambda j, bb, i: (bb, i, 0)),
                  pl.BlockSpec((1, 2, 8, d), lambda j, bb, i: (bb, 0, 0, 0)),
                  pl.BlockSpec((d, tn), lambda j, bb, i: (0, j))],
        out_specs=pl.BlockSpec((1, tm, tn), lambda j, bb, i: (bb, i, j)),
        compiler_params=_cparams(("arbitrary", "arbitrary", "arbitrary")),
        name="in_proj",
    )(xs, mod, w)


def _rope_tables(n_ctx, t, head_dim):
    idx = jnp.arange(t, dtype=jnp.int32)
    row = (idx // GRID_W).astype(F32)
    col = (idx % GRID_W).astype(F32)
    n = head_dim // 2
    inv = ROPE_BASE ** (-jnp.arange(0, n, 2, dtype=F32) / n)

    def half(pos):
        ang = pos[:, None] * inv[None, :]
        c, s = jnp.cos(ang), jnp.sin(ang)
        return jnp.concatenate([c, c], -1), jnp.concatenate([-s, s], -1)

    cr, sr = half(row)
    cc, sc = half(col)
    cos = jnp.concatenate([cr, cc], -1)
    sin = jnp.concatenate([sr, sc], -1)
    reps = LANES // head_dim
    cos = jnp.tile(cos, (1, reps))
    sin = jnp.tile(sin, (1, reps))
    cos = jnp.concatenate([jnp.ones((n_ctx, LANES), F32), cos], 0)
    sin = jnp.concatenate([jnp.zeros((n_ctx, LANES), F32), sin], 0)
    return cos, sin


def _rope(x, cos, sin, quarter):
    lane = lax.broadcasted_iota(jnp.int32, x.shape, 1)
    first = (lane & (2 * quarter - 1)) < quarter
    rot = jnp.where(first, pltpu.roll(x, LANES - quarter, 1), pltpu.roll(x, quarter, 1))
    return x * cos + rot * sin


def _ret_kernel(lg_ref, q_ref, k_ref, v_ref, g_ref, cos_ref, sin_ref, gng_ref, gnb_ref, o_ref,
                ks_ref, sb_ref, *, n_ctx, chunk):
    hh = pl.program_id(1)
    lgf = lg_ref[hh]
    lgb = lg_ref[RET_HEADS + hh]
    n = q_ref.shape[1]
    c_ = chunk
    nch = n // c_
    n_cc = n_ctx // c_
    dk = q_ref.shape[2]

    ri = lax.broadcasted_iota(jnp.int32, (c_, dk), 0).astype(F32)
    df = jnp.exp((ri + 1.0) * lgf)
    db = jnp.exp((c_ - ri) * lgb)
    ef = jnp.exp((c_ - 1.0 - ri) * lgf)
    eb = jnp.exp(ri * lgb)
    gfc = jnp.exp(c_ * lgf)
    gbc = jnp.exp(c_ * lgb)
    ii = lax.broadcasted_iota(jnp.int32, (c_, c_), 0)
    jj = lax.broadcasted_iota(jnp.int32, (c_, c_), 1)
    dij = (ii - jj).astype(F32)
    w = jnp.where(ii >= jj, jnp.exp(dij * lgf), jnp.exp(-dij * lgb))

    def rope_k(c, carry):
        r0 = pl.multiple_of(c * c_, c_)
        kc = k_ref[0, pl.ds(r0, c_), :].astype(F32) * (RET_DK ** -0.5)
        kc = _rope(kc, cos_ref[pl.ds(r0, c_), :], sin_ref[pl.ds(r0, c_), :], 32)
        ks_ref[pl.ds(r0, c_), :] = kc.astype(BF16)
        return carry

    lax.fori_loop(0, nch, rope_k, 0, unroll=3)

    def bwd_step(c, s):
        r0 = pl.multiple_of(c * c_, c_)
        sb_ref[c] = s
        kc = (ks_ref[pl.ds(r0, c_), :].astype(F32) * eb).astype(BF16)
        return s * gbc + _dot_tn(kc, v_ref[0, pl.ds(r0, c_), :])

    s = jnp.zeros((dk, dk), F32)
    s = lax.fori_loop(0, n_cc, lambda t, s_: bwd_step(n_cc - 1 - t, s_), s)
    lax.fori_loop(0, nch - n_cc, lambda t, s_: bwd_step(nch - 1 - t, s_), s, unroll=4)

    gng = gng_ref[...]
    gnb = gnb_ref[...]

    def fwd_step(c, sf):
        r0 = pl.multiple_of(c * c_, c_)
        qc = _rope(q_ref[0, pl.ds(r0, c_), :].astype(F32), cos_ref[pl.ds(r0, c_), :],
                   sin_ref[pl.ds(r0, c_), :], 32)
        kc = ks_ref[pl.ds(r0, c_), :]
        vc = v_ref[0, pl.ds(r0, c_), :]
        a = _dot_nt(qc.astype(BF16), kc) * w
        qi = jnp.concatenate([qc * df, qc * db], axis=1).astype(BF16)
        si = jnp.concatenate([sf, sb_ref[c]], axis=0).astype(BF16)
        o = _dot(a.astype(BF16), vc) + _dot(qi, si)
        sf_new = sf * gfc + _dot_tn((kc.astype(F32) * ef).astype(BF16), vc)
        mu = jnp.mean(o, axis=-1, keepdims=True)
        oc = o - mu
        var = jnp.mean(oc * oc, axis=-1, keepdims=True)
        y = oc * lax.rsqrt(var + LN_EPS) * gng + gnb
        gate = g_ref[0, pl.ds(r0, c_), :].astype(F32)
        o_ref[0, pl.ds(r0, c_), :] = (y * _silu(gate)).astype(o_ref.dtype)
        return sf_new

    lax.fori_loop(0, nch, fwd_step, jnp.zeros((dk, dk), F32), unroll=3)


def _retention(z, log_gamma, cos, sin, gn_g, gn_b, n_ctx):
    b, n, _ = z.shape
    blk = lambda off: pl.BlockSpec((1, n, LANES), lambda bb, h, lg: (bb, 0, off + h))
    grid_spec = pltpu.PrefetchScalarGridSpec(
        num_scalar_prefetch=1,
        grid=(b, RET_HEADS),
        in_specs=[blk(RET_B), blk(RET_B + 4), blk(RET_B + 8), blk(RET_B + 12),
                  pl.BlockSpec((n, LANES), lambda bb, h, lg: (0, 0)),
                  pl.BlockSpec((n, LANES), lambda bb, h, lg: (0, 0)),
                  pl.BlockSpec((1, LANES), lambda bb, h, lg: (0, h)),
                  pl.BlockSpec((1, LANES), lambda bb, h, lg: (0, h))],
        out_specs=pl.BlockSpec((1, n, LANES), lambda bb, h, lg: (bb, 0, h)),
        scratch_shapes=[pltpu.VMEM((n, LANES), BF16),
                        pltpu.VMEM((n // RET_CHUNK, RET_DK, RET_DK), F32)],
    )
    return pl.pallas_call(
        functools.partial(_ret_kernel, n_ctx=n_ctx, chunk=RET_CHUNK),
        out_shape=jax.ShapeDtypeStruct((b, n, BRANCH_W), BF16),
        grid_spec=grid_spec,
        compiler_params=_cparams(("arbitrary", "arbitrary")),
        name="retention",
    )(log_gamma, z, z, z, z, cos, sin, gn_g.reshape(1, -1), gn_b.reshape(1, -1))


def _hg_constants():
    c = HG_CHUNK
    r = np.arange(c)[:, None]
    t = np.arange(c)[None, :]
    mats = []
    level = np.full((2, c, c), len(HG_LEVELS), np.int32)
    for d in range(2):
        blocks = []
        for li, s in enumerate(HG_LEVELS):
            if s == 0:
                level[d][np.arange(c), np.arange(c)] = li
                continue
            base = (r // (2 * s)) * (2 * s)
            if d == 0:
                m = base + s - 1
                mat = (t > np.minimum(r, m)) & (t <= np.maximum(r, m))
                q_side = (r % (2 * s)) >= s
            else:
                m = base + s
                mat = (t >= np.minimum(r, m)) & (t < np.maximum(r, m))
                q_side = (r % (2 * s)) < s
            blocks.append(mat.astype(np.float32))
            same = (r // (2 * s)) == (t // (2 * s))
            k_side_t = ((t % (2 * s)) < s) if d == 0 else ((t % (2 * s)) >= s)
            level[d][same & q_side & k_side_t] = li
        if d == 0:
            blocks.append((t <= r).astype(np.float32))
            blocks.append((t > r).astype(np.float32))
        else:
            blocks.append((t >= r).astype(np.float32))
            blocks.append((t < r).astype(np.float32))
        mats.append(np.concatenate(blocks, 0))
    mats = np.stack(mats)
    masks = (level[:, None] == np.arange(len(HG_LEVELS))[None, :, None, None]).astype(np.float32)
    upper = np.stack([np.broadcast_to(((np.arange(c) % (2 * s)) >= s)[:, None], (c, c)) for s in HG_LEVELS if s > 0])
    return np.concatenate([mats, mats], axis=2), np.maximum(masks[0], masks[1]), upper.astype(np.float32)


def _hg_gates(zf, lb):
    en = jnp.exp(-jnp.abs(zf))
    inv = 1.0 / (1.0 + en)
    pos = zf >= 0
    sg = jnp.where(pos, inv, en * inv)
    sgn = jnp.where(pos, en * inv, inv)
    one_m = 1.0 - lb
    return jnp.log(lb + one_m * sg), one_m * sgn


def _hg_kernel(q_ref, ff_ref, fb_ref, v_ref, g_ref, lb_ref, ng_ref, m_ref, lv_ref, up_ref, o_ref,
               lfb_ref, kb_ref, sb_ref, ktb_ref, dtot_ref, *, n_ctx):
    c_ = HG_CHUNK
    n = q_ref.shape[1]
    nch = n // c_
    n_cc = n_ctx // c_
    dk = q_ref.shape[2]
    nl = len(HG_LEVELS)
    lbf = lb_ref[0:1, :]
    lbb = lb_ref[1:2, :]

    def mdot(mat, lf):
        hi, lo = _split_bf16(lf)
        return _dot(mat, jnp.concatenate([hi, lo], axis=0))

    def prep_b(c, carry):
        r0 = pl.multiple_of(c * c_, c_)
        lf, key = _hg_gates(fb_ref[0, pl.ds(r0, c_), :].astype(F32), lbb)
        lfb_ref[pl.ds(r0, c_), :] = lf
        kb_ref[pl.ds(r0, c_), :] = key
        e = mdot(m_ref[1, (nl - 1) * c_:(nl + 1) * c_, :], lf)
        ktb_ref[pl.ds(r0, c_), :] = (key * jnp.exp(e[c_:2 * c_, :])).astype(BF16)
        dtot_ref[c] = jnp.broadcast_to(jnp.exp(e[0:1, :]), (8, dk))
        return carry

    lax.fori_loop(0, nch, prep_b, 0, unroll=3)

    def bwd_step(c, st):
        r0 = pl.multiple_of(c * c_, c_)
        sb_ref[c] = st
        return st * dtot_ref[c][0:1, :] + _dot_tn(v_ref[0, pl.ds(r0, c_), :], ktb_ref[pl.ds(r0, c_), :])

    st = jnp.zeros((dk, dk), F32)
    st = lax.fori_loop(0, n_cc, lambda t, s_: bwd_step(n_cc - 1 - t, s_), st, unroll=2)
    lax.fori_loop(0, nch - n_cc, lambda t, s_: bwd_step(nch - 1 - t, s_), st, unroll=8)

    ng = ng_ref[...]

    def mdot_pair(mat, lf0, lf1):
        hi0, lo0 = _split_bf16(lf0)
        hi1, lo1 = _split_bf16(lf1)
        rhs = jnp.concatenate([jnp.concatenate([hi0, hi1], axis=1), jnp.concatenate([lo0, lo1], axis=1)], axis=0)
        return _dot(mat, rhs)

    def intra(qs, keyf, keyb, ef, eb):
        a = jnp.zeros((c_, c_), F32)
        qsb, kfb, kbb = qs.astype(BF16), keyf.astype(BF16), keyb.astype(BF16)
        for li, s in enumerate(HG_LEVELS):
            if s == 0:
                qt = jnp.concatenate([qsb, qsb], axis=1)
                kt = jnp.concatenate([kfb, kbb], axis=1)
            else:
                decf = jnp.exp(ef[li * c_:(li + 1) * c_, :])
                decb = jnp.exp(eb[li * c_:(li + 1) * c_, :])
                up = up_ref[li]
                dqf = decf * up
                dkb = decb * up
                qt = jnp.concatenate([qsb * dqf.astype(BF16), qsb * (decb - dkb).astype(BF16)], axis=1)
                kt = jnp.concatenate([kfb * (decf - dqf).astype(BF16), kbb * dkb.astype(BF16)], axis=1)
            a = a + _dot_nt(qt, kt) * lv_ref[li]
        cumf = ef[(nl - 1) * c_:nl * c_, :]
        restf = ef[nl * c_:(nl + 1) * c_, :]
        cumb = eb[(nl - 1) * c_:nl * c_, :]
        return a, cumf, restf, cumb

    def fwd_chunk(c, sf, qs, keyf, ef, eb):
        r0 = pl.multiple_of(c * c_, c_)
        vc = v_ref[0, pl.ds(r0, c_), :]
        a, cumf, restf, cumb = intra(qs, keyf, kb_ref[pl.ds(r0, c_), :], ef, eb)
        qi = jnp.concatenate([qs * jnp.exp(cumf), qs * jnp.exp(cumb)], axis=1).astype(BF16)
        si = jnp.concatenate([sf, sb_ref[c]], axis=1).astype(BF16)
        o = _dot(a.astype(BF16), vc) + _dot_nt(qi, si)
        totf = cumf[c_ - 1:c_, :]
        sf_new = sf * jnp.exp(totf) + _dot_tn(vc, (keyf * jnp.exp(restf)).astype(BF16))
        on = o * lax.rsqrt(jnp.mean(o * o, axis=-1, keepdims=True) + NORM_EPS) * ng
        gate = g_ref[0, pl.ds(r0, c_), :].astype(F32)
        o_ref[0, pl.ds(r0, c_), :] = (on * _silu(gate)).astype(o_ref.dtype)
        return sf_new

    def fwd_pair(p, sf):
        r0 = pl.multiple_of(p * (2 * c_), 2 * c_)
        qs, keyf, lff = [], [], []
        for t in range(2):
            rows = pl.ds(r0 + t * c_, c_)
            qs.append(_silu(q_ref[0, rows, :].astype(F32)))
            lf, key = _hg_gates(ff_ref[0, rows, :].astype(F32), lbf)
            lff.append(lf)
            keyf.append(key)
        ef = mdot_pair(m_ref[0], lff[0], lff[1])
        eb = mdot_pair(m_ref[1], lfb_ref[pl.ds(r0, c_), :], lfb_ref[pl.ds(r0 + c_, c_), :])
        for t in range(2):
            sf = fwd_chunk(2 * p + t, sf, qs[t], keyf[t], ef[:, t * dk:(t + 1) * dk], eb[:, t * dk:(t + 1) * dk])
        return sf

    lax.fori_loop(0, nch // 2, fwd_pair, jnp.zeros((dk, dk), F32))


def _hgrn2(z, lower, norm_g, n_ctx):
    b, n, _ = z.shape
    mats, level, upper = _hg_constants()
    mats = jnp.asarray(mats, BF16)
    level = jnp.asarray(level)
    upper = jnp.asarray(upper)
    blk = lambda off: pl.BlockSpec((1, n, LANES), lambda bb, h: (bb, 0, off + h))
    return pl.pallas_call(
        functools.partial(_hg_kernel, n_ctx=n_ctx),
        out_shape=jax.ShapeDtypeStruct((b, n, BRANCH_W), BF16),
        grid=(b, HG_HEADS),
        in_specs=[blk(HG_B), blk(HG_B + 4), blk(HG_B + 8), blk(HG_B + 12), blk(HG_B + 16),
                  pl.BlockSpec((2, LANES), lambda bb, h: (0, h)),
                  pl.BlockSpec((1, LANES), lambda bb, h: (0, h)),
                  pl.BlockSpec(mats.shape, lambda bb, h: (0, 0, 0)),
                  pl.BlockSpec(level.shape, lambda bb, h: (0, 0, 0)),
                  pl.BlockSpec(upper.shape, lambda bb, h: (0, 0, 0))],
        out_specs=pl.BlockSpec((1, n, LANES), lambda bb, h: (bb, 0, h)),
        scratch_shapes=[pltpu.VMEM((n, LANES), F32), pltpu.VMEM((n, LANES), F32),
                        pltpu.VMEM((n // HG_CHUNK, HG_DK, HG_DK), F32),
                        pltpu.VMEM((n, LANES), BF16), pltpu.VMEM((n // HG_CHUNK, 8, LANES), F32)],
        compiler_params=_cparams(("arbitrary", "arbitrary")),
        name="hgrn2",
    )(z, z, z, z, z, lower, norm_g.reshape(1, -1), mats, level, upper)


def _na_bias_table(rpb):
    qc = np.arange(GRID_W)[:, None]
    kc = np.arange(GRID_W)[None, :]
    col0 = np.clip(qc - NA_KW // 2, 0, GRID_W - NA_KW)
    valid = (kc >= col0) & (kc < col0 + NA_KW)
    dc = np.clip(kc - qc + (NA_KW - 1), 0, 2 * NA_KW - 2)
    pick = (dc[:, :, None] == np.arange(2 * NA_KW - 1)).astype(np.float32)
    t15 = jnp.einsum('hrc,qkc->hrqk', rpb.astype(F32), pick, precision=lax.Precision.HIGHEST)
    t = jnp.stack([t15[:, dl:dl + NA_KH] for dl in range(NA_KH)], axis=1)
    t = jnp.where(jnp.asarray(valid)[None, None, None], t, NEG_INF)
    t = t.transpose(0, 1, 3, 2, 4).reshape(rpb.shape[0] // 2, 2, NA_KH, GRID_W, NA_KH * GRID_W)
    return t.transpose(0, 2, 1, 3, 4).reshape(rpb.shape[0] // 2, NA_KH, 2 * GRID_W, NA_KH * GRID_W)


def _softmax_pv(parts):
    m = parts[0][0].max(axis=-1, keepdims=True)
    for s, _ in parts[1:]:
        m = jnp.maximum(m, s.max(axis=-1, keepdims=True))
    l = None
    o = None
    for s, v in parts:
        p = jnp.exp(s - m)
        ls = p.sum(axis=-1, keepdims=True)
        os_ = _dot(p.astype(BF16), v)
        l = ls if l is None else l + ls
        o = os_ if o is None else o + os_
    return o / l


def _na_kernel(q_ref, k_ref, v_ref, tab_ref, o_ref, *, n_ctx):
    n = q_ref.shape[1]
    rows = (n - n_ctx) // GRID_W
    wlen = NA_KH * GRID_W
    scale = NA_HD ** -0.5
    kx = k_ref[0, 0:n_ctx, :]
    vx = v_ref[0, 0:n_ctx, :]

    def head_mask(shape, hh):
        lane = lax.broadcasted_iota(jnp.int32, shape, 1)
        return (lane < NA_HD) if hh == 0 else (lane >= NA_HD)

    def stack_heads(qf):
        return jnp.concatenate([jnp.where(head_mask(qf.shape, 0), qf, 0.0),
                                jnp.where(head_mask(qf.shape, 1), qf, 0.0)], axis=0).astype(BF16)

    def unstack_heads(o):
        m = o.shape[0] // 2
        return jnp.where(head_mask((m, LANES), 0), o[:m], o[m:])

    qx = stack_heads(q_ref[0, 0:n_ctx, :].astype(F32) * scale)
    o_ref[0, 0:n_ctx, :] = unstack_heads(_softmax_pv([(_dot_nt(qx, kx), vx)])).astype(o_ref.dtype)

    def row_step(r, carry):
        start = jnp.clip(r - NA_KH // 2, 0, rows - NA_KH)
        delta = start - r + (NA_KH - 1)
        q0 = pl.multiple_of(n_ctx + r * GRID_W, GRID_W)
        k0 = pl.multiple_of(n_ctx + start * GRID_W, GRID_W)
        qr = stack_heads(q_ref[0, pl.ds(q0, GRID_W), :].astype(F32) * scale)
        kl = k_ref[0, pl.ds(k0, wlen), :]
        vl = v_ref[0, pl.ds(k0, wlen), :]
        s_loc = _dot_nt(qr, kl) + tab_ref[0, delta]
        s_ctx = _dot_nt(qr, kx)
        o = unstack_heads(_softmax_pv([(s_loc, vl), (s_ctx, vx)]))
        o_ref[0, pl.ds(q0, GRID_W), :] = o.astype(o_ref.dtype)
        return carry

    lax.fori_loop(0, rows, row_step, 0, unroll=4)


def _neighbourhood(z, table, n_ctx):
    b, n, _ = z.shape
    npair = NA_HEADS // 2
    blk = lambda off: pl.BlockSpec((1, n, LANES), lambda bb, p: (bb, 0, off + p))
    return pl.pallas_call(
        functools.partial(_na_kernel, n_ctx=n_ctx),
        out_shape=jax.ShapeDtypeStruct((b, n, BRANCH_W), BF16),
        grid=(b, npair),
        in_specs=[blk(NA_B), blk(NA_B + 4), blk(NA_B + 8),
                  pl.BlockSpec((1, NA_KH, 2 * GRID_W, NA_KH * GRID_W), lambda bb, p: (p, 0, 0, 0))],
        out_specs=pl.BlockSpec((1, n, LANES), lambda bb, p: (bb, 0, p)),
        compiler_params=_cparams(("arbitrary", "arbitrary")),
        name="neighbourhood_attn",
    )(z, z, z, table)


def _gqa_kernel(q_ref, k_ref, v_ref, cos_ref, sin_ref, qg_ref, kg_ref, bd_ref, o_ref, ks_ref, vs_ref, *, n_ctx):
    i = pl.program_id(1)
    tq = q_ref.shape[1]
    n = k_ref.shape[1]
    grp = GQA_HEADS // GQA_KV_HEADS
    bd = bd_ref[...]

    def norm_rope(x, g, cos, sin):
        hi, lo = _split_bf16(x * x)
        ms = _dot(hi, bd) + _dot(lo, bd)
        xn = x * lax.rsqrt(ms + NORM_EPS) * g
        return _rope(xn, cos, sin, GQA_HD // 4)

    @pl.when(i == 0)
    def _():
        def prep(c, carry):
            rows = pl.ds(pl.multiple_of(c * tq, tq), tq)
            kn = norm_rope(k_ref[0, rows, :].astype(F32), kg_ref[...], cos_ref[rows, :], sin_ref[rows, :])
            vv = v_ref[0, rows, :].astype(F32)
            lo_half = lax.broadcasted_iota(jnp.int32, kn.shape, 1) < GQA_HD
            ksw = pltpu.roll(kn, GQA_HD, 1)
            vsw = pltpu.roll(vv, GQA_HD, 1)
            ks_ref[0, rows, :] = jnp.where(lo_half, kn, ksw).astype(BF16)
            ks_ref[1, rows, :] = jnp.where(lo_half, ksw, kn).astype(BF16)
            vs_ref[0, rows, 0:LANES] = jnp.where(lo_half, vv, vsw).astype(BF16)
            vs_ref[1, rows, 0:LANES] = jnp.where(lo_half, vsw, vv).astype(BF16)
            ones = jnp.ones((tq, LANES), BF16)
            vs_ref[0, rows, LANES:] = ones
            vs_ref[1, rows, LANES:] = ones
            return carry

        lax.fori_loop(0, n // tq, prep, 0)

    qrows = pl.ds(pl.multiple_of(i * tq, tq), tq)
    cos = cos_ref[qrows, :]
    sin = sin_ref[qrows, :]
    qg = qg_ref[...]

    def attend(nk):
        for j in range(BRANCH_W // LANES):
            x = q_ref[0, :, j * LANES:(j + 1) * LANES].astype(F32)
            qb = (norm_rope(x, qg, cos, sin) * (GQA_HD ** -0.5)).astype(BF16)
            lane = lax.broadcasted_iota(jnp.int32, qb.shape, 1)
            g = (2 * j) // grp
            zero = jnp.zeros_like(qb)
            qs = jnp.concatenate([jnp.where(lane < GQA_HD, qb, zero), jnp.where(lane >= GQA_HD, qb, zero)], axis=0)
            sc = _dot_nt(qs, ks_ref[g, 0:nk, :])
            p = jnp.exp(sc - sc.max(axis=-1, keepdims=True)).astype(BF16)
            ov = _dot(p, vs_ref[g, 0:nk, :])
            o = ov[:, :LANES] / ov[:, LANES:]
            o_ref[0, :, j * LANES:(j + 1) * LANES] = jnp.where(lane < GQA_HD, o[:tq], o[tq:]).astype(o_ref.dtype)

    @pl.when(i * tq < n_ctx)
    def _():
        attend(n_ctx)

    @pl.when(i * tq >= n_ctx)
    def _():
        attend(n)


def _gqa(z, cos, sin, qn_g, kn_g, n_ctx, tq):
    b, n, _ = z.shape
    bd = np.kron(np.eye(LANES // GQA_HD), np.ones((GQA_HD, GQA_HD))) / GQA_HD
    qg = jnp.tile(qn_g.reshape(1, -1), (1, LANES // GQA_HD))
    kg = jnp.tile(kn_g.reshape(1, -1), (1, LANES // GQA_HD))
    qb = (GQ_B * LANES) // BRANCH_W
    const = lambda shape: pl.BlockSpec(shape, lambda bb, i: (0,) * len(shape))
    return pl.pallas_call(
        functools.partial(_gqa_kernel, n_ctx=n_ctx),
        out_shape=jax.ShapeDtypeStruct((b, n, BRANCH_W), BF16),
        grid=(b, n // tq),
        in_specs=[pl.BlockSpec((1, tq, BRANCH_W), lambda bb, i: (bb, i, qb)),
                  pl.BlockSpec((1, n, LANES), lambda bb, i: (bb, 0, GQ_B + 4)),
                  pl.BlockSpec((1, n, LANES), lambda bb, i: (bb, 0, GQ_B + 5)),
                  const((n, LANES)), const((n, LANES)), const((1, LANES)), const((1, LANES)),
                  const((LANES, LANES))],
        out_specs=pl.BlockSpec((1, tq, BRANCH_W), lambda bb, i: (bb, i, 0)),
        scratch_shapes=[pltpu.VMEM((GQA_KV_HEADS, n, LANES), BF16),
                        pltpu.VMEM((GQA_KV_HEADS, n, 2 * LANES), BF16)],
        compiler_params=_cparams(("arbitrary", "arbitrary")),
        name="gqa_attn",
    )(z, z, z, cos, sin, qg, kg, jnp.asarray(bd, BF16))


def _layernorm_rows(v, g, b):
    mu = jnp.mean(v, axis=-1, keepdims=True)
    vc = v - mu
    var = jnp.mean(vc * vc, axis=-1, keepdims=True)
    return vc * lax.rsqrt(var + LN_EPS) * g + b


def _pack_bf16_pairs(v):
    w = v.shape[1] // 2
    bits = lax.bitcast_convert_type(v.astype(BF16).astype(F32), jnp.int32)
    return bits[:, w:] | lax.shift_right_logical(bits[:, :w], 16)


def _unpack_bf16_pairs(pieces):
    lo = [lax.bitcast_convert_type(lax.shift_left(p, 16), F32) for p in pieces]
    hi = [lax.bitcast_convert_type(p & jnp.int32(-65536), F32) for p in pieces]
    return jnp.concatenate(lo + hi, axis=1)


def _store_pieces(ref, lead, packed):
    for h in range(SC_PIECES):
        ref[(h,) + lead] = packed[:, h * SC_ROW_WORDS:(h + 1) * SC_ROW_WORDS]


def _merge_kernel(z_ref, r_ref, h_ref, a_ref, q_ref, x_ref, mod_ref, wb_ref, wo_ref, lng_ref, lnb_ref,
                  wrh_ref, wrl_ref, br_ref, tri_ref, xo_ref, h2_ref, te_ref, tg_ref, rk_ref, cnt_ref,
                  run_ref, *, n_ctx, tm):
    i = pl.program_id(1)

    @pl.when((pl.program_id(0) == 0) & (i == 0))
    def _():
        run_ref[...] = jnp.zeros_like(run_ref)

    is_ctx = i * tm < n_ctx
    mod = jnp.where(is_ctx, mod_ref[0, 0], mod_ref[0, 1])
    acc = None
    for nb, br in enumerate((r_ref, h_ref, a_ref, q_ref)):
        yb = _dot(br[0], wb_ref[nb])
        gt = _sigmoid(z_ref[0, :, nb * D_MODEL:(nb + 1) * D_MODEL].astype(F32))
        acc = gt * yb if acc is None else acc + gt * yb
    y = _dot(acc.astype(BF16), wo_ref[...])
    xn = _layernorm_rows(DN_ALPHA * x_ref[0] + mod[2:3, :] * y, lng_ref[...], lnb_ref[...])
    xo_ref[0] = xn
    h2 = xn * mod[3:4, :] + mod[4:5, :]
    h2_hi, h2_lo = _split_bf16(h2)
    _store_pieces(h2_ref, (0,), _pack_bf16_pairs(h2))
    wrh = wrh_ref[...]
    logit = _dot_nt(wrh, h2_hi) + _dot_nt(wrh, h2_lo) + _dot_nt(wrl_ref[...], h2_hi) + br_ref[...]
    eidx = lax.broadcasted_iota(jnp.int32, logit.shape, 0)
    vals, idxs = [], []
    for _ in range(TOP_K):
        m = logit.max(axis=0, keepdims=True)
        sel = jnp.where(logit == m, eidx, N_EXPERTS).min(axis=0, keepdims=True)
        vals.append(m)
        idxs.append(sel)
        logit = jnp.where(eidx == sel, -jnp.inf, logit)
    ex = [jnp.exp(v - vals[0]) for v in vals]
    tot = ex[0] + ex[1] + ex[2] + ex[3]
    te_ref[0] = jnp.concatenate(idxs, axis=0)
    tg_ref[0] = jnp.concatenate([e / tot for e in ex], axis=0)
    base = run_ref[:, 0:1]
    tri = tri_ref[...]
    ranks = []
    for sel in idxs:
        hit = eidx == sel
        ones = jnp.where(hit, 1.0, 0.0)
        pre = _dot(ones.astype(BF16), tri)
        ranks.append(jnp.sum(jnp.where(hit, pre + base, 0.0), axis=0, keepdims=True))
        base = base + jnp.sum(ones, axis=1, keepdims=True)
    rk_ref[0] = jnp.concatenate(ranks, axis=0).astype(jnp.int32)
    run_ref[...] = jnp.broadcast_to(base, run_ref.shape)
    cnt_ref[...] = run_ref[...]


def _merge(z, outs, xs, mod, wb, wo, ln_g, ln_b, wr_hi, wr_lo, b_r, n_ctx, tm):
    b, n, d = xs.shape
    row = lambda w: pl.BlockSpec((1, tm, w), lambda bb, i: (bb, i, 0))
    full = lambda a: pl.BlockSpec(a.shape, lambda bb, i: (0,) * a.ndim)
    topk = pl.BlockSpec((1, TOP_K, tm), lambda bb, i: (bb, 0, i))
    ln_g = ln_g.reshape(1, d)
    ln_b = ln_b.reshape(1, d)
    b_r = b_r.reshape(N_EXPERTS, 1)
    tri = jnp.asarray(np.triu(np.ones((tm, tm), np.float32), 1), BF16)
    return pl.pallas_call(
        functools.partial(_merge_kernel, n_ctx=n_ctx, tm=tm),
        out_shape=(jax.ShapeDtypeStruct((b, n, d), F32),
                   jax.ShapeDtypeStruct((SC_PIECES, b, n, SC_ROW_WORDS), jnp.int32),
                   jax.ShapeDtypeStruct((b, TOP_K, n), jnp.int32),
                   jax.ShapeDtypeStruct((b, TOP_K, n), F32),
                   jax.ShapeDtypeStruct((b, TOP_K, n), jnp.int32),
                   jax.ShapeDtypeStruct((N_EXPERTS, LANES), F32)),
        grid=(b, n // tm),
        in_specs=[row(N_BRANCH * D_MODEL), row(BRANCH_W), row(BRANCH_W), row(BRANCH_W), row(BRANCH_W),
                  row(d),
                  pl.BlockSpec((1, 2, 8, d), lambda bb, i: (bb, 0, 0, 0)),
                  full(wb), full(wo), full(ln_g), full(ln_b), full(wr_hi), full(wr_lo), full(b_r), full(tri)],
        out_specs=(row(d), pl.BlockSpec((SC_PIECES, 1, tm, SC_ROW_WORDS), lambda bb, i: (0, bb, i, 0)),
                   topk, topk, topk,
                   pl.BlockSpec((N_EXPERTS, LANES), lambda bb, i: (0, 0))),
        scratch_shapes=[pltpu.VMEM((N_EXPERTS, LANES), F32)],
        compiler_params=_cparams(("arbitrary", "arbitrary")),
        name="merge_ln_router",
    )(z, *outs, xs, mod, wb, wo, ln_g, ln_b, wr_hi, wr_lo, b_r, tri)


def _ffn_kernel(be_ref, nv_ref, nw_ref, x_ref, wgu_ref, bgu_ref, wd_ref, bd_ref, o_ref, wgu_s, wd_s):
    i = pl.program_id(0)
    nv = nv_ref[i]

    @pl.when(nw_ref[i] > 0)
    def _():
        wgu_s[...] = wgu_ref[0, 0].astype(BF16)
        wd_s[...] = wd_ref[0, 0].astype(BF16)

    @pl.when(nv > 0)
    def _():
        rows = lax.broadcasted_iota(jnp.int32, (MOE_BLOCK, 1), 0)
        x = _unpack_bf16_pairs([x_ref[h] for h in range(SC_PIECES)])
        x = jnp.where(rows < nv, x, 0.0).astype(BF16)
        gu = _dot(x, wgu_s[...]) + bgu_ref[0, 0]
        g = jnp.minimum(gu[:, :D_FF_EXPERT], SWIGLU_LIMIT)
        u = jnp.clip(gu[:, D_FF_EXPERT:], -SWIGLU_LIMIT, SWIGLU_LIMIT)
        act = g * _sigmoid(SWIGLU_ALPHA * g) * (u + 1.0)
        _store_pieces(o_ref, (), _pack_bf16_pairs(_dot(act.astype(BF16), wd_s[...]) + bd_ref[0, 0]))

    @pl.when(nv <= 0)
    def _():
        o_ref[...] = jnp.zeros_like(o_ref)


def _expert_ffn(xb, block_e, n_valid, layer, wgu, bgu, wd, bd):
    _, n_slots, _ = xb.shape
    n_blocks = n_slots // MOE_BLOCK
    depth, ne, d, f2 = wgu.shape
    new_w = jnp.concatenate([jnp.ones((1,), jnp.int32), (block_e[1:] != block_e[:-1]).astype(jnp.int32)])
    slot_blk = pl.BlockSpec((SC_PIECES, MOE_BLOCK, SC_ROW_WORDS), lambda i, be, nv, nw: (0, i, 0))
    wblk = lambda r, c: pl.BlockSpec((1, 1, r, c), lambda i, be, nv, nw: (layer, be[i], 0, 0))
    grid_spec = pltpu.PrefetchScalarGridSpec(
        num_scalar_prefetch=3,
        grid=(n_blocks,),
        in_specs=[slot_blk, wblk(d, f2), wblk(1, f2), wblk(f2 // 2, d), wblk(1, d)],
        out_specs=slot_blk,
        scratch_shapes=[pltpu.VMEM((d, f2), BF16), pltpu.VMEM((f2 // 2, d), BF16)],
    )
    return pl.pallas_call(
        _ffn_kernel,
        out_shape=jax.ShapeDtypeStruct(xb.shape, jnp.int32),
        grid_spec=grid_spec,
        compiler_params=_cparams(("arbitrary",)),
        name="expert_ffn",
    )(block_e, n_valid, new_w, xb, wgu, bgu.reshape(depth, ne, 1, f2), wd, bd.reshape(depth, ne, 1, d))


def _combine_kernel(y_ref, g_ref, x_ref, mod_ref, lng_ref, lnb_ref, xo_ref, *, n_ctx, tm):
    i = pl.program_id(1)
    is_ctx = i * tm < n_ctx
    g2 = jnp.where(is_ctx, mod_ref[0, 0, 5:6, :], mod_ref[0, 1, 5:6, :])
    gate = g_ref[...]
    y = None
    for k in range(TOP_K):
        t = gate[:, k:k + 1] * _unpack_bf16_pairs([y_ref[k, h] for h in range(SC_PIECES)])
        y = t if y is None else y + t
    xo_ref[0] = _layernorm_rows(DN_ALPHA * x_ref[0] + g2 * y, lng_ref[...], lnb_ref[...])


def _combine(yk, gate, xs, mod, ln_g, ln_b, n_ctx, tm):
    b, n, d = xs.shape
    nt = n // tm
    return pl.pallas_call(
        functools.partial(_combine_kernel, n_ctx=n_ctx, tm=tm),
        out_shape=jax.ShapeDtypeStruct((b, n, d), F32),
        grid=(b, nt),
        in_specs=[pl.BlockSpec((TOP_K, SC_PIECES, tm, SC_ROW_WORDS), lambda bb, i: (0, 0, bb * nt + i, 0)),
                  pl.BlockSpec((tm, TOP_K), lambda bb, i: (bb * nt + i, 0)),
                  pl.BlockSpec((1, tm, d), lambda bb, i: (bb, i, 0)),
                  pl.BlockSpec((1, 2, 8, d), lambda bb, i: (bb, 0, 0, 0)),
                  pl.BlockSpec((1, d), lambda bb, i: (0, 0)),
                  pl.BlockSpec((1, d), lambda bb, i: (0, 0))],
        out_specs=pl.BlockSpec((1, tm, d), lambda bb, i: (bb, i, 0)),
        compiler_params=_cparams(("arbitrary", "arbitrary")),
        name="moe_combine_ln",
    )(yk, gate, xs, mod, ln_g.reshape(1, d), ln_b.reshape(1, d))


def _sc_mesh():
    return plsc.VectorSubcoreMesh(core_axis_name="core", subcore_axis_name="subcore",
                                  num_cores=SC_CORES, num_subcores=SC_SUBCORES)


def _sc_piece_index(idx, rows):
    return jnp.concatenate([idx + h * rows for h in range(SC_PIECES)], axis=1)


def _sc_scatter_rows(x, dest, n_out):
    p, n, w = x.shape
    x2 = x.reshape(p * n, w)
    dest2 = _sc_piece_index(dest, n_out)
    kk, n2 = dest2.shape

    @functools.partial(pl.kernel, out_type=jax.ShapeDtypeStruct((p * n_out, w), x.dtype),
                       mesh=_sc_mesh(), scratch_types=[], name="moe_dispatch_scatter")
    def scatter(x_hbm, i_hbm, o_hbm):
        def body(x_vmem, i_vmem):
            for j in range(kk):
                pltpu.sync_copy(x_vmem, o_hbm.at[i_vmem.at[j]])

        pltpu.emit_pipeline(
            body, grid=(n2 // SC_WINDOW,),
            in_specs=[pl.BlockSpec((SC_WINDOW, SC_ROW_WORDS), lambda i: (i, 0)),
                      pl.BlockSpec((kk, SC_WINDOW), lambda i: (0, i))],
            out_specs=[], core_axis_name=("core", "subcore"),
            dimension_semantics=(pltpu.PARALLEL,))(x_hbm, i_hbm)

    return scatter(x2, dest2).reshape(p, n_out, w)


def _sc_gather_rows(table, idx):
    p, v, w = table.shape
    kk, n = idx.shape
    t2 = table.reshape(p * v, w)
    idx2 = _sc_piece_index(idx, v)
    m = kk * n * p

    @functools.partial(pl.kernel, out_type=jax.ShapeDtypeStruct((m, SC_ROW_WORDS), table.dtype),
                       mesh=_sc_mesh(), scratch_types=[], name="moe_combine_gather")
    def gather(t_hbm, i_hbm, o_hbm):
        def body(i_vmem, o_vmem):
            pltpu.sync_copy(t_hbm.at[i_vmem.at[0]], o_vmem)

        pltpu.emit_pipeline(
            body, grid=(m // SC_WINDOW,),
            in_specs=[pl.BlockSpec((1, SC_WINDOW), lambda i: (0, i))],
            out_specs=[pl.BlockSpec((SC_WINDOW, SC_ROW_WORDS), lambda i: (i, 0))],
            core_axis_name=("core", "subcore"),
            dimension_semantics=(pltpu.PARALLEL,))(i_hbm, o_hbm)

    return gather(t2, idx2.reshape(1, m)).reshape(kk, p, n, w)


def _moe(h2p, top_e, top_g, rank, cnt, layer, wgu, bgu, wd, bd):
    p, b, n, w = h2p.shape
    n_tok = b * n
    nk = n_tok * TOP_K
    experts = jnp.arange(N_EXPERTS, dtype=jnp.int32)
    counts = cnt[:, 0].astype(jnp.int32)
    padded = (counts + MOE_BLOCK - 1) // MOE_BLOCK * MOE_BLOCK
    ends_p = jnp.cumsum(padded)
    start_p = ends_p - padded
    n_blocks = (nk + N_EXPERTS * (MOE_BLOCK - 1) + MOE_BLOCK - 1) // MOE_BLOCK
    block_start = jnp.arange(n_blocks, dtype=jnp.int32) * MOE_BLOCK
    block_e = jnp.minimum(jnp.sum(ends_p[None, :] <= block_start[:, None], axis=1), N_EXPERTS - 1).astype(jnp.int32)
    is_e = block_e[:, None] == experts[None, :]
    filled = jnp.sum(jnp.where(is_e, (start_p + counts)[None, :], 0), axis=1)
    n_valid = jnp.clip(filled - block_start, 0, MOE_BLOCK).astype(jnp.int32)
    dest = rank + jnp.sum(jnp.where(top_e[..., None] == experts, start_p, 0), axis=-1)
    dest = dest.transpose(1, 0, 2).reshape(TOP_K, n_tok)
    gate = top_g.transpose(0, 2, 1).reshape(n_tok, TOP_K)
    xb = _sc_scatter_rows(h2p.reshape(p, n_tok, w), dest, n_blocks * MOE_BLOCK)
    yb = _expert_ffn(xb, block_e, n_valid, layer, wgu, bgu, wd, bd)
    yk = _sc_gather_rows(yb, dest)
    return yk, gate


def kernel(x, c, ctx, c_ctx, w_mod, b_mod, w_in, ret_decay, ret_gn_g, ret_gn_b, hg_lb, hg_norm_g,
           na_rpb, gq_qn_g, gq_kn_g, w_branch, w_out, ln_g, ln_b, w_router, b_router, w_gu, b_gu,
           w_down, b_down):
    b_, t_, d = x.shape
    n_ctx = ctx.shape[1]
    depth = w_mod.shape[0]
    tm = 256

    sm = jax.nn.softmax(hg_lb.astype(F32), axis=1)
    lower = jnp.cumsum(sm, axis=1) - sm[:, :1]
    log_gamma = jax.nn.log_sigmoid(ret_decay.astype(F32)).reshape(depth, 2 * RET_HEADS)

    cc = jnp.concatenate([c, c_ctx[None, :], jnp.zeros((16 - b_ - 1, d), F32)], axis=0)
    modv = _mod_all(cc, w_mod, b_mod).reshape(depth, 16, 6, d)
    one = jnp.asarray([0.0, 1.0, 0.0, 0.0, 1.0, 0.0], F32)[None, None, :, None]
    modv = (modv + one)[:, :, jnp.asarray([1, 0, 2, 4, 3, 5])]
    modv = jnp.concatenate([modv, jnp.zeros((depth, 16, 2, d), F32)], axis=2)
    mod = jnp.stack([jnp.broadcast_to(modv[:, b_:b_ + 1], (depth, b_, 8, d)), modv[:, :b_]], axis=2)

    cos_r, sin_r = _rope_tables(n_ctx, t_, RET_DK)
    cos_g, sin_g = _rope_tables(n_ctx, t_, GQA_HD)

    xs = jnp.concatenate([ctx, x], axis=1)
    tm_in = 768 if xs.shape[1] % 768 == 0 else tm
    w_in_b = _w_in_prep(w_in)
    for l in range(depth):
        z = _in_proj(xs, mod[l], w_in_b[l], n_ctx, tm_in, PROJ_PAD // 4)
        ret_o = _retention(z, log_gamma[l], cos_r, sin_r, ret_gn_g[l], ret_gn_b[l], n_ctx)
        hg_o = _hgrn2(z, lower[:, l], hg_norm_g[l], n_ctx)
        na_o = _neighbourhood(z, _na_bias_table(na_rpb[l]), n_ctx)
        gq_o = _gqa(z, cos_g, sin_g, gq_qn_g[l], gq_kn_g[l], n_ctx, tm)
        wr_hi, wr_lo = _split_bf16(w_router[l].T)
        xs, h2p, top_e, top_g, rank, cnt = _merge(z, (ret_o, hg_o, na_o, gq_o), xs, mod[l],
                                                  w_branch[l].astype(BF16), w_out[l].astype(BF16),
                                                  ln_g[l, 0], ln_b[l, 0], wr_hi, wr_lo, b_router[l], n_ctx, tm)
        yk, gate = _moe(h2p, top_e, top_g, rank, cnt, l, w_gu, b_gu, w_down, b_down)
        xs = _combine(yk, gate, xs, mod[l], ln_g[l, 1], ln_b[l, 1], n_ctx, tm)
    return xs[:, n_ctx:]
```

```python
import functools

import jax
import jax.numpy as jnp
from jax import lax
import numpy as np
from jax.experimental import pallas as pl
from jax.experimental.pallas import tpu as pltpu
from jax.experimental.pallas import tpu_sc as plsc

D_MODEL = 1024
DEPTH = 4
GRID_W = 64
N_BRANCH = 4
BRANCH_W = D_MODEL // 2
RET_HEADS = 4
RET_DK = BRANCH_W // RET_HEADS
HG_HEADS = 4
HG_DK = BRANCH_W // HG_HEADS
NA_HEADS = 8
NA_HD = BRANCH_W // NA_HEADS
NA_KH = 8
NA_KW = 16
GQA_HEADS = 8
GQA_KV_HEADS = 2
GQA_HD = BRANCH_W // GQA_HEADS
GQA_KV_W = GQA_KV_HEADS * GQA_HD
ROPE_BASE = 10000.0
N_EXPERTS = 32
TOP_K = 4
D_FF_EXPERT = D_MODEL
SWIGLU_LIMIT = 7.0
SWIGLU_ALPHA = 1.702
MOE_BLOCK = 256
LN_EPS = 1e-5
NORM_EPS = 1e-6
NEG_INF = -1e30
DN_ALPHA = (2 * DEPTH) ** 0.25

LANES = 128
BF16 = jnp.bfloat16
F32 = jnp.float32
VMEM_LIMIT = 56 * 1024 * 1024
SC_CORES = 2
SC_SUBCORES = 16
SC_WINDOW = 128
SC_ROW_WORDS = 256
SC_PIECES = D_MODEL // 2 // SC_ROW_WORDS

PROJ_ORIG = 4 * BRANCH_W + 5 * BRANCH_W + 3 * BRANCH_W + BRANCH_W + 2 * GQA_KV_W
PROJ_TOTAL = PROJ_ORIG + N_BRANCH * D_MODEL
PROJ_PAD = 11264
GATE_B = 0
RET_B = (N_BRANCH * D_MODEL) // LANES
HG_B = RET_B + 16
NA_B = HG_B + 20
GQ_B = NA_B + 12

RET_CHUNK = 256
HG_CHUNK = 128
LOG2E = 1.4426950408889634
HG_LEVELS = (64, 32, 16, 8, 4, 2, 1, 0)


def _cparams(sem):
    return pltpu.CompilerParams(dimension_semantics=sem, vmem_limit_bytes=VMEM_LIMIT)


def _dot(a, b):
    return jnp.dot(a, b, preferred_element_type=F32)


def _dot_nt(a, b):
    return lax.dot_general(a, b, (((1,), (1,)), ((), ())), preferred_element_type=F32)


def _dot_tn(a, b):
    return lax.dot_general(a, b, (((0,), (0,)), ((), ())), preferred_element_type=F32)


def _split_bf16(x):
    hi = x.astype(BF16)
    lo = (x - hi.astype(F32)).astype(BF16)
    return hi, lo


def _sigmoid(x):
    return 1.0 / (1.0 + jnp.exp(-x))


def _silu(x):
    return x * _sigmoid(x)


def _mod_kernel(c_ref, w_ref, b_ref, o_ref):
    s = _silu(c_ref[...]).astype(BF16)
    o_ref[0] = _dot(s, w_ref[0].astype(BF16)) + b_ref[0]


def _mod_all(cc, w_mod, b_mod):
    depth, d, n = w_mod.shape
    r = cc.shape[0]
    tn = 1536
    return pl.pallas_call(
        _mod_kernel,
        out_shape=jax.ShapeDtypeStruct((depth, r, n), F32),
        grid=(depth, n // tn),
        in_specs=[pl.BlockSpec((r, d), lambda l, j: (0, 0)),
                  pl.BlockSpec((1, d, tn), lambda l, j: (l, 0, j)),
                  pl.BlockSpec((1, 1, tn), lambda l, j: (l, 0, j))],
        out_specs=pl.BlockSpec((1, r, tn), lambda l, j: (l, 0, j)),
        compiler_params=_cparams(("arbitrary", "arbitrary")),
        name="adaln_mod",
    )(cc, w_mod, b_mod.reshape(depth, 1, n))


W_PREP_TILE = 256


def _w_in_prep_kernel(w_ref, o_ref, *, n_src):
    j = pl.program_id(1)
    o_ref[0] = jnp.where(j < n_src, w_ref[0], 0.0).astype(BF16)


def _w_in_prep(w_in):
    depth, d, total = w_in.shape
    n_src = total // W_PREP_TILE
    shift = PROJ_ORIG // W_PREP_TILE
    src = lambda l, j: (l, 0, jnp.where(j < n_src, (j + shift) % n_src, 0))
    return pl.pallas_call(
        functools.partial(_w_in_prep_kernel, n_src=n_src),
        out_shape=jax.ShapeDtypeStruct((depth, d, PROJ_PAD), BF16),
        grid=(depth, PROJ_PAD // W_PREP_TILE),
        in_specs=[pl.BlockSpec((1, d, W_PREP_TILE), src)],
        out_specs=pl.BlockSpec((1, d, W_PREP_TILE), lambda l, j: (l, 0, j)),
        compiler_params=_cparams(("arbitrary", "arbitrary")),
        name="w_in_prep",
    )(w_in)


def _in_proj_kernel(x_ref, mod_ref, w_ref, o_ref, *, n_ctx, tm):
    i = pl.program_id(2)
    x = x_ref[0]
    mc = mod_ref[0, 0]
    ml = mod_ref[0, 1]
    rows = i * tm + lax.broadcasted_iota(jnp.int32, (tm, 1), 0)
    is_ctx = rows < n_ctx
    scale = jnp.where(is_ctx, mc[0:1, :], ml[0:1, :])
    shift = jnp.where(is_ctx, mc[1:2, :], ml[1:2, :])
    h = (x * scale + shift).astype(BF16)
    o_ref[0] = _dot(h, w_ref[...]).astype(o_ref.dtype)


def _in_proj(xs, mod, w, n_ctx, tm, tn):
    b, n, d = xs.shape
    ncol = w.shape[1]
    return pl.pallas_call(
        functools.partial(_in_proj_kernel, n_ctx=n_ctx, tm=tm),
        out_shape=jax.ShapeDtypeStruct((b, n, ncol), BF16),
        grid=(ncol // tn, b, n // tm),
        in_specs=[pl.BlockSpec((1, tm, d), lambda j, bb, i: (bb, i, 0)),
                  pl.BlockSpec((1, 2, 8, d), lambda j, bb, i: (bb, 0, 0, 0)),
                  pl.BlockSpec((d, tn), lambda j, bb, i: (0, j))],
        out_specs=pl.BlockSpec((1, tm, tn), lambda j, bb, i: (bb, i, j)),
        compiler_params=_cparams(("arbitrary", "arbitrary", "arbitrary")),
        name="in_proj",
    )(xs, mod, w)


def _rope_tables(n_ctx, t, head_dim):
    idx = jnp.arange(t, dtype=jnp.int32)
    row = (idx // GRID_W).astype(F32)
    col = (idx % GRID_W).astype(F32)
    n = head_dim // 2
    inv = ROPE_BASE ** (-jnp.arange(0, n, 2, dtype=F32) / n)

    def half(pos):
        ang = pos[:, None] * inv[None, :]
        c, s = jnp.cos(ang), jnp.sin(ang)
        return jnp.concatenate([c, c], -1), jnp.concatenate([-s, s], -1)

    cr, sr = half(row)
    cc, sc = half(col)
    cos = jnp.concatenate([cr, cc], -1)
    sin = jnp.concatenate([sr, sc], -1)
    reps = LANES // head_dim
    cos = jnp.tile(cos, (1, reps))
    sin = jnp.tile(sin, (1, reps))
    cos = jnp.concatenate([jnp.ones((n_ctx, LANES), F32), cos], 0)
    sin = jnp.concatenate([jnp.zeros((n_ctx, LANES), F32), sin], 0)
    return cos, sin


def _rope(x, cos, sin, quarter):
    lane = lax.broadcasted_iota(jnp.int32, x.shape, 1)
    first = (lane & (2 * quarter - 1)) < quarter
    rot = jnp.where(first, pltpu.roll(x, LANES - quarter, 1), pltpu.roll(x, quarter, 1))
    return x * cos + rot * sin


def _ret_kernel(lg_ref, q_ref, k_ref, v_ref, g_ref, cos_ref, sin_ref, gng_ref, gnb_ref, o_ref,
                ks_ref, sb_ref, *, n_ctx, chunk):
    hh = pl.program_id(1)
    lgf = lg_ref[hh]
    lgb = lg_ref[RET_HEADS + hh]
    n = q_ref.shape[1]
    c_ = chunk
    nch = n // c_
    n_cc = n_ctx // c_
    dk = q_ref.shape[2]

    ri = lax.broadcasted_iota(jnp.int32, (c_, dk), 0).astype(F32)
    df = jnp.exp((ri + 1.0) * lgf)
    db = jnp.exp((c_ - ri) * lgb)
    ef = jnp.exp((c_ - 1.0 - ri) * lgf)
    eb = jnp.exp(ri * lgb)
    gfc = jnp.exp(c_ * lgf)
    gbc = jnp.exp(c_ * lgb)
    ii = lax.broadcasted_iota(jnp.int32, (c_, c_), 0)
    jj = lax.broadcasted_iota(jnp.int32, (c_, c_), 1)
    dij = (ii - jj).astype(F32)
    w = jnp.where(ii >= jj, jnp.exp(dij * lgf), jnp.exp(-dij * lgb))

    def rope_k(c, carry):
        r0 = pl.multiple_of(c * c_, c_)
        kc = k_ref[0, pl.ds(r0, c_), :].astype(F32) * (RET_DK ** -0.5)
        kc = _rope(kc, cos_ref[pl.ds(r0, c_), :], sin_ref[pl.ds(r0, c_), :], 32)
        ks_ref[pl.ds(r0, c_), :] = kc.astype(BF16)
        return carry

    lax.fori_loop(0, nch, rope_k, 0, unroll=3)

    def bwd_step(c, s):
        r0 = pl.multiple_of(c * c_, c_)
        sb_ref[c] = s
        kc = (ks_ref[pl.ds(r0, c_), :].astype(F32) * eb).astype(BF16)
        return s * gbc + _dot_tn(kc, v_ref[0, pl.ds(r0, c_), :])

    s = jnp.zeros((dk, dk), F32)
    s = lax.fori_loop(0, n_cc, lambda t, s_: bwd_step(n_cc - 1 - t, s_), s)
    lax.fori_loop(0, nch - n_cc, lambda t, s_: bwd_step(nch - 1 - t, s_), s, unroll=4)

    gng = gng_ref[...]
    gnb = gnb_ref[...]

    def fwd_step(c, sf):
        r0 = pl.multiple_of(c * c_, c_)
        qc = _rope(q_ref[0, pl.ds(r0, c_), :].astype(F32), cos_ref[pl.ds(r0, c_), :],
                   sin_ref[pl.ds(r0, c_), :], 32)
        kc = ks_ref[pl.ds(r0, c_), :]
        vc = v_ref[0, pl.ds(r0, c_), :]
        a = _dot_nt(qc.astype(BF16), kc) * w
        qi = jnp.concatenate([qc * df, qc * db], axis=1).astype(BF16)
        si = jnp.concatenate([sf, sb_ref[c]], axis=0).astype(BF16)
        o = _dot(a.astype(BF16), vc) + _dot(qi, si)
        sf_new = sf * gfc + _dot_tn((kc.astype(F32) * ef).astype(BF16), vc)
        mu = jnp.mean(o, axis=-1, keepdims=True)
        oc = o - mu
        var = jnp.mean(oc * oc, axis=-1, keepdims=True)
        y = oc * lax.rsqrt(var + LN_EPS) * gng + gnb
        gate = g_ref[0, pl.ds(r0, c_), :].astype(F32)
        o_ref[0, pl.ds(r0, c_), :] = (y * _silu(gate)).astype(o_ref.dtype)
        return sf_new

    lax.fori_loop(0, nch, fwd_step, jnp.zeros((dk, dk), F32), unroll=3)


def _retention(z, log_gamma, cos, sin, gn_g, gn_b, n_ctx):
    b, n, _ = z.shape
    blk = lambda off: pl.BlockSpec((1, n, LANES), lambda bb, h, lg: (bb, 0, off + h))
    grid_spec = pltpu.PrefetchScalarGridSpec(
        num_scalar_prefetch=1,
        grid=(b, RET_HEADS),
        in_specs=[blk(RET_B), blk(RET_B + 4), blk(RET_B + 8), blk(RET_B + 12),
                  pl.BlockSpec((n, LANES), lambda bb, h, lg: (0, 0)),
                  pl.BlockSpec((n, LANES), lambda bb, h, lg: (0, 0)),
                  pl.BlockSpec((1, LANES), lambda bb, h, lg: (0, h)),
                  pl.BlockSpec((1, LANES), lambda bb, h, lg: (0, h))],
        out_specs=pl.BlockSpec((1, n, LANES), lambda bb, h, lg: (bb, 0, h)),
        scratch_shapes=[pltpu.VMEM((n, LANES), BF16),
                        pltpu.VMEM((n // RET_CHUNK, RET_DK, RET_DK), F32)],
    )
    return pl.pallas_call(
        functools.partial(_ret_kernel, n_ctx=n_ctx, chunk=RET_CHUNK),
        out_shape=jax.ShapeDtypeStruct((b, n, BRANCH_W), BF16),
        grid_spec=grid_spec,
        compiler_params=_cparams(("arbitrary", "arbitrary")),
        name="retention",
    )(log_gamma, z, z, z, z, cos, sin, gn_g.reshape(1, -1), gn_b.reshape(1, -1))


def _hg_constants():
    c = HG_CHUNK
    r = np.arange(c)[:, None]
    t = np.arange(c)[None, :]
    mats = []
    level = np.full((2, c, c), len(HG_LEVELS), np.int32)
    for d in range(2):
        blocks = []
        for li, s in enumerate(HG_LEVELS):
            if s == 0:
                level[d][np.arange(c), np.arange(c)] = li
                continue
            base = (r // (2 * s)) * (2 * s)
            if d == 0:
                m = base + s - 1
                mat = (t > np.minimum(r, m)) & (t <= np.maximum(r, m))
                q_side = (r % (2 * s)) >= s
            else:
                m = base + s
                mat = (t >= np.minimum(r, m)) & (t < np.maximum(r, m))
                q_side = (r % (2 * s)) < s
            blocks.append(mat.astype(np.float32))
            same = (r // (2 * s)) == (t // (2 * s))
            k_side_t = ((t % (2 * s)) < s) if d == 0 else ((t % (2 * s)) >= s)
            level[d][same & q_side & k_side_t] = li
        if d == 0:
            blocks.append((t <= r).astype(np.float32))
            blocks.append((t > r).astype(np.float32))
        else:
            blocks.append((t >= r).astype(np.float32))
            blocks.append((t < r).astype(np.float32))
        mats.append(np.concatenate(blocks, 0))
    mats = np.stack(mats)
    masks = (level[:, None] == np.arange(len(HG_LEVELS))[None, :, None, None]).astype(np.float32)
    upper = np.stack([np.broadcast_to(((np.arange(c) % (2 * s)) >= s)[:, None], (c, c)) for s in HG_LEVELS if s > 0])
    return np.concatenate([mats, mats], axis=2), np.maximum(masks[0], masks[1]), upper.astype(np.float32)


def _hg_gates(zf, lb):
    en = jnp.exp(-jnp.abs(zf))
    inv = 1.0 / (1.0 + en)
    pos = zf >= 0
    sg = jnp.where(pos, inv, en * inv)
    sgn = jnp.where(pos, en * inv, inv)
    one_m = 1.0 - lb
    return jnp.log(lb + one_m * sg), one_m * sgn


def _hg_kernel(q_ref, ff_ref, fb_ref, v_ref, g_ref, lb_ref, ng_ref, m_ref, lv_ref, up_ref, o_ref,
               lfb_ref, kb_ref, sb_ref, ktb_ref, dtot_ref, *, n_ctx):
    c_ = HG_CHUNK
    n = q_ref.shape[1]
    nch = n // c_
    n_cc = n_ctx // c_
    dk = q_ref.shape[2]
    nl = len(HG_LEVELS)
    lbf = lb_ref[0:1, :]
    lbb = lb_ref[1:2, :]

    def mdot(mat, lf):
        hi, lo = _split_bf16(lf)
        return _dot(mat, jnp.concatenate([hi, lo], axis=0))

    def prep_b(c, carry):
        r0 = pl.multiple_of(c * c_, c_)
        lf, key = _hg_gates(fb_ref[0, pl.ds(r0, c_), :].astype(F32), lbb)
        lf = lf * LOG2E
        lfb_ref[pl.ds(r0, c_), :] = lf
        kb_ref[pl.ds(r0, c_), :] = key
        e = mdot(m_ref[1, (nl - 1) * c_:(nl + 1) * c_, :], lf)
        ktb_ref[pl.ds(r0, c_), :] = (key * jnp.exp2(e[c_:2 * c_, :])).astype(BF16)
        dtot_ref[c] = jnp.broadcast_to(jnp.exp2(e[0:1, :]), (8, dk))
        return carry

    lax.fori_loop(0, nch, prep_b, 0, unroll=3)

    def bwd_step(c, st):
        r0 = pl.multiple_of(c * c_, c_)
        sb_ref[c] = st
        return st * dtot_ref[c][0:1, :] + _dot_tn(v_ref[0, pl.ds(r0, c_), :], ktb_ref[pl.ds(r0, c_), :])

    st = jnp.zeros((dk, dk), F32)
    st = lax.fori_loop(0, n_cc, lambda t, s_: bwd_step(n_cc - 1 - t, s_), st, unroll=2)
    lax.fori_loop(0, nch - n_cc, lambda t, s_: bwd_step(nch - 1 - t, s_), st, unroll=8)

    ng = ng_ref[...]

    def mdot_pair(mat, lf0, lf1):
        hi0, lo0 = _split_bf16(lf0)
        hi1, lo1 = _split_bf16(lf1)
        rhs = jnp.concatenate([jnp.concatenate([hi0, hi1], axis=1), jnp.concatenate([lo0, lo1], axis=1)], axis=0)
        return _dot(mat, rhs)

    def intra(qs, keyf, keyb, ef, eb):
        a = jnp.zeros((c_, c_), F32)
        qsb, kfb, kbb = qs.astype(BF16), keyf.astype(BF16), keyb.astype(BF16)
        for li, s in enumerate(HG_LEVELS):
            if s == 0:
                qt = jnp.concatenate([qsb, qsb], axis=1)
                kt = jnp.concatenate([kfb, kbb], axis=1)
            else:
                decf = jnp.exp2(ef[li * c_:(li + 1) * c_, :])
                decb = jnp.exp2(eb[li * c_:(li + 1) * c_, :])
                up = up_ref[li]
                dqf = decf * up
                dkb = decb * up
                qt = jnp.concatenate([qsb * dqf.astype(BF16), qsb * (decb - dkb).astype(BF16)], axis=1)
                kt = jnp.concatenate([kfb * (decf - dqf).astype(BF16), kbb * dkb.astype(BF16)], axis=1)
            a = a + _dot_nt(qt, kt) * lv_ref[li]
        cumf = ef[(nl - 1) * c_:nl * c_, :]
        restf = ef[nl * c_:(nl + 1) * c_, :]
        cumb = eb[(nl - 1) * c_:nl * c_, :]
        return a, cumf, restf, cumb

    def fwd_chunk(c, sf, qs, keyf, ef, eb):
        r0 = pl.multiple_of(c * c_, c_)
        vc = v_ref[0, pl.ds(r0, c_), :]
        a, cumf, restf, cumb = intra(qs, keyf, kb_ref[pl.ds(r0, c_), :], ef, eb)
        qi = jnp.concatenate([qs * jnp.exp2(cumf), qs * jnp.exp2(cumb)], axis=1).astype(BF16)
        si = jnp.concatenate([sf, sb_ref[c]], axis=1).astype(BF16)
        o = _dot(a.astype(BF16), vc) + _dot_nt(qi, si)
        totf = cumf[c_ - 1:c_, :]
        sf_new = sf * jnp.exp2(totf) + _dot_tn(vc, (keyf * jnp.exp2(restf)).astype(BF16))
        on = o * lax.rsqrt(jnp.mean(o * o, axis=-1, keepdims=True) + NORM_EPS) * ng
        gate = g_ref[0, pl.ds(r0, c_), :].astype(F32)
        o_ref[0, pl.ds(r0, c_), :] = (on * _silu(gate)).astype(o_ref.dtype)
        return sf_new

    def fwd_pair(p, sf):
        r0 = pl.multiple_of(p * (2 * c_), 2 * c_)
        qs, keyf, lff = [], [], []
        for t in range(2):
            rows = pl.ds(r0 + t * c_, c_)
            qs.append(_silu(q_ref[0, rows, :].astype(F32)))
            lf, key = _hg_gates(ff_ref[0, rows, :].astype(F32), lbf)
            lff.append(lf * LOG2E)
            keyf.append(key)
        ef = mdot_pair(m_ref[0], lff[0], lff[1])
        eb = mdot_pair(m_ref[1], lfb_ref[pl.ds(r0, c_), :], lfb_ref[pl.ds(r0 + c_, c_), :])
        for t in range(2):
            sf = fwd_chunk(2 * p + t, sf, qs[t], keyf[t], ef[:, t * dk:(t + 1) * dk], eb[:, t * dk:(t + 1) * dk])
        return sf

    lax.fori_loop(0, nch // 2, fwd_pair, jnp.zeros((dk, dk), F32))


def _hgrn2(z, lower, norm_g, n_ctx):
    b, n, _ = z.shape
    mats, level, upper = _hg_constants()
    mats = jnp.asarray(mats, BF16)
    level = jnp.asarray(level)
    upper = jnp.asarray(upper)
    blk = lambda off: pl.BlockSpec((1, n, LANES), lambda bb, h: (bb, 0, off + h))
    return pl.pallas_call(
        functools.partial(_hg_kernel, n_ctx=n_ctx),
        out_shape=jax.ShapeDtypeStruct((b, n, BRANCH_W), BF16),
        grid=(b, HG_HEADS),
        in_specs=[blk(HG_B), blk(HG_B + 4), blk(HG_B + 8), blk(HG_B + 12), blk(HG_B + 16),
                  pl.BlockSpec((2, LANES), lambda bb, h: (0, h)),
                  pl.BlockSpec((1, LANES), lambda bb, h: (0, h)),
                  pl.BlockSpec(mats.shape, lambda bb, h: (0, 0, 0)),
                  pl.BlockSpec(level.shape, lambda bb, h: (0, 0, 0)),
                  pl.BlockSpec(upper.shape, lambda bb, h: (0, 0, 0))],
        out_specs=pl.BlockSpec((1, n, LANES), lambda bb, h: (bb, 0, h)),
        scratch_shapes=[pltpu.VMEM((n, LANES), F32), pltpu.VMEM((n, LANES), F32),
                        pltpu.VMEM((n // HG_CHUNK, HG_DK, HG_DK), F32),
                        pltpu.VMEM((n, LANES), BF16), pltpu.VMEM((n // HG_CHUNK, 8, LANES), F32)],
        compiler_params=_cparams(("arbitrary", "arbitrary")),
        name="hgrn2",
    )(z, z, z, z, z, lower, norm_g.reshape(1, -1), mats, level, upper)


def _na_bias_table(rpb):
    qc = np.arange(GRID_W)[:, None]
    kc = np.arange(GRID_W)[None, :]
    col0 = np.clip(qc - NA_KW // 2, 0, GRID_W - NA_KW)
    valid = (kc >= col0) & (kc < col0 + NA_KW)
    dc = np.clip(kc - qc + (NA_KW - 1), 0, 2 * NA_KW - 2)
    pick = (dc[:, :, None] == np.arange(2 * NA_KW - 1)).astype(np.float32)
    t15 = jnp.einsum('hrc,qkc->hrqk', rpb.astype(F32), pick, precision=lax.Precision.HIGHEST)
    t = jnp.stack([t15[:, dl:dl + NA_KH] for dl in range(NA_KH)], axis=1)
    t = jnp.where(jnp.asarray(valid)[None, None, None], t * LOG2E, NEG_INF)
    t = t.transpose(0, 1, 3, 2, 4).reshape(rpb.shape[0] // 2, 2, NA_KH, GRID_W, NA_KH * GRID_W)
    return t.transpose(0, 2, 1, 3, 4).reshape(rpb.shape[0] // 2, NA_KH, 2 * GRID_W, NA_KH * GRID_W)


def _softmax_pv(parts):
    m = parts[0][0].max(axis=-1, keepdims=True)
    for s, _ in parts[1:]:
        m = jnp.maximum(m, s.max(axis=-1, keepdims=True))
    l = None
    o = None
    for s, v in parts:
        p = jnp.exp2(s - m)
        ls = p.sum(axis=-1, keepdims=True)
        os_ = _dot(p.astype(BF16), v)
        l = ls if l is None else l + ls
        o = os_ if o is None else o + os_
    return o / l


def _na_kernel(q_ref, k_ref, v_ref, tab_ref, o_ref, *, n_ctx):
    n = q_ref.shape[1]
    rows = (n - n_ctx) // GRID_W
    wlen = NA_KH * GRID_W
    scale = NA_HD ** -0.5 * LOG2E
    kx = k_ref[0, 0:n_ctx, :]
    vx = v_ref[0, 0:n_ctx, :]

    def head_mask(shape, hh):
        lane = lax.broadcasted_iota(jnp.int32, shape, 1)
        return (lane < NA_HD) if hh == 0 else (lane >= NA_HD)

    def stack_heads(qf):
        return jnp.concatenate([jnp.where(head_mask(qf.shape, 0), qf, 0.0),
                                jnp.where(head_mask(qf.shape, 1), qf, 0.0)], axis=0).astype(BF16)

    def unstack_heads(o):
        m = o.shape[0] // 2
        return jnp.where(head_mask((m, LANES), 0), o[:m], o[m:])

    qx = stack_heads(q_ref[0, 0:n_ctx, :].astype(F32) * scale)
    o_ref[0, 0:n_ctx, :] = unstack_heads(_softmax_pv([(_dot_nt(qx, kx), vx)])).astype(o_ref.dtype)

    def row_step(r, carry):
        start = jnp.clip(r - NA_KH // 2, 0, rows - NA_KH)
        delta = start - r + (NA_KH - 1)
        q0 = pl.multiple_of(n_ctx + r * GRID_W, GRID_W)
        k0 = pl.multiple_of(n_ctx + start * GRID_W, GRID_W)
        qr = stack_heads(q_ref[0, pl.ds(q0, GRID_W), :].astype(F32) * scale)
        kl = k_ref[0, pl.ds(k0, wlen), :]
        vl = v_ref[0, pl.ds(k0, wlen), :]
        s_loc = _dot_nt(qr, kl) + tab_ref[0, delta]
        s_ctx = _dot_nt(qr, kx)
        o = unstack_heads(_softmax_pv([(s_loc, vl), (s_ctx, vx)]))
        o_ref[0, pl.ds(q0, GRID_W), :] = o.astype(o_ref.dtype)
        return carry

    lax.fori_loop(0, rows, row_step, 0, unroll=4)


def _neighbourhood(z, table, n_ctx):
    b, n, _ = z.shape
    npair = NA_HEADS // 2
    blk = lambda off: pl.BlockSpec((1, n, LANES), lambda bb, p: (bb, 0, off + p))
    return pl.pallas_call(
        functools.partial(_na_kernel, n_ctx=n_ctx),
        out_shape=jax.ShapeDtypeStruct((b, n, BRANCH_W), BF16),
        grid=(b, npair),
        in_specs=[blk(NA_B), blk(NA_B + 4), blk(NA_B + 8),
                  pl.BlockSpec((1, NA_KH, 2 * GRID_W, NA_KH * GRID_W), lambda bb, p: (p, 0, 0, 0))],
        out_specs=pl.BlockSpec((1, n, LANES), lambda bb, p: (bb, 0, p)),
        compiler_params=_cparams(("arbitrary", "arbitrary")),
        name="neighbourhood_attn",
    )(z, z, z, table)


def _gqa_kernel(q_ref, k_ref, v_ref, cos_ref, sin_ref, qg_ref, kg_ref, bd_ref, o_ref, ks_ref, vs_ref, *, n_ctx):
    i = pl.program_id(1)
    tq = q_ref.shape[1]
    n = k_ref.shape[1]
    grp = GQA_HEADS // GQA_KV_HEADS
    bd = bd_ref[...]

    def norm_rope(x, g, cos, sin):
        hi, lo = _split_bf16(x * x)
        ms = _dot(hi, bd) + _dot(lo, bd)
        xn = x * lax.rsqrt(ms + NORM_EPS) * g
        return _rope(xn, cos, sin, GQA_HD // 4)

    @pl.when(i == 0)
    def _():
        def prep(c, carry):
            rows = pl.ds(pl.multiple_of(c * tq, tq), tq)
            kn = norm_rope(k_ref[0, rows, :].astype(F32), kg_ref[...], cos_ref[rows, :], sin_ref[rows, :])
            vv = v_ref[0, rows, :].astype(F32)
            lo_half = lax.broadcasted_iota(jnp.int32, kn.shape, 1) < GQA_HD
            ksw = pltpu.roll(kn, GQA_HD, 1)
            vsw = pltpu.roll(vv, GQA_HD, 1)
            ks_ref[0, rows, :] = jnp.where(lo_half, kn, ksw).astype(BF16)
            ks_ref[1, rows, :] = jnp.where(lo_half, ksw, kn).astype(BF16)
            vs_ref[0, rows, 0:LANES] = jnp.where(lo_half, vv, vsw).astype(BF16)
            vs_ref[1, rows, 0:LANES] = jnp.where(lo_half, vsw, vv).astype(BF16)
            ones = jnp.ones((tq, LANES), BF16)
            vs_ref[0, rows, LANES:] = ones
            vs_ref[1, rows, LANES:] = ones
            return carry

        lax.fori_loop(0, n // tq, prep, 0)

    qrows = pl.ds(pl.multiple_of(i * tq, tq), tq)
    cos = cos_ref[qrows, :]
    sin = sin_ref[qrows, :]
    qg = qg_ref[...]

    def attend(nk):
        for j in range(BRANCH_W // LANES):
            x = q_ref[0, :, j * LANES:(j + 1) * LANES].astype(F32)
            qb = (norm_rope(x, qg, cos, sin) * (GQA_HD ** -0.5 * LOG2E)).astype(BF16)
            lane = lax.broadcasted_iota(jnp.int32, qb.shape, 1)
            g = (2 * j) // grp
            zero = jnp.zeros_like(qb)
            qs = jnp.concatenate([jnp.where(lane < GQA_HD, qb, zero), jnp.where(lane >= GQA_HD, qb, zero)], axis=0)
            sc = _dot_nt(qs, ks_ref[g, 0:nk, :])
            p = jnp.exp2(sc - sc.max(axis=-1, keepdims=True)).astype(BF16)
            ov = _dot(p, vs_ref[g, 0:nk, :])
            o = ov[:, :LANES] / ov[:, LANES:]
            o_ref[0, :, j * LANES:(j + 1) * LANES] = jnp.where(lane < GQA_HD, o[:tq], o[tq:]).astype(o_ref.dtype)

    @pl.when(i * tq < n_ctx)
    def _():
        attend(n_ctx)

    @pl.when(i * tq >= n_ctx)
    def _():
        attend(n)


def _gqa(z, cos, sin, qn_g, kn_g, n_ctx, tq):
    b, n, _ = z.shape
    bd = np.kron(np.eye(LANES // GQA_HD), np.ones((GQA_HD, GQA_HD))) / GQA_HD
    qg = jnp.tile(qn_g.reshape(1, -1), (1, LANES // GQA_HD))
    kg = jnp.tile(kn_g.reshape(1, -1), (1, LANES // GQA_HD))
    qb = (GQ_B * LANES) // BRANCH_W
    const = lambda shape: pl.BlockSpec(shape, lambda bb, i: (0,) * len(shape))
    return pl.pallas_call(
        functools.partial(_gqa_kernel, n_ctx=n_ctx),
        out_shape=jax.ShapeDtypeStruct((b, n, BRANCH_W), BF16),
        grid=(b, n // tq),
        in_specs=[pl.BlockSpec((1, tq, BRANCH_W), lambda bb, i: (bb, i, qb)),
                  pl.BlockSpec((1, n, LANES), lambda bb, i: (bb, 0, GQ_B + 4)),
                  pl.BlockSpec((1, n, LANES), lambda bb, i: (bb, 0, GQ_B + 5)),
                  const((n, LANES)), const((n, LANES)), const((1, LANES)), const((1, LANES)),
                  const((LANES, LANES))],
        out_specs=pl.BlockSpec((1, tq, BRANCH_W), lambda bb, i: (bb, i, 0)),
        scratch_shapes=[pltpu.VMEM((GQA_KV_HEADS, n, LANES), BF16),
                        pltpu.VMEM((GQA_KV_HEADS, n, 2 * LANES), BF16)],
        compiler_params=_cparams(("arbitrary", "arbitrary")),
        name="gqa_attn",
    )(z, z, z, cos, sin, qg, kg, jnp.asarray(bd, BF16))


def _layernorm_rows(v, g, b):
    mu = jnp.mean(v, axis=-1, keepdims=True)
    vc = v - mu
    var = jnp.mean(vc * vc, axis=-1, keepdims=True)
    return vc * lax.rsqrt(var + LN_EPS) * g + b


def _pack_bf16_pairs(v):
    w = v.shape[1] // 2
    bits = lax.bitcast_convert_type(v.astype(BF16).astype(F32), jnp.int32)
    return bits[:, w:] | lax.shift_right_logical(bits[:, :w], 16)


def _unpack_bf16_pairs(pieces):
    lo = [lax.bitcast_convert_type(lax.shift_left(p, 16), F32) for p in pieces]
    hi = [lax.bitcast_convert_type(p & jnp.int32(-65536), F32) for p in pieces]
    return jnp.concatenate(lo + hi, axis=1)


def _store_pieces(ref, lead, packed):
    for h in range(SC_PIECES):
        ref[(h,) + lead] = packed[:, h * SC_ROW_WORDS:(h + 1) * SC_ROW_WORDS]


def _merge_kernel(z_ref, r_ref, h_ref, a_ref, q_ref, x_ref, mod_ref, wb_ref, wo_ref, lng_ref, lnb_ref,
                  wrh_ref, wrl_ref, br_ref, tri_ref, xo_ref, h2_ref, te_ref, tg_ref, rk_ref, cnt_ref,
                  run_ref, *, n_ctx, tm):
    i = pl.program_id(1)

    @pl.when((pl.program_id(0) == 0) & (i == 0))
    def _():
        run_ref[...] = jnp.zeros_like(run_ref)

    is_ctx = i * tm < n_ctx
    mod = jnp.where(is_ctx, mod_ref[0, 0], mod_ref[0, 1])
    acc = None
    for nb, br in enumerate((r_ref, h_ref, a_ref, q_ref)):
        yb = _dot(br[0], wb_ref[nb])
        gt = _sigmoid(z_ref[0, :, nb * D_MODEL:(nb + 1) * D_MODEL].astype(F32))
        acc = gt * yb if acc is None else acc + gt * yb
    y = _dot(acc.astype(BF16), wo_ref[...])
    xn = _layernorm_rows(DN_ALPHA * x_ref[0] + mod[2:3, :] * y, lng_ref[...], lnb_ref[...])
    xo_ref[0] = xn
    h2 = xn * mod[3:4, :] + mod[4:5, :]
    h2_hi, h2_lo = _split_bf16(h2)
    _store_pieces(h2_ref, (0,), _pack_bf16_pairs(h2))
    wrh = wrh_ref[...]
    logit = _dot_nt(wrh, h2_hi) + _dot_nt(wrh, h2_lo) + _dot_nt(wrl_ref[...], h2_hi) + br_ref[...]
    eidx = lax.broadcasted_iota(jnp.int32, logit.shape, 0)
    vals, idxs = [], []
    for _ in range(TOP_K):
        m = logit.max(axis=0, keepdims=True)
        sel = jnp.where(logit == m, eidx, N_EXPERTS).min(axis=0, keepdims=True)
        vals.append(m)
        idxs.append(sel)
        logit = jnp.where(eidx == sel, -jnp.inf, logit)
    ex = [jnp.exp(v - vals[0]) for v in vals]
    tot = ex[0] + ex[1] + ex[2] + ex[3]
    te_ref[0] = jnp.concatenate(idxs, axis=0)
    tg_ref[0] = jnp.concatenate([e / tot for e in ex], axis=0)
    base = run_ref[:, 0:1]
    tri = tri_ref[...]
    ranks = []
    for sel in idxs:
        hit = eidx == sel
        ones = jnp.where(hit, 1.0, 0.0)
        pre = _dot(ones.astype(BF16), tri)
        ranks.append(jnp.sum(jnp.where(hit, pre + base, 0.0), axis=0, keepdims=True))
        base = base + jnp.sum(ones, axis=1, keepdims=True)
    rk_ref[0] = jnp.concatenate(ranks, axis=0).astype(jnp.int32)
    run_ref[...] = jnp.broadcast_to(base, run_ref.shape)
    cnt_ref[...] = run_ref[...]


def _merge(z, outs, xs, mod, wb, wo, ln_g, ln_b, wr_hi, wr_lo, b_r, n_ctx, tm):
    b, n, d = xs.shape
    row = lambda w: pl.BlockSpec((1, tm, w), lambda bb, i: (bb, i, 0))
    full = lambda a: pl.BlockSpec(a.shape, lambda bb, i: (0,) * a.ndim)
    topk = pl.BlockSpec((1, TOP_K, tm), lambda bb, i: (bb, 0, i))
    ln_g = ln_g.reshape(1, d)
    ln_b = ln_b.reshape(1, d)
    b_r = b_r.reshape(N_EXPERTS, 1)
    tri = jnp.asarray(np.triu(np.ones((tm, tm), np.float32), 1), BF16)
    return pl.pallas_call(
        functools.partial(_merge_kernel, n_ctx=n_ctx, tm=tm),
        out_shape=(jax.ShapeDtypeStruct((b, n, d), F32),
                   jax.ShapeDtypeStruct((SC_PIECES, b, n, SC_ROW_WORDS), jnp.int32),
                   jax.ShapeDtypeStruct((b, TOP_K, n), jnp.int32),
                   jax.ShapeDtypeStruct((b, TOP_K, n), F32),
                   jax.ShapeDtypeStruct((b, TOP_K, n), jnp.int32),
                   jax.ShapeDtypeStruct((N_EXPERTS, LANES), F32)),
        grid=(b, n // tm),
        in_specs=[row(N_BRANCH * D_MODEL), row(BRANCH_W), row(BRANCH_W), row(BRANCH_W), row(BRANCH_W),
                  row(d),
                  pl.BlockSpec((1, 2, 8, d), lambda bb, i: (bb, 0, 0, 0)),
                  full(wb), full(wo), full(ln_g), full(ln_b), full(wr_hi), full(wr_lo), full(b_r), full(tri)],
        out_specs=(row(d), pl.BlockSpec((SC_PIECES, 1, tm, SC_ROW_WORDS), lambda bb, i: (0, bb, i, 0)),
                   topk, topk, topk,
                   pl.BlockSpec((N_EXPERTS, LANES), lambda bb, i: (0, 0))),
        scratch_shapes=[pltpu.VMEM((N_EXPERTS, LANES), F32)],
        compiler_params=_cparams(("arbitrary", "arbitrary")),
        name="merge_ln_router",
    )(z, *outs, xs, mod, wb, wo, ln_g, ln_b, wr_hi, wr_lo, b_r, tri)


def _ffn_kernel(be_ref, nv_ref, nw_ref, x_ref, wgu_ref, bgu_ref, wd_ref, bd_ref, o_ref, wgu_s, wd_s):
    i = pl.program_id(0)
    nv = nv_ref[i]

    @pl.when(nw_ref[i] > 0)
    def _():
        wgu_s[...] = wgu_ref[0, 0].astype(BF16)
        wd_s[...] = wd_ref[0, 0].astype(BF16)

    @pl.when(nv > 0)
    def _():
        rows = lax.broadcasted_iota(jnp.int32, (MOE_BLOCK, 1), 0)
        x = _unpack_bf16_pairs([x_ref[h] for h in range(SC_PIECES)])
        x = jnp.where(rows < nv, x, 0.0).astype(BF16)
        gu = _dot(x, wgu_s[...]) + bgu_ref[0, 0]
        g = jnp.minimum(gu[:, :D_FF_EXPERT], SWIGLU_LIMIT)
        u = jnp.clip(gu[:, D_FF_EXPERT:], -SWIGLU_LIMIT, SWIGLU_LIMIT)
        act = g * _sigmoid(SWIGLU_ALPHA * g) * (u + 1.0)
        _store_pieces(o_ref, (), _pack_bf16_pairs(_dot(act.astype(BF16), wd_s[...]) + bd_ref[0, 0]))

    @pl.when(nv <= 0)
    def _():
        o_ref[...] = jnp.zeros_like(o_ref)


def _expert_ffn(xb, block_e, n_valid, layer, wgu, bgu, wd, bd):
    _, n_slots, _ = xb.shape
    n_blocks = n_slots // MOE_BLOCK
    depth, ne, d, f2 = wgu.shape
    new_w = jnp.concatenate([jnp.ones((1,), jnp.int32), (block_e[1:] != block_e[:-1]).astype(jnp.int32)])
    slot_blk = pl.BlockSpec((SC_PIECES, MOE_BLOCK, SC_ROW_WORDS), lambda i, be, nv, nw: (0, i, 0))
    wblk = lambda r, c: pl.BlockSpec((1, 1, r, c), lambda i, be, nv, nw: (layer, be[i], 0, 0))
    grid_spec = pltpu.PrefetchScalarGridSpec(
        num_scalar_prefetch=3,
        grid=(n_blocks,),
        in_specs=[slot_blk, wblk(d, f2), wblk(1, f2), wblk(f2 // 2, d), wblk(1, d)],
        out_specs=slot_blk,
        scratch_shapes=[pltpu.VMEM((d, f2), BF16), pltpu.VMEM((f2 // 2, d), BF16)],
    )
    return pl.pallas_call(
        _ffn_kernel,
        out_shape=jax.ShapeDtypeStruct(xb.shape, jnp.int32),
        grid_spec=grid_spec,
        compiler_params=_cparams(("arbitrary",)),
        name="expert_ffn",
    )(block_e, n_valid, new_w, xb, wgu, bgu.reshape(depth, ne, 1, f2), wd, bd.reshape(depth, ne, 1, d))


def _combine_kernel(y_ref, g_ref, x_ref, mod_ref, lng_ref, lnb_ref, xo_ref, *, n_ctx, tm):
    i = pl.program_id(1)
    is_ctx = i * tm < n_ctx
    g2 = jnp.where(is_ctx, mod_ref[0, 0, 5:6, :], mod_ref[0, 1, 5:6, :])
    gate = g_ref[...]
    y = None
    for k in range(TOP_K):
        t = gate[:, k:k + 1] * _unpack_bf16_pairs([y_ref[k, h] for h in range(SC_PIECES)])
        y = t if y is None else y + t
    xo_ref[0] = _layernorm_rows(DN_ALPHA * x_ref[0] + g2 * y, lng_ref[...], lnb_ref[...])


def _combine(yk, gate, xs, mod, ln_g, ln_b, n_ctx, tm):
    b, n, d = xs.shape
    nt = n // tm
    return pl.pallas_call(
        functools.partial(_combine_kernel, n_ctx=n_ctx, tm=tm),
        out_shape=jax.ShapeDtypeStruct((b, n, d), F32),
        grid=(b, nt),
        in_specs=[pl.BlockSpec((TOP_K, SC_PIECES, tm, SC_ROW_WORDS), lambda bb, i: (0, 0, bb * nt + i, 0)),
                  pl.BlockSpec((tm, TOP_K), lambda bb, i: (bb * nt + i, 0)),
                  pl.BlockSpec((1, tm, d), lambda bb, i: (bb, i, 0)),
                  pl.BlockSpec((1, 2, 8, d), lambda bb, i: (bb, 0, 0, 0)),
                  pl.BlockSpec((1, d), lambda bb, i: (0, 0)),
                  pl.BlockSpec((1, d), lambda bb, i: (0, 0))],
        out_specs=pl.BlockSpec((1, tm, d), lambda bb, i: (bb, i, 0)),
        compiler_params=_cparams(("arbitrary", "arbitrary")),
        name="moe_combine_ln",
    )(yk, gate, xs, mod, ln_g.reshape(1, d), ln_b.reshape(1, d))


def _sc_mesh():
    return plsc.VectorSubcoreMesh(core_axis_name="core", subcore_axis_name="subcore",
                                  num_cores=SC_CORES, num_subcores=SC_SUBCORES)


def _sc_piece_index(idx, rows):
    return jnp.concatenate([idx + h * rows for h in range(SC_PIECES)], axis=1)


def _sc_scatter_rows(x, dest, n_out):
    p, n, w = x.shape
    x2 = x.reshape(p * n, w)
    dest2 = _sc_piece_index(dest, n_out)
    kk, n2 = dest2.shape

    @functools.partial(pl.kernel, out_type=jax.ShapeDtypeStruct((p * n_out, w), x.dtype),
                       mesh=_sc_mesh(), scratch_types=[], name="moe_dispatch_scatter")
    def scatter(x_hbm, i_hbm, o_hbm):
        def body(x_vmem, i_vmem):
            for j in range(kk):
                pltpu.sync_copy(x_vmem, o_hbm.at[i_vmem.at[j]])

        pltpu.emit_pipeline(
            body, grid=(n2 // SC_WINDOW,),
            in_specs=[pl.BlockSpec((SC_WINDOW, SC_ROW_WORDS), lambda i: (i, 0)),
                      pl.BlockSpec((kk, SC_WINDOW), lambda i: (0, i))],
            out_specs=[], core_axis_name=("core", "subcore"),
            dimension_semantics=(pltpu.PARALLEL,))(x_hbm, i_hbm)

    return scatter(x2, dest2).reshape(p, n_out, w)


def _sc_gather_rows(table, idx):
    p, v, w = table.shape
    kk, n = idx.shape
    t2 = table.reshape(p * v, w)
    idx2 = _sc_piece_index(idx, v)
    m = kk * n * p

    @functools.partial(pl.kernel, out_type=jax.ShapeDtypeStruct((m, SC_ROW_WORDS), table.dtype),
                       mesh=_sc_mesh(), scratch_types=[], name="moe_combine_gather")
    def gather(t_hbm, i_hbm, o_hbm):
        def body(i_vmem, o_vmem):
            pltpu.sync_copy(t_hbm.at[i_vmem.at[0]], o_vmem)

        pltpu.emit_pipeline(
            body, grid=(m // SC_WINDOW,),
            in_specs=[pl.BlockSpec((1, SC_WINDOW), lambda i: (0, i))],
            out_specs=[pl.BlockSpec((SC_WINDOW, SC_ROW_WORDS), lambda i: (i, 0))],
            core_axis_name=("core", "subcore"),
            dimension_semantics=(pltpu.PARALLEL,))(i_hbm, o_hbm)

    return gather(t2, idx2.reshape(1, m)).reshape(kk, p, n, w)


def _moe(h2p, top_e, top_g, rank, cnt, layer, wgu, bgu, wd, bd):
    p, b, n, w = h2p.shape
    n_tok = b * n
    nk = n_tok * TOP_K
    experts = jnp.arange(N_EXPERTS, dtype=jnp.int32)
    counts = cnt[:, 0].astype(jnp.int32)
    padded = (counts + MOE_BLOCK - 1) // MOE_BLOCK * MOE_BLOCK
    ends_p = jnp.cumsum(padded)
    start_p = ends_p - padded
    n_blocks = (nk + N_EXPERTS * (MOE_BLOCK - 1) + MOE_BLOCK - 1) // MOE_BLOCK
    block_start = jnp.arange(n_blocks, dtype=jnp.int32) * MOE_BLOCK
    block_e = jnp.minimum(jnp.sum(ends_p[None, :] <= block_start[:, None], axis=1), N_EXPERTS - 1).astype(jnp.int32)
    is_e = block_e[:, None] == experts[None, :]
    filled = jnp.sum(jnp.where(is_e, (start_p + counts)[None, :], 0), axis=1)
    n_valid = jnp.clip(filled - block_start, 0, MOE_BLOCK).astype(jnp.int32)
    dest = rank + jnp.sum(jnp.where(top_e[..., None] == experts, start_p, 0), axis=-1)
    dest = dest.transpose(1, 0, 2).reshape(TOP_K, n_tok)
    gate = top_g.transpose(0, 2, 1).reshape(n_tok, TOP_K)
    xb = _sc_scatter_rows(h2p.reshape(p, n_tok, w), dest, n_blocks * MOE_BLOCK)
    yb = _expert_ffn(xb, block_e, n_valid, layer, wgu, bgu, wd, bd)
    yk = _sc_gather_rows(yb, dest)
    return yk, gate


def kernel(x, c, ctx, c_ctx, w_mod, b_mod, w_in, ret_decay, ret_gn_g, ret_gn_b, hg_lb, hg_norm_g,
           na_rpb, gq_qn_g, gq_kn_g, w_branch, w_out, ln_g, ln_b, w_router, b_router, w_gu, b_gu,
           w_down, b_down):
    b_, t_, d = x.shape
    n_ctx = ctx.shape[1]
    depth = w_mod.shape[0]
    tm = 256

    sm = jax.nn.softmax(hg_lb.astype(F32), axis=1)
    lower = jnp.cumsum(sm, axis=1) - sm[:, :1]
    log_gamma = jax.nn.log_sigmoid(ret_decay.astype(F32)).reshape(depth, 2 * RET_HEADS)

    cc = jnp.concatenate([c, c_ctx[None, :], jnp.zeros((16 - b_ - 1, d), F32)], axis=0)
    modv = _mod_all(cc, w_mod, b_mod).reshape(depth, 16, 6, d)
    one = jnp.asarray([0.0, 1.0, 0.0, 0.0, 1.0, 0.0], F32)[None, None, :, None]
    modv = (modv + one)[:, :, jnp.asarray([1, 0, 2, 4, 3, 5])]
    modv = jnp.concatenate([modv, jnp.zeros((depth, 16, 2, d), F32)], axis=2)
    mod = jnp.stack([jnp.broadcast_to(modv[:, b_:b_ + 1], (depth, b_, 8, d)), modv[:, :b_]], axis=2)

    cos_r, sin_r = _rope_tables(n_ctx, t_, RET_DK)
    cos_g, sin_g = _rope_tables(n_ctx, t_, GQA_HD)

    xs = jnp.concatenate([ctx, x], axis=1)
    tm_in = 768 if xs.shape[1] % 768 == 0 else tm
    w_in_b = _w_in_prep(w_in)
    for l in range(depth):
        z = _in_proj(xs, mod[l], w_in_b[l], n_ctx, tm_in, PROJ_PAD // 4)
        ret_o = _retention(z, log_gamma[l], cos_r, sin_r, ret_gn_g[l], ret_gn_b[l], n_ctx)
        hg_o = _hgrn2(z, lower[:, l], hg_norm_g[l], n_ctx)
        na_o = _neighbourhood(z, _na_bias_table(na_rpb[l]), n_ctx)
        gq_o = _gqa(z, cos_g, sin_g, gq_qn_g[l], gq_kn_g[l], n_ctx, tm)
        wr_hi, wr_lo = _split_bf16(w_router[l].T)
        xs, h2p, top_e, top_g, rank, cnt = _merge(z, (ret_o, hg_o, na_o, gq_o), xs, mod[l],
                                                  w_branch[l].astype(BF16), w_out[l].astype(BF16),
                                                  ln_g[l, 0], ln_b[l, 0], wr_hi, wr_lo, b_router[l], n_ctx, tm)
        yk, gate = _moe(h2p, top_e, top_g, rank, cnt, l, w_gu, b_gu, w_down, b_down)
        xs = _combine(yk, gate, xs, mod[l], ln_g[l, 1], ln_b[l, 1], n_ctx, tm)
    return xs[:, n_ctx:]
```

```python
import functools

import jax
import jax.numpy as jnp
from jax import lax
import numpy as np
from jax.experimental import pallas as pl
from jax.experimental.pallas import tpu as pltpu
from jax.experimental.pallas import tpu_sc as plsc

D_MODEL = 1024
DEPTH = 4
GRID_W = 64
N_BRANCH = 4
BRANCH_W = D_MODEL // 2
RET_HEADS = 4
RET_DK = BRANCH_W // RET_HEADS
HG_HEADS = 4
HG_DK = BRANCH_W // HG_HEADS
NA_HEADS = 8
NA_HD = BRANCH_W // NA_HEADS
NA_KH = 8
NA_KW = 16
GQA_HEADS = 8
GQA_KV_HEADS = 2
GQA_HD = BRANCH_W // GQA_HEADS
GQA_KV_W = GQA_KV_HEADS * GQA_HD
ROPE_BASE = 10000.0
N_EXPERTS = 32
TOP_K = 4
D_FF_EXPERT = D_MODEL
SWIGLU_LIMIT = 7.0
SWIGLU_ALPHA = 1.702
MOE_BLOCK = 256
LN_EPS = 1e-5
NORM_EPS = 1e-6
NEG_INF = -1e30
DN_ALPHA = (2 * DEPTH) ** 0.25

LANES = 128
BF16 = jnp.bfloat16
F32 = jnp.float32
VMEM_LIMIT = 56 * 1024 * 1024
SC_CORES = 2
SC_SUBCORES = 16
SC_WINDOW = 128
SC_ROW_WORDS = 256
SC_PIECES = D_MODEL // 2 // SC_ROW_WORDS

PROJ_ORIG = 4 * BRANCH_W + 5 * BRANCH_W + 3 * BRANCH_W + BRANCH_W + 2 * GQA_KV_W
PROJ_TOTAL = PROJ_ORIG + N_BRANCH * D_MODEL
PROJ_TILES = 2
RET_B = (N_BRANCH * D_MODEL) // LANES
HG_B = RET_B + 16
NA_B = HG_B + 20
GQ_B = NA_B + 12

RET_CHUNK = 256
HG_CHUNK = 128
LOG2E = 1.4426950408889634
HG_LEVELS = (64, 32, 16, 8, 4, 2, 1, 0)


def _cparams(sem):
    return pltpu.CompilerParams(dimension_semantics=sem, vmem_limit_bytes=VMEM_LIMIT)


def _dot(a, b):
    return jnp.dot(a, b, preferred_element_type=F32)


def _dot_nt(a, b):
    return lax.dot_general(a, b, (((1,), (1,)), ((), ())), preferred_element_type=F32)


def _dot_tn(a, b):
    return lax.dot_general(a, b, (((0,), (0,)), ((), ())), preferred_element_type=F32)


def _split_bf16(x):
    hi = x.astype(BF16)
    lo = (x - hi.astype(F32)).astype(BF16)
    return hi, lo


def _sigmoid(x):
    return 1.0 / (1.0 + jnp.exp(-x))


def _silu(x):
    return x * _sigmoid(x)


def _mod_kernel(c_ref, w_ref, b_ref, o_ref):
    s = _silu(c_ref[...]).astype(BF16)
    o_ref[0] = _dot(s, w_ref[0].astype(BF16)) + b_ref[0]


def _mod_all(cc, w_mod, b_mod):
    depth, d, n = w_mod.shape
    r = cc.shape[0]
    tn = 1536
    return pl.pallas_call(
        _mod_kernel,
        out_shape=jax.ShapeDtypeStruct((depth, r, n), F32),
        grid=(depth, n // tn),
        in_specs=[pl.BlockSpec((r, d), lambda l, j: (0, 0)),
                  pl.BlockSpec((1, d, tn), lambda l, j: (l, 0, j)),
                  pl.BlockSpec((1, 1, tn), lambda l, j: (l, 0, j))],
        out_specs=pl.BlockSpec((1, r, tn), lambda l, j: (l, 0, j)),
        compiler_params=_cparams(("arbitrary", "arbitrary")),
        name="adaln_mod",
    )(cc, w_mod, b_mod.reshape(depth, 1, n))


W_PREP_TILE = 256


def _w_in_prep_kernel(w_ref, o_ref):
    o_ref[0] = w_ref[0].astype(BF16)


def _w_in_prep(w_in):
    depth, d, total = w_in.shape
    n_src = total // W_PREP_TILE
    shift = PROJ_ORIG // W_PREP_TILE
    return pl.pallas_call(
        _w_in_prep_kernel,
        out_shape=jax.ShapeDtypeStruct((depth, d, total), BF16),
        grid=(depth, n_src),
        in_specs=[pl.BlockSpec((1, d, W_PREP_TILE), lambda l, j: (l, 0, (j + shift) % n_src))],
        out_specs=pl.BlockSpec((1, d, W_PREP_TILE), lambda l, j: (l, 0, j)),
        compiler_params=_cparams(("arbitrary", "arbitrary")),
        name="w_in_prep",
    )(w_in)


def _in_proj_kernel(x_ref, mod_ref, w_ref, o_ref, *, n_ctx, tm):
    i = pl.program_id(2)
    x = x_ref[0]
    mc = mod_ref[0, 0]
    ml = mod_ref[0, 1]
    rows = i * tm + lax.broadcasted_iota(jnp.int32, (tm, 1), 0)
    is_ctx = rows < n_ctx
    scale = jnp.where(is_ctx, mc[0:1, :], ml[0:1, :])
    shift = jnp.where(is_ctx, mc[1:2, :], ml[1:2, :])
    h = (x * scale + shift).astype(BF16)
    o_ref[0] = _dot(h, w_ref[...]).astype(o_ref.dtype)


def _in_proj(xs, mod, w, n_ctx, tm, tn):
    b, n, d = xs.shape
    ncol = w.shape[1]
    return pl.pallas_call(
        functools.partial(_in_proj_kernel, n_ctx=n_ctx, tm=tm),
        out_shape=jax.ShapeDtypeStruct((b, n, ncol), BF16),
        grid=(ncol // tn, b, n // tm),
        in_specs=[pl.BlockSpec((1, tm, d), lambda j, bb, i: (bb, i, 0)),
                  pl.BlockSpec((1, 2, 8, d), lambda j, bb, i: (bb, 0, 0, 0)),
                  pl.BlockSpec((d, tn), lambda j, bb, i: (0, j))],
        out_specs=pl.BlockSpec((1, tm, tn), lambda j, bb, i: (bb, i, j)),
        compiler_params=_cparams(("arbitrary", "arbitrary", "arbitrary")),
        name="in_proj",
    )(xs, mod, w)


def _rope_tables(n_ctx, t, head_dim):
    idx = jnp.arange(t, dtype=jnp.int32)
    row = (idx // GRID_W).astype(F32)
    col = (idx % GRID_W).astype(F32)
    n = head_dim // 2
    inv = ROPE_BASE ** (-jnp.arange(0, n, 2, dtype=F32) / n)

    def half(pos):
        ang = pos[:, None] * inv[None, :]
        c, s = jnp.cos(ang), jnp.sin(ang)
        return jnp.concatenate([c, c], -1), jnp.concatenate([-s, s], -1)

    cr, sr = half(row)
    cc, sc = half(col)
    cos = jnp.concatenate([cr, cc], -1)
    sin = jnp.concatenate([sr, sc], -1)
    reps = LANES // head_dim
    cos = jnp.tile(cos, (1, reps))
    sin = jnp.tile(sin, (1, reps))
    cos = jnp.concatenate([jnp.ones((n_ctx, LANES), F32), cos], 0)
    sin = jnp.concatenate([jnp.zeros((n_ctx, LANES), F32), sin], 0)
    return cos, sin


def _rope(x, cos, sin, quarter):
    lane = lax.broadcasted_iota(jnp.int32, x.shape, 1)
    first = (lane & (2 * quarter - 1)) < quarter
    rot = jnp.where(first, pltpu.roll(x, LANES - quarter, 1), pltpu.roll(x, quarter, 1))
    return x * cos + rot * sin


def _ret_kernel(lg_ref, q_ref, k_ref, v_ref, g_ref, cos_ref, sin_ref, gng_ref, gnb_ref, o_ref,
                ks_ref, sb_ref, *, n_ctx, chunk):
    hh = pl.program_id(1)
    lgf = lg_ref[hh]
    lgb = lg_ref[RET_HEADS + hh]
    n = q_ref.shape[1]
    c_ = chunk
    nch = n // c_
    n_cc = n_ctx // c_
    dk = q_ref.shape[2]

    ri = lax.broadcasted_iota(jnp.int32, (c_, dk), 0).astype(F32)
    df = jnp.exp((ri + 1.0) * lgf)
    db = jnp.exp((c_ - ri) * lgb)
    ef = jnp.exp((c_ - 1.0 - ri) * lgf)
    eb = jnp.exp(ri * lgb)
    gfc = jnp.exp(c_ * lgf)
    gbc = jnp.exp(c_ * lgb)
    ii = lax.broadcasted_iota(jnp.int32, (c_, c_), 0)
    jj = lax.broadcasted_iota(jnp.int32, (c_, c_), 1)
    dij = (ii - jj).astype(F32)
    w = jnp.where(ii >= jj, jnp.exp(dij * lgf), jnp.exp(-dij * lgb))

    def rope_k(c, carry):
        r0 = pl.multiple_of(c * c_, c_)
        kc = k_ref[0, pl.ds(r0, c_), :].astype(F32) * (RET_DK ** -0.5)
        kc = _rope(kc, cos_ref[pl.ds(r0, c_), :], sin_ref[pl.ds(r0, c_), :], 32)
        ks_ref[pl.ds(r0, c_), :] = kc.astype(BF16)
        return carry

    lax.fori_loop(0, nch, rope_k, 0, unroll=3)

    def bwd_step(c, s):
        r0 = pl.multiple_of(c * c_, c_)
        sb_ref[c] = s
        kc = (ks_ref[pl.ds(r0, c_), :].astype(F32) * eb).astype(BF16)
        return s * gbc + _dot_tn(kc, v_ref[0, pl.ds(r0, c_), :])

    s = jnp.zeros((dk, dk), F32)
    s = lax.fori_loop(0, n_cc, lambda t, s_: bwd_step(n_cc - 1 - t, s_), s)
    lax.fori_loop(0, nch - n_cc, lambda t, s_: bwd_step(nch - 1 - t, s_), s, unroll=4)

    gng = gng_ref[...]
    gnb = gnb_ref[...]

    def fwd_step(c, sf):
        r0 = pl.multiple_of(c * c_, c_)
        qc = _rope(q_ref[0, pl.ds(r0, c_), :].astype(F32), cos_ref[pl.ds(r0, c_), :],
                   sin_ref[pl.ds(r0, c_), :], 32)
        kc = ks_ref[pl.ds(r0, c_), :]
        vc = v_ref[0, pl.ds(r0, c_), :]
        a = _dot_nt(qc.astype(BF16), kc) * w
        qi = jnp.concatenate([qc * df, qc * db], axis=1).astype(BF16)
        si = jnp.concatenate([sf, sb_ref[c]], axis=0).astype(BF16)
        o = _dot(a.astype(BF16), vc) + _dot(qi, si)
        sf_new = sf * gfc + _dot_tn((kc.astype(F32) * ef).astype(BF16), vc)
        mu = jnp.mean(o, axis=-1, keepdims=True)
        oc = o - mu
        var = jnp.mean(oc * oc, axis=-1, keepdims=True)
        y = oc * lax.rsqrt(var + LN_EPS) * gng + gnb
        gate = g_ref[0, pl.ds(r0, c_), :].astype(F32)
        o_ref[0, pl.ds(r0, c_), :] = (y * _silu(gate)).astype(o_ref.dtype)
        return sf_new

    lax.fori_loop(0, nch, fwd_step, jnp.zeros((dk, dk), F32), unroll=3)


def _retention(z, log_gamma, cos, sin, gn_g, gn_b, n_ctx):
    b, n, _ = z.shape
    blk = lambda off: pl.BlockSpec((1, n, LANES), lambda bb, h, lg: (bb, 0, off + h))
    grid_spec = pltpu.PrefetchScalarGridSpec(
        num_scalar_prefetch=1,
        grid=(b, RET_HEADS),
        in_specs=[blk(RET_B), blk(RET_B + 4), blk(RET_B + 8), blk(RET_B + 12),
                  pl.BlockSpec((n, LANES), lambda bb, h, lg: (0, 0)),
                  pl.BlockSpec((n, LANES), lambda bb, h, lg: (0, 0)),
                  pl.BlockSpec((1, LANES), lambda bb, h, lg: (0, h)),
                  pl.BlockSpec((1, LANES), lambda bb, h, lg: (0, h))],
        out_specs=pl.BlockSpec((1, n, LANES), lambda bb, h, lg: (bb, 0, h)),
        scratch_shapes=[pltpu.VMEM((n, LANES), BF16),
                        pltpu.VMEM((n // RET_CHUNK, RET_DK, RET_DK), F32)],
    )
    return pl.pallas_call(
        functools.partial(_ret_kernel, n_ctx=n_ctx, chunk=RET_CHUNK),
        out_shape=jax.ShapeDtypeStruct((b, n, BRANCH_W), BF16),
        grid_spec=grid_spec,
        compiler_params=_cparams(("arbitrary", "arbitrary")),
        name="retention",
    )(log_gamma, z, z, z, z, cos, sin, gn_g.reshape(1, -1), gn_b.reshape(1, -1))


def _hg_constants():
    c = HG_CHUNK
    r = np.arange(c)[:, None]
    t = np.arange(c)[None, :]
    mats = []
    level = np.full((2, c, c), len(HG_LEVELS), np.int32)
    for d in range(2):
        blocks = []
        for li, s in enumerate(HG_LEVELS):
            if s == 0:
                level[d][np.arange(c), np.arange(c)] = li
                continue
            base = (r // (2 * s)) * (2 * s)
            if d == 0:
                m = base + s - 1
                mat = (t > np.minimum(r, m)) & (t <= np.maximum(r, m))
                q_side = (r % (2 * s)) >= s
            else:
                m = base + s
                mat = (t >= np.minimum(r, m)) & (t < np.maximum(r, m))
                q_side = (r % (2 * s)) < s
            blocks.append(mat.astype(np.float32))
            same = (r // (2 * s)) == (t // (2 * s))
            k_side_t = ((t % (2 * s)) < s) if d == 0 else ((t % (2 * s)) >= s)
            level[d][same & q_side & k_side_t] = li
        if d == 0:
            blocks.append((t <= r).astype(np.float32))
            blocks.append((t > r).astype(np.float32))
        else:
            blocks.append((t >= r).astype(np.float32))
            blocks.append((t < r).astype(np.float32))
        mats.append(np.concatenate(blocks, 0))
    mats = np.stack(mats)
    masks = (level[:, None] == np.arange(len(HG_LEVELS))[None, :, None, None]).astype(np.float32)
    upper = np.stack([np.broadcast_to(((np.arange(c) % (2 * s)) >= s)[:, None], (c, c)) for s in HG_LEVELS if s > 0])
    return np.concatenate([mats, mats], axis=2), np.maximum(masks[0], masks[1]), upper.astype(np.float32)


def _hg_gates(zf, lb):
    en = jnp.exp(-jnp.abs(zf))
    inv = 1.0 / (1.0 + en)
    pos = zf >= 0
    sg = jnp.where(pos, inv, en * inv)
    sgn = jnp.where(pos, en * inv, inv)
    one_m = 1.0 - lb
    return jnp.log(lb + one_m * sg), one_m * sgn


def _hg_kernel(q_ref, ff_ref, fb_ref, v_ref, g_ref, lb_ref, ng_ref, m_ref, lv_ref, up_ref, o_ref,
               lfb_ref, kb_ref, sb_ref, ktb_ref, dtot_ref, *, n_ctx):
    c_ = HG_CHUNK
    n = q_ref.shape[1]
    nch = n // c_
    n_cc = n_ctx // c_
    dk = q_ref.shape[2]
    nl = len(HG_LEVELS)
    lbf = lb_ref[0:1, :]
    lbb = lb_ref[1:2, :]

    def mdot(mat, lf):
        hi, lo = _split_bf16(lf)
        return _dot(mat, jnp.concatenate([hi, lo], axis=0))

    def prep_b(c, carry):
        r0 = pl.multiple_of(c * c_, c_)
        lf, key = _hg_gates(fb_ref[0, pl.ds(r0, c_), :].astype(F32), lbb)
        lf = lf * LOG2E
        lfb_ref[pl.ds(r0, c_), :] = lf
        kb_ref[pl.ds(r0, c_), :] = key
        e = mdot(m_ref[1, (nl - 1) * c_:(nl + 1) * c_, :], lf)
        ktb_ref[pl.ds(r0, c_), :] = (key * jnp.exp2(e[c_:2 * c_, :])).astype(BF16)
        dtot_ref[c] = jnp.broadcast_to(jnp.exp2(e[0:1, :]), (8, dk))
        return carry

    lax.fori_loop(0, nch, prep_b, 0, unroll=3)

    def bwd_step(c, st):
        r0 = pl.multiple_of(c * c_, c_)
        sb_ref[c] = st
        return st * dtot_ref[c][0:1, :] + _dot_tn(v_ref[0, pl.ds(r0, c_), :], ktb_ref[pl.ds(r0, c_), :])

    st = jnp.zeros((dk, dk), F32)
    st = lax.fori_loop(0, n_cc, lambda t, s_: bwd_step(n_cc - 1 - t, s_), st, unroll=2)
    lax.fori_loop(0, nch - n_cc, lambda t, s_: bwd_step(nch - 1 - t, s_), st, unroll=8)

    ng = ng_ref[...]

    def mdot_pair(mat, lf0, lf1):
        hi0, lo0 = _split_bf16(lf0)
        hi1, lo1 = _split_bf16(lf1)
        rhs = jnp.concatenate([jnp.concatenate([hi0, hi1], axis=1), jnp.concatenate([lo0, lo1], axis=1)], axis=0)
        return _dot(mat, rhs)

    def intra(qs, keyf, keyb, ef, eb):
        a = jnp.zeros((c_, c_), F32)
        qsb, kfb, kbb = qs.astype(BF16), keyf.astype(BF16), keyb.astype(BF16)
        for li, s in enumerate(HG_LEVELS):
            if s == 0:
                qt = jnp.concatenate([qsb, qsb], axis=1)
                kt = jnp.concatenate([kfb, kbb], axis=1)
            else:
                decf = jnp.exp2(ef[li * c_:(li + 1) * c_, :])
                decb = jnp.exp2(eb[li * c_:(li + 1) * c_, :])
                up = up_ref[li]
                dqf = decf * up
                dkb = decb * up
                qt = jnp.concatenate([qsb * dqf.astype(BF16), qsb * (decb - dkb).astype(BF16)], axis=1)
                kt = jnp.concatenate([kfb * (decf - dqf).astype(BF16), kbb * dkb.astype(BF16)], axis=1)
            a = a + _dot_nt(qt, kt) * lv_ref[li]
        cumf = ef[(nl - 1) * c_:nl * c_, :]
        restf = ef[nl * c_:(nl + 1) * c_, :]
        cumb = eb[(nl - 1) * c_:nl * c_, :]
        return a, cumf, restf, cumb

    def fwd_chunk(c, sf, qs, keyf, ef, eb):
        r0 = pl.multiple_of(c * c_, c_)
        vc = v_ref[0, pl.ds(r0, c_), :]
        a, cumf, restf, cumb = intra(qs, keyf, kb_ref[pl.ds(r0, c_), :], ef, eb)
        qi = jnp.concatenate([qs * jnp.exp2(cumf), qs * jnp.exp2(cumb)], axis=1).astype(BF16)
        si = jnp.concatenate([sf, sb_ref[c]], axis=1).astype(BF16)
        o = _dot(a.astype(BF16), vc) + _dot_nt(qi, si)
        totf = cumf[c_ - 1:c_, :]
        sf_new = sf * jnp.exp2(totf) + _dot_tn(vc, (keyf * jnp.exp2(restf)).astype(BF16))
        on = o * lax.rsqrt(jnp.mean(o * o, axis=-1, keepdims=True) + NORM_EPS) * ng
        gate = g_ref[0, pl.ds(r0, c_), :].astype(F32)
        o_ref[0, pl.ds(r0, c_), :] = (on * _silu(gate)).astype(o_ref.dtype)
        return sf_new

    def fwd_pair(p, sf):
        r0 = pl.multiple_of(p * (2 * c_), 2 * c_)
        qs, keyf, lff = [], [], []
        for t in range(2):
            rows = pl.ds(r0 + t * c_, c_)
            qs.append(_silu(q_ref[0, rows, :].astype(F32)))
            lf, key = _hg_gates(ff_ref[0, rows, :].astype(F32), lbf)
            lff.append(lf * LOG2E)
            keyf.append(key)
        ef = mdot_pair(m_ref[0], lff[0], lff[1])
        eb = mdot_pair(m_ref[1], lfb_ref[pl.ds(r0, c_), :], lfb_ref[pl.ds(r0 + c_, c_), :])
        for t in range(2):
            sf = fwd_chunk(2 * p + t, sf, qs[t], keyf[t], ef[:, t * dk:(t + 1) * dk], eb[:, t * dk:(t + 1) * dk])
        return sf

    lax.fori_loop(0, nch // 2, fwd_pair, jnp.zeros((dk, dk), F32))


def _hgrn2(z, lower, norm_g, n_ctx):
    b, n, _ = z.shape
    mats, level, upper = _hg_constants()
    mats = jnp.asarray(mats, BF16)
    level = jnp.asarray(level)
    upper = jnp.asarray(upper)
    blk = lambda off: pl.BlockSpec((1, n, LANES), lambda bb, h: (bb, 0, off + h))
    return pl.pallas_call(
        functools.partial(_hg_kernel, n_ctx=n_ctx),
        out_shape=jax.ShapeDtypeStruct((b, n, BRANCH_W), BF16),
        grid=(b, HG_HEADS),
        in_specs=[blk(HG_B), blk(HG_B + 4), blk(HG_B + 8), blk(HG_B + 12), blk(HG_B + 16),
                  pl.BlockSpec((2, LANES), lambda bb, h: (0, h)),
                  pl.BlockSpec((1, LANES), lambda bb, h: (0, h)),
                  pl.BlockSpec(mats.shape, lambda bb, h: (0, 0, 0)),
                  pl.BlockSpec(level.shape, lambda bb, h: (0, 0, 0)),
                  pl.BlockSpec(upper.shape, lambda bb, h: (0, 0, 0))],
        out_specs=pl.BlockSpec((1, n, LANES), lambda bb, h: (bb, 0, h)),
        scratch_shapes=[pltpu.VMEM((n, LANES), F32), pltpu.VMEM((n, LANES), F32),
                        pltpu.VMEM((n // HG_CHUNK, HG_DK, HG_DK), F32),
                        pltpu.VMEM((n, LANES), BF16), pltpu.VMEM((n // HG_CHUNK, 8, LANES), F32)],
        compiler_params=_cparams(("arbitrary", "arbitrary")),
        name="hgrn2",
    )(z, z, z, z, z, lower, norm_g.reshape(1, -1), mats, level, upper)


def _na_bias_table(rpb):
    qc = np.arange(GRID_W)[:, None]
    kc = np.arange(GRID_W)[None, :]
    col0 = np.clip(qc - NA_KW // 2, 0, GRID_W - NA_KW)
    valid = (kc >= col0) & (kc < col0 + NA_KW)
    dc = np.clip(kc - qc + (NA_KW - 1), 0, 2 * NA_KW - 2)
    pick = (dc[:, :, None] == np.arange(2 * NA_KW - 1)).astype(np.float32)
    t15 = jnp.einsum('hrc,qkc->hrqk', rpb.astype(F32), pick, precision=lax.Precision.HIGHEST)
    t = jnp.stack([t15[:, dl:dl + NA_KH] for dl in range(NA_KH)], axis=1)
    t = jnp.where(jnp.asarray(valid)[None, None, None], t * LOG2E, NEG_INF)
    t = t.transpose(0, 1, 3, 2, 4).reshape(rpb.shape[0] // 2, 2, NA_KH, GRID_W, NA_KH * GRID_W)
    return t.transpose(0, 2, 1, 3, 4).reshape(rpb.shape[0] // 2, NA_KH, 2 * GRID_W, NA_KH * GRID_W)


def _softmax_pv(parts):
    m = parts[0][0].max(axis=-1, keepdims=True)
    for s, _ in parts[1:]:
        m = jnp.maximum(m, s.max(axis=-1, keepdims=True))
    l = None
    o = None
    for s, v in parts:
        p = jnp.exp2(s - m)
        ls = p.sum(axis=-1, keepdims=True)
        os_ = _dot(p.astype(BF16), v)
        l = ls if l is None else l + ls
        o = os_ if o is None else o + os_
    return o / l


def _na_kernel(q_ref, k_ref, v_ref, tab_ref, o_ref, *, n_ctx):
    n = q_ref.shape[1]
    rows = (n - n_ctx) // GRID_W
    wlen = NA_KH * GRID_W
    scale = NA_HD ** -0.5 * LOG2E
    kx = k_ref[0, 0:n_ctx, :]
    vx = v_ref[0, 0:n_ctx, :]

    def head_mask(shape, hh):
        lane = lax.broadcasted_iota(jnp.int32, shape, 1)
        return (lane < NA_HD) if hh == 0 else (lane >= NA_HD)

    def stack_heads(qf):
        return jnp.concatenate([jnp.where(head_mask(qf.shape, 0), qf, 0.0),
                                jnp.where(head_mask(qf.shape, 1), qf, 0.0)], axis=0).astype(BF16)

    def unstack_heads(o):
        m = o.shape[0] // 2
        return jnp.where(head_mask((m, LANES), 0), o[:m], o[m:])

    qx = stack_heads(q_ref[0, 0:n_ctx, :].astype(F32) * scale)
    o_ref[0, 0:n_ctx, :] = unstack_heads(_softmax_pv([(_dot_nt(qx, kx), vx)])).astype(o_ref.dtype)

    def row_step(r, carry):
        start = jnp.clip(r - NA_KH // 2, 0, rows - NA_KH)
        delta = start - r + (NA_KH - 1)
        q0 = pl.multiple_of(n_ctx + r * GRID_W, GRID_W)
        k0 = pl.multiple_of(n_ctx + start * GRID_W, GRID_W)
        qr = stack_heads(q_ref[0, pl.ds(q0, GRID_W), :].astype(F32) * scale)
        kl = k_ref[0, pl.ds(k0, wlen), :]
        vl = v_ref[0, pl.ds(k0, wlen), :]
        s_loc = _dot_nt(qr, kl) + tab_ref[0, delta]
        s_ctx = _dot_nt(qr, kx)
        o = unstack_heads(_softmax_pv([(s_loc, vl), (s_ctx, vx)]))
        o_ref[0, pl.ds(q0, GRID_W), :] = o.astype(o_ref.dtype)
        return carry

    lax.fori_loop(0, rows, row_step, 0, unroll=4)


def _neighbourhood(z, table, n_ctx):
    b, n, _ = z.shape
    npair = NA_HEADS // 2
    blk = lambda off: pl.BlockSpec((1, n, LANES), lambda bb, p: (bb, 0, off + p))
    return pl.pallas_call(
        functools.partial(_na_kernel, n_ctx=n_ctx),
        out_shape=jax.ShapeDtypeStruct((b, n, BRANCH_W), BF16),
        grid=(b, npair),
        in_specs=[blk(NA_B), blk(NA_B + 4), blk(NA_B + 8),
                  pl.BlockSpec((1, NA_KH, 2 * GRID_W, NA_KH * GRID_W), lambda bb, p: (p, 0, 0, 0))],
        out_specs=pl.BlockSpec((1, n, LANES), lambda bb, p: (bb, 0, p)),
        compiler_params=_cparams(("arbitrary", "arbitrary")),
        name="neighbourhood_attn",
    )(z, z, z, table)


def _gqa_kernel(q_ref, k_ref, v_ref, cos_ref, sin_ref, qg_ref, kg_ref, bd_ref, o_ref, ks_ref, vs_ref, *, n_ctx):
    i = pl.program_id(1)
    tq = q_ref.shape[1]
    n = k_ref.shape[1]
    grp = GQA_HEADS // GQA_KV_HEADS
    bd = bd_ref[...]

    def norm_rope(x, g, cos, sin):
        hi, lo = _split_bf16(x * x)
        ms = _dot(hi, bd) + _dot(lo, bd)
        xn = x * lax.rsqrt(ms + NORM_EPS) * g
        return _rope(xn, cos, sin, GQA_HD // 4)

    @pl.when(i == 0)
    def _():
        def prep(c, carry):
            rows = pl.ds(pl.multiple_of(c * tq, tq), tq)
            kn = norm_rope(k_ref[0, rows, :].astype(F32), kg_ref[...], cos_ref[rows, :], sin_ref[rows, :])
            vv = v_ref[0, rows, :].astype(F32)
            lo_half = lax.broadcasted_iota(jnp.int32, kn.shape, 1) < GQA_HD
            ksw = pltpu.roll(kn, GQA_HD, 1)
            vsw = pltpu.roll(vv, GQA_HD, 1)
            ks_ref[0, rows, :] = jnp.where(lo_half, kn, ksw).astype(BF16)
            ks_ref[1, rows, :] = jnp.where(lo_half, ksw, kn).astype(BF16)
            vs_ref[0, rows, 0:LANES] = jnp.where(lo_half, vv, vsw).astype(BF16)
            vs_ref[1, rows, 0:LANES] = jnp.where(lo_half, vsw, vv).astype(BF16)
            ones = jnp.ones((tq, LANES), BF16)
            vs_ref[0, rows, LANES:] = ones
            vs_ref[1, rows, LANES:] = ones
            return carry

        lax.fori_loop(0, n // tq, prep, 0)

    qrows = pl.ds(pl.multiple_of(i * tq, tq), tq)
    cos = cos_ref[qrows, :]
    sin = sin_ref[qrows, :]
    qg = qg_ref[...]

    def attend(nk):
        for j in range(BRANCH_W // LANES):
            x = q_ref[0, :, j * LANES:(j + 1) * LANES].astype(F32)
            qb = (norm_rope(x, qg, cos, sin) * (GQA_HD ** -0.5 * LOG2E)).astype(BF16)
            lane = lax.broadcasted_iota(jnp.int32, qb.shape, 1)
            g = (2 * j) // grp
            zero = jnp.zeros_like(qb)
            qs = jnp.concatenate([jnp.where(lane < GQA_HD, qb, zero), jnp.where(lane >= GQA_HD, qb, zero)], axis=0)
            sc = _dot_nt(qs, ks_ref[g, 0:nk, :])
            p = jnp.exp2(sc - sc.max(axis=-1, keepdims=True)).astype(BF16)
            ov = _dot(p, vs_ref[g, 0:nk, :])
            o = ov[:, :LANES] / ov[:, LANES:]
            o_ref[0, :, j * LANES:(j + 1) * LANES] = jnp.where(lane < GQA_HD, o[:tq], o[tq:]).astype(o_ref.dtype)

    @pl.when(i * tq < n_ctx)
    def _():
        attend(n_ctx)

    @pl.when(i * tq >= n_ctx)
    def _():
        attend(n)


def _gqa(z, cos, sin, qn_g, kn_g, n_ctx, tq):
    b, n, _ = z.shape
    bd = np.kron(np.eye(LANES // GQA_HD), np.ones((GQA_HD, GQA_HD))) / GQA_HD
    qg = jnp.tile(qn_g.reshape(1, -1), (1, LANES // GQA_HD))
    kg = jnp.tile(kn_g.reshape(1, -1), (1, LANES // GQA_HD))
    qb = (GQ_B * LANES) // BRANCH_W
    const = lambda shape: pl.BlockSpec(shape, lambda bb, i: (0,) * len(shape))
    return pl.pallas_call(
        functools.partial(_gqa_kernel, n_ctx=n_ctx),
        out_shape=jax.ShapeDtypeStruct((b, n, BRANCH_W), BF16),
        grid=(b, n // tq),
        in_specs=[pl.BlockSpec((1, tq, BRANCH_W), lambda bb, i: (bb, i, qb)),
                  pl.BlockSpec((1, n, LANES), lambda bb, i: (bb, 0, GQ_B + 4)),
                  pl.BlockSpec((1, n, LANES), lambda bb, i: (bb, 0, GQ_B + 5)),
                  const((n, LANES)), const((n, LANES)), const((1, LANES)), const((1, LANES)),
                  const((LANES, LANES))],
        out_specs=pl.BlockSpec((1, tq, BRANCH_W), lambda bb, i: (bb, i, 0)),
        scratch_shapes=[pltpu.VMEM((GQA_KV_HEADS, n, LANES), BF16),
                        pltpu.VMEM((GQA_KV_HEADS, n, 2 * LANES), BF16)],
        compiler_params=_cparams(("arbitrary", "arbitrary")),
        name="gqa_attn",
    )(z, z, z, cos, sin, qg, kg, jnp.asarray(bd, BF16))


def _layernorm_rows(v, g, b):
    mu = jnp.mean(v, axis=-1, keepdims=True)
    vc = v - mu
    var = jnp.mean(vc * vc, axis=-1, keepdims=True)
    return vc * lax.rsqrt(var + LN_EPS) * g + b


def _pack_bf16_pairs(v):
    w = v.shape[1] // 2
    bits = lax.bitcast_convert_type(v.astype(BF16).astype(F32), jnp.int32)
    return bits[:, w:] | lax.shift_right_logical(bits[:, :w], 16)


def _unpack_bf16_pairs(pieces):
    lo = [lax.bitcast_convert_type(lax.shift_left(p, 16), F32) for p in pieces]
    hi = [lax.bitcast_convert_type(p & jnp.int32(-65536), F32) for p in pieces]
    return jnp.concatenate(lo + hi, axis=1)


def _store_pieces(ref, lead, packed):
    for h in range(SC_PIECES):
        ref[(h,) + lead] = packed[:, h * SC_ROW_WORDS:(h + 1) * SC_ROW_WORDS]


def _merge_kernel(z_ref, r_ref, h_ref, a_ref, q_ref, x_ref, mod_ref, wb_ref, wo_ref, lng_ref, lnb_ref,
                  wrh_ref, wrl_ref, br_ref, tri_ref, xo_ref, h2_ref, te_ref, tg_ref, rk_ref, cnt_ref,
                  run_ref, *, n_ctx, tm):
    i = pl.program_id(1)

    @pl.when((pl.program_id(0) == 0) & (i == 0))
    def _():
        run_ref[...] = jnp.zeros_like(run_ref)

    is_ctx = i * tm < n_ctx
    mod = jnp.where(is_ctx, mod_ref[0, 0], mod_ref[0, 1])
    acc = None
    for nb, br in enumerate((r_ref, h_ref, a_ref, q_ref)):
        yb = _dot(br[0], wb_ref[nb])
        gt = _sigmoid(z_ref[0, :, nb * D_MODEL:(nb + 1) * D_MODEL].astype(F32))
        acc = gt * yb if acc is None else acc + gt * yb
    y = _dot(acc.astype(BF16), wo_ref[...])
    xn = _layernorm_rows(DN_ALPHA * x_ref[0] + mod[2:3, :] * y, lng_ref[...], lnb_ref[...])
    xo_ref[0] = xn
    h2 = xn * mod[3:4, :] + mod[4:5, :]
    h2_hi, h2_lo = _split_bf16(h2)
    _store_pieces(h2_ref, (0,), _pack_bf16_pairs(h2))
    wrh = wrh_ref[...]
    logit = _dot_nt(wrh, h2_hi) + _dot_nt(wrh, h2_lo) + _dot_nt(wrl_ref[...], h2_hi) + br_ref[...]
    eidx = lax.broadcasted_iota(jnp.int32, logit.shape, 0)
    vals, idxs = [], []
    for _ in range(TOP_K):
        m = logit.max(axis=0, keepdims=True)
        sel = jnp.where(logit == m, eidx, N_EXPERTS).min(axis=0, keepdims=True)
        vals.append(m)
        idxs.append(sel)
        logit = jnp.where(eidx == sel, -jnp.inf, logit)
    ex = [jnp.exp(v - vals[0]) for v in vals]
    tot = ex[0] + ex[1] + ex[2] + ex[3]
    te_ref[0] = jnp.concatenate(idxs, axis=0)
    tg_ref[0] = jnp.concatenate([e / tot for e in ex], axis=0)
    base = run_ref[:, 0:1]
    tri = tri_ref[...]
    ranks = []
    for sel in idxs:
        hit = eidx == sel
        ones = jnp.where(hit, 1.0, 0.0)
        pre = _dot(ones.astype(BF16), tri)
        ranks.append(jnp.sum(jnp.where(hit, pre + base, 0.0), axis=0, keepdims=True))
        base = base + jnp.sum(ones, axis=1, keepdims=True)
    rk_ref[0] = jnp.concatenate(ranks, axis=0).astype(jnp.int32)
    run_ref[...] = jnp.broadcast_to(base, run_ref.shape)
    cnt_ref[...] = run_ref[...]


def _merge(z, outs, xs, mod, wb, wo, ln_g, ln_b, wr_hi, wr_lo, b_r, n_ctx, tm):
    b, n, d = xs.shape
    row = lambda w: pl.BlockSpec((1, tm, w), lambda bb, i: (bb, i, 0))
    full = lambda a: pl.BlockSpec(a.shape, lambda bb, i: (0,) * a.ndim)
    topk = pl.BlockSpec((1, TOP_K, tm), lambda bb, i: (bb, 0, i))
    ln_g = ln_g.reshape(1, d)
    ln_b = ln_b.reshape(1, d)
    b_r = b_r.reshape(N_EXPERTS, 1)
    tri = jnp.asarray(np.triu(np.ones((tm, tm), np.float32), 1), BF16)
    return pl.pallas_call(
        functools.partial(_merge_kernel, n_ctx=n_ctx, tm=tm),
        out_shape=(jax.ShapeDtypeStruct((b, n, d), F32),
                   jax.ShapeDtypeStruct((SC_PIECES, b, n, SC_ROW_WORDS), jnp.int32),
                   jax.ShapeDtypeStruct((b, TOP_K, n), jnp.int32),
                   jax.ShapeDtypeStruct((b, TOP_K, n), F32),
                   jax.ShapeDtypeStruct((b, TOP_K, n), jnp.int32),
                   jax.ShapeDtypeStruct((N_EXPERTS, LANES), F32)),
        grid=(b, n // tm),
        in_specs=[row(N_BRANCH * D_MODEL), row(BRANCH_W), row(BRANCH_W), row(BRANCH_W), row(BRANCH_W),
                  row(d),
                  pl.BlockSpec((1, 2, 8, d), lambda bb, i: (bb, 0, 0, 0)),
                  full(wb), full(wo), full(ln_g), full(ln_b), full(wr_hi), full(wr_lo), full(b_r), full(tri)],
        out_specs=(row(d), pl.BlockSpec((SC_PIECES, 1, tm, SC_ROW_WORDS), lambda bb, i: (0, bb, i, 0)),
                   topk, topk, topk,
                   pl.BlockSpec((N_EXPERTS, LANES), lambda bb, i: (0, 0))),
        scratch_shapes=[pltpu.VMEM((N_EXPERTS, LANES), F32)],
        compiler_params=_cparams(("arbitrary", "arbitrary")),
        name="merge_ln_router",
    )(z, *outs, xs, mod, wb, wo, ln_g, ln_b, wr_hi, wr_lo, b_r, tri)


def _ffn_kernel(be_ref, nv_ref, nw_ref, x_ref, wgu_ref, bgu_ref, wd_ref, bd_ref, o_ref, wgu_s, wd_s):
    i = pl.program_id(0)
    nv = nv_ref[i]

    @pl.when(nw_ref[i] > 0)
    def _():
        wgu_s[...] = wgu_ref[0, 0].astype(BF16)
        wd_s[...] = wd_ref[0, 0].astype(BF16)

    @pl.when(nv > 0)
    def _():
        rows = lax.broadcasted_iota(jnp.int32, (MOE_BLOCK, 1), 0)
        x = _unpack_bf16_pairs([x_ref[h] for h in range(SC_PIECES)])
        x = jnp.where(rows < nv, x, 0.0).astype(BF16)
        gu = _dot(x, wgu_s[...]) + bgu_ref[0, 0]
        g = jnp.minimum(gu[:, :D_FF_EXPERT], SWIGLU_LIMIT)
        u = jnp.clip(gu[:, D_FF_EXPERT:], -SWIGLU_LIMIT, SWIGLU_LIMIT)
        act = g * _sigmoid(SWIGLU_ALPHA * g) * (u + 1.0)
        _store_pieces(o_ref, (), _pack_bf16_pairs(_dot(act.astype(BF16), wd_s[...]) + bd_ref[0, 0]))

    @pl.when(nv <= 0)
    def _():
        o_ref[...] = jnp.zeros_like(o_ref)


def _expert_ffn(xb, block_e, n_valid, layer, wgu, bgu, wd, bd):
    _, n_slots, _ = xb.shape
    n_blocks = n_slots // MOE_BLOCK
    depth, ne, d, f2 = wgu.shape
    new_w = jnp.concatenate([jnp.ones((1,), jnp.int32), (block_e[1:] != block_e[:-1]).astype(jnp.int32)])
    slot_blk = pl.BlockSpec((SC_PIECES, MOE_BLOCK, SC_ROW_WORDS), lambda i, be, nv, nw: (0, i, 0))
    wblk = lambda r, c: pl.BlockSpec((1, 1, r, c), lambda i, be, nv, nw: (layer, be[i], 0, 0))
    grid_spec = pltpu.PrefetchScalarGridSpec(
        num_scalar_prefetch=3,
        grid=(n_blocks,),
        in_specs=[slot_blk, wblk(d, f2), wblk(1, f2), wblk(f2 // 2, d), wblk(1, d)],
        out_specs=slot_blk,
        scratch_shapes=[pltpu.VMEM((d, f2), BF16), pltpu.VMEM((f2 // 2, d), BF16)],
    )
    return pl.pallas_call(
        _ffn_kernel,
        out_shape=jax.ShapeDtypeStruct(xb.shape, jnp.int32),
        grid_spec=grid_spec,
        compiler_params=_cparams(("arbitrary",)),
        name="expert_ffn",
    )(block_e, n_valid, new_w, xb, wgu, bgu.reshape(depth, ne, 1, f2), wd, bd.reshape(depth, ne, 1, d))


def _combine_kernel(y_ref, g_ref, x_ref, mod_ref, lng_ref, lnb_ref, xo_ref, *, n_ctx, tm):
    i = pl.program_id(1)
    rows = i * tm + lax.broadcasted_iota(jnp.int32, (tm, 1), 0)
    g2 = jnp.where(rows < n_ctx, mod_ref[0, 0, 5:6, :], mod_ref[0, 1, 5:6, :])
    gate = g_ref[...]
    y = None
    for k in range(TOP_K):
        t = gate[:, k:k + 1] * _unpack_bf16_pairs([y_ref[k, h] for h in range(SC_PIECES)])
        y = t if y is None else y + t
    xo_ref[0] = _layernorm_rows(DN_ALPHA * x_ref[0] + g2 * y, lng_ref[...], lnb_ref[...])


def _combine(yk, gate, xs, mod, ln_g, ln_b, n_ctx, tm):
    b, n, d = xs.shape
    nt = n // tm
    return pl.pallas_call(
        functools.partial(_combine_kernel, n_ctx=n_ctx, tm=tm),
        out_shape=jax.ShapeDtypeStruct((b, n, d), F32),
        grid=(b, nt),
        in_specs=[pl.BlockSpec((TOP_K, SC_PIECES, tm, SC_ROW_WORDS), lambda bb, i: (0, 0, bb * nt + i, 0)),
                  pl.BlockSpec((tm, TOP_K), lambda bb, i: (bb * nt + i, 0)),
                  pl.BlockSpec((1, tm, d), lambda bb, i: (bb, i, 0)),
                  pl.BlockSpec((1, 2, 8, d), lambda bb, i: (bb, 0, 0, 0)),
                  pl.BlockSpec((1, d), lambda bb, i: (0, 0)),
                  pl.BlockSpec((1, d), lambda bb, i: (0, 0))],
        out_specs=pl.BlockSpec((1, tm, d), lambda bb, i: (bb, i, 0)),
        compiler_params=_cparams(("arbitrary", "arbitrary")),
        name="moe_combine_ln",
    )(yk, gate, xs, mod, ln_g.reshape(1, d), ln_b.reshape(1, d))


def _sc_mesh():
    return plsc.VectorSubcoreMesh(core_axis_name="core", subcore_axis_name="subcore",
                                  num_cores=SC_CORES, num_subcores=SC_SUBCORES)


def _sc_piece_index(idx, rows):
    return jnp.concatenate([idx + h * rows for h in range(SC_PIECES)], axis=1)


def _sc_scatter_rows(x, dest, n_out):
    p, n, w = x.shape
    x2 = x.reshape(p * n, w)
    dest2 = _sc_piece_index(dest, n_out)
    kk, n2 = dest2.shape

    @functools.partial(pl.kernel, out_type=jax.ShapeDtypeStruct((p * n_out, w), x.dtype),
                       mesh=_sc_mesh(), scratch_types=[], name="moe_dispatch_scatter")
    def scatter(x_hbm, i_hbm, o_hbm):
        def body(x_vmem, i_vmem):
            for j in range(kk):
                pltpu.sync_copy(x_vmem, o_hbm.at[i_vmem.at[j]])

        pltpu.emit_pipeline(
            body, grid=(n2 // SC_WINDOW,),
            in_specs=[pl.BlockSpec((SC_WINDOW, SC_ROW_WORDS), lambda i: (i, 0)),
                      pl.BlockSpec((kk, SC_WINDOW), lambda i: (0, i))],
            out_specs=[], core_axis_name=("core", "subcore"),
            dimension_semantics=(pltpu.PARALLEL,))(x_hbm, i_hbm)

    return scatter(x2, dest2).reshape(p, n_out, w)


def _sc_gather_rows(table, idx):
    p, v, w = table.shape
    kk, n = idx.shape
    t2 = table.reshape(p * v, w)
    idx2 = _sc_piece_index(idx, v)
    m = kk * n * p

    @functools.partial(pl.kernel, out_type=jax.ShapeDtypeStruct((m, SC_ROW_WORDS), table.dtype),
                       mesh=_sc_mesh(), scratch_types=[], name="moe_combine_gather")
    def gather(t_hbm, i_hbm, o_hbm):
        def body(i_vmem, o_vmem):
            pltpu.sync_copy(t_hbm.at[i_vmem.at[0]], o_vmem)

        pltpu.emit_pipeline(
            body, grid=(m // SC_WINDOW,),
            in_specs=[pl.BlockSpec((1, SC_WINDOW), lambda i: (0, i))],
            out_specs=[pl.BlockSpec((SC_WINDOW, SC_ROW_WORDS), lambda i: (i, 0))],
            core_axis_name=("core", "subcore"),
            dimension_semantics=(pltpu.PARALLEL,))(i_hbm, o_hbm)

    return gather(t2, idx2.reshape(1, m)).reshape(kk, p, n, w)


def _moe(h2p, top_e, top_g, rank, cnt, layer, wgu, bgu, wd, bd):
    p, b, n, w = h2p.shape
    n_tok = b * n
    nk = n_tok * TOP_K
    experts = jnp.arange(N_EXPERTS, dtype=jnp.int32)
    counts = cnt[:, 0].astype(jnp.int32)
    padded = (counts + MOE_BLOCK - 1) // MOE_BLOCK * MOE_BLOCK
    ends_p = jnp.cumsum(padded)
    start_p = ends_p - padded
    n_blocks = (nk + N_EXPERTS * (MOE_BLOCK - 1) + MOE_BLOCK - 1) // MOE_BLOCK
    block_start = jnp.arange(n_blocks, dtype=jnp.int32) * MOE_BLOCK
    block_e = jnp.minimum(jnp.sum(ends_p[None, :] <= block_start[:, None], axis=1), N_EXPERTS - 1).astype(jnp.int32)
    is_e = block_e[:, None] == experts[None, :]
    filled = jnp.sum(jnp.where(is_e, (start_p + counts)[None, :], 0), axis=1)
    n_valid = jnp.clip(filled - block_start, 0, MOE_BLOCK).astype(jnp.int32)
    dest = rank + jnp.sum(jnp.where(top_e[..., None] == experts, start_p, 0), axis=-1)
    dest = dest.transpose(1, 0, 2).reshape(TOP_K, n_tok)
    gate = top_g.transpose(0, 2, 1).reshape(n_tok, TOP_K)
    xb = _sc_scatter_rows(h2p.reshape(p, n_tok, w), dest, n_blocks * MOE_BLOCK)
    yb = _expert_ffn(xb, block_e, n_valid, layer, wgu, bgu, wd, bd)
    yk = _sc_gather_rows(yb, dest)
    return yk, gate


def kernel(x, c, ctx, c_ctx, w_mod, b_mod, w_in, ret_decay, ret_gn_g, ret_gn_b, hg_lb, hg_norm_g,
           na_rpb, gq_qn_g, gq_kn_g, w_branch, w_out, ln_g, ln_b, w_router, b_router, w_gu, b_gu,
           w_down, b_down):
    b_, t_, d = x.shape
    n_ctx = ctx.shape[1]
    depth = w_mod.shape[0]
    tm = 256

    sm = jax.nn.softmax(hg_lb.astype(F32), axis=1)
    lower = jnp.cumsum(sm, axis=1) - sm[:, :1]
    log_gamma = jax.nn.log_sigmoid(ret_decay.astype(F32)).reshape(depth, 2 * RET_HEADS)

    cc = jnp.concatenate([c, c_ctx[None, :], jnp.zeros((16 - b_ - 1, d), F32)], axis=0)
    modv = _mod_all(cc, w_mod, b_mod).reshape(depth, 16, 6, d)
    one = jnp.asarray([0.0, 1.0, 0.0, 0.0, 1.0, 0.0], F32)[None, None, :, None]
    modv = (modv + one)[:, :, jnp.asarray([1, 0, 2, 4, 3, 5])]
    modv = jnp.concatenate([modv, jnp.zeros((depth, 16, 2, d), F32)], axis=2)
    mod = jnp.stack([jnp.broadcast_to(modv[:, b_:b_ + 1], (depth, b_, 8, d)), modv[:, :b_]], axis=2)

    cos_r, sin_r = _rope_tables(n_ctx, t_, RET_DK)
    cos_g, sin_g = _rope_tables(n_ctx, t_, GQA_HD)

    xs = jnp.concatenate([ctx, x], axis=1)
    tm_in = 768 if xs.shape[1] % 768 == 0 else tm
    w_in_b = _w_in_prep(w_in)
    for l in range(depth):
        z = _in_proj(xs, mod[l], w_in_b[l], n_ctx, tm_in, PROJ_TOTAL // PROJ_TILES)
        ret_o = _retention(z, log_gamma[l], cos_r, sin_r, ret_gn_g[l], ret_gn_b[l], n_ctx)
        hg_o = _hgrn2(z, lower[:, l], hg_norm_g[l], n_ctx)
        na_o = _neighbourhood(z, _na_bias_table(na_rpb[l]), n_ctx)
        gq_o = _gqa(z, cos_g, sin_g, gq_qn_g[l], gq_kn_g[l], n_ctx, tm)
        wr_hi, wr_lo = _split_bf16(w_router[l].T)
        xs, h2p, top_e, top_g, rank, cnt = _merge(z, (ret_o, hg_o, na_o, gq_o), xs, mod[l],
                                                  w_branch[l].astype(BF16), w_out[l].astype(BF16),
                                                  ln_g[l, 0], ln_b[l, 0], wr_hi, wr_lo, b_router[l], n_ctx, tm)
        yk, gate = _moe(h2p, top_e, top_g, rank, cnt, l, w_gu, b_gu, w_down, b_down)
        xs = _combine(yk, gate, xs, mod[l], ln_g[l, 1], ln_b[l, 1], n_ctx, tm_in)
    return xs[:, n_ctx:]
```

```python
import functools

import jax
import jax.numpy as jnp
from jax import lax
import numpy as np
from jax.experimental import pallas as pl
from jax.experimental.pallas import tpu as pltpu
from jax.experimental.pallas import tpu_sc as plsc

D_MODEL = 1024
DEPTH = 4
GRID_W = 64
N_BRANCH = 4
BRANCH_W = D_MODEL // 2
RET_HEADS = 4
RET_DK = BRANCH_W // RET_HEADS
HG_HEADS = 4
HG_DK = BRANCH_W // HG_HEADS
NA_HEADS = 8
NA_HD = BRANCH_W // NA_HEADS
NA_KH = 8
NA_KW = 16
GQA_HEADS = 8
GQA_KV_HEADS = 2
GQA_HD = BRANCH_W // GQA_HEADS
GQA_KV_W = GQA_KV_HEADS * GQA_HD
ROPE_BASE = 10000.0
N_EXPERTS = 32
TOP_K = 4
D_FF_EXPERT = D_MODEL
SWIGLU_LIMIT = 7.0
SWIGLU_ALPHA = 1.702
MOE_BLOCK = 256
LN_EPS = 1e-5
NORM_EPS = 1e-6
NEG_INF = -1e30
DN_ALPHA = (2 * DEPTH) ** 0.25

LANES = 128
BF16 = jnp.bfloat16
F32 = jnp.float32
VMEM_LIMIT = 56 * 1024 * 1024
SC_CORES = 2
SC_SUBCORES = 16
SC_WINDOW = 128
SC_ROW_WORDS = 256
SC_PIECES = D_MODEL // 2 // SC_ROW_WORDS

PROJ_ORIG = 4 * BRANCH_W + 5 * BRANCH_W + 3 * BRANCH_W + BRANCH_W + 2 * GQA_KV_W
PROJ_TOTAL = PROJ_ORIG + N_BRANCH * D_MODEL
PROJ_TILES = 2
RET_B = (N_BRANCH * D_MODEL) // LANES
HG_B = RET_B + 16
NA_B = HG_B + 20
GQ_B = NA_B + 12

RET_CHUNK = 256
HG_CHUNK = 128
LOG2E = 1.4426950408889634
HG_LEVELS = (64, 32, 16, 8, 4, 2, 1, 0)


def _cparams(sem):
    return pltpu.CompilerParams(dimension_semantics=sem, vmem_limit_bytes=VMEM_LIMIT)


def _dot(a, b):
    return jnp.dot(a, b, preferred_element_type=F32)


def _dot_nt(a, b):
    return lax.dot_general(a, b, (((1,), (1,)), ((), ())), preferred_element_type=F32)


def _dot_tn(a, b):
    return lax.dot_general(a, b, (((0,), (0,)), ((), ())), preferred_element_type=F32)


def _split_bf16(x):
    hi = x.astype(BF16)
    lo = (x - hi.astype(F32)).astype(BF16)
    return hi, lo


def _sigmoid(x):
    return 1.0 / (1.0 + jnp.exp(-x))


def _silu(x):
    return x * _sigmoid(x)


def _mod_kernel(c_ref, w_ref, b_ref, o_ref):
    s = _silu(c_ref[...]).astype(BF16)
    o_ref[0] = _dot(s, w_ref[0].astype(BF16)) + b_ref[0]


def _mod_all(cc, w_mod, b_mod):
    depth, d, n = w_mod.shape
    r = cc.shape[0]
    tn = 1536
    return pl.pallas_call(
        _mod_kernel,
        out_shape=jax.ShapeDtypeStruct((depth, r, n), F32),
        grid=(depth, n // tn),
        in_specs=[pl.BlockSpec((r, d), lambda l, j: (0, 0)),
                  pl.BlockSpec((1, d, tn), lambda l, j: (l, 0, j)),
                  pl.BlockSpec((1, 1, tn), lambda l, j: (l, 0, j))],
        out_specs=pl.BlockSpec((1, r, tn), lambda l, j: (l, 0, j)),
        compiler_params=_cparams(("arbitrary", "arbitrary")),
        name="adaln_mod",
    )(cc, w_mod, b_mod.reshape(depth, 1, n))


W_PREP_TILE = 256


def _w_in_prep_kernel(w_ref, o_ref):
    o_ref[0] = w_ref[0].astype(BF16)


def _w_in_prep(w_in):
    depth, d, total = w_in.shape
    n_src = total // W_PREP_TILE
    shift = PROJ_ORIG // W_PREP_TILE
    return pl.pallas_call(
        _w_in_prep_kernel,
        out_shape=jax.ShapeDtypeStruct((depth, d, total), BF16),
        grid=(depth, n_src),
        in_specs=[pl.BlockSpec((1, d, W_PREP_TILE), lambda l, j: (l, 0, (j + shift) % n_src))],
        out_specs=pl.BlockSpec((1, d, W_PREP_TILE), lambda l, j: (l, 0, j)),
        compiler_params=_cparams(("arbitrary", "arbitrary")),
        name="w_in_prep",
    )(w_in)


def _in_proj_kernel(x_ref, mod_ref, w_ref, o_ref, *, n_ctx, tm):
    i = pl.program_id(2)
    x = x_ref[0]
    mc = mod_ref[0, 0]
    ml = mod_ref[0, 1]
    rows = i * tm + lax.broadcasted_iota(jnp.int32, (tm, 1), 0)
    is_ctx = rows < n_ctx
    scale = jnp.where(is_ctx, mc[0:1, :], ml[0:1, :])
    shift = jnp.where(is_ctx, mc[1:2, :], ml[1:2, :])
    h = (x * scale + shift).astype(BF16)
    o_ref[0] = _dot(h, w_ref[...]).astype(o_ref.dtype)


def _in_proj(xs, mod, w, n_ctx, tm, tn):
    b, n, d = xs.shape
    ncol = w.shape[1]
    return pl.pallas_call(
        functools.partial(_in_proj_kernel, n_ctx=n_ctx, tm=tm),
        out_shape=jax.ShapeDtypeStruct((b, n, ncol), BF16),
        grid=(ncol // tn, b, n // tm),
        in_specs=[pl.BlockSpec((1, tm, d), lambda j, bb, i: (bb, i, 0)),
                  pl.BlockSpec((1, 2, 8, d), lambda j, bb, i: (bb, 0, 0, 0)),
                  pl.BlockSpec((d, tn), lambda j, bb, i: (0, j))],
        out_specs=pl.BlockSpec((1, tm, tn), lambda j, bb, i: (bb, i, j)),
        compiler_params=_cparams(("arbitrary", "arbitrary", "arbitrary")),
        name="in_proj",
    )(xs, mod, w)


def _rope_tables(n_ctx, t, head_dim):
    idx = jnp.arange(t, dtype=jnp.int32)
    row = (idx // GRID_W).astype(F32)
    col = (idx % GRID_W).astype(F32)
    n = head_dim // 2
    inv = ROPE_BASE ** (-jnp.arange(0, n, 2, dtype=F32) / n)

    def half(pos):
        ang = pos[:, None] * inv[None, :]
        c, s = jnp.cos(ang), jnp.sin(ang)
        return jnp.concatenate([c, c], -1), jnp.concatenate([-s, s], -1)

    cr, sr = half(row)
    cc, sc = half(col)
    cos = jnp.concatenate([cr, cc], -1)
    sin = jnp.concatenate([sr, sc], -1)
    reps = LANES // head_dim
    cos = jnp.tile(cos, (1, reps))
    sin = jnp.tile(sin, (1, reps))
    cos = jnp.concatenate([jnp.ones((n_ctx, LANES), F32), cos], 0)
    sin = jnp.concatenate([jnp.zeros((n_ctx, LANES), F32), sin], 0)
    return cos, sin


def _rope(x, cos, sin, quarter):
    lane = lax.broadcasted_iota(jnp.int32, x.shape, 1)
    first = (lane & (2 * quarter - 1)) < quarter
    rot = jnp.where(first, pltpu.roll(x, LANES - quarter, 1), pltpu.roll(x, quarter, 1))
    return x * cos + rot * sin


def _ret_kernel(lg_ref, q_ref, k_ref, v_ref, g_ref, cos_ref, sin_ref, gng_ref, gnb_ref, o_ref,
                ks_ref, sb_ref, *, n_ctx, chunk):
    hh = pl.program_id(1)
    lgf = lg_ref[hh]
    lgb = lg_ref[RET_HEADS + hh]
    n = q_ref.shape[1]
    c_ = chunk
    nch = n // c_
    n_cc = n_ctx // c_
    dk = q_ref.shape[2]

    ri = lax.broadcasted_iota(jnp.int32, (c_, dk), 0).astype(F32)
    df = jnp.exp((ri + 1.0) * lgf)
    db = jnp.exp((c_ - ri) * lgb)
    ef = jnp.exp((c_ - 1.0 - ri) * lgf)
    eb = jnp.exp(ri * lgb)
    gfc = jnp.exp(c_ * lgf)
    gbc = jnp.exp(c_ * lgb)
    ii = lax.broadcasted_iota(jnp.int32, (c_, c_), 0)
    jj = lax.broadcasted_iota(jnp.int32, (c_, c_), 1)
    dij = (ii - jj).astype(F32)
    w = jnp.where(ii >= jj, jnp.exp(dij * lgf), jnp.exp(-dij * lgb))

    def rope_k(c, carry):
        r0 = pl.multiple_of(c * c_, c_)
        kc = k_ref[0, pl.ds(r0, c_), :].astype(F32) * (RET_DK ** -0.5)
        kc = _rope(kc, cos_ref[pl.ds(r0, c_), :], sin_ref[pl.ds(r0, c_), :], 32)
        ks_ref[pl.ds(r0, c_), :] = kc.astype(BF16)
        return carry

    lax.fori_loop(0, nch, rope_k, 0, unroll=3)

    def bwd_step(c, s):
        r0 = pl.multiple_of(c * c_, c_)
        sb_ref[c] = s
        kc = (ks_ref[pl.ds(r0, c_), :].astype(F32) * eb).astype(BF16)
        return s * gbc + _dot_tn(kc, v_ref[0, pl.ds(r0, c_), :])

    s = jnp.zeros((dk, dk), F32)
    s = lax.fori_loop(0, n_cc, lambda t, s_: bwd_step(n_cc - 1 - t, s_), s)
    lax.fori_loop(0, nch - n_cc, lambda t, s_: bwd_step(nch - 1 - t, s_), s, unroll=4)

    gng = gng_ref[...]
    gnb = gnb_ref[...]

    def fwd_step(c, sf):
        r0 = pl.multiple_of(c * c_, c_)
        qc = _rope(q_ref[0, pl.ds(r0, c_), :].astype(F32), cos_ref[pl.ds(r0, c_), :],
                   sin_ref[pl.ds(r0, c_), :], 32)
        kc = ks_ref[pl.ds(r0, c_), :]
        vc = v_ref[0, pl.ds(r0, c_), :]
        a = _dot_nt(qc.astype(BF16), kc) * w
        qi = jnp.concatenate([qc * df, qc * db], axis=1).astype(BF16)
        si = jnp.concatenate([sf, sb_ref[c]], axis=0).astype(BF16)
        o = _dot(a.astype(BF16), vc) + _dot(qi, si)
        sf_new = sf * gfc + _dot_tn((kc.astype(F32) * ef).astype(BF16), vc)
        mu = jnp.mean(o, axis=-1, keepdims=True)
        oc = o - mu
        var = jnp.mean(oc * oc, axis=-1, keepdims=True)
        y = oc * lax.rsqrt(var + LN_EPS) * gng + gnb
        gate = g_ref[0, pl.ds(r0, c_), :].astype(F32)
        o_ref[0, pl.ds(r0, c_), :] = (y * _silu(gate)).astype(o_ref.dtype)
        return sf_new

    lax.fori_loop(0, nch, fwd_step, jnp.zeros((dk, dk), F32), unroll=3)


def _retention(z, log_gamma, cos, sin, gn_g, gn_b, n_ctx):
    b, n, _ = z.shape
    blk = lambda off: pl.BlockSpec((1, n, LANES), lambda bb, h, lg: (bb, 0, off + h))
    grid_spec = pltpu.PrefetchScalarGridSpec(
        num_scalar_prefetch=1,
        grid=(b, RET_HEADS),
        in_specs=[blk(RET_B), blk(RET_B + 4), blk(RET_B + 8), blk(RET_B + 12),
                  pl.BlockSpec((n, LANES), lambda bb, h, lg: (0, 0)),
                  pl.BlockSpec((n, LANES), lambda bb, h, lg: (0, 0)),
                  pl.BlockSpec((1, LANES), lambda bb, h, lg: (0, h)),
                  pl.BlockSpec((1, LANES), lambda bb, h, lg: (0, h))],
        out_specs=pl.BlockSpec((1, n, LANES), lambda bb, h, lg: (bb, 0, h)),
        scratch_shapes=[pltpu.VMEM((n, LANES), BF16),
                        pltpu.VMEM((n // RET_CHUNK, RET_DK, RET_DK), F32)],
    )
    return pl.pallas_call(
        functools.partial(_ret_kernel, n_ctx=n_ctx, chunk=RET_CHUNK),
        out_shape=jax.ShapeDtypeStruct((b, n, BRANCH_W), BF16),
        grid_spec=grid_spec,
        compiler_params=_cparams(("arbitrary", "arbitrary")),
        name="retention",
    )(log_gamma, z, z, z, z, cos, sin, gn_g.reshape(1, -1), gn_b.reshape(1, -1))


def _hg_constants():
    c = HG_CHUNK
    r = np.arange(c)[:, None]
    t = np.arange(c)[None, :]
    mats = []
    level = np.full((2, c, c), len(HG_LEVELS), np.int32)
    for d in range(2):
        blocks = []
        for li, s in enumerate(HG_LEVELS):
            if s == 0:
                level[d][np.arange(c), np.arange(c)] = li
                continue
            base = (r // (2 * s)) * (2 * s)
            if d == 0:
                m = base + s - 1
                mat = (t > np.minimum(r, m)) & (t <= np.maximum(r, m))
                q_side = (r % (2 * s)) >= s
            else:
                m = base + s
                mat = (t >= np.minimum(r, m)) & (t < np.maximum(r, m))
                q_side = (r % (2 * s)) < s
            blocks.append(mat.astype(np.float32))
            same = (r // (2 * s)) == (t // (2 * s))
            k_side_t = ((t % (2 * s)) < s) if d == 0 else ((t % (2 * s)) >= s)
            level[d][same & q_side & k_side_t] = li
        if d == 0:
            blocks.append((t <= r).astype(np.float32))
            blocks.append((t > r).astype(np.float32))
        else:
            blocks.append((t >= r).astype(np.float32))
            blocks.append((t < r).astype(np.float32))
        mats.append(np.concatenate(blocks, 0))
    mats = np.stack(mats)
    masks = (level[:, None] == np.arange(len(HG_LEVELS))[None, :, None, None]).astype(np.float32)
    upper = np.stack([np.broadcast_to(((np.arange(c) % (2 * s)) >= s)[:, None], (c, c)) for s in HG_LEVELS if s > 0])
    return np.concatenate([mats, mats], axis=2), np.maximum(masks[0], masks[1]), upper.astype(np.float32)


def _hg_gates(zf, lb):
    en = jnp.exp(-jnp.abs(zf))
    inv = 1.0 / (1.0 + en)
    pos = zf >= 0
    sg = jnp.where(pos, inv, en * inv)
    sgn = jnp.where(pos, en * inv, inv)
    one_m = 1.0 - lb
    return jnp.log(lb + one_m * sg), one_m * sgn


def _hg_kernel(q_ref, ff_ref, fb_ref, v_ref, g_ref, lb_ref, ng_ref, m_ref, lv_ref, up_ref, o_ref,
               lfb_ref, kb_ref, sb_ref, ktb_ref, dtot_ref, *, n_ctx):
    c_ = HG_CHUNK
    n = q_ref.shape[1]
    nch = n // c_
    n_cc = n_ctx // c_
    dk = q_ref.shape[2]
    nl = len(HG_LEVELS)
    lbf = lb_ref[0:1, :]
    lbb = lb_ref[1:2, :]

    def mdot(mat, lf):
        hi, lo = _split_bf16(lf)
        return _dot(mat, jnp.concatenate([hi, lo], axis=0))

    def prep_b(c, carry):
        r0 = pl.multiple_of(c * c_, c_)
        lf, key = _hg_gates(fb_ref[0, pl.ds(r0, c_), :].astype(F32), lbb)
        lf = lf * LOG2E
        lfb_ref[pl.ds(r0, c_), :] = lf
        kb_ref[pl.ds(r0, c_), :] = key
        e = mdot(m_ref[1, (nl - 1) * c_:(nl + 1) * c_, :], lf)
        ktb_ref[pl.ds(r0, c_), :] = (key * jnp.exp2(e[c_:2 * c_, :])).astype(BF16)
        dtot_ref[c] = jnp.broadcast_to(jnp.exp2(e[0:1, :]), (8, dk))
        return carry

    lax.fori_loop(0, nch, prep_b, 0, unroll=3)

    def bwd_step(c, st):
        r0 = pl.multiple_of(c * c_, c_)
        sb_ref[c] = st
        return st * dtot_ref[c][0:1, :] + _dot_tn(v_ref[0, pl.ds(r0, c_), :], ktb_ref[pl.ds(r0, c_), :])

    st = jnp.zeros((dk, dk), F32)
    st = lax.fori_loop(0, n_cc, lambda t, s_: bwd_step(n_cc - 1 - t, s_), st, unroll=2)
    lax.fori_loop(0, nch - n_cc, lambda t, s_: bwd_step(nch - 1 - t, s_), st, unroll=8)

    ng = ng_ref[...]

    def mdot_pair(mat, lf0, lf1):
        hi0, lo0 = _split_bf16(lf0)
        hi1, lo1 = _split_bf16(lf1)
        rhs = jnp.concatenate([jnp.concatenate([hi0, hi1], axis=1), jnp.concatenate([lo0, lo1], axis=1)], axis=0)
        return _dot(mat, rhs)

    def intra(qs, keyf, keyb, ef, eb):
        a = jnp.zeros((c_, c_), F32)
        qsb, kfb, kbb = qs.astype(BF16), keyf.astype(BF16), keyb.astype(BF16)
        for li, s in enumerate(HG_LEVELS):
            if s == 0:
                qt = jnp.concatenate([qsb, qsb], axis=1)
                kt = jnp.concatenate([kfb, kbb], axis=1)
            else:
                decf = jnp.exp2(ef[li * c_:(li + 1) * c_, :])
                decb = jnp.exp2(eb[li * c_:(li + 1) * c_, :])
                up = up_ref[li]
                dqf = decf * up
                dkb = decb * up
                qt = jnp.concatenate([qsb * dqf.astype(BF16), qsb * (decb - dkb).astype(BF16)], axis=1)
                kt = jnp.concatenate([kfb * (decf - dqf).astype(BF16), kbb * dkb.astype(BF16)], axis=1)
            a = a + _dot_nt(qt, kt) * lv_ref[li]
        cumf = ef[(nl - 1) * c_:nl * c_, :]
        restf = ef[nl * c_:(nl + 1) * c_, :]
        cumb = eb[(nl - 1) * c_:nl * c_, :]
        return a, cumf, restf, cumb

    def fwd_chunk(c, sf, qs, keyf, ef, eb):
        r0 = pl.multiple_of(c * c_, c_)
        vc = v_ref[0, pl.ds(r0, c_), :]
        a, cumf, restf, cumb = intra(qs, keyf, kb_ref[pl.ds(r0, c_), :], ef, eb)
        qi = jnp.concatenate([qs * jnp.exp2(cumf), qs * jnp.exp2(cumb)], axis=1).astype(BF16)
        si = jnp.concatenate([sf, sb_ref[c]], axis=1).astype(BF16)
        o = _dot(a.astype(BF16), vc) + _dot_nt(qi, si)
        totf = cumf[c_ - 1:c_, :]
        sf_new = sf * jnp.exp2(totf) + _dot_tn(vc, (keyf * jnp.exp2(restf)).astype(BF16))
        on = o * lax.rsqrt(jnp.mean(o * o, axis=-1, keepdims=True) + NORM_EPS) * ng
        gate = g_ref[0, pl.ds(r0, c_), :].astype(F32)
        o_ref[0, pl.ds(r0, c_), :] = (on * _silu(gate)).astype(o_ref.dtype)
        return sf_new

    def fwd_pair(p, sf):
        r0 = pl.multiple_of(p * (2 * c_), 2 * c_)
        qs, keyf, lff = [], [], []
        for t in range(2):
            rows = pl.ds(r0 + t * c_, c_)
            qs.append(_silu(q_ref[0, rows, :].astype(F32)))
            lf, key = _hg_gates(ff_ref[0, rows, :].astype(F32), lbf)
            lff.append(lf * LOG2E)
            keyf.append(key)
        ef = mdot_pair(m_ref[0], lff[0], lff[1])
        eb = mdot_pair(m_ref[1], lfb_ref[pl.ds(r0, c_), :], lfb_ref[pl.ds(r0 + c_, c_), :])
        for t in range(2):
            sf = fwd_chunk(2 * p + t, sf, qs[t], keyf[t], ef[:, t * dk:(t + 1) * dk], eb[:, t * dk:(t + 1) * dk])
        return sf

    lax.fori_loop(0, nch // 2, fwd_pair, jnp.zeros((dk, dk), F32))


def _hgrn2(z, lower, norm_g, n_ctx):
    b, n, _ = z.shape
    mats, level, upper = _hg_constants()
    mats = jnp.asarray(mats, BF16)
    level = jnp.asarray(level)
    upper = jnp.asarray(upper)
    blk = lambda off: pl.BlockSpec((1, n, LANES), lambda bb, h: (bb, 0, off + h))
    return pl.pallas_call(
        functools.partial(_hg_kernel, n_ctx=n_ctx),
        out_shape=jax.ShapeDtypeStruct((b, n, BRANCH_W), BF16),
        grid=(b, HG_HEADS),
        in_specs=[blk(HG_B), blk(HG_B + 4), blk(HG_B + 8), blk(HG_B + 12), blk(HG_B + 16),
                  pl.BlockSpec((2, LANES), lambda bb, h: (0, h)),
                  pl.BlockSpec((1, LANES), lambda bb, h: (0, h)),
                  pl.BlockSpec(mats.shape, lambda bb, h: (0, 0, 0)),
                  pl.BlockSpec(level.shape, lambda bb, h: (0, 0, 0)),
                  pl.BlockSpec(upper.shape, lambda bb, h: (0, 0, 0))],
        out_specs=pl.BlockSpec((1, n, LANES), lambda bb, h: (bb, 0, h)),
        scratch_shapes=[pltpu.VMEM((n, LANES), F32), pltpu.VMEM((n, LANES), F32),
                        pltpu.VMEM((n // HG_CHUNK, HG_DK, HG_DK), F32),
                        pltpu.VMEM((n, LANES), BF16), pltpu.VMEM((n // HG_CHUNK, 8, LANES), F32)],
        compiler_params=_cparams(("arbitrary", "arbitrary")),
        name="hgrn2",
    )(z, z, z, z, z, lower, norm_g.reshape(1, -1), mats, level, upper)


def _na_bias_table(rpb):
    qc = np.arange(GRID_W)[:, None]
    kc = np.arange(GRID_W)[None, :]
    col0 = np.clip(qc - NA_KW // 2, 0, GRID_W - NA_KW)
    valid = (kc >= col0) & (kc < col0 + NA_KW)
    dc = np.clip(kc - qc + (NA_KW - 1), 0, 2 * NA_KW - 2)
    pick = (dc[:, :, None] == np.arange(2 * NA_KW - 1)).astype(np.float32)
    t15 = jnp.einsum('hrc,qkc->hrqk', rpb.astype(F32), pick, precision=lax.Precision.HIGHEST)
    t = jnp.stack([t15[:, dl:dl + NA_KH] for dl in range(NA_KH)], axis=1)
    t = jnp.where(jnp.asarray(valid)[None, None, None], t * LOG2E, NEG_INF)
    t = t.transpose(0, 1, 3, 2, 4).reshape(rpb.shape[0] // 2, 2, NA_KH, GRID_W, NA_KH * GRID_W)
    return t.transpose(0, 2, 1, 3, 4).reshape(rpb.shape[0] // 2, NA_KH, 2 * GRID_W, NA_KH * GRID_W)


def _softmax_pv(parts):
    m = parts[0][0].max(axis=-1, keepdims=True)
    for s, _ in parts[1:]:
        m = jnp.maximum(m, s.max(axis=-1, keepdims=True))
    l = None
    o = None
    for s, v in parts:
        p = jnp.exp2(s - m)
        ls = p.sum(axis=-1, keepdims=True)
        os_ = _dot(p.astype(BF16), v)
        l = ls if l is None else l + ls
        o = os_ if o is None else o + os_
    return o / l


def _na_kernel(q_ref, k_ref, v_ref, tab_ref, o_ref, *, n_ctx):
    n = q_ref.shape[1]
    rows = (n - n_ctx) // GRID_W
    wlen = NA_KH * GRID_W
    scale = NA_HD ** -0.5 * LOG2E
    kx = k_ref[0, 0:n_ctx, :]
    vx = v_ref[0, 0:n_ctx, :]

    def head_mask(shape, hh):
        lane = lax.broadcasted_iota(jnp.int32, shape, 1)
        return (lane < NA_HD) if hh == 0 else (lane >= NA_HD)

    def stack_heads(qf):
        return jnp.concatenate([jnp.where(head_mask(qf.shape, 0), qf, 0.0),
                                jnp.where(head_mask(qf.shape, 1), qf, 0.0)], axis=0).astype(BF16)

    def unstack_heads(o):
        m = o.shape[0] // 2
        return jnp.where(head_mask((m, LANES), 0), o[:m], o[m:])

    qx = stack_heads(q_ref[0, 0:n_ctx, :].astype(F32) * scale)
    o_ref[0, 0:n_ctx, :] = unstack_heads(_softmax_pv([(_dot_nt(qx, kx), vx)])).astype(o_ref.dtype)

    def row_step(r, carry):
        start = jnp.clip(r - NA_KH // 2, 0, rows - NA_KH)
        delta = start - r + (NA_KH - 1)
        q0 = pl.multiple_of(n_ctx + r * GRID_W, GRID_W)
        k0 = pl.multiple_of(n_ctx + start * GRID_W, GRID_W)
        qr = stack_heads(q_ref[0, pl.ds(q0, GRID_W), :].astype(F32) * scale)
        kl = k_ref[0, pl.ds(k0, wlen), :]
        vl = v_ref[0, pl.ds(k0, wlen), :]
        s_loc = _dot_nt(qr, kl) + tab_ref[0, delta]
        s_ctx = _dot_nt(qr, kx)
        o = unstack_heads(_softmax_pv([(s_loc, vl), (s_ctx, vx)]))
        o_ref[0, pl.ds(q0, GRID_W), :] = o.astype(o_ref.dtype)
        return carry

    lax.fori_loop(0, rows, row_step, 0, unroll=4)


def _neighbourhood(z, table, n_ctx):
    b, n, _ = z.shape
    npair = NA_HEADS // 2
    blk = lambda off: pl.BlockSpec((1, n, LANES), lambda bb, p: (bb, 0, off + p))
    return pl.pallas_call(
        functools.partial(_na_kernel, n_ctx=n_ctx),
        out_shape=jax.ShapeDtypeStruct((b, n, BRANCH_W), BF16),
        grid=(b, npair),
        in_specs=[blk(NA_B), blk(NA_B + 4), blk(NA_B + 8),
                  pl.BlockSpec((1, NA_KH, 2 * GRID_W, NA_KH * GRID_W), lambda bb, p: (p, 0, 0, 0))],
        out_specs=pl.BlockSpec((1, n, LANES), lambda bb, p: (bb, 0, p)),
        compiler_params=_cparams(("arbitrary", "arbitrary")),
        name="neighbourhood_attn",
    )(z, z, z, table)


def _gqa_kernel(q_ref, k_ref, v_ref, cos_ref, sin_ref, qg_ref, kg_ref, bd_ref, o_ref, ks_ref, vs_ref, *, n_ctx):
    i = pl.program_id(1)
    tq = q_ref.shape[1]
    n = k_ref.shape[1]
    grp = GQA_HEADS // GQA_KV_HEADS
    bd = bd_ref[...]

    def norm_rope(x, g, cos, sin):
        hi, lo = _split_bf16(x * x)
        ms = _dot(hi, bd) + _dot(lo, bd)
        xn = x * lax.rsqrt(ms + NORM_EPS) * g
        return _rope(xn, cos, sin, GQA_HD // 4)

    @pl.when(i == 0)
    def _():
        def prep(c, carry):
            rows = pl.ds(pl.multiple_of(c * tq, tq), tq)
            kn = norm_rope(k_ref[0, rows, :].astype(F32), kg_ref[...], cos_ref[rows, :], sin_ref[rows, :])
            vv = v_ref[0, rows, :].astype(F32)
            lo_half = lax.broadcasted_iota(jnp.int32, kn.shape, 1) < GQA_HD
            ksw = pltpu.roll(kn, GQA_HD, 1)
            vsw = pltpu.roll(vv, GQA_HD, 1)
            ks_ref[0, rows, :] = jnp.where(lo_half, kn, ksw).astype(BF16)
            ks_ref[1, rows, :] = jnp.where(lo_half, ksw, kn).astype(BF16)
            vs_ref[0, rows, 0:LANES] = jnp.where(lo_half, vv, vsw).astype(BF16)
            vs_ref[1, rows, 0:LANES] = jnp.where(lo_half, vsw, vv).astype(BF16)
            ones = jnp.ones((tq, LANES), BF16)
            vs_ref[0, rows, LANES:] = ones
            vs_ref[1, rows, LANES:] = ones
            return carry

        lax.fori_loop(0, n // tq, prep, 0)

    qrows = pl.ds(pl.multiple_of(i * tq, tq), tq)
    cos = cos_ref[qrows, :]
    sin = sin_ref[qrows, :]
    qg = qg_ref[...]

    def attend(nk):
        for j in range(BRANCH_W // LANES):
            x = q_ref[0, :, j * LANES:(j + 1) * LANES].astype(F32)
            qb = (norm_rope(x, qg, cos, sin) * (GQA_HD ** -0.5 * LOG2E)).astype(BF16)
            lane = lax.broadcasted_iota(jnp.int32, qb.shape, 1)
            g = (2 * j) // grp
            zero = jnp.zeros_like(qb)
            qs = jnp.concatenate([jnp.where(lane < GQA_HD, qb, zero), jnp.where(lane >= GQA_HD, qb, zero)], axis=0)
            sc = _dot_nt(qs, ks_ref[g, 0:nk, :])
            p = jnp.exp2(sc - sc.max(axis=-1, keepdims=True)).astype(BF16)
            ov = _dot(p, vs_ref[g, 0:nk, :])
            o = ov[:, :LANES] / ov[:, LANES:]
            o_ref[0, :, j * LANES:(j + 1) * LANES] = jnp.where(lane < GQA_HD, o[:tq], o[tq:]).astype(o_ref.dtype)

    @pl.when(i * tq < n_ctx)
    def _():
        attend(n_ctx)

    @pl.when(i * tq >= n_ctx)
    def _():
        attend(n)


def _gqa(z, cos, sin, qn_g, kn_g, n_ctx, tq):
    b, n, _ = z.shape
    bd = np.kron(np.eye(LANES // GQA_HD), np.ones((GQA_HD, GQA_HD))) / GQA_HD
    qg = jnp.tile(qn_g.reshape(1, -1), (1, LANES // GQA_HD))
    kg = jnp.tile(kn_g.reshape(1, -1), (1, LANES // GQA_HD))
    qb = (GQ_B * LANES) // BRANCH_W
    const = lambda shape: pl.BlockSpec(shape, lambda bb, i: (0,) * len(shape))
    return pl.pallas_call(
        functools.partial(_gqa_kernel, n_ctx=n_ctx),
        out_shape=jax.ShapeDtypeStruct((b, n, BRANCH_W), BF16),
        grid=(b, n // tq),
        in_specs=[pl.BlockSpec((1, tq, BRANCH_W), lambda bb, i: (bb, i, qb)),
                  pl.BlockSpec((1, n, LANES), lambda bb, i: (bb, 0, GQ_B + 4)),
                  pl.BlockSpec((1, n, LANES), lambda bb, i: (bb, 0, GQ_B + 5)),
                  const((n, LANES)), const((n, LANES)), const((1, LANES)), const((1, LANES)),
                  const((LANES, LANES))],
        out_specs=pl.BlockSpec((1, tq, BRANCH_W), lambda bb, i: (bb, i, 0)),
        scratch_shapes=[pltpu.VMEM((GQA_KV_HEADS, n, LANES), BF16),
                        pltpu.VMEM((GQA_KV_HEADS, n, 2 * LANES), BF16)],
        compiler_params=_cparams(("arbitrary", "arbitrary")),
        name="gqa_attn",
    )(z, z, z, cos, sin, qg, kg, jnp.asarray(bd, BF16))


def _layernorm_rows(v, g, b):
    mu = jnp.mean(v, axis=-1, keepdims=True)
    vc = v - mu
    var = jnp.mean(vc * vc, axis=-1, keepdims=True)
    return vc * lax.rsqrt(var + LN_EPS) * g + b


def _pack_bf16_pairs(v):
    w = v.shape[1] // 2
    bits = lax.bitcast_convert_type(v.astype(BF16).astype(F32), jnp.int32)
    return bits[:, w:] | lax.shift_right_logical(bits[:, :w], 16)


def _unpack_bf16_pairs(pieces):
    lo = [lax.bitcast_convert_type(lax.shift_left(p, 16), F32) for p in pieces]
    hi = [lax.bitcast_convert_type(p & jnp.int32(-65536), F32) for p in pieces]
    return jnp.concatenate(lo + hi, axis=1)


def _store_pieces(ref, lead, packed):
    for h in range(SC_PIECES):
        ref[(h,) + lead] = packed[:, h * SC_ROW_WORDS:(h + 1) * SC_ROW_WORDS]


def _merge_kernel(z_ref, r_ref, h_ref, a_ref, q_ref, x_ref, mod_ref, wb_ref, wo_ref, lng_ref, lnb_ref,
                  wrh_ref, wrl_ref, br_ref, tri_ref, xo_ref, h2_ref, te_ref, tg_ref, rk_ref, cnt_ref,
                  run_ref, *, n_ctx, tm):
    i = pl.program_id(1)

    @pl.when((pl.program_id(0) == 0) & (i == 0))
    def _():
        run_ref[...] = jnp.zeros_like(run_ref)

    is_ctx = i * tm < n_ctx
    mod = jnp.where(is_ctx, mod_ref[0, 0], mod_ref[0, 1])
    acc = None
    for nb, br in enumerate((r_ref, h_ref, a_ref, q_ref)):
        yb = _dot(br[0], wb_ref[nb])
        gt = _sigmoid(z_ref[0, :, nb * D_MODEL:(nb + 1) * D_MODEL].astype(F32))
        acc = gt * yb if acc is None else acc + gt * yb
    y = _dot(acc.astype(BF16), wo_ref[...])
    xn = _layernorm_rows(DN_ALPHA * x_ref[0] + mod[2:3, :] * y, lng_ref[...], lnb_ref[...])
    xo_ref[0] = xn
    h2 = xn * mod[3:4, :] + mod[4:5, :]
    h2_hi, h2_lo = _split_bf16(h2)
    _store_pieces(h2_ref, (0,), _pack_bf16_pairs(h2))
    wrh = wrh_ref[...]
    logit = _dot_nt(wrh, h2_hi) + _dot_nt(wrh, h2_lo) + _dot_nt(wrl_ref[...], h2_hi) + br_ref[...]
    eidx = lax.broadcasted_iota(jnp.int32, logit.shape, 0)
    vals, idxs = [], []
    for _ in range(TOP_K):
        m = logit.max(axis=0, keepdims=True)
        sel = jnp.where(logit == m, eidx, N_EXPERTS).min(axis=0, keepdims=True)
        vals.append(m)
        idxs.append(sel)
        logit = jnp.where(eidx == sel, -jnp.inf, logit)
    ex = [jnp.exp(v - vals[0]) for v in vals]
    tot = ex[0] + ex[1] + ex[2] + ex[3]
    te_ref[0] = jnp.concatenate(idxs, axis=0)
    tg_ref[0] = jnp.concatenate([e / tot for e in ex], axis=0)
    base = run_ref[:, 0:1]
    tri = tri_ref[...]
    ranks = []
    for sel in idxs:
        hit = eidx == sel
        ones = jnp.where(hit, 1.0, 0.0)
        pre = _dot(ones.astype(BF16), tri)
        ranks.append(jnp.sum(jnp.where(hit, pre + base, 0.0), axis=0, keepdims=True))
        base = base + jnp.sum(ones, axis=1, keepdims=True)
    rk_ref[0] = jnp.concatenate(ranks, axis=0).astype(jnp.int32)
    run_ref[...] = jnp.broadcast_to(base, run_ref.shape)
    cnt_ref[...] = run_ref[...]


def _merge(z, outs, xs, mod, wb, wo, ln_g, ln_b, wr_hi, wr_lo, b_r, n_ctx, tm):
    b, n, d = xs.shape
    row = lambda w: pl.BlockSpec((1, tm, w), lambda bb, i: (bb, i, 0))
    full = lambda a: pl.BlockSpec(a.shape, lambda bb, i: (0,) * a.ndim)
    topk = pl.BlockSpec((1, TOP_K, tm), lambda bb, i: (bb, 0, i))
    ln_g = ln_g.reshape(1, d)
    ln_b = ln_b.reshape(1, d)
    b_r = b_r.reshape(N_EXPERTS, 1)
    tri = jnp.asarray(np.triu(np.ones((tm, tm), np.float32), 1), BF16)
    return pl.pallas_call(
        functools.partial(_merge_kernel, n_ctx=n_ctx, tm=tm),
        out_shape=(jax.ShapeDtypeStruct((b, n, d), F32),
                   jax.ShapeDtypeStruct((SC_PIECES, b, n, SC_ROW_WORDS), jnp.int32),
                   jax.ShapeDtypeStruct((b, TOP_K, n), jnp.int32),
                   jax.ShapeDtypeStruct((b, TOP_K, n), F32),
                   jax.ShapeDtypeStruct((b, TOP_K, n), jnp.int32),
                   jax.ShapeDtypeStruct((N_EXPERTS, LANES), F32)),
        grid=(b, n // tm),
        in_specs=[row(N_BRANCH * D_MODEL), row(BRANCH_W), row(BRANCH_W), row(BRANCH_W), row(BRANCH_W),
                  row(d),
                  pl.BlockSpec((1, 2, 8, d), lambda bb, i: (bb, 0, 0, 0)),
                  full(wb), full(wo), full(ln_g), full(ln_b), full(wr_hi), full(wr_lo), full(b_r), full(tri)],
        out_specs=(row(d), pl.BlockSpec((SC_PIECES, 1, tm, SC_ROW_WORDS), lambda bb, i: (0, bb, i, 0)),
                   topk, topk, topk,
                   pl.BlockSpec((N_EXPERTS, LANES), lambda bb, i: (0, 0))),
        scratch_shapes=[pltpu.VMEM((N_EXPERTS, LANES), F32)],
        compiler_params=_cparams(("arbitrary", "arbitrary")),
        name="merge_ln_router",
    )(z, *outs, xs, mod, wb, wo, ln_g, ln_b, wr_hi, wr_lo, b_r, tri)


def _ffn_kernel(be_ref, nv_ref, nw_ref, nx_ref, x_ref, wgu_hbm, bgu_ref, wd_hbm, bd_ref, o_ref,
                wgu_s, wd_s, gu_stage, d_stage, sems, *, layer):
    i = pl.program_id(0)
    nv = nv_ref[i]

    def weight_copies(e):
        return (pltpu.make_async_copy(wgu_hbm.at[layer, e], gu_stage, sems.at[0]),
                pltpu.make_async_copy(wd_hbm.at[layer, e], d_stage, sems.at[1]))

    @pl.when(nw_ref[i] > 0)
    def _():
        @pl.when(i == 0)
        def _():
            for cp in weight_copies(be_ref[0]):
                cp.start()

        for cp in weight_copies(be_ref[i]):
            cp.wait()
        wgu_s[...] = gu_stage[...].astype(BF16)
        wd_s[...] = d_stage[...].astype(BF16)

        @pl.when(nx_ref[i] >= 0)
        def _():
            for cp in weight_copies(nx_ref[i]):
                cp.start()

    @pl.when(nv > 0)
    def _():
        rows = lax.broadcasted_iota(jnp.int32, (MOE_BLOCK, 1), 0)
        x = _unpack_bf16_pairs([x_ref[h] for h in range(SC_PIECES)])
        x = jnp.where(rows < nv, x, 0.0).astype(BF16)
        gu = _dot(x, wgu_s[...]) + bgu_ref[0, 0]
        g = jnp.minimum(gu[:, :D_FF_EXPERT], SWIGLU_LIMIT)
        u = jnp.clip(gu[:, D_FF_EXPERT:], -SWIGLU_LIMIT, SWIGLU_LIMIT)
        act = g * _sigmoid(SWIGLU_ALPHA * g) * (u + 1.0)
        _store_pieces(o_ref, (), _pack_bf16_pairs(_dot(act.astype(BF16), wd_s[...]) + bd_ref[0, 0]))

    @pl.when(nv <= 0)
    def _():
        o_ref[...] = jnp.zeros_like(o_ref)


def _expert_ffn(xb, block_e, n_valid, layer, wgu, bgu, wd, bd):
    _, n_slots, _ = xb.shape
    n_blocks = n_slots // MOE_BLOCK
    depth, ne, d, f2 = wgu.shape
    new_w = jnp.concatenate([jnp.ones((1,), jnp.int32), (block_e[1:] != block_e[:-1]).astype(jnp.int32)])
    starts = jnp.where(new_w > 0, jnp.arange(n_blocks, dtype=jnp.int32), n_blocks)
    next_start = jnp.concatenate([lax.cummin(starts, reverse=True)[1:], jnp.full((1,), n_blocks, jnp.int32)])
    next_e = jnp.where(next_start < n_blocks, block_e[jnp.minimum(next_start, n_blocks - 1)], -1).astype(jnp.int32)
    slot_blk = pl.BlockSpec((SC_PIECES, MOE_BLOCK, SC_ROW_WORDS), lambda i, be, nv, nw, nx: (0, i, 0))
    bblk = lambda c: pl.BlockSpec((1, 1, 1, c), lambda i, be, nv, nw, nx: (layer, be[i], 0, 0))
    grid_spec = pltpu.PrefetchScalarGridSpec(
        num_scalar_prefetch=4,
        grid=(n_blocks,),
        in_specs=[slot_blk, pl.BlockSpec(memory_space=pl.ANY), bblk(f2),
                  pl.BlockSpec(memory_space=pl.ANY), bblk(d)],
        out_specs=slot_blk,
        scratch_shapes=[pltpu.VMEM((d, f2), BF16), pltpu.VMEM((f2 // 2, d), BF16),
                        pltpu.VMEM((d, f2), F32), pltpu.VMEM((f2 // 2, d), F32),
                        pltpu.SemaphoreType.DMA((2,))],
    )
    return pl.pallas_call(
        functools.partial(_ffn_kernel, layer=layer),
        out_shape=jax.ShapeDtypeStruct(xb.shape, jnp.int32),
        grid_spec=grid_spec,
        compiler_params=_cparams(("arbitrary",)),
        name="expert_ffn",
    )(block_e, n_valid, new_w, next_e, xb, wgu, bgu.reshape(depth, ne, 1, f2), wd, bd.reshape(depth, ne, 1, d))


def _combine_kernel(y_ref, g_ref, x_ref, mod_ref, lng_ref, lnb_ref, xo_ref, *, n_ctx, tm):
    i = pl.program_id(1)
    rows = i * tm + lax.broadcasted_iota(jnp.int32, (tm, 1), 0)
    g2 = jnp.where(rows < n_ctx, mod_ref[0, 0, 5:6, :], mod_ref[0, 1, 5:6, :])
    gate = g_ref[...]
    y = None
    for k in range(TOP_K):
        t = gate[:, k:k + 1] * _unpack_bf16_pairs([y_ref[k, h] for h in range(SC_PIECES)])
        y = t if y is None else y + t
    xo_ref[0] = _layernorm_rows(DN_ALPHA * x_ref[0] + g2 * y, lng_ref[...], lnb_ref[...])


def _combine(yk, gate, xs, mod, ln_g, ln_b, n_ctx, tm):
    b, n, d = xs.shape
    nt = n // tm
    return pl.pallas_call(
        functools.partial(_combine_kernel, n_ctx=n_ctx, tm=tm),
        out_shape=jax.ShapeDtypeStruct((b, n, d), F32),
        grid=(b, nt),
        in_specs=[pl.BlockSpec((TOP_K, SC_PIECES, tm, SC_ROW_WORDS), lambda bb, i: (0, 0, bb * nt + i, 0)),
                  pl.BlockSpec((tm, TOP_K), lambda bb, i: (bb * nt + i, 0)),
                  pl.BlockSpec((1, tm, d), lambda bb, i: (bb, i, 0)),
                  pl.BlockSpec((1, 2, 8, d), lambda bb, i: (bb, 0, 0, 0)),
                  pl.BlockSpec((1, d), lambda bb, i: (0, 0)),
                  pl.BlockSpec((1, d), lambda bb, i: (0, 0))],
        out_specs=pl.BlockSpec((1, tm, d), lambda bb, i: (bb, i, 0)),
        compiler_params=_cparams(("arbitrary", "arbitrary")),
        name="moe_combine_ln",
    )(yk, gate, xs, mod, ln_g.reshape(1, d), ln_b.reshape(1, d))


def _sc_mesh():
    return plsc.VectorSubcoreMesh(core_axis_name="core", subcore_axis_name="subcore",
                                  num_cores=SC_CORES, num_subcores=SC_SUBCORES)


def _sc_piece_index(idx, rows):
    return jnp.concatenate([idx + h * rows for h in range(SC_PIECES)], axis=1)


def _sc_scatter_rows(x, dest, n_out):
    p, n, w = x.shape
    x2 = x.reshape(p * n, w)
    dest2 = _sc_piece_index(dest, n_out)
    kk, n2 = dest2.shape

    @functools.partial(pl.kernel, out_type=jax.ShapeDtypeStruct((p * n_out, w), x.dtype),
                       mesh=_sc_mesh(), scratch_types=[], name="moe_dispatch_scatter")
    def scatter(x_hbm, i_hbm, o_hbm):
        def body(x_vmem, i_vmem):
            for j in range(kk):
                pltpu.sync_copy(x_vmem, o_hbm.at[i_vmem.at[j]])

        pltpu.emit_pipeline(
            body, grid=(n2 // SC_WINDOW,),
            in_specs=[pl.BlockSpec((SC_WINDOW, SC_ROW_WORDS), lambda i: (i, 0)),
                      pl.BlockSpec((kk, SC_WINDOW), lambda i: (0, i))],
            out_specs=[], core_axis_name=("core", "subcore"),
            dimension_semantics=(pltpu.PARALLEL,))(x_hbm, i_hbm)

    return scatter(x2, dest2).reshape(p, n_out, w)


def _sc_gather_rows(table, idx):
    p, v, w = table.shape
    kk, n = idx.shape
    t2 = table.reshape(p * v, w)
    idx2 = _sc_piece_index(idx, v)
    m = kk * n * p

    @functools.partial(pl.kernel, out_type=jax.ShapeDtypeStruct((m, SC_ROW_WORDS), table.dtype),
                       mesh=_sc_mesh(), scratch_types=[], name="moe_combine_gather")
    def gather(t_hbm, i_hbm, o_hbm):
        def body(i_vmem, o_vmem):
            pltpu.sync_copy(t_hbm.at[i_vmem.at[0]], o_vmem)

        pltpu.emit_pipeline(
            body, grid=(m // SC_WINDOW,),
            in_specs=[pl.BlockSpec((1, SC_WINDOW), lambda i: (0, i))],
            out_specs=[pl.BlockSpec((SC_WINDOW, SC_ROW_WORDS), lambda i: (i, 0))],
            core_axis_name=("core", "subcore"),
            dimension_semantics=(pltpu.PARALLEL,))(i_hbm, o_hbm)

    return gather(t2, idx2.reshape(1, m)).reshape(kk, p, n, w)


def _moe(h2p, top_e, top_g, rank, cnt, layer, wgu, bgu, wd, bd):
    p, b, n, w = h2p.shape
    n_tok = b * n
    nk = n_tok * TOP_K
    experts = jnp.arange(N_EXPERTS, dtype=jnp.int32)
    counts = cnt[:, 0].astype(jnp.int32)
    padded = (counts + MOE_BLOCK - 1) // MOE_BLOCK * MOE_BLOCK
    ends_p = jnp.cumsum(padded)
    start_p = ends_p - padded
    n_blocks = (nk + N_EXPERTS * (MOE_BLOCK - 1) + MOE_BLOCK - 1) // MOE_BLOCK
    block_start = jnp.arange(n_blocks, dtype=jnp.int32) * MOE_BLOCK
    block_e = jnp.minimum(jnp.sum(ends_p[None, :] <= block_start[:, None], axis=1), N_EXPERTS - 1).astype(jnp.int32)
    is_e = block_e[:, None] == experts[None, :]
    filled = jnp.sum(jnp.where(is_e, (start_p + counts)[None, :], 0), axis=1)
    n_valid = jnp.clip(filled - block_start, 0, MOE_BLOCK).astype(jnp.int32)
    dest = rank + jnp.sum(jnp.where(top_e[..., None] == experts, start_p, 0), axis=-1)
    dest = dest.transpose(1, 0, 2).reshape(TOP_K, n_tok)
    gate = top_g.transpose(0, 2, 1).reshape(n_tok, TOP_K)
    xb = _sc_scatter_rows(h2p.reshape(p, n_tok, w), dest, n_blocks * MOE_BLOCK)
    yb = _expert_ffn(xb, block_e, n_valid, layer, wgu, bgu, wd, bd)
    yk = _sc_gather_rows(yb, dest)
    return yk, gate


def kernel(x, c, ctx, c_ctx, w_mod, b_mod, w_in, ret_decay, ret_gn_g, ret_gn_b, hg_lb, hg_norm_g,
           na_rpb, gq_qn_g, gq_kn_g, w_branch, w_out, ln_g, ln_b, w_router, b_router, w_gu, b_gu,
           w_down, b_down):
    b_, t_, d = x.shape
    n_ctx = ctx.shape[1]
    depth = w_mod.shape[0]
    tm = 256

    sm = jax.nn.softmax(hg_lb.astype(F32), axis=1)
    lower = jnp.cumsum(sm, axis=1) - sm[:, :1]
    log_gamma = jax.nn.log_sigmoid(ret_decay.astype(F32)).reshape(depth, 2 * RET_HEADS)

    cc = jnp.concatenate([c, c_ctx[None, :], jnp.zeros((16 - b_ - 1, d), F32)], axis=0)
    modv = _mod_all(cc, w_mod, b_mod).reshape(depth, 16, 6, d)
    one = jnp.asarray([0.0, 1.0, 0.0, 0.0, 1.0, 0.0], F32)[None, None, :, None]
    modv = (modv + one)[:, :, jnp.asarray([1, 0, 2, 4, 3, 5])]
    modv = jnp.concatenate([modv, jnp.zeros((depth, 16, 2, d), F32)], axis=2)
    mod = jnp.stack([jnp.broadcast_to(modv[:, b_:b_ + 1], (depth, b_, 8, d)), modv[:, :b_]], axis=2)

    cos_r, sin_r = _rope_tables(n_ctx, t_, RET_DK)
    cos_g, sin_g = _rope_tables(n_ctx, t_, GQA_HD)

    xs = jnp.concatenate([ctx, x], axis=1)
    tm_in = 768 if xs.shape[1] % 768 == 0 else tm
    w_in_b = _w_in_prep(w_in)
    for l in range(depth):
        z = _in_proj(xs, mod[l], w_in_b[l], n_ctx, tm_in, PROJ_TOTAL // PROJ_TILES)
        ret_o = _retention(z, log_gamma[l], cos_r, sin_r, ret_gn_g[l], ret_gn_b[l], n_ctx)
        hg_o = _hgrn2(z, lower[:, l], hg_norm_g[l], n_ctx)
        na_o = _neighbourhood(z, _na_bias_table(na_rpb[l]), n_ctx)
        gq_o = _gqa(z, cos_g, sin_g, gq_qn_g[l], gq_kn_g[l], n_ctx, tm)
        wr_hi, wr_lo = _split_bf16(w_router[l].T)
        xs, h2p, top_e, top_g, rank, cnt = _merge(z, (ret_o, hg_o, na_o, gq_o), xs, mod[l],
                                                  w_branch[l].astype(BF16), w_out[l].astype(BF16),
                                                  ln_g[l, 0], ln_b[l, 0], wr_hi, wr_lo, b_router[l], n_ctx, tm)
        yk, gate = _moe(h2p, top_e, top_g, rank, cnt, l, w_gu, b_gu, w_down, b_down)
        xs = _combine(yk, gate, xs, mod[l], ln_g[l, 1], ln_b[l, 1], n_ctx, tm_in)
    return xs[:, n_ctx:]
```

```python
import functools

import jax
import jax.numpy as jnp
from jax import lax
import numpy as np
from jax.experimental import pallas as pl
from jax.experimental.pallas import tpu as pltpu
from jax.experimental.pallas import tpu_sc as plsc

D_MODEL = 1024
DEPTH = 4
GRID_W = 64
N_BRANCH = 4
BRANCH_W = D_MODEL // 2
RET_HEADS = 4
RET_DK = BRANCH_W // RET_HEADS
HG_HEADS = 4
HG_DK = BRANCH_W // HG_HEADS
NA_HEADS = 8
NA_HD = BRANCH_W // NA_HEADS
NA_KH = 8
NA_KW = 16
NA_UNION = 10
GQA_HEADS = 8
GQA_KV_HEADS = 2
GQA_HD = BRANCH_W // GQA_HEADS
GQA_KV_W = GQA_KV_HEADS * GQA_HD
ROPE_BASE = 10000.0
N_EXPERTS = 32
TOP_K = 4
D_FF_EXPERT = D_MODEL
SWIGLU_LIMIT = 7.0
SWIGLU_ALPHA = 1.702
MOE_BLOCK = 256
LN_EPS = 1e-5
NORM_EPS = 1e-6
NEG_INF = -1e30
DN_ALPHA = (2 * DEPTH) ** 0.25

LANES = 128
BF16 = jnp.bfloat16
F32 = jnp.float32
VMEM_LIMIT = 56 * 1024 * 1024
SC_CORES = 2
SC_SUBCORES = 16
SC_WINDOW = 128
SC_ROW_WORDS = 256
SC_PIECES = D_MODEL // 2 // SC_ROW_WORDS

PROJ_ORIG = 4 * BRANCH_W + 5 * BRANCH_W + 3 * BRANCH_W + BRANCH_W + 2 * GQA_KV_W
PROJ_TOTAL = PROJ_ORIG + N_BRANCH * D_MODEL
PROJ_TILES = 2
RET_B = (N_BRANCH * D_MODEL) // LANES
HG_B = RET_B + 16
NA_B = HG_B + 20
GQ_B = NA_B + 12

RET_CHUNK = 256
HG_CHUNK = 128
LOG2E = 1.4426950408889634
HG_LEVELS = (64, 32, 16, 8, 4, 2, 1, 0)


def _cparams(sem):
    return pltpu.CompilerParams(dimension_semantics=sem, vmem_limit_bytes=VMEM_LIMIT)


def _dot(a, b):
    return jnp.dot(a, b, preferred_element_type=F32)


def _dot_nt(a, b):
    return lax.dot_general(a, b, (((1,), (1,)), ((), ())), preferred_element_type=F32)


def _dot_tn(a, b):
    return lax.dot_general(a, b, (((0,), (0,)), ((), ())), preferred_element_type=F32)


def _split_bf16(x):
    hi = x.astype(BF16)
    lo = (x - hi.astype(F32)).astype(BF16)
    return hi, lo


def _sigmoid(x):
    return 1.0 / (1.0 + jnp.exp(-x))


def _silu(x):
    return x * _sigmoid(x)


def _mod_kernel(c_ref, w_ref, b_ref, o_ref):
    s = _silu(c_ref[...]).astype(BF16)
    o_ref[0] = _dot(s, w_ref[0].astype(BF16)) + b_ref[0]


def _mod_all(cc, w_mod, b_mod):
    depth, d, n = w_mod.shape
    r = cc.shape[0]
    tn = 1536
    return pl.pallas_call(
        _mod_kernel,
        out_shape=jax.ShapeDtypeStruct((depth, r, n), F32),
        grid=(depth, n // tn),
        in_specs=[pl.BlockSpec((r, d), lambda l, j: (0, 0)),
                  pl.BlockSpec((1, d, tn), lambda l, j: (l, 0, j)),
                  pl.BlockSpec((1, 1, tn), lambda l, j: (l, 0, j))],
        out_specs=pl.BlockSpec((1, r, tn), lambda l, j: (l, 0, j)),
        compiler_params=_cparams(("arbitrary", "arbitrary")),
        name="adaln_mod",
    )(cc, w_mod, b_mod.reshape(depth, 1, n))


W_PREP_TILE = 256


def _w_in_prep_kernel(w_ref, o_ref):
    o_ref[0] = w_ref[0].astype(BF16)


def _w_in_prep(w_in):
    depth, d, total = w_in.shape
    n_src = total // W_PREP_TILE
    shift = PROJ_ORIG // W_PREP_TILE
    return pl.pallas_call(
        _w_in_prep_kernel,
        out_shape=jax.ShapeDtypeStruct((depth, d, total), BF16),
        grid=(depth, n_src),
        in_specs=[pl.BlockSpec((1, d, W_PREP_TILE), lambda l, j: (l, 0, (j + shift) % n_src))],
        out_specs=pl.BlockSpec((1, d, W_PREP_TILE), lambda l, j: (l, 0, j)),
        compiler_params=_cparams(("arbitrary", "arbitrary")),
        name="w_in_prep",
    )(w_in)


def _in_proj_kernel(x_ref, mod_ref, w_ref, o_ref, *, n_ctx, tm):
    i = pl.program_id(2)
    x = x_ref[0]
    mc = mod_ref[0, 0]
    ml = mod_ref[0, 1]
    rows = i * tm + lax.broadcasted_iota(jnp.int32, (tm, 1), 0)
    is_ctx = rows < n_ctx
    scale = jnp.where(is_ctx, mc[0:1, :], ml[0:1, :])
    shift = jnp.where(is_ctx, mc[1:2, :], ml[1:2, :])
    h = (x * scale + shift).astype(BF16)
    o_ref[0] = _dot(h, w_ref[...]).astype(o_ref.dtype)


def _in_proj(xs, mod, w, n_ctx, tm, tn):
    b, n, d = xs.shape
    ncol = w.shape[1]
    return pl.pallas_call(
        functools.partial(_in_proj_kernel, n_ctx=n_ctx, tm=tm),
        out_shape=jax.ShapeDtypeStruct((b, n, ncol), BF16),
        grid=(ncol // tn, b, n // tm),
        in_specs=[pl.BlockSpec((1, tm, d), lambda j, bb, i: (bb, i, 0)),
                  pl.BlockSpec((1, 2, 8, d), lambda j, bb, i: (bb, 0, 0, 0)),
                  pl.BlockSpec((d, tn), lambda j, bb, i: (0, j))],
        out_specs=pl.BlockSpec((1, tm, tn), lambda j, bb, i: (bb, i, j)),
        compiler_params=_cparams(("arbitrary", "arbitrary", "arbitrary")),
        name="in_proj",
    )(xs, mod, w)


def _rope_tables(n_ctx, t, head_dim):
    idx = jnp.arange(t, dtype=jnp.int32)
    row = (idx // GRID_W).astype(F32)
    col = (idx % GRID_W).astype(F32)
    n = head_dim // 2
    inv = ROPE_BASE ** (-jnp.arange(0, n, 2, dtype=F32) / n)

    def half(pos):
        ang = pos[:, None] * inv[None, :]
        c, s = jnp.cos(ang), jnp.sin(ang)
        return jnp.concatenate([c, c], -1), jnp.concatenate([-s, s], -1)

    cr, sr = half(row)
    cc, sc = half(col)
    cos = jnp.concatenate([cr, cc], -1)
    sin = jnp.concatenate([sr, sc], -1)
    reps = LANES // head_dim
    cos = jnp.tile(cos, (1, reps))
    sin = jnp.tile(sin, (1, reps))
    cos = jnp.concatenate([jnp.ones((n_ctx, LANES), F32), cos], 0)
    sin = jnp.concatenate([jnp.zeros((n_ctx, LANES), F32), sin], 0)
    return cos, sin


def _rope(x, cos, sin, quarter):
    lane = lax.broadcasted_iota(jnp.int32, x.shape, 1)
    first = (lane & (2 * quarter - 1)) < quarter
    rot = jnp.where(first, pltpu.roll(x, LANES - quarter, 1), pltpu.roll(x, quarter, 1))
    return x * cos + rot * sin


def _ret_kernel(lg_ref, q_ref, k_ref, v_ref, g_ref, cos_ref, sin_ref, gng_ref, gnb_ref, o_ref,
                ks_ref, sb_ref, *, n_ctx, chunk):
    hh = pl.program_id(1)
    lgf = lg_ref[hh]
    lgb = lg_ref[RET_HEADS + hh]
    n = q_ref.shape[1]
    c_ = chunk
    nch = n // c_
    n_cc = n_ctx // c_
    dk = q_ref.shape[2]

    ri = lax.broadcasted_iota(jnp.int32, (c_, dk), 0).astype(F32)
    df = jnp.exp((ri + 1.0) * lgf)
    db = jnp.exp((c_ - ri) * lgb)
    ef = jnp.exp((c_ - 1.0 - ri) * lgf)
    eb = jnp.exp(ri * lgb)
    gfc = jnp.exp(c_ * lgf)
    gbc = jnp.exp(c_ * lgb)
    ii = lax.broadcasted_iota(jnp.int32, (c_, c_), 0)
    jj = lax.broadcasted_iota(jnp.int32, (c_, c_), 1)
    dij = (ii - jj).astype(F32)
    w = jnp.where(ii >= jj, jnp.exp(dij * lgf), jnp.exp(-dij * lgb))

    def rope_k(c, carry):
        r0 = pl.multiple_of(c * c_, c_)
        kc = k_ref[0, pl.ds(r0, c_), :].astype(F32) * (RET_DK ** -0.5)
        kc = _rope(kc, cos_ref[pl.ds(r0, c_), :], sin_ref[pl.ds(r0, c_), :], 32)
        ks_ref[pl.ds(r0, c_), :] = kc.astype(BF16)
        return carry

    lax.fori_loop(0, nch, rope_k, 0, unroll=3)

    def bwd_step(c, s):
        r0 = pl.multiple_of(c * c_, c_)
        sb_ref[c] = s
        kc = (ks_ref[pl.ds(r0, c_), :].astype(F32) * eb).astype(BF16)
        return s * gbc + _dot_tn(kc, v_ref[0, pl.ds(r0, c_), :])

    s = jnp.zeros((dk, dk), F32)
    s = lax.fori_loop(0, n_cc, lambda t, s_: bwd_step(n_cc - 1 - t, s_), s)
    lax.fori_loop(0, nch - n_cc, lambda t, s_: bwd_step(nch - 1 - t, s_), s, unroll=4)

    gng = gng_ref[...]
    gnb = gnb_ref[...]

    def fwd_step(c, sf):
        r0 = pl.multiple_of(c * c_, c_)
        qc = _rope(q_ref[0, pl.ds(r0, c_), :].astype(F32), cos_ref[pl.ds(r0, c_), :],
                   sin_ref[pl.ds(r0, c_), :], 32)
        kc = ks_ref[pl.ds(r0, c_), :]
        vc = v_ref[0, pl.ds(r0, c_), :]
        a = _dot_nt(qc.astype(BF16), kc) * w
        qi = jnp.concatenate([qc * df, qc * db], axis=1).astype(BF16)
        si = jnp.concatenate([sf, sb_ref[c]], axis=0).astype(BF16)
        o = _dot(a.astype(BF16), vc) + _dot(qi, si)
        sf_new = sf * gfc + _dot_tn((kc.astype(F32) * ef).astype(BF16), vc)
        mu = jnp.mean(o, axis=-1, keepdims=True)
        oc = o - mu
        var = jnp.mean(oc * oc, axis=-1, keepdims=True)
        y = oc * lax.rsqrt(var + LN_EPS) * gng + gnb
        gate = g_ref[0, pl.ds(r0, c_), :].astype(F32)
        o_ref[0, pl.ds(r0, c_), :] = (y * _silu(gate)).astype(o_ref.dtype)
        return sf_new

    lax.fori_loop(0, nch, fwd_step, jnp.zeros((dk, dk), F32), unroll=3)


def _retention(z, log_gamma, cos, sin, gn_g, gn_b, n_ctx):
    b, n, _ = z.shape
    blk = lambda off: pl.BlockSpec((1, n, LANES), lambda bb, h, lg: (bb, 0, off + h))
    grid_spec = pltpu.PrefetchScalarGridSpec(
        num_scalar_prefetch=1,
        grid=(b, RET_HEADS),
        in_specs=[blk(RET_B), blk(RET_B + 4), blk(RET_B + 8), blk(RET_B + 12),
                  pl.BlockSpec((n, LANES), lambda bb, h, lg: (0, 0)),
                  pl.BlockSpec((n, LANES), lambda bb, h, lg: (0, 0)),
                  pl.BlockSpec((1, LANES), lambda bb, h, lg: (0, h)),
                  pl.BlockSpec((1, LANES), lambda bb, h, lg: (0, h))],
        out_specs=pl.BlockSpec((1, n, LANES), lambda bb, h, lg: (bb, 0, h)),
        scratch_shapes=[pltpu.VMEM((n, LANES), BF16),
                        pltpu.VMEM((n // RET_CHUNK, RET_DK, RET_DK), F32)],
    )
    return pl.pallas_call(
        functools.partial(_ret_kernel, n_ctx=n_ctx, chunk=RET_CHUNK),
        out_shape=jax.ShapeDtypeStruct((b, n, BRANCH_W), BF16),
        grid_spec=grid_spec,
        compiler_params=_cparams(("arbitrary", "arbitrary")),
        name="retention",
    )(log_gamma, z, z, z, z, cos, sin, gn_g.reshape(1, -1), gn_b.reshape(1, -1))


def _hg_constants():
    c = HG_CHUNK
    r = np.arange(c)[:, None]
    t = np.arange(c)[None, :]
    mats = []
    level = np.full((2, c, c), len(HG_LEVELS), np.int32)
    for d in range(2):
        blocks = []
        for li, s in enumerate(HG_LEVELS):
            if s == 0:
                level[d][np.arange(c), np.arange(c)] = li
                continue
            base = (r // (2 * s)) * (2 * s)
            if d == 0:
                m = base + s - 1
                mat = (t > np.minimum(r, m)) & (t <= np.maximum(r, m))
                q_side = (r % (2 * s)) >= s
            else:
                m = base + s
                mat = (t >= np.minimum(r, m)) & (t < np.maximum(r, m))
                q_side = (r % (2 * s)) < s
            blocks.append(mat.astype(np.float32))
            same = (r // (2 * s)) == (t // (2 * s))
            k_side_t = ((t % (2 * s)) < s) if d == 0 else ((t % (2 * s)) >= s)
            level[d][same & q_side & k_side_t] = li
        if d == 0:
            blocks.append((t <= r).astype(np.float32))
            blocks.append((t > r).astype(np.float32))
        else:
            blocks.append((t >= r).astype(np.float32))
            blocks.append((t < r).astype(np.float32))
        mats.append(np.concatenate(blocks, 0))
    mats = np.stack(mats)
    masks = (level[:, None] == np.arange(len(HG_LEVELS))[None, :, None, None]).astype(np.float32)
    upper = np.stack([np.broadcast_to(((np.arange(c) % (2 * s)) >= s)[:, None], (c, c)) for s in HG_LEVELS if s > 0])
    return np.concatenate([mats, mats], axis=2), np.maximum(masks[0], masks[1]), upper.astype(np.float32)


def _hg_gates(zf, lb):
    en = jnp.exp(-jnp.abs(zf))
    inv = 1.0 / (1.0 + en)
    pos = zf >= 0
    sg = jnp.where(pos, inv, en * inv)
    sgn = jnp.where(pos, en * inv, inv)
    one_m = 1.0 - lb
    return jnp.log(lb + one_m * sg), one_m * sgn


def _hg_kernel(q_ref, ff_ref, fb_ref, v_ref, g_ref, lb_ref, ng_ref, m_ref, lv_ref, up_ref, o_ref,
               lfb_ref, kb_ref, sb_ref, ktb_ref, dtot_ref, *, n_ctx):
    c_ = HG_CHUNK
    n = q_ref.shape[1]
    nch = n // c_
    n_cc = n_ctx // c_
    dk = q_ref.shape[2]
    nl = len(HG_LEVELS)
    lbf = lb_ref[0:1, :]
    lbb = lb_ref[1:2, :]

    def mdot(mat, lf):
        hi, lo = _split_bf16(lf)
        return _dot(mat, jnp.concatenate([hi, lo], axis=0))

    def prep_b(c, carry):
        r0 = pl.multiple_of(c * c_, c_)
        lf, key = _hg_gates(fb_ref[0, pl.ds(r0, c_), :].astype(F32), lbb)
        lf = lf * LOG2E
        lfb_ref[pl.ds(r0, c_), :] = lf
        kb_ref[pl.ds(r0, c_), :] = key
        e = mdot(m_ref[1, (nl - 1) * c_:(nl + 1) * c_, :], lf)
        ktb_ref[pl.ds(r0, c_), :] = (key * jnp.exp2(e[c_:2 * c_, :])).astype(BF16)
        dtot_ref[c] = jnp.broadcast_to(jnp.exp2(e[0:1, :]), (8, dk))
        return carry

    lax.fori_loop(0, nch, prep_b, 0, unroll=3)

    def bwd_step(c, st):
        r0 = pl.multiple_of(c * c_, c_)
        sb_ref[c] = st
        return st * dtot_ref[c][0:1, :] + _dot_tn(v_ref[0, pl.ds(r0, c_), :], ktb_ref[pl.ds(r0, c_), :])

    st = jnp.zeros((dk, dk), F32)
    st = lax.fori_loop(0, n_cc, lambda t, s_: bwd_step(n_cc - 1 - t, s_), st, unroll=2)
    lax.fori_loop(0, nch - n_cc, lambda t, s_: bwd_step(nch - 1 - t, s_), st, unroll=8)

    ng = ng_ref[...]

    def mdot_pair(mat, lf0, lf1):
        hi0, lo0 = _split_bf16(lf0)
        hi1, lo1 = _split_bf16(lf1)
        rhs = jnp.concatenate([jnp.concatenate([hi0, hi1], axis=1), jnp.concatenate([lo0, lo1], axis=1)], axis=0)
        return _dot(mat, rhs)

    def intra(qs, keyf, keyb, ef, eb):
        a = jnp.zeros((c_, c_), F32)
        qsb, kfb, kbb = qs.astype(BF16), keyf.astype(BF16), keyb.astype(BF16)
        for li, s in enumerate(HG_LEVELS):
            if s == 0:
                qt = jnp.concatenate([qsb, qsb], axis=1)
                kt = jnp.concatenate([kfb, kbb], axis=1)
            else:
                decf = jnp.exp2(ef[li * c_:(li + 1) * c_, :])
                decb = jnp.exp2(eb[li * c_:(li + 1) * c_, :])
                up = up_ref[li]
                dqf = decf * up
                dkb = decb * up
                qt = jnp.concatenate([qsb * dqf.astype(BF16), qsb * (decb - dkb).astype(BF16)], axis=1)
                kt = jnp.concatenate([kfb * (decf - dqf).astype(BF16), kbb * dkb.astype(BF16)], axis=1)
            a = a + _dot_nt(qt, kt) * lv_ref[li]
        cumf = ef[(nl - 1) * c_:nl * c_, :]
        restf = ef[nl * c_:(nl + 1) * c_, :]
        cumb = eb[(nl - 1) * c_:nl * c_, :]
        return a, cumf, restf, cumb

    def fwd_chunk(c, sf, qs, keyf, ef, eb):
        r0 = pl.multiple_of(c * c_, c_)
        vc = v_ref[0, pl.ds(r0, c_), :]
        a, cumf, restf, cumb = intra(qs, keyf, kb_ref[pl.ds(r0, c_), :], ef, eb)
        qi = jnp.concatenate([qs * jnp.exp2(cumf), qs * jnp.exp2(cumb)], axis=1).astype(BF16)
        si = jnp.concatenate([sf, sb_ref[c]], axis=1).astype(BF16)
        o = _dot(a.astype(BF16), vc) + _dot_nt(qi, si)
        totf = cumf[c_ - 1:c_, :]
        sf_new = sf * jnp.exp2(totf) + _dot_tn(vc, (keyf * jnp.exp2(restf)).astype(BF16))
        on = o * lax.rsqrt(jnp.mean(o * o, axis=-1, keepdims=True) + NORM_EPS) * ng
        gate = g_ref[0, pl.ds(r0, c_), :].astype(F32)
        o_ref[0, pl.ds(r0, c_), :] = (on * _silu(gate)).astype(o_ref.dtype)
        return sf_new

    def fwd_pair(p, sf):
        r0 = pl.multiple_of(p * (2 * c_), 2 * c_)
        qs, keyf, lff = [], [], []
        for t in range(2):
            rows = pl.ds(r0 + t * c_, c_)
            qs.append(_silu(q_ref[0, rows, :].astype(F32)))
            lf, key = _hg_gates(ff_ref[0, rows, :].astype(F32), lbf)
            lff.append(lf * LOG2E)
            keyf.append(key)
        ef = mdot_pair(m_ref[0], lff[0], lff[1])
        eb = mdot_pair(m_ref[1], lfb_ref[pl.ds(r0, c_), :], lfb_ref[pl.ds(r0 + c_, c_), :])
        for t in range(2):
            sf = fwd_chunk(2 * p + t, sf, qs[t], keyf[t], ef[:, t * dk:(t + 1) * dk], eb[:, t * dk:(t + 1) * dk])
        return sf

    lax.fori_loop(0, nch // 2, fwd_pair, jnp.zeros((dk, dk), F32))


def _hgrn2(z, lower, norm_g, n_ctx):
    b, n, _ = z.shape
    mats, level, upper = _hg_constants()
    mats = jnp.asarray(mats, BF16)
    level = jnp.asarray(level)
    upper = jnp.asarray(upper)
    blk = lambda off: pl.BlockSpec((1, n, LANES), lambda bb, h: (bb, 0, off + h))
    return pl.pallas_call(
        functools.partial(_hg_kernel, n_ctx=n_ctx),
        out_shape=jax.ShapeDtypeStruct((b, n, BRANCH_W), BF16),
        grid=(b, HG_HEADS),
        in_specs=[blk(HG_B), blk(HG_B + 4), blk(HG_B + 8), blk(HG_B + 12), blk(HG_B + 16),
                  pl.BlockSpec((2, LANES), lambda bb, h: (0, h)),
                  pl.BlockSpec((1, LANES), lambda bb, h: (0, h)),
                  pl.BlockSpec(mats.shape, lambda bb, h: (0, 0, 0)),
                  pl.BlockSpec(level.shape, lambda bb, h: (0, 0, 0)),
                  pl.BlockSpec(upper.shape, lambda bb, h: (0, 0, 0))],
        out_specs=pl.BlockSpec((1, n, LANES), lambda bb, h: (bb, 0, h)),
        scratch_shapes=[pltpu.VMEM((n, LANES), F32), pltpu.VMEM((n, LANES), F32),
                        pltpu.VMEM((n // HG_CHUNK, HG_DK, HG_DK), F32),
                        pltpu.VMEM((n, LANES), BF16), pltpu.VMEM((n // HG_CHUNK, 8, LANES), F32)],
        compiler_params=_cparams(("arbitrary", "arbitrary")),
        name="hgrn2",
    )(z, z, z, z, z, lower, norm_g.reshape(1, -1), mats, level, upper)


def _na_bias_table(rpb):
    qc = np.arange(GRID_W)[:, None]
    kc = np.arange(GRID_W)[None, :]
    col0 = np.clip(qc - NA_KW // 2, 0, GRID_W - NA_KW)
    valid = (kc >= col0) & (kc < col0 + NA_KW)
    dc = np.clip(kc - qc + (NA_KW - 1), 0, 2 * NA_KW - 2)
    pick = (dc[:, :, None] == np.arange(2 * NA_KW - 1)).astype(np.float32)
    t15 = jnp.einsum('hrc,qkc->hrqk', rpb.astype(F32), pick, precision=lax.Precision.HIGHEST)
    t = jnp.stack([t15[:, dl:dl + NA_KH] for dl in range(NA_KH)], axis=1)
    t = jnp.where(jnp.asarray(valid)[None, None, None], t * LOG2E, NEG_INF)
    t = t.transpose(0, 1, 3, 2, 4).reshape(rpb.shape[0] // 2, 2, NA_KH, GRID_W, NA_KH * GRID_W)
    offs = [jnp.pad(t, ((0, 0),) * 4 + ((o * GRID_W, (NA_UNION - NA_KH - o) * GRID_W),), constant_values=NEG_INF)
            for o in range(NA_UNION - NA_KH + 1)]
    return jnp.stack(offs, axis=2).transpose(0, 3, 2, 1, 4, 5)


def _softmax_pv(parts):
    m = parts[0][0].max(axis=-1, keepdims=True)
    for s, _ in parts[1:]:
        m = jnp.maximum(m, s.max(axis=-1, keepdims=True))
    l = None
    o = None
    for s, v in parts:
        p = jnp.exp2(s - m)
        ls = p.sum(axis=-1, keepdims=True)
        os_ = _dot(p.astype(BF16), v)
        l = ls if l is None else l + ls
        o = os_ if o is None else o + os_
    return o / l


def _na_kernel(q_ref, k_ref, v_ref, tab_ref, o_ref, *, n_ctx):
    n = q_ref.shape[1]
    rows = (n - n_ctx) // GRID_W
    ulen = NA_UNION * GRID_W
    scale = NA_HD ** -0.5 * LOG2E
    kx = k_ref[0, 0:n_ctx, :]
    vx = v_ref[0, 0:n_ctx, :]

    def head_mask(shape, hh):
        lane = lax.broadcasted_iota(jnp.int32, shape, 1)
        return (lane < NA_HD) if hh == 0 else (lane >= NA_HD)

    def stack_heads(qf):
        return jnp.concatenate([jnp.where(head_mask(qf.shape, 0), qf, 0.0),
                                jnp.where(head_mask(qf.shape, 1), qf, 0.0)], axis=0).astype(BF16)

    def unstack_heads(o):
        m = o.shape[0] // 2
        return jnp.where(head_mask((m, LANES), 0), o[:m], o[m:])

    qx = stack_heads(q_ref[0, 0:n_ctx, :].astype(F32) * scale)
    o_ref[0, 0:n_ctx, :] = unstack_heads(_softmax_pv([(_dot_nt(qx, kx), vx)])).astype(o_ref.dtype)

    def pair_step(t, carry):
        r = 2 * t
        start_u = jnp.clip(r - NA_KH // 2, 0, rows - NA_UNION)
        q0 = pl.multiple_of(n_ctx + r * GRID_W, 2 * GRID_W)
        k0 = pl.multiple_of(n_ctx + start_u * GRID_W, GRID_W)
        qr = stack_heads(q_ref[0, pl.ds(q0, 2 * GRID_W), :].astype(F32) * scale)
        kl = k_ref[0, pl.ds(k0, ulen), :]
        vl = v_ref[0, pl.ds(k0, ulen), :]
        bias = []
        for hh in range(2):
            for i in range(2):
                start = jnp.clip(r + i - NA_KH // 2, 0, rows - NA_KH)
                bias.append(tab_ref[0, start - (r + i) + (NA_KH - 1), start - start_u, hh])
        s_loc = _dot_nt(qr, kl) + jnp.concatenate(bias, axis=0)
        s_ctx = _dot_nt(qr, kx)
        o = unstack_heads(_softmax_pv([(s_loc, vl), (s_ctx, vx)]))
        o_ref[0, pl.ds(q0, 2 * GRID_W), :] = o.astype(o_ref.dtype)
        return carry

    lax.fori_loop(0, rows // 2, pair_step, 0, unroll=2)


def _neighbourhood(z, table, n_ctx):
    b, n, _ = z.shape
    npair = NA_HEADS // 2
    blk = lambda off: pl.BlockSpec((1, n, LANES), lambda bb, p: (bb, 0, off + p))
    return pl.pallas_call(
        functools.partial(_na_kernel, n_ctx=n_ctx),
        out_shape=jax.ShapeDtypeStruct((b, n, BRANCH_W), BF16),
        grid=(b, npair),
        in_specs=[blk(NA_B), blk(NA_B + 4), blk(NA_B + 8),
                  pl.BlockSpec((1, NA_KH, NA_UNION - NA_KH + 1, 2, GRID_W, NA_UNION * GRID_W),
                               lambda bb, p: (p, 0, 0, 0, 0, 0))],
        out_specs=pl.BlockSpec((1, n, LANES), lambda bb, p: (bb, 0, p)),
        compiler_params=_cparams(("arbitrary", "arbitrary")),
        name="neighbourhood_attn",
    )(z, z, z, table)


def _gqa_kernel(q_ref, k_ref, v_ref, cos_ref, sin_ref, qg_ref, kg_ref, bd_ref, o_ref, ks_ref, vs_ref, *, n_ctx):
    i = pl.program_id(1)
    tq = q_ref.shape[1]
    n = k_ref.shape[1]
    grp = GQA_HEADS // GQA_KV_HEADS
    bd = bd_ref[...]

    def norm_rope(x, g, cos, sin):
        hi, lo = _split_bf16(x * x)
        ms = _dot(hi, bd) + _dot(lo, bd)
        xn = x * lax.rsqrt(ms + NORM_EPS) * g
        return _rope(xn, cos, sin, GQA_HD // 4)

    @pl.when(i == 0)
    def _():
        def prep(c, carry):
            rows = pl.ds(pl.multiple_of(c * tq, tq), tq)
            kn = norm_rope(k_ref[0, rows, :].astype(F32), kg_ref[...], cos_ref[rows, :], sin_ref[rows, :])
            vv = v_ref[0, rows, :].astype(F32)
            lo_half = lax.broadcasted_iota(jnp.int32, kn.shape, 1) < GQA_HD
            ksw = pltpu.roll(kn, GQA_HD, 1)
            vsw = pltpu.roll(vv, GQA_HD, 1)
            ks_ref[0, rows, :] = jnp.where(lo_half, kn, ksw).astype(BF16)
            ks_ref[1, rows, :] = jnp.where(lo_half, ksw, kn).astype(BF16)
            vs_ref[0, rows, 0:LANES] = jnp.where(lo_half, vv, vsw).astype(BF16)
            vs_ref[1, rows, 0:LANES] = jnp.where(lo_half, vsw, vv).astype(BF16)
            ones = jnp.ones((tq, LANES), BF16)
            vs_ref[0, rows, LANES:] = ones
            vs_ref[1, rows, LANES:] = ones
            return carry

        lax.fori_loop(0, n // tq, prep, 0)

    qrows = pl.ds(pl.multiple_of(i * tq, tq), tq)
    cos = cos_ref[qrows, :]
    sin = sin_ref[qrows, :]
    qg = qg_ref[...]

    def attend(nk):
        for j in range(BRANCH_W // LANES):
            x = q_ref[0, :, j * LANES:(j + 1) * LANES].astype(F32)
            qb = (norm_rope(x, qg, cos, sin) * (GQA_HD ** -0.5 * LOG2E)).astype(BF16)
            lane = lax.broadcasted_iota(jnp.int32, qb.shape, 1)
            g = (2 * j) // grp
            zero = jnp.zeros_like(qb)
            qs = jnp.concatenate([jnp.where(lane < GQA_HD, qb, zero), jnp.where(lane >= GQA_HD, qb, zero)], axis=0)
            sc = _dot_nt(qs, ks_ref[g, 0:nk, :])
            p = jnp.exp2(sc - sc.max(axis=-1, keepdims=True)).astype(BF16)
            ov = _dot(p, vs_ref[g, 0:nk, :])
            o = ov[:, :LANES] / ov[:, LANES:]
            o_ref[0, :, j * LANES:(j + 1) * LANES] = jnp.where(lane < GQA_HD, o[:tq], o[tq:]).astype(o_ref.dtype)

    @pl.when(i * tq < n_ctx)
    def _():
        attend(n_ctx)

    @pl.when(i * tq >= n_ctx)
    def _():
        attend(n)


def _gqa(z, cos, sin, qn_g, kn_g, n_ctx, tq):
    b, n, _ = z.shape
    bd = np.kron(np.eye(LANES // GQA_HD), np.ones((GQA_HD, GQA_HD))) / GQA_HD
    qg = jnp.tile(qn_g.reshape(1, -1), (1, LANES // GQA_HD))
    kg = jnp.tile(kn_g.reshape(1, -1), (1, LANES // GQA_HD))
    qb = (GQ_B * LANES) // BRANCH_W
    const = lambda shape: pl.BlockSpec(shape, lambda bb, i: (0,) * len(shape))
    return pl.pallas_call(
        functools.partial(_gqa_kernel, n_ctx=n_ctx),
        out_shape=jax.ShapeDtypeStruct((b, n, BRANCH_W), BF16),
        grid=(b, n // tq),
        in_specs=[pl.BlockSpec((1, tq, BRANCH_W), lambda bb, i: (bb, i, qb)),
                  pl.BlockSpec((1, n, LANES), lambda bb, i: (bb, 0, GQ_B + 4)),
                  pl.BlockSpec((1, n, LANES), lambda bb, i: (bb, 0, GQ_B + 5)),
                  const((n, LANES)), const((n, LANES)), const((1, LANES)), const((1, LANES)),
                  const((LANES, LANES))],
        out_specs=pl.BlockSpec((1, tq, BRANCH_W), lambda bb, i: (bb, i, 0)),
        scratch_shapes=[pltpu.VMEM((GQA_KV_HEADS, n, LANES), BF16),
                        pltpu.VMEM((GQA_KV_HEADS, n, 2 * LANES), BF16)],
        compiler_params=_cparams(("arbitrary", "arbitrary")),
        name="gqa_attn",
    )(z, z, z, cos, sin, qg, kg, jnp.asarray(bd, BF16))


def _layernorm_rows(v, g, b):
    mu = jnp.mean(v, axis=-1, keepdims=True)
    vc = v - mu
    var = jnp.mean(vc * vc, axis=-1, keepdims=True)
    return vc * lax.rsqrt(var + LN_EPS) * g + b


def _pack_bf16_pairs(v):
    w = v.shape[1] // 2
    bits = lax.bitcast_convert_type(v.astype(BF16).astype(F32), jnp.int32)
    return bits[:, w:] | lax.shift_right_logical(bits[:, :w], 16)


def _unpack_bf16_pairs(pieces):
    lo = [lax.bitcast_convert_type(lax.shift_left(p, 16), F32) for p in pieces]
    hi = [lax.bitcast_convert_type(p & jnp.int32(-65536), F32) for p in pieces]
    return jnp.concatenate(lo + hi, axis=1)


def _store_pieces(ref, lead, packed):
    for h in range(SC_PIECES):
        ref[(h,) + lead] = packed[:, h * SC_ROW_WORDS:(h + 1) * SC_ROW_WORDS]


def _merge_kernel(z_ref, r_ref, h_ref, a_ref, q_ref, x_ref, mod_ref, wb_ref, wo_ref, lng_ref, lnb_ref,
                  wrh_ref, wrl_ref, br_ref, tri_ref, xo_ref, h2_ref, te_ref, tg_ref, rk_ref, cnt_ref,
                  run_ref, *, n_ctx, tm):
    i = pl.program_id(1)

    @pl.when((pl.program_id(0) == 0) & (i == 0))
    def _():
        run_ref[...] = jnp.zeros_like(run_ref)

    is_ctx = i * tm < n_ctx
    mod = jnp.where(is_ctx, mod_ref[0, 0], mod_ref[0, 1])
    acc = None
    for nb, br in enumerate((r_ref, h_ref, a_ref, q_ref)):
        yb = _dot(br[0], wb_ref[nb])
        gt = _sigmoid(z_ref[0, :, nb * D_MODEL:(nb + 1) * D_MODEL].astype(F32))
        acc = gt * yb if acc is None else acc + gt * yb
    y = _dot(acc.astype(BF16), wo_ref[...])
    xn = _layernorm_rows(DN_ALPHA * x_ref[0] + mod[2:3, :] * y, lng_ref[...], lnb_ref[...])
    xo_ref[0] = xn
    h2 = xn * mod[3:4, :] + mod[4:5, :]
    h2_hi, h2_lo = _split_bf16(h2)
    _store_pieces(h2_ref, (0,), _pack_bf16_pairs(h2))
    wrh = wrh_ref[...]
    logit = _dot_nt(wrh, h2_hi) + _dot_nt(wrh, h2_lo) + _dot_nt(wrl_ref[...], h2_hi) + br_ref[...]
    eidx = lax.broadcasted_iota(jnp.int32, logit.shape, 0)
    vals, idxs = [], []
    for _ in range(TOP_K):
        m = logit.max(axis=0, keepdims=True)
        sel = jnp.where(logit == m, eidx, N_EXPERTS).min(axis=0, keepdims=True)
        vals.append(m)
        idxs.append(sel)
        logit = jnp.where(eidx == sel, -jnp.inf, logit)
    ex = [jnp.exp(v - vals[0]) for v in vals]
    tot = ex[0] + ex[1] + ex[2] + ex[3]
    te_ref[0] = jnp.concatenate(idxs, axis=0)
    tg_ref[0] = jnp.concatenate([e / tot for e in ex], axis=0)
    base = run_ref[:, 0:1]
    tri = tri_ref[...]
    ranks = []
    for sel in idxs:
        hit = eidx == sel
        ones = jnp.where(hit, 1.0, 0.0)
        pre = _dot(ones.astype(BF16), tri)
        ranks.append(jnp.sum(jnp.where(hit, pre + base, 0.0), axis=0, keepdims=True))
        base = base + jnp.sum(ones, axis=1, keepdims=True)
    rk_ref[0] = jnp.concatenate(ranks, axis=0).astype(jnp.int32)
    run_ref[...] = jnp.broadcast_to(base, run_ref.shape)
    cnt_ref[...] = run_ref[...]


def _merge(z, outs, xs, mod, wb, wo, ln_g, ln_b, wr_hi, wr_lo, b_r, n_ctx, tm):
    b, n, d = xs.shape
    row = lambda w: pl.BlockSpec((1, tm, w), lambda bb, i: (bb, i, 0))
    full = lambda a: pl.BlockSpec(a.shape, lambda bb, i: (0,) * a.ndim)
    topk = pl.BlockSpec((1, TOP_K, tm), lambda bb, i: (bb, 0, i))
    ln_g = ln_g.reshape(1, d)
    ln_b = ln_b.reshape(1, d)
    b_r = b_r.reshape(N_EXPERTS, 1)
    tri = jnp.asarray(np.triu(np.ones((tm, tm), np.float32), 1), BF16)
    return pl.pallas_call(
        functools.partial(_merge_kernel, n_ctx=n_ctx, tm=tm),
        out_shape=(jax.ShapeDtypeStruct((b, n, d), F32),
                   jax.ShapeDtypeStruct((SC_PIECES, b, n, SC_ROW_WORDS), jnp.int32),
                   jax.ShapeDtypeStruct((b, TOP_K, n), jnp.int32),
                   jax.ShapeDtypeStruct((b, TOP_K, n), F32),
                   jax.ShapeDtypeStruct((b, TOP_K, n), jnp.int32),
                   jax.ShapeDtypeStruct((N_EXPERTS, LANES), F32)),
        grid=(b, n // tm),
        in_specs=[row(N_BRANCH * D_MODEL), row(BRANCH_W), row(BRANCH_W), row(BRANCH_W), row(BRANCH_W),
                  row(d),
                  pl.BlockSpec((1, 2, 8, d), lambda bb, i: (bb, 0, 0, 0)),
                  full(wb), full(wo), full(ln_g), full(ln_b), full(wr_hi), full(wr_lo), full(b_r), full(tri)],
        out_specs=(row(d), pl.BlockSpec((SC_PIECES, 1, tm, SC_ROW_WORDS), lambda bb, i: (0, bb, i, 0)),
                   topk, topk, topk,
                   pl.BlockSpec((N_EXPERTS, LANES), lambda bb, i: (0, 0))),
        scratch_shapes=[pltpu.VMEM((N_EXPERTS, LANES), F32)],
        compiler_params=_cparams(("arbitrary", "arbitrary")),
        name="merge_ln_router",
    )(z, *outs, xs, mod, wb, wo, ln_g, ln_b, wr_hi, wr_lo, b_r, tri)


def _ffn_kernel(be_ref, nv_ref, nw_ref, nx_ref, x_ref, wgu_hbm, bgu_ref, wd_hbm, bd_ref, o_ref,
                wgu_s, wd_s, gu_stage, d_stage, sems, *, layer):
    i = pl.program_id(0)
    nv = nv_ref[i]

    def weight_copies(e):
        return (pltpu.make_async_copy(wgu_hbm.at[layer, e], gu_stage, sems.at[0]),
                pltpu.make_async_copy(wd_hbm.at[layer, e], d_stage, sems.at[1]))

    @pl.when(nw_ref[i] > 0)
    def _():
        @pl.when(i == 0)
        def _():
            for cp in weight_copies(be_ref[0]):
                cp.start()

        for cp in weight_copies(be_ref[i]):
            cp.wait()
        wgu_s[...] = gu_stage[...].astype(BF16)
        wd_s[...] = d_stage[...].astype(BF16)

        @pl.when(nx_ref[i] >= 0)
        def _():
            for cp in weight_copies(nx_ref[i]):
                cp.start()

    @pl.when(nv > 0)
    def _():
        rows = lax.broadcasted_iota(jnp.int32, (MOE_BLOCK, 1), 0)
        x = _unpack_bf16_pairs([x_ref[h] for h in range(SC_PIECES)])
        x = jnp.where(rows < nv, x, 0.0).astype(BF16)
        gu = _dot(x, wgu_s[...]) + bgu_ref[0, 0]
        g = jnp.minimum(gu[:, :D_FF_EXPERT], SWIGLU_LIMIT)
        u = jnp.clip(gu[:, D_FF_EXPERT:], -SWIGLU_LIMIT, SWIGLU_LIMIT)
        act = g * _sigmoid(SWIGLU_ALPHA * g) * (u + 1.0)
        _store_pieces(o_ref, (), _pack_bf16_pairs(_dot(act.astype(BF16), wd_s[...]) + bd_ref[0, 0]))

    @pl.when(nv <= 0)
    def _():
        o_ref[...] = jnp.zeros_like(o_ref)


def _expert_ffn(xb, block_e, n_valid, layer, wgu, bgu, wd, bd):
    _, n_slots, _ = xb.shape
    n_blocks = n_slots // MOE_BLOCK
    depth, ne, d, f2 = wgu.shape
    new_w = jnp.concatenate([jnp.ones((1,), jnp.int32), (block_e[1:] != block_e[:-1]).astype(jnp.int32)])
    starts = jnp.where(new_w > 0, jnp.arange(n_blocks, dtype=jnp.int32), n_blocks)
    next_start = jnp.concatenate([lax.cummin(starts, reverse=True)[1:], jnp.full((1,), n_blocks, jnp.int32)])
    next_e = jnp.where(next_start < n_blocks, block_e[jnp.minimum(next_start, n_blocks - 1)], -1).astype(jnp.int32)
    slot_blk = pl.BlockSpec((SC_PIECES, MOE_BLOCK, SC_ROW_WORDS), lambda i, be, nv, nw, nx: (0, i, 0))
    bblk = lambda c: pl.BlockSpec((1, 1, 1, c), lambda i, be, nv, nw, nx: (layer, be[i], 0, 0))
    grid_spec = pltpu.PrefetchScalarGridSpec(
        num_scalar_prefetch=4,
        grid=(n_blocks,),
        in_specs=[slot_blk, pl.BlockSpec(memory_space=pl.ANY), bblk(f2),
                  pl.BlockSpec(memory_space=pl.ANY), bblk(d)],
        out_specs=slot_blk,
        scratch_shapes=[pltpu.VMEM((d, f2), BF16), pltpu.VMEM((f2 // 2, d), BF16),
                        pltpu.VMEM((d, f2), F32), pltpu.VMEM((f2 // 2, d), F32),
                        pltpu.SemaphoreType.DMA((2,))],
    )
    return pl.pallas_call(
        functools.partial(_ffn_kernel, layer=layer),
        out_shape=jax.ShapeDtypeStruct(xb.shape, jnp.int32),
        grid_spec=grid_spec,
        compiler_params=_cparams(("arbitrary",)),
        name="expert_ffn",
    )(block_e, n_valid, new_w, next_e, xb, wgu, bgu.reshape(depth, ne, 1, f2), wd, bd.reshape(depth, ne, 1, d))


def _combine_kernel(y_ref, g_ref, x_ref, mod_ref, lng_ref, lnb_ref, xo_ref, *, n_ctx, tm):
    i = pl.program_id(1)
    rows = i * tm + lax.broadcasted_iota(jnp.int32, (tm, 1), 0)
    g2 = jnp.where(rows < n_ctx, mod_ref[0, 0, 5:6, :], mod_ref[0, 1, 5:6, :])
    gate = g_ref[...]
    y = None
    for k in range(TOP_K):
        t = gate[:, k:k + 1] * _unpack_bf16_pairs([y_ref[k, h] for h in range(SC_PIECES)])
        y = t if y is None else y + t
    xo_ref[0] = _layernorm_rows(DN_ALPHA * x_ref[0] + g2 * y, lng_ref[...], lnb_ref[...])


def _combine(yk, gate, xs, mod, ln_g, ln_b, n_ctx, tm):
    b, n, d = xs.shape
    nt = n // tm
    return pl.pallas_call(
        functools.partial(_combine_kernel, n_ctx=n_ctx, tm=tm),
        out_shape=jax.ShapeDtypeStruct((b, n, d), F32),
        grid=(b, nt),
        in_specs=[pl.BlockSpec((TOP_K, SC_PIECES, tm, SC_ROW_WORDS), lambda bb, i: (0, 0, bb * nt + i, 0)),
                  pl.BlockSpec((tm, TOP_K), lambda bb, i: (bb * nt + i, 0)),
                  pl.BlockSpec((1, tm, d), lambda bb, i: (bb, i, 0)),
                  pl.BlockSpec((1, 2, 8, d), lambda bb, i: (bb, 0, 0, 0)),
                  pl.BlockSpec((1, d), lambda bb, i: (0, 0)),
                  pl.BlockSpec((1, d), lambda bb, i: (0, 0))],
        out_specs=pl.BlockSpec((1, tm, d), lambda bb, i: (bb, i, 0)),
        compiler_params=_cparams(("arbitrary", "arbitrary")),
        name="moe_combine_ln",
    )(yk, gate, xs, mod, ln_g.reshape(1, d), ln_b.reshape(1, d))


def _sc_mesh():
    return plsc.VectorSubcoreMesh(core_axis_name="core", subcore_axis_name="subcore",
                                  num_cores=SC_CORES, num_subcores=SC_SUBCORES)


def _sc_piece_index(idx, rows):
    return jnp.concatenate([idx + h * rows for h in range(SC_PIECES)], axis=1)


def _sc_scatter_rows(x, dest, n_out):
    p, n, w = x.shape
    x2 = x.reshape(p * n, w)
    dest2 = _sc_piece_index(dest, n_out)
    kk, n2 = dest2.shape

    @functools.partial(pl.kernel, out_type=jax.ShapeDtypeStruct((p * n_out, w), x.dtype),
                       mesh=_sc_mesh(), scratch_types=[], name="moe_dispatch_scatter")
    def scatter(x_hbm, i_hbm, o_hbm):
        def body(x_vmem, i_vmem):
            for j in range(kk):
                pltpu.sync_copy(x_vmem, o_hbm.at[i_vmem.at[j]])

        pltpu.emit_pipeline(
            body, grid=(n2 // SC_WINDOW,),
            in_specs=[pl.BlockSpec((SC_WINDOW, SC_ROW_WORDS), lambda i: (i, 0)),
                      pl.BlockSpec((kk, SC_WINDOW), lambda i: (0, i))],
            out_specs=[], core_axis_name=("core", "subcore"),
            dimension_semantics=(pltpu.PARALLEL,))(x_hbm, i_hbm)

    return scatter(x2, dest2).reshape(p, n_out, w)


def _sc_gather_rows(table, idx):
    p, v, w = table.shape
    kk, n = idx.shape
    t2 = table.reshape(p * v, w)
    idx2 = _sc_piece_index(idx, v)
    m = kk * n * p

    @functools.partial(pl.kernel, out_type=jax.ShapeDtypeStruct((m, SC_ROW_WORDS), table.dtype),
                       mesh=_sc_mesh(), scratch_types=[], name="moe_combine_gather")
    def gather(t_hbm, i_hbm, o_hbm):
        def body(i_vmem, o_vmem):
            pltpu.sync_copy(t_hbm.at[i_vmem.at[0]], o_vmem)

        pltpu.emit_pipeline(
            body, grid=(m // SC_WINDOW,),
            in_specs=[pl.BlockSpec((1, SC_WINDOW), lambda i: (0, i))],
            out_specs=[pl.BlockSpec((SC_WINDOW, SC_ROW_WORDS), lambda i: (i, 0))],
            core_axis_name=("core", "subcore"),
            dimension_semantics=(pltpu.PARALLEL,))(i_hbm, o_hbm)

    return gather(t2, idx2.reshape(1, m)).reshape(kk, p, n, w)


def _moe(h2p, top_e, top_g, rank, cnt, layer, wgu, bgu, wd, bd):
    p, b, n, w = h2p.shape
    n_tok = b * n
    nk = n_tok * TOP_K
    experts = jnp.arange(N_EXPERTS, dtype=jnp.int32)
    counts = cnt[:, 0].astype(jnp.int32)
    padded = (counts + MOE_BLOCK - 1) // MOE_BLOCK * MOE_BLOCK
    ends_p = jnp.cumsum(padded)
    start_p = ends_p - padded
    n_blocks = (nk + N_EXPERTS * (MOE_BLOCK - 1) + MOE_BLOCK - 1) // MOE_BLOCK
    block_start = jnp.arange(n_blocks, dtype=jnp.int32) * MOE_BLOCK
    block_e = jnp.minimum(jnp.sum(ends_p[None, :] <= block_start[:, None], axis=1), N_EXPERTS - 1).astype(jnp.int32)
    is_e = block_e[:, None] == experts[None, :]
    filled = jnp.sum(jnp.where(is_e, (start_p + counts)[None, :], 0), axis=1)
    n_valid = jnp.clip(filled - block_start, 0, MOE_BLOCK).astype(jnp.int32)
    dest = rank + jnp.sum(jnp.where(top_e[..., None] == experts, start_p, 0), axis=-1)
    dest = dest.transpose(1, 0, 2).reshape(TOP_K, n_tok)
    gate = top_g.transpose(0, 2, 1).reshape(n_tok, TOP_K)
    xb = _sc_scatter_rows(h2p.reshape(p, n_tok, w), dest, n_blocks * MOE_BLOCK)
    yb = _expert_ffn(xb, block_e, n_valid, layer, wgu, bgu, wd, bd)
    yk = _sc_gather_rows(yb, dest)
    return yk, gate


def kernel(x, c, ctx, c_ctx, w_mod, b_mod, w_in, ret_decay, ret_gn_g, ret_gn_b, hg_lb, hg_norm_g,
           na_rpb, gq_qn_g, gq_kn_g, w_branch, w_out, ln_g, ln_b, w_router, b_router, w_gu, b_gu,
           w_down, b_down):
    b_, t_, d = x.shape
    n_ctx = ctx.shape[1]
    depth = w_mod.shape[0]
    tm = 256

    sm = jax.nn.softmax(hg_lb.astype(F32), axis=1)
    lower = jnp.cumsum(sm, axis=1) - sm[:, :1]
    log_gamma = jax.nn.log_sigmoid(ret_decay.astype(F32)).reshape(depth, 2 * RET_HEADS)

    cc = jnp.concatenate([c, c_ctx[None, :], jnp.zeros((16 - b_ - 1, d), F32)], axis=0)
    modv = _mod_all(cc, w_mod, b_mod).reshape(depth, 16, 6, d)
    one = jnp.asarray([0.0, 1.0, 0.0, 0.0, 1.0, 0.0], F32)[None, None, :, None]
    modv = (modv + one)[:, :, jnp.asarray([1, 0, 2, 4, 3, 5])]
    modv = jnp.concatenate([modv, jnp.zeros((depth, 16, 2, d), F32)], axis=2)
    mod = jnp.stack([jnp.broadcast_to(modv[:, b_:b_ + 1], (depth, b_, 8, d)), modv[:, :b_]], axis=2)

    cos_r, sin_r = _rope_tables(n_ctx, t_, RET_DK)
    cos_g, sin_g = _rope_tables(n_ctx, t_, GQA_HD)

    xs = jnp.concatenate([ctx, x], axis=1)
    tm_in = 768 if xs.shape[1] % 768 == 0 else tm
    w_in_b = _w_in_prep(w_in)
    for l in range(depth):
        z = _in_proj(xs, mod[l], w_in_b[l], n_ctx, tm_in, PROJ_TOTAL // PROJ_TILES)
        ret_o = _retention(z, log_gamma[l], cos_r, sin_r, ret_gn_g[l], ret_gn_b[l], n_ctx)
        hg_o = _hgrn2(z, lower[:, l], hg_norm_g[l], n_ctx)
        na_o = _neighbourhood(z, _na_bias_table(na_rpb[l]), n_ctx)
        gq_o = _gqa(z, cos_g, sin_g, gq_qn_g[l], gq_kn_g[l], n_ctx, tm)
        wr_hi, wr_lo = _split_bf16(w_router[l].T)
        xs, h2p, top_e, top_g, rank, cnt = _merge(z, (ret_o, hg_o, na_o, gq_o), xs, mod[l],
                                                  w_branch[l].astype(BF16), w_out[l].astype(BF16),
                                                  ln_g[l, 0], ln_b[l, 0], wr_hi, wr_lo, b_router[l], n_ctx, tm)
        yk, gate = _moe(h2p, top_e, top_g, rank, cnt, l, w_gu, b_gu, w_down, b_down)
        xs = _combine(yk, gate, xs, mod[l], ln_g[l, 1], ln_b[l, 1], n_ctx, tm_in)
    return xs[:, n_ctx:]
```

```python
import functools

import jax
import jax.numpy as jnp
from jax import lax
import numpy as np
from jax.experimental import pallas as pl
from jax.experimental.pallas import tpu as pltpu
from jax.experimental.pallas import tpu_sc as plsc

D_MODEL = 1024
DEPTH = 4
GRID_W = 64
N_BRANCH = 4
BRANCH_W = D_MODEL // 2
RET_HEADS = 4
RET_DK = BRANCH_W // RET_HEADS
HG_HEADS = 4
HG_DK = BRANCH_W // HG_HEADS
NA_HEADS = 8
NA_HD = BRANCH_W // NA_HEADS
NA_KH = 8
NA_KW = 16
NA_UNION = 10
GQA_HEADS = 8
GQA_KV_HEADS = 2
GQA_HD = BRANCH_W // GQA_HEADS
GQA_KV_W = GQA_KV_HEADS * GQA_HD
ROPE_BASE = 10000.0
N_EXPERTS = 32
TOP_K = 4
D_FF_EXPERT = D_MODEL
SWIGLU_LIMIT = 7.0
SWIGLU_ALPHA = 1.702
MOE_BLOCK = 256
LN_EPS = 1e-5
NORM_EPS = 1e-6
NEG_INF = -1e30
DN_ALPHA = (2 * DEPTH) ** 0.25

LANES = 128
BF16 = jnp.bfloat16
F32 = jnp.float32
VMEM_LIMIT = 56 * 1024 * 1024
SC_CORES = 2
SC_SUBCORES = 16
SC_WINDOW = 128
SC_ROW_WORDS = 256
SC_PIECES = D_MODEL // 2 // SC_ROW_WORDS

PROJ_ORIG = 4 * BRANCH_W + 5 * BRANCH_W + 3 * BRANCH_W + BRANCH_W + 2 * GQA_KV_W
PROJ_TOTAL = PROJ_ORIG + N_BRANCH * D_MODEL
PROJ_TILES = 2
RET_B = (N_BRANCH * D_MODEL) // LANES
HG_B = RET_B + 16
NA_B = HG_B + 20
GQ_B = NA_B + 12

RET_CHUNK = 256
HG_CHUNK = 128
LOG2E = 1.4426950408889634
HG_LEVELS = (64, 32, 16, 8, 4, 2, 1, 0)


def _cparams(sem):
    return pltpu.CompilerParams(dimension_semantics=sem, vmem_limit_bytes=VMEM_LIMIT)


def _dot(a, b):
    return jnp.dot(a, b, preferred_element_type=F32)


def _dot_nt(a, b):
    return lax.dot_general(a, b, (((1,), (1,)), ((), ())), preferred_element_type=F32)


def _dot_tn(a, b):
    return lax.dot_general(a, b, (((0,), (0,)), ((), ())), preferred_element_type=F32)


def _split_bf16(x):
    hi = x.astype(BF16)
    lo = (x - hi.astype(F32)).astype(BF16)
    return hi, lo


def _sigmoid(x):
    return 1.0 / (1.0 + jnp.exp(-x))


def _silu(x):
    return x * _sigmoid(x)


def _mod_kernel(c_ref, w_ref, b_ref, o_ref):
    s = _silu(c_ref[...]).astype(BF16)
    o_ref[0] = _dot(s, w_ref[0].astype(BF16)) + b_ref[0]


def _mod_all(cc, w_mod, b_mod):
    depth, d, n = w_mod.shape
    r = cc.shape[0]
    tn = 1536
    return pl.pallas_call(
        _mod_kernel,
        out_shape=jax.ShapeDtypeStruct((depth, r, n), F32),
        grid=(depth, n // tn),
        in_specs=[pl.BlockSpec((r, d), lambda l, j: (0, 0)),
                  pl.BlockSpec((1, d, tn), lambda l, j: (l, 0, j)),
                  pl.BlockSpec((1, 1, tn), lambda l, j: (l, 0, j))],
        out_specs=pl.BlockSpec((1, r, tn), lambda l, j: (l, 0, j)),
        compiler_params=_cparams(("arbitrary", "arbitrary")),
        name="adaln_mod",
    )(cc, w_mod, b_mod.reshape(depth, 1, n))


W_PREP_TILE = 256


def _w_in_prep_kernel(w_ref, o_ref):
    o_ref[0] = w_ref[0].astype(BF16)


def _w_in_prep(w_in):
    depth, d, total = w_in.shape
    n_src = total // W_PREP_TILE
    shift = PROJ_ORIG // W_PREP_TILE
    return pl.pallas_call(
        _w_in_prep_kernel,
        out_shape=jax.ShapeDtypeStruct((depth, d, total), BF16),
        grid=(depth, n_src),
        in_specs=[pl.BlockSpec((1, d, W_PREP_TILE), lambda l, j: (l, 0, (j + shift) % n_src))],
        out_specs=pl.BlockSpec((1, d, W_PREP_TILE), lambda l, j: (l, 0, j)),
        compiler_params=_cparams(("arbitrary", "arbitrary")),
        name="w_in_prep",
    )(w_in)


def _in_proj_kernel(x_ref, mod_ref, w_ref, o_ref, *, n_ctx, tm):
    i = pl.program_id(2)
    x = x_ref[0]
    mc = mod_ref[0, 0]
    ml = mod_ref[0, 1]
    rows = i * tm + lax.broadcasted_iota(jnp.int32, (tm, 1), 0)
    is_ctx = rows < n_ctx
    scale = jnp.where(is_ctx, mc[0:1, :], ml[0:1, :])
    shift = jnp.where(is_ctx, mc[1:2, :], ml[1:2, :])
    h = (x * scale + shift).astype(BF16)
    o_ref[0] = _dot(h, w_ref[...]).astype(o_ref.dtype)


def _in_proj(xs, mod, w, n_ctx, tm, tn):
    b, n, d = xs.shape
    ncol = w.shape[1]
    return pl.pallas_call(
        functools.partial(_in_proj_kernel, n_ctx=n_ctx, tm=tm),
        out_shape=jax.ShapeDtypeStruct((b, n, ncol), BF16),
        grid=(ncol // tn, b, n // tm),
        in_specs=[pl.BlockSpec((1, tm, d), lambda j, bb, i: (bb, i, 0)),
                  pl.BlockSpec((1, 2, 8, d), lambda j, bb, i: (bb, 0, 0, 0)),
                  pl.BlockSpec((d, tn), lambda j, bb, i: (0, j))],
        out_specs=pl.BlockSpec((1, tm, tn), lambda j, bb, i: (bb, i, j)),
        compiler_params=_cparams(("arbitrary", "arbitrary", "arbitrary")),
        name="in_proj",
    )(xs, mod, w)


def _rope_tables(n_ctx, t, head_dim):
    idx = jnp.arange(t, dtype=jnp.int32)
    row = (idx // GRID_W).astype(F32)
    col = (idx % GRID_W).astype(F32)
    n = head_dim // 2
    inv = ROPE_BASE ** (-jnp.arange(0, n, 2, dtype=F32) / n)

    def half(pos):
        ang = pos[:, None] * inv[None, :]
        c, s = jnp.cos(ang), jnp.sin(ang)
        return jnp.concatenate([c, c], -1), jnp.concatenate([-s, s], -1)

    cr, sr = half(row)
    cc, sc = half(col)
    cos = jnp.concatenate([cr, cc], -1)
    sin = jnp.concatenate([sr, sc], -1)
    reps = LANES // head_dim
    cos = jnp.tile(cos, (1, reps))
    sin = jnp.tile(sin, (1, reps))
    cos = jnp.concatenate([jnp.ones((n_ctx, LANES), F32), cos], 0)
    sin = jnp.concatenate([jnp.zeros((n_ctx, LANES), F32), sin], 0)
    return cos, sin


def _rope(x, cos, sin, quarter):
    lane = lax.broadcasted_iota(jnp.int32, x.shape, 1)
    first = (lane & (2 * quarter - 1)) < quarter
    rot = jnp.where(first, pltpu.roll(x, LANES - quarter, 1), pltpu.roll(x, quarter, 1))
    return x * cos + rot * sin


def _ret_kernel(lg_ref, q_ref, k_ref, v_ref, g_ref, cos_ref, sin_ref, gng_ref, gnb_ref, o_ref,
                ks_ref, sb_ref, *, n_ctx, chunk):
    hh = pl.program_id(1)
    lgf = lg_ref[hh]
    lgb = lg_ref[RET_HEADS + hh]
    n = q_ref.shape[1]
    c_ = chunk
    nch = n // c_
    n_cc = n_ctx // c_
    dk = q_ref.shape[2]

    ri = lax.broadcasted_iota(jnp.int32, (c_, dk), 0).astype(F32)
    df = jnp.exp((ri + 1.0) * lgf)
    db = jnp.exp((c_ - ri) * lgb)
    ef = jnp.exp((c_ - 1.0 - ri) * lgf)
    eb = jnp.exp(ri * lgb)
    gfc = jnp.exp(c_ * lgf)
    gbc = jnp.exp(c_ * lgb)
    ii = lax.broadcasted_iota(jnp.int32, (c_, c_), 0)
    jj = lax.broadcasted_iota(jnp.int32, (c_, c_), 1)
    dij = (ii - jj).astype(F32)
    w = jnp.where(ii >= jj, jnp.exp(dij * lgf), jnp.exp(-dij * lgb))

    def rope_k(c, carry):
        r0 = pl.multiple_of(c * c_, c_)
        kc = k_ref[0, pl.ds(r0, c_), :].astype(F32) * (RET_DK ** -0.5)
        kc = _rope(kc, cos_ref[pl.ds(r0, c_), :], sin_ref[pl.ds(r0, c_), :], 32)
        ks_ref[pl.ds(r0, c_), :] = kc.astype(BF16)
        return carry

    lax.fori_loop(0, nch, rope_k, 0, unroll=3)

    def bwd_step(c, s):
        r0 = pl.multiple_of(c * c_, c_)
        sb_ref[c] = s
        kc = (ks_ref[pl.ds(r0, c_), :].astype(F32) * eb).astype(BF16)
        return s * gbc + _dot_tn(kc, v_ref[0, pl.ds(r0, c_), :])

    s = jnp.zeros((dk, dk), F32)
    s = lax.fori_loop(0, n_cc, lambda t, s_: bwd_step(n_cc - 1 - t, s_), s)
    lax.fori_loop(0, nch - n_cc, lambda t, s_: bwd_step(nch - 1 - t, s_), s, unroll=4)

    gng = gng_ref[...]
    gnb = gnb_ref[...]

    def fwd_step(c, sf):
        r0 = pl.multiple_of(c * c_, c_)
        qc = _rope(q_ref[0, pl.ds(r0, c_), :].astype(F32), cos_ref[pl.ds(r0, c_), :],
                   sin_ref[pl.ds(r0, c_), :], 32)
        kc = ks_ref[pl.ds(r0, c_), :]
        vc = v_ref[0, pl.ds(r0, c_), :]
        a = _dot_nt(qc.astype(BF16), kc) * w
        qi = jnp.concatenate([qc * df, qc * db], axis=1).astype(BF16)
        si = jnp.concatenate([sf, sb_ref[c]], axis=0).astype(BF16)
        o = _dot(a.astype(BF16), vc) + _dot(qi, si)
        sf_new = sf * gfc + _dot_tn((kc.astype(F32) * ef).astype(BF16), vc)
        mu = jnp.mean(o, axis=-1, keepdims=True)
        oc = o - mu
        var = jnp.mean(oc * oc, axis=-1, keepdims=True)
        y = oc * lax.rsqrt(var + LN_EPS) * gng + gnb
        gate = g_ref[0, pl.ds(r0, c_), :].astype(F32)
        o_ref[0, pl.ds(r0, c_), :] = (y * _silu(gate)).astype(o_ref.dtype)
        return sf_new

    lax.fori_loop(0, nch, fwd_step, jnp.zeros((dk, dk), F32), unroll=3)


def _retention(z, log_gamma, cos, sin, gn_g, gn_b, n_ctx):
    b, n, _ = z.shape
    blk = lambda off: pl.BlockSpec((1, n, LANES), lambda bb, h, lg: (bb, 0, off + h))
    grid_spec = pltpu.PrefetchScalarGridSpec(
        num_scalar_prefetch=1,
        grid=(b, RET_HEADS),
        in_specs=[blk(RET_B), blk(RET_B + 4), blk(RET_B + 8), blk(RET_B + 12),
                  pl.BlockSpec((n, LANES), lambda bb, h, lg: (0, 0)),
                  pl.BlockSpec((n, LANES), lambda bb, h, lg: (0, 0)),
                  pl.BlockSpec((1, LANES), lambda bb, h, lg: (0, h)),
                  pl.BlockSpec((1, LANES), lambda bb, h, lg: (0, h))],
        out_specs=pl.BlockSpec((1, n, LANES), lambda bb, h, lg: (bb, 0, h)),
        scratch_shapes=[pltpu.VMEM((n, LANES), BF16),
                        pltpu.VMEM((n // RET_CHUNK, RET_DK, RET_DK), F32)],
    )
    return pl.pallas_call(
        functools.partial(_ret_kernel, n_ctx=n_ctx, chunk=RET_CHUNK),
        out_shape=jax.ShapeDtypeStruct((b, n, BRANCH_W), BF16),
        grid_spec=grid_spec,
        compiler_params=_cparams(("arbitrary", "arbitrary")),
        name="retention",
    )(log_gamma, z, z, z, z, cos, sin, gn_g.reshape(1, -1), gn_b.reshape(1, -1))


def _hg_constants():
    c = HG_CHUNK
    r = np.arange(c)[:, None]
    t = np.arange(c)[None, :]
    mats = []
    level = np.full((2, c, c), len(HG_LEVELS), np.int32)
    for d in range(2):
        blocks = []
        for li, s in enumerate(HG_LEVELS):
            if s == 0:
                level[d][np.arange(c), np.arange(c)] = li
                continue
            base = (r // (2 * s)) * (2 * s)
            if d == 0:
                m = base + s - 1
                mat = (t > np.minimum(r, m)) & (t <= np.maximum(r, m))
                q_side = (r % (2 * s)) >= s
            else:
                m = base + s
                mat = (t >= np.minimum(r, m)) & (t < np.maximum(r, m))
                q_side = (r % (2 * s)) < s
            blocks.append(mat.astype(np.float32))
            same = (r // (2 * s)) == (t // (2 * s))
            k_side_t = ((t % (2 * s)) < s) if d == 0 else ((t % (2 * s)) >= s)
            level[d][same & q_side & k_side_t] = li
        if d == 0:
            blocks.append((t <= r).astype(np.float32))
            blocks.append((t > r).astype(np.float32))
        else:
            blocks.append((t >= r).astype(np.float32))
            blocks.append((t < r).astype(np.float32))
        mats.append(np.concatenate(blocks, 0))
    mats = np.stack(mats)
    masks = (level[:, None] == np.arange(len(HG_LEVELS))[None, :, None, None]).astype(np.float32)
    upper = np.stack([np.broadcast_to(((np.arange(c) % (2 * s)) >= s)[:, None], (c, c)) for s in HG_LEVELS if s > 0])
    return np.concatenate([mats, mats], axis=2), np.maximum(masks[0], masks[1]), upper.astype(np.float32)


def _hg_gates(zf, lb):
    en = jnp.exp(-jnp.abs(zf))
    inv = 1.0 / (1.0 + en)
    pos = zf >= 0
    sg = jnp.where(pos, inv, en * inv)
    sgn = jnp.where(pos, en * inv, inv)
    one_m = 1.0 - lb
    return jnp.log(lb + one_m * sg), one_m * sgn


def _hg_kernel(q_ref, ff_ref, fb_ref, v_ref, g_ref, lb_ref, ng_ref, m_ref, lv_ref, up_ref, o_ref,
               lfb_ref, kb_ref, sb_ref, ktb_ref, dtot_ref, *, n_ctx):
    c_ = HG_CHUNK
    n = q_ref.shape[1]
    nch = n // c_
    n_cc = n_ctx // c_
    dk = q_ref.shape[2]
    nl = len(HG_LEVELS)
    lbf = lb_ref[0:1, :]
    lbb = lb_ref[1:2, :]

    def mdot(mat, lf):
        hi, lo = _split_bf16(lf)
        return _dot(mat, jnp.concatenate([hi, lo], axis=0))

    def prep_b(c, carry):
        r0 = pl.multiple_of(c * c_, c_)
        lf, key = _hg_gates(fb_ref[0, pl.ds(r0, c_), :].astype(F32), lbb)
        lf = lf * LOG2E
        lfb_ref[pl.ds(r0, c_), :] = lf
        kb_ref[pl.ds(r0, c_), :] = key
        e = mdot(m_ref[1, (nl - 1) * c_:(nl + 1) * c_, :], lf)
        ktb_ref[pl.ds(r0, c_), :] = (key * jnp.exp2(e[c_:2 * c_, :])).astype(BF16)
        dtot_ref[c] = jnp.broadcast_to(jnp.exp2(e[0:1, :]), (8, dk))
        return carry

    lax.fori_loop(0, nch, prep_b, 0, unroll=3)

    def bwd_step(c, st):
        r0 = pl.multiple_of(c * c_, c_)
        sb_ref[c] = st
        return st * dtot_ref[c][0:1, :] + _dot_tn(v_ref[0, pl.ds(r0, c_), :], ktb_ref[pl.ds(r0, c_), :])

    st = jnp.zeros((dk, dk), F32)
    st = lax.fori_loop(0, n_cc, lambda t, s_: bwd_step(n_cc - 1 - t, s_), st, unroll=2)
    lax.fori_loop(0, nch - n_cc, lambda t, s_: bwd_step(nch - 1 - t, s_), st, unroll=8)

    ng = ng_ref[...]

    def mdot_pair(mat, lf0, lf1):
        hi0, lo0 = _split_bf16(lf0)
        hi1, lo1 = _split_bf16(lf1)
        rhs = jnp.concatenate([jnp.concatenate([hi0, hi1], axis=1), jnp.concatenate([lo0, lo1], axis=1)], axis=0)
        return _dot(mat, rhs)

    def intra(qs, keyf, keyb, ef, eb):
        a = jnp.zeros((c_, c_), F32)
        qsb, kfb, kbb = qs.astype(BF16), keyf.astype(BF16), keyb.astype(BF16)
        for li, s in enumerate(HG_LEVELS):
            if s == 0:
                qt = jnp.concatenate([qsb, qsb], axis=1)
                kt = jnp.concatenate([kfb, kbb], axis=1)
            else:
                decf = jnp.exp2(ef[li * c_:(li + 1) * c_, :])
                decb = jnp.exp2(eb[li * c_:(li + 1) * c_, :])
                up = up_ref[li]
                dqf = decf * up
                dkb = decb * up
                qt = jnp.concatenate([qsb * dqf.astype(BF16), qsb * (decb - dkb).astype(BF16)], axis=1)
                kt = jnp.concatenate([kfb * (decf - dqf).astype(BF16), kbb * dkb.astype(BF16)], axis=1)
            a = a + _dot_nt(qt, kt) * lv_ref[li]
        cumf = ef[(nl - 1) * c_:nl * c_, :]
        restf = ef[nl * c_:(nl + 1) * c_, :]
        cumb = eb[(nl - 1) * c_:nl * c_, :]
        return a, cumf, restf, cumb

    def fwd_chunk(c, sf, qs, keyf, ef, eb):
        r0 = pl.multiple_of(c * c_, c_)
        vc = v_ref[0, pl.ds(r0, c_), :]
        a, cumf, restf, cumb = intra(qs, keyf, kb_ref[pl.ds(r0, c_), :], ef, eb)
        qi = jnp.concatenate([qs * jnp.exp2(cumf), qs * jnp.exp2(cumb)], axis=1).astype(BF16)
        si = jnp.concatenate([sf, sb_ref[c]], axis=1).astype(BF16)
        o = _dot(a.astype(BF16), vc) + _dot_nt(qi, si)
        totf = cumf[c_ - 1:c_, :]
        sf_new = sf * jnp.exp2(totf) + _dot_tn(vc, (keyf * jnp.exp2(restf)).astype(BF16))
        on = o * lax.rsqrt(jnp.mean(o * o, axis=-1, keepdims=True) + NORM_EPS) * ng
        gate = g_ref[0, pl.ds(r0, c_), :].astype(F32)
        o_ref[0, pl.ds(r0, c_), :] = (on * _silu(gate)).astype(o_ref.dtype)
        return sf_new

    def fwd_pair(p, sf):
        r0 = pl.multiple_of(p * (2 * c_), 2 * c_)
        qs, keyf, lff = [], [], []
        for t in range(2):
            rows = pl.ds(r0 + t * c_, c_)
            qs.append(_silu(q_ref[0, rows, :].astype(F32)))
            lf, key = _hg_gates(ff_ref[0, rows, :].astype(F32), lbf)
            lff.append(lf * LOG2E)
            keyf.append(key)
        ef = mdot_pair(m_ref[0], lff[0], lff[1])
        eb = mdot_pair(m_ref[1], lfb_ref[pl.ds(r0, c_), :], lfb_ref[pl.ds(r0 + c_, c_), :])
        for t in range(2):
            sf = fwd_chunk(2 * p + t, sf, qs[t], keyf[t], ef[:, t * dk:(t + 1) * dk], eb[:, t * dk:(t + 1) * dk])
        return sf

    lax.fori_loop(0, nch // 2, fwd_pair, jnp.zeros((dk, dk), F32))


def _hgrn2(z, lower, norm_g, n_ctx):
    b, n, _ = z.shape
    mats, level, upper = _hg_constants()
    mats = jnp.asarray(mats, BF16)
    level = jnp.asarray(level)
    upper = jnp.asarray(upper)
    blk = lambda off: pl.BlockSpec((1, n, LANES), lambda bb, h: (bb, 0, off + h))
    return pl.pallas_call(
        functools.partial(_hg_kernel, n_ctx=n_ctx),
        out_shape=jax.ShapeDtypeStruct((b, n, BRANCH_W), BF16),
        grid=(b, HG_HEADS),
        in_specs=[blk(HG_B), blk(HG_B + 4), blk(HG_B + 8), blk(HG_B + 12), blk(HG_B + 16),
                  pl.BlockSpec((2, LANES), lambda bb, h: (0, h)),
                  pl.BlockSpec((1, LANES), lambda bb, h: (0, h)),
                  pl.BlockSpec(mats.shape, lambda bb, h: (0, 0, 0)),
                  pl.BlockSpec(level.shape, lambda bb, h: (0, 0, 0)),
                  pl.BlockSpec(upper.shape, lambda bb, h: (0, 0, 0))],
        out_specs=pl.BlockSpec((1, n, LANES), lambda bb, h: (bb, 0, h)),
        scratch_shapes=[pltpu.VMEM((n, LANES), F32), pltpu.VMEM((n, LANES), F32),
                        pltpu.VMEM((n // HG_CHUNK, HG_DK, HG_DK), F32),
                        pltpu.VMEM((n, LANES), BF16), pltpu.VMEM((n // HG_CHUNK, 8, LANES), F32)],
        compiler_params=_cparams(("arbitrary", "arbitrary")),
        name="hgrn2",
    )(z, z, z, z, z, lower, norm_g.reshape(1, -1), mats, level, upper)


def _na_bias_table(rpb):
    qc = np.arange(GRID_W)[:, None]
    kc = np.arange(GRID_W)[None, :]
    col0 = np.clip(qc - NA_KW // 2, 0, GRID_W - NA_KW)
    valid = (kc >= col0) & (kc < col0 + NA_KW)
    dc = np.clip(kc - qc + (NA_KW - 1), 0, 2 * NA_KW - 2)
    pick = (dc[:, :, None] == np.arange(2 * NA_KW - 1)).astype(np.float32)
    t15 = jnp.einsum('hrc,qkc->hrqk', rpb.astype(F32), pick, precision=lax.Precision.HIGHEST)
    t = jnp.stack([t15[:, dl:dl + NA_KH] for dl in range(NA_KH)], axis=1)
    t = jnp.where(jnp.asarray(valid)[None, None, None], t * LOG2E, NEG_INF)
    t = t.transpose(0, 1, 3, 2, 4).reshape(rpb.shape[0] // 2, 2, NA_KH, GRID_W, NA_KH * GRID_W)
    t = t.transpose(0, 2, 1, 3, 4)
    offs = [jnp.pad(t, ((0, 0),) * 4 + ((o * GRID_W, (NA_UNION - NA_KH - o) * GRID_W),), constant_values=NEG_INF)
            for o in range(NA_UNION - NA_KH + 1)]
    return jnp.stack(offs, axis=2)


def _softmax_pv(parts):
    m = parts[0][0].max(axis=-1, keepdims=True)
    for s, _ in parts[1:]:
        m = jnp.maximum(m, s.max(axis=-1, keepdims=True))
    l = None
    o = None
    for s, v in parts:
        p = jnp.exp2(s - m)
        ls = p.sum(axis=-1, keepdims=True)
        os_ = _dot(p.astype(BF16), v)
        l = ls if l is None else l + ls
        o = os_ if o is None else o + os_
    return o / l


def _na_kernel(q_ref, k_ref, v_ref, tab_ref, o_ref, *, n_ctx):
    n = q_ref.shape[1]
    rows = (n - n_ctx) // GRID_W
    ulen = NA_UNION * GRID_W
    scale = NA_HD ** -0.5 * LOG2E
    kx = k_ref[0, 0:n_ctx, :]
    vx = v_ref[0, 0:n_ctx, :]

    def head_mask(shape, hh):
        lane = lax.broadcasted_iota(jnp.int32, shape, 1)
        return (lane < NA_HD) if hh == 0 else (lane >= NA_HD)

    def stack_heads(qf):
        return jnp.concatenate([jnp.where(head_mask(qf.shape, 0), qf, 0.0),
                                jnp.where(head_mask(qf.shape, 1), qf, 0.0)], axis=0).astype(BF16)

    def unstack_heads(o):
        m = o.shape[0] // 2
        return jnp.where(head_mask((m, LANES), 0), o[:m], o[m:])

    qx = stack_heads(q_ref[0, 0:n_ctx, :].astype(F32) * scale)
    o_ref[0, 0:n_ctx, :] = unstack_heads(_softmax_pv([(_dot_nt(qx, kx), vx)])).astype(o_ref.dtype)

    def pair_step(t, carry):
        r = 2 * t
        start_u = jnp.clip(r - NA_KH // 2, 0, rows - NA_UNION)
        q0 = pl.multiple_of(n_ctx + r * GRID_W, 2 * GRID_W)
        k0 = pl.multiple_of(n_ctx + start_u * GRID_W, GRID_W)
        qr = stack_heads(q_ref[0, pl.ds(q0, 2 * GRID_W), :].astype(F32) * scale)
        kl = k_ref[0, pl.ds(k0, ulen), :]
        vl = v_ref[0, pl.ds(k0, ulen), :]
        bias = []
        for hh in range(2):
            for i in range(2):
                start = jnp.clip(r + i - NA_KH // 2, 0, rows - NA_KH)
                bias.append(tab_ref[0, start - (r + i) + (NA_KH - 1), start - start_u, hh])
        s_loc = _dot_nt(qr, kl) + jnp.concatenate(bias, axis=0)
        s_ctx = _dot_nt(qr, kx)
        o = unstack_heads(_softmax_pv([(s_loc, vl), (s_ctx, vx)]))
        o_ref[0, pl.ds(q0, 2 * GRID_W), :] = o.astype(o_ref.dtype)
        return carry

    lax.fori_loop(0, rows // 2, pair_step, 0, unroll=2)


def _neighbourhood(z, table, n_ctx):
    b, n, _ = z.shape
    npair = NA_HEADS // 2
    blk = lambda off: pl.BlockSpec((1, n, LANES), lambda bb, p: (bb, 0, off + p))
    return pl.pallas_call(
        functools.partial(_na_kernel, n_ctx=n_ctx),
        out_shape=jax.ShapeDtypeStruct((b, n, BRANCH_W), BF16),
        grid=(b, npair),
        in_specs=[blk(NA_B), blk(NA_B + 4), blk(NA_B + 8),
                  pl.BlockSpec((1, NA_KH, NA_UNION - NA_KH + 1, 2, GRID_W, NA_UNION * GRID_W),
                               lambda bb, p: (p, 0, 0, 0, 0, 0))],
        out_specs=pl.BlockSpec((1, n, LANES), lambda bb, p: (bb, 0, p)),
        compiler_params=_cparams(("arbitrary", "arbitrary")),
        name="neighbourhood_attn",
    )(z, z, z, table)


def _gqa_kernel(q_ref, k_ref, v_ref, cos_ref, sin_ref, qg_ref, kg_ref, bd_ref, o_ref, ks_ref, vs_ref, *, n_ctx):
    i = pl.program_id(1)
    tq = q_ref.shape[1]
    n = k_ref.shape[1]
    grp = GQA_HEADS // GQA_KV_HEADS
    bd = bd_ref[...]

    def norm_rope(x, g, cos, sin):
        hi, lo = _split_bf16(x * x)
        ms = _dot(hi, bd) + _dot(lo, bd)
        xn = x * lax.rsqrt(ms + NORM_EPS) * g
        return _rope(xn, cos, sin, GQA_HD // 4)

    @pl.when(i == 0)
    def _():
        def prep(c, carry):
            rows = pl.ds(pl.multiple_of(c * tq, tq), tq)
            kn = norm_rope(k_ref[0, rows, :].astype(F32), kg_ref[...], cos_ref[rows, :], sin_ref[rows, :])
            vv = v_ref[0, rows, :].astype(F32)
            lo_half = lax.broadcasted_iota(jnp.int32, kn.shape, 1) < GQA_HD
            ksw = pltpu.roll(kn, GQA_HD, 1)
            vsw = pltpu.roll(vv, GQA_HD, 1)
            ks_ref[0, rows, :] = jnp.where(lo_half, kn, ksw).astype(BF16)
            ks_ref[1, rows, :] = jnp.where(lo_half, ksw, kn).astype(BF16)
            vs_ref[0, rows, 0:LANES] = jnp.where(lo_half, vv, vsw).astype(BF16)
            vs_ref[1, rows, 0:LANES] = jnp.where(lo_half, vsw, vv).astype(BF16)
            ones = jnp.ones((tq, LANES), BF16)
            vs_ref[0, rows, LANES:] = ones
            vs_ref[1, rows, LANES:] = ones
            return carry

        lax.fori_loop(0, n // tq, prep, 0)

    qrows = pl.ds(pl.multiple_of(i * tq, tq), tq)
    cos = cos_ref[qrows, :]
    sin = sin_ref[qrows, :]
    qg = qg_ref[...]

    def attend(nk):
        for j in range(BRANCH_W // LANES):
            x = q_ref[0, :, j * LANES:(j + 1) * LANES].astype(F32)
            qb = (norm_rope(x, qg, cos, sin) * (GQA_HD ** -0.5 * LOG2E)).astype(BF16)
            lane = lax.broadcasted_iota(jnp.int32, qb.shape, 1)
            g = (2 * j) // grp
            zero = jnp.zeros_like(qb)
            qs = jnp.concatenate([jnp.where(lane < GQA_HD, qb, zero), jnp.where(lane >= GQA_HD, qb, zero)], axis=0)
            sc = _dot_nt(qs, ks_ref[g, 0:nk, :])
            p = jnp.exp2(sc - sc.max(axis=-1, keepdims=True)).astype(BF16)
            ov = _dot(p, vs_ref[g, 0:nk, :])
            o = ov[:, :LANES] / ov[:, LANES:]
            o_ref[0, :, j * LANES:(j + 1) * LANES] = jnp.where(lane < GQA_HD, o[:tq], o[tq:]).astype(o_ref.dtype)

    @pl.when(i * tq < n_ctx)
    def _():
        attend(n_ctx)

    @pl.when(i * tq >= n_ctx)
    def _():
        attend(n)


def _gqa(z, cos, sin, qn_g, kn_g, n_ctx, tq):
    b, n, _ = z.shape
    bd = np.kron(np.eye(LANES // GQA_HD), np.ones((GQA_HD, GQA_HD))) / GQA_HD
    qg = jnp.tile(qn_g.reshape(1, -1), (1, LANES // GQA_HD))
    kg = jnp.tile(kn_g.reshape(1, -1), (1, LANES // GQA_HD))
    qb = (GQ_B * LANES) // BRANCH_W
    const = lambda shape: pl.BlockSpec(shape, lambda bb, i: (0,) * len(shape))
    return pl.pallas_call(
        functools.partial(_gqa_kernel, n_ctx=n_ctx),
        out_shape=jax.ShapeDtypeStruct((b, n, BRANCH_W), BF16),
        grid=(b, n // tq),
        in_specs=[pl.BlockSpec((1, tq, BRANCH_W), lambda bb, i: (bb, i, qb)),
                  pl.BlockSpec((1, n, LANES), lambda bb, i: (bb, 0, GQ_B + 4)),
                  pl.BlockSpec((1, n, LANES), lambda bb, i: (bb, 0, GQ_B + 5)),
                  const((n, LANES)), const((n, LANES)), const((1, LANES)), const((1, LANES)),
                  const((LANES, LANES))],
        out_specs=pl.BlockSpec((1, tq, BRANCH_W), lambda bb, i: (bb, i, 0)),
        scratch_shapes=[pltpu.VMEM((GQA_KV_HEADS, n, LANES), BF16),
                        pltpu.VMEM((GQA_KV_HEADS, n, 2 * LANES), BF16)],
        compiler_params=_cparams(("arbitrary", "arbitrary")),
        name="gqa_attn",
    )(z, z, z, cos, sin, qg, kg, jnp.asarray(bd, BF16))


def _layernorm_rows(v, g, b):
    mu = jnp.mean(v, axis=-1, keepdims=True)
    vc = v - mu
    var = jnp.mean(vc * vc, axis=-1, keepdims=True)
    return vc * lax.rsqrt(var + LN_EPS) * g + b


def _pack_bf16_pairs(v):
    w = v.shape[1] // 2
    bits = lax.bitcast_convert_type(v.astype(BF16).astype(F32), jnp.int32)
    return bits[:, w:] | lax.shift_right_logical(bits[:, :w], 16)


def _unpack_bf16_pairs(pieces):
    lo = [lax.bitcast_convert_type(lax.shift_left(p, 16), F32) for p in pieces]
    hi = [lax.bitcast_convert_type(p & jnp.int32(-65536), F32) for p in pieces]
    return jnp.concatenate(lo + hi, axis=1)


def _store_pieces(ref, lead, packed):
    for h in range(SC_PIECES):
        ref[(h,) + lead] = packed[:, h * SC_ROW_WORDS:(h + 1) * SC_ROW_WORDS]


def _merge_kernel(z_ref, r_ref, h_ref, a_ref, q_ref, x_ref, mod_ref, wb_ref, wo_ref, lng_ref, lnb_ref,
                  wrh_ref, wrl_ref, br_ref, tri_ref, xo_ref, h2_ref, te_ref, tg_ref, rk_ref, cnt_ref,
                  run_ref, *, n_ctx, tm):
    i = pl.program_id(1)

    @pl.when((pl.program_id(0) == 0) & (i == 0))
    def _():
        run_ref[...] = jnp.zeros_like(run_ref)

    is_ctx = i * tm < n_ctx
    mod = jnp.where(is_ctx, mod_ref[0, 0], mod_ref[0, 1])
    acc = None
    for nb, br in enumerate((r_ref, h_ref, a_ref, q_ref)):
        yb = _dot(br[0], wb_ref[nb])
        gt = _sigmoid(z_ref[0, :, nb * D_MODEL:(nb + 1) * D_MODEL].astype(F32))
        acc = gt * yb if acc is None else acc + gt * yb
    y = _dot(acc.astype(BF16), wo_ref[...])
    xn = _layernorm_rows(DN_ALPHA * x_ref[0] + mod[2:3, :] * y, lng_ref[...], lnb_ref[...])
    xo_ref[0] = xn
    h2 = xn * mod[3:4, :] + mod[4:5, :]
    h2_hi, h2_lo = _split_bf16(h2)
    _store_pieces(h2_ref, (0,), _pack_bf16_pairs(h2))
    wrh = wrh_ref[...]
    logit = _dot_nt(wrh, h2_hi) + _dot_nt(wrh, h2_lo) + _dot_nt(wrl_ref[...], h2_hi) + br_ref[...]
    eidx = lax.broadcasted_iota(jnp.int32, logit.shape, 0)
    vals, idxs = [], []
    for _ in range(TOP_K):
        m = logit.max(axis=0, keepdims=True)
        sel = jnp.where(logit == m, eidx, N_EXPERTS).min(axis=0, keepdims=True)
        vals.append(m)
        idxs.append(sel)
        logit = jnp.where(eidx == sel, -jnp.inf, logit)
    ex = [jnp.exp(v - vals[0]) for v in vals]
    tot = ex[0] + ex[1] + ex[2] + ex[3]
    te_ref[0] = jnp.concatenate(idxs, axis=0)
    tg_ref[0] = jnp.concatenate([e / tot for e in ex], axis=0)
    base = run_ref[:, 0:1]
    tri = tri_ref[...]
    ranks = []
    for sel in idxs:
        hit = eidx == sel
        ones = jnp.where(hit, 1.0, 0.0)
        pre = _dot(ones.astype(BF16), tri)
        ranks.append(jnp.sum(jnp.where(hit, pre + base, 0.0), axis=0, keepdims=True))
        base = base + jnp.sum(ones, axis=1, keepdims=True)
    rk_ref[0] = jnp.concatenate(ranks, axis=0).astype(jnp.int32)
    run_ref[...] = jnp.broadcast_to(base, run_ref.shape)
    cnt_ref[...] = run_ref[...]


def _merge(z, outs, xs, mod, wb, wo, ln_g, ln_b, wr_hi, wr_lo, b_r, n_ctx, tm):
    b, n, d = xs.shape
    row = lambda w: pl.BlockSpec((1, tm, w), lambda bb, i: (bb, i, 0))
    full = lambda a: pl.BlockSpec(a.shape, lambda bb, i: (0,) * a.ndim)
    topk = pl.BlockSpec((1, TOP_K, tm), lambda bb, i: (bb, 0, i))
    ln_g = ln_g.reshape(1, d)
    ln_b = ln_b.reshape(1, d)
    b_r = b_r.reshape(N_EXPERTS, 1)
    tri = jnp.asarray(np.triu(np.ones((tm, tm), np.float32), 1), BF16)
    return pl.pallas_call(
        functools.partial(_merge_kernel, n_ctx=n_ctx, tm=tm),
        out_shape=(jax.ShapeDtypeStruct((b, n, d), F32),
                   jax.ShapeDtypeStruct((SC_PIECES, b, n, SC_ROW_WORDS), jnp.int32),
                   jax.ShapeDtypeStruct((b, TOP_K, n), jnp.int32),
                   jax.ShapeDtypeStruct((b, TOP_K, n), F32),
                   jax.ShapeDtypeStruct((b, TOP_K, n), jnp.int32),
                   jax.ShapeDtypeStruct((N_EXPERTS, LANES), F32)),
        grid=(b, n // tm),
        in_specs=[row(N_BRANCH * D_MODEL), row(BRANCH_W), row(BRANCH_W), row(BRANCH_W), row(BRANCH_W),
                  row(d),
                  pl.BlockSpec((1, 2, 8, d), lambda bb, i: (bb, 0, 0, 0)),
                  full(wb), full(wo), full(ln_g), full(ln_b), full(wr_hi), full(wr_lo), full(b_r), full(tri)],
        out_specs=(row(d), pl.BlockSpec((SC_PIECES, 1, tm, SC_ROW_WORDS), lambda bb, i: (0, bb, i, 0)),
                   topk, topk, topk,
                   pl.BlockSpec((N_EXPERTS, LANES), lambda bb, i: (0, 0))),
        scratch_shapes=[pltpu.VMEM((N_EXPERTS, LANES), F32)],
        compiler_params=_cparams(("arbitrary", "arbitrary")),
        name="merge_ln_router",
    )(z, *outs, xs, mod, wb, wo, ln_g, ln_b, wr_hi, wr_lo, b_r, tri)


def _ffn_kernel(be_ref, nv_ref, nw_ref, nx_ref, x_ref, wgu_hbm, bgu_ref, wd_hbm, bd_ref, o_ref,
                wgu_s, wd_s, gu_stage, d_stage, sems, *, layer):
    i = pl.program_id(0)
    nv = nv_ref[i]

    def weight_copies(e):
        return (pltpu.make_async_copy(wgu_hbm.at[layer, e], gu_stage, sems.at[0]),
                pltpu.make_async_copy(wd_hbm.at[layer, e], d_stage, sems.at[1]))

    @pl.when(nw_ref[i] > 0)
    def _():
        @pl.when(i == 0)
        def _():
            for cp in weight_copies(be_ref[0]):
                cp.start()

        for cp in weight_copies(be_ref[i]):
            cp.wait()
        wgu_s[...] = gu_stage[...].astype(BF16)
        wd_s[...] = d_stage[...].astype(BF16)

        @pl.when(nx_ref[i] >= 0)
        def _():
            for cp in weight_copies(nx_ref[i]):
                cp.start()

    @pl.when(nv > 0)
    def _():
        rows = lax.broadcasted_iota(jnp.int32, (MOE_BLOCK, 1), 0)
        x = _unpack_bf16_pairs([x_ref[h] for h in range(SC_PIECES)])
        x = jnp.where(rows < nv, x, 0.0).astype(BF16)
        gu = _dot(x, wgu_s[...]) + bgu_ref[0, 0]
        g = jnp.minimum(gu[:, :D_FF_EXPERT], SWIGLU_LIMIT)
        u = jnp.clip(gu[:, D_FF_EXPERT:], -SWIGLU_LIMIT, SWIGLU_LIMIT)
        act = g * _sigmoid(SWIGLU_ALPHA * g) * (u + 1.0)
        _store_pieces(o_ref, (), _pack_bf16_pairs(_dot(act.astype(BF16), wd_s[...]) + bd_ref[0, 0]))

    @pl.when(nv <= 0)
    def _():
        o_ref[...] = jnp.zeros_like(o_ref)


def _expert_ffn(xb, block_e, n_valid, layer, wgu, bgu, wd, bd):
    _, n_slots, _ = xb.shape
    n_blocks = n_slots // MOE_BLOCK
    depth, ne, d, f2 = wgu.shape
    new_w = jnp.concatenate([jnp.ones((1,), jnp.int32), (block_e[1:] != block_e[:-1]).astype(jnp.int32)])
    starts = jnp.where(new_w > 0, jnp.arange(n_blocks, dtype=jnp.int32), n_blocks)
    next_start = jnp.concatenate([lax.cummin(starts, reverse=True)[1:], jnp.full((1,), n_blocks, jnp.int32)])
    next_e = jnp.where(next_start < n_blocks, block_e[jnp.minimum(next_start, n_blocks - 1)], -1).astype(jnp.int32)
    slot_blk = pl.BlockSpec((SC_PIECES, MOE_BLOCK, SC_ROW_WORDS), lambda i, be, nv, nw, nx: (0, i, 0))
    bblk = lambda c: pl.BlockSpec((1, 1, 1, c), lambda i, be, nv, nw, nx: (layer, be[i], 0, 0))
    grid_spec = pltpu.PrefetchScalarGridSpec(
        num_scalar_prefetch=4,
        grid=(n_blocks,),
        in_specs=[slot_blk, pl.BlockSpec(memory_space=pl.ANY), bblk(f2),
                  pl.BlockSpec(memory_space=pl.ANY), bblk(d)],
        out_specs=slot_blk,
        scratch_shapes=[pltpu.VMEM((d, f2), BF16), pltpu.VMEM((f2 // 2, d), BF16),
                        pltpu.VMEM((d, f2), F32), pltpu.VMEM((f2 // 2, d), F32),
                        pltpu.SemaphoreType.DMA((2,))],
    )
    return pl.pallas_call(
        functools.partial(_ffn_kernel, layer=layer),
        out_shape=jax.ShapeDtypeStruct(xb.shape, jnp.int32),
        grid_spec=grid_spec,
        compiler_params=_cparams(("arbitrary",)),
        name="expert_ffn",
    )(block_e, n_valid, new_w, next_e, xb, wgu, bgu.reshape(depth, ne, 1, f2), wd, bd.reshape(depth, ne, 1, d))


def _combine_kernel(y_ref, g_ref, x_ref, mod_ref, lng_ref, lnb_ref, xo_ref, *, n_ctx, tm):
    i = pl.program_id(1)
    rows = i * tm + lax.broadcasted_iota(jnp.int32, (tm, 1), 0)
    g2 = jnp.where(rows < n_ctx, mod_ref[0, 0, 5:6, :], mod_ref[0, 1, 5:6, :])
    gate = g_ref[...]
    y = None
    for k in range(TOP_K):
        t = gate[:, k:k + 1] * _unpack_bf16_pairs([y_ref[k, h] for h in range(SC_PIECES)])
        y = t if y is None else y + t
    xo_ref[0] = _layernorm_rows(DN_ALPHA * x_ref[0] + g2 * y, lng_ref[...], lnb_ref[...])


def _combine(yk, gate, xs, mod, ln_g, ln_b, n_ctx, tm):
    b, n, d = xs.shape
    nt = n // tm
    return pl.pallas_call(
        functools.partial(_combine_kernel, n_ctx=n_ctx, tm=tm),
        out_shape=jax.ShapeDtypeStruct((b, n, d), F32),
        grid=(b, nt),
        in_specs=[pl.BlockSpec((TOP_K, SC_PIECES, tm, SC_ROW_WORDS), lambda bb, i: (0, 0, bb * nt + i, 0)),
                  pl.BlockSpec((tm, TOP_K), lambda bb, i: (bb * nt + i, 0)),
                  pl.BlockSpec((1, tm, d), lambda bb, i: (bb, i, 0)),
                  pl.BlockSpec((1, 2, 8, d), lambda bb, i: (bb, 0, 0, 0)),
                  pl.BlockSpec((1, d), lambda bb, i: (0, 0)),
                  pl.BlockSpec((1, d), lambda bb, i: (0, 0))],
        out_specs=pl.BlockSpec((1, tm, d), lambda bb, i: (bb, i, 0)),
        compiler_params=_cparams(("arbitrary", "arbitrary")),
        name="moe_combine_ln",
    )(yk, gate, xs, mod, ln_g.reshape(1, d), ln_b.reshape(1, d))


def _sc_mesh():
    return plsc.VectorSubcoreMesh(core_axis_name="core", subcore_axis_name="subcore",
                                  num_cores=SC_CORES, num_subcores=SC_SUBCORES)


def _sc_piece_index(idx, rows):
    return jnp.concatenate([idx + h * rows for h in range(SC_PIECES)], axis=1)


def _sc_scatter_rows(x, dest, n_out):
    p, n, w = x.shape
    x2 = x.reshape(p * n, w)
    dest2 = _sc_piece_index(dest, n_out)
    kk, n2 = dest2.shape

    @functools.partial(pl.kernel, out_type=jax.ShapeDtypeStruct((p * n_out, w), x.dtype),
                       mesh=_sc_mesh(), scratch_types=[], name="moe_dispatch_scatter")
    def scatter(x_hbm, i_hbm, o_hbm):
        def body(x_vmem, i_vmem):
            for j in range(kk):
                pltpu.sync_copy(x_vmem, o_hbm.at[i_vmem.at[j]])

        pltpu.emit_pipeline(
            body, grid=(n2 // SC_WINDOW,),
            in_specs=[pl.BlockSpec((SC_WINDOW, SC_ROW_WORDS), lambda i: (i, 0)),
                      pl.BlockSpec((kk, SC_WINDOW), lambda i: (0, i))],
            out_specs=[], core_axis_name=("core", "subcore"),
            dimension_semantics=(pltpu.PARALLEL,))(x_hbm, i_hbm)

    return scatter(x2, dest2).reshape(p, n_out, w)


def _sc_gather_rows(table, idx):
    p, v, w = table.shape
    kk, n = idx.shape
    t2 = table.reshape(p * v, w)
    idx2 = _sc_piece_index(idx, v)
    m = kk * n * p

    @functools.partial(pl.kernel, out_type=jax.ShapeDtypeStruct((m, SC_ROW_WORDS), table.dtype),
                       mesh=_sc_mesh(), scratch_types=[], name="moe_combine_gather")
    def gather(t_hbm, i_hbm, o_hbm):
        def body(i_vmem, o_vmem):
            pltpu.sync_copy(t_hbm.at[i_vmem.at[0]], o_vmem)

        pltpu.emit_pipeline(
            body, grid=(m // SC_WINDOW,),
            in_specs=[pl.BlockSpec((1, SC_WINDOW), lambda i: (0, i))],
            out_specs=[pl.BlockSpec((SC_WINDOW, SC_ROW_WORDS), lambda i: (i, 0))],
            core_axis_name=("core", "subcore"),
            dimension_semantics=(pltpu.PARALLEL,))(i_hbm, o_hbm)

    return gather(t2, idx2.reshape(1, m)).reshape(kk, p, n, w)


def _moe(h2p, top_e, top_g, rank, cnt, layer, wgu, bgu, wd, bd):
    p, b, n, w = h2p.shape
    n_tok = b * n
    nk = n_tok * TOP_K
    experts = jnp.arange(N_EXPERTS, dtype=jnp.int32)
    counts = cnt[:, 0].astype(jnp.int32)
    padded = (counts + MOE_BLOCK - 1) // MOE_BLOCK * MOE_BLOCK
    ends_p = jnp.cumsum(padded)
    start_p = ends_p - padded
    n_blocks = (nk + N_EXPERTS * (MOE_BLOCK - 1) + MOE_BLOCK - 1) // MOE_BLOCK
    block_start = jnp.arange(n_blocks, dtype=jnp.int32) * MOE_BLOCK
    block_e = jnp.minimum(jnp.sum(ends_p[None, :] <= block_start[:, None], axis=1), N_EXPERTS - 1).astype(jnp.int32)
    is_e = block_e[:, None] == experts[None, :]
    filled = jnp.sum(jnp.where(is_e, (start_p + counts)[None, :], 0), axis=1)
    n_valid = jnp.clip(filled - block_start, 0, MOE_BLOCK).astype(jnp.int32)
    dest = rank + jnp.sum(jnp.where(top_e[..., None] == experts, start_p, 0), axis=-1)
    dest = dest.transpose(1, 0, 2).reshape(TOP_K, n_tok)
    gate = top_g.transpose(0, 2, 1).reshape(n_tok, TOP_K)
    xb = _sc_scatter_rows(h2p.reshape(p, n_tok, w), dest, n_blocks * MOE_BLOCK)
    yb = _expert_ffn(xb, block_e, n_valid, layer, wgu, bgu, wd, bd)
    yk = _sc_gather_rows(yb, dest)
    return yk, gate


def kernel(x, c, ctx, c_ctx, w_mod, b_mod, w_in, ret_decay, ret_gn_g, ret_gn_b, hg_lb, hg_norm_g,
           na_rpb, gq_qn_g, gq_kn_g, w_branch, w_out, ln_g, ln_b, w_router, b_router, w_gu, b_gu,
           w_down, b_down):
    b_, t_, d = x.shape
    n_ctx = ctx.shape[1]
    depth = w_mod.shape[0]
    tm = 256

    sm = jax.nn.softmax(hg_lb.astype(F32), axis=1)
    lower = jnp.cumsum(sm, axis=1) - sm[:, :1]
    log_gamma = jax.nn.log_sigmoid(ret_decay.astype(F32)).reshape(depth, 2 * RET_HEADS)

    cc = jnp.concatenate([c, c_ctx[None, :], jnp.zeros((16 - b_ - 1, d), F32)], axis=0)
    modv = _mod_all(cc, w_mod, b_mod).reshape(depth, 16, 6, d)
    one = jnp.asarray([0.0, 1.0, 0.0, 0.0, 1.0, 0.0], F32)[None, None, :, None]
    modv = (modv + one)[:, :, jnp.asarray([1, 0, 2, 4, 3, 5])]
    modv = jnp.concatenate([modv, jnp.zeros((depth, 16, 2, d), F32)], axis=2)
    mod = jnp.stack([jnp.broadcast_to(modv[:, b_:b_ + 1], (depth, b_, 8, d)), modv[:, :b_]], axis=2)

    cos_r, sin_r = _rope_tables(n_ctx, t_, RET_DK)
    cos_g, sin_g = _rope_tables(n_ctx, t_, GQA_HD)

    xs = jnp.concatenate([ctx, x], axis=1)
    tm_in = 768 if xs.shape[1] % 768 == 0 else tm
    w_in_b = _w_in_prep(w_in)
    for l in range(depth):
        z = _in_proj(xs, mod[l], w_in_b[l], n_ctx, tm_in, PROJ_TOTAL // PROJ_TILES)
        ret_o = _retention(z, log_gamma[l], cos_r, sin_r, ret_gn_g[l], ret_gn_b[l], n_ctx)
        hg_o = _hgrn2(z, lower[:, l], hg_norm_g[l], n_ctx)
        na_o = _neighbourhood(z, _na_bias_table(na_rpb[l]), n_ctx)
        gq_o = _gqa(z, cos_g, sin_g, gq_qn_g[l], gq_kn_g[l], n_ctx, tm)
        wr_hi, wr_lo = _split_bf16(w_router[l].T)
        xs, h2p, top_e, top_g, rank, cnt = _merge(z, (ret_o, hg_o, na_o, gq_o), xs, mod[l],
                                                  w_branch[l].astype(BF16), w_out[l].astype(BF16),
                                                  ln_g[l, 0], ln_b[l, 0], wr_hi, wr_lo, b_router[l], n_ctx, tm)
        yk, gate = _moe(h2p, top_e, top_g, rank, cnt, l, w_gu, b_gu, w_down, b_down)
        xs = _combine(yk, gate, xs, mod[l], ln_g[l, 1], ln_b[l, 1], n_ctx, tm_in)
    return xs[:, n_ctx:]
```
